```python
import math, functools
import jax, jax.numpy as jnp
from jax import lax
import numpy as np

D_MODEL = 1024
BATCH = 16
SEQ = 2048
DEPTH = 1
DEC_BATCH = 4
DEC_SEQ = 4096
PAST_LEN = 128

MIX_WIDTH = D_MODEL
GM_WIDTH = MIX_WIDTH // 2
RW_WIDTH = MIX_WIDTH - GM_WIDTH
GM_HEADS = 4
GM_HEAD_DIM = GM_WIDTH // GM_HEADS
CHUNK = 128
RW_HEAD_DIM = 64
RW_HEADS = RW_WIDTH // RW_HEAD_DIM
DECAY_LORA = 32
ICLR_LORA = 32
GATE_LORA = 96
RW_SHIFT_COLS = 3 * RW_WIDTH + DECAY_LORA + ICLR_LORA + GATE_LORA
IN_COLS = 2 * GM_WIDTH + RW_SHIFT_COLS
N_EXPERTS = 32
TOP_K = 4
D_EXPERT = D_MODEL
SWIGLU_LIMIT = 7.0
SWIGLU_ALPHA = 1.702
MOE_BLOCK = 256
RMS_EPS = 1e-5
LN_EPS = 1e-5
GN_EPS = 64e-5

kernel_name = "hymba_gmlp_rwkv7_moe_encoder"

F32 = jnp.float32


def rmsnorm(x, g):
    xf = x.astype(F32)
    y = xf * lax.rsqrt(jnp.mean(xf * xf, axis=-1, keepdims=True) + RMS_EPS)
    return (y * g.astype(F32)).astype(x.dtype)


def centred_shift_delta(z):
    prev = jnp.pad(z[:, :-1], ((0, 0), (1, 0), (0, 0)))
    nxt = jnp.pad(z[:, 1:], ((0, 0), (0, 1), (0, 0)))
    return 0.5 * (prev + nxt) - z


def gmlp_group(u, v, ln_g, ln_b, ws, bs):
    B, T, _ = u.shape
    u = jax.nn.gelu(u)
    vf = jax.nn.gelu(v.astype(F32))
    mean = jnp.mean(vf, axis=-1, keepdims=True)
    var = jnp.mean(jnp.square(vf - mean), axis=-1, keepdims=True)
    vn = ((vf - mean) * lax.rsqrt(var + LN_EPS) * ln_g.astype(F32) + ln_b.astype(F32)).astype(u.dtype)
    vc = vn.reshape(B, T // CHUNK, CHUNK, GM_HEADS, GM_HEAD_DIM)
    mixed = jnp.einsum('hpq,bnqhd->bnphd', ws, vc) + bs.T[None, None, :, :, None]
    return u * mixed.reshape(B, T, GM_WIDTH)


def wkv_scan(r, w, k, v, a, b, reverse):
    B, T, H, N = r.shape

    def step(S, inp):
        r_t, w_t, k_t, v_t, a_t, b_t = inp
        sa = jnp.einsum('bhvk,bhk->bhv', S, a_t)
        S = S * w_t[:, :, None, :] + sa[..., None] * b_t[:, :, None, :] + v_t[..., None] * k_t[:, :, None, :]
        return S, jnp.einsum('bhvk,bhk->bhv', S, r_t)

    xs = tuple(jnp.swapaxes(t, 0, 1) for t in (r, w, k, v, a, b))
    S0 = jnp.zeros((B, H, N, N), F32)
    _, y = lax.scan(step, S0, xs, reverse=reverse)
    return jnp.swapaxes(y, 0, 1)


def rwkv_group(z, mu, w0, w2, a0, a2, g2, k_k, k_a, r_k, gn_g, gn_b):
    B, T, _ = z.shape
    C = RW_WIDTH
    zf = z.astype(F32)
    zf = zf + centred_shift_delta(zf) * mu.astype(F32)
    r = zf[..., :C]
    k = zf[..., C:2 * C]
    v = zf[..., 2 * C:3 * C]
    o = 3 * C
    w_lo = zf[..., o:o + DECAY_LORA]
    o += DECAY_LORA
    a_lo = zf[..., o:o + ICLR_LORA]
    o += ICLR_LORA
    g_lo = zf[..., o:o + GATE_LORA]

    def heads(t):
        return t.reshape(B, T, RW_HEADS, RW_HEAD_DIM)

    kk = heads(k * k_k.astype(F32))
    kk = kk / jnp.maximum(jnp.sqrt(jnp.sum(kk * kk, axis=-1, keepdims=True)), 1e-12)
    rh, vh = heads(r), heads(v)
    r_kf = r_k.astype(F32)
    y = jnp.zeros_like(rh)
    bonus = jnp.zeros_like(rh)
    for d, rev in ((0, False), (1, True)):
        w_log = -jax.nn.softplus(-(w0[d].astype(F32) + jnp.tanh(w_lo) @ w2[d].astype(F32))) - 0.5
        decay = jnp.exp(-jnp.exp(w_log))
        a = jax.nn.sigmoid(a0[d].astype(F32) + a_lo @ a2[d].astype(F32))
        kd = heads(k * (1.0 + (a - 1.0) * k_a.astype(F32)))
        y = y + wkv_scan(rh, heads(decay), kd, vh, -kk, kk * heads(a), rev)
        bonus = bonus + jnp.sum(rh * kd * r_kf, axis=-1, keepdims=True) * vh
    mean = jnp.mean(y, axis=-1, keepdims=True)
    var = jnp.mean(jnp.square(y - mean), axis=-1, keepdims=True)
    gn_gh = gn_g.astype(F32).reshape(RW_HEADS, RW_HEAD_DIM)
    gn_bh = gn_b.astype(F32).reshape(RW_HEADS, RW_HEAD_DIM)
    yn = (y - mean) * lax.rsqrt(var + GN_EPS) * gn_gh + gn_bh
    g = jax.nn.sigmoid(g_lo) @ g2.astype(F32)
    return ((yn + bonus).reshape(B, T, C) * g).astype(z.dtype)


def moe(h, router_w, router_b, w_gu, b_gu, w_down, b_down):
    n_tok, d = h.shape
    logits = h.astype(F32) @ router_w.astype(F32) + router_b.astype(F32)
    top_v, top_e = lax.top_k(logits, TOP_K)
    gates = jax.nn.softmax(top_v, axis=-1)
    n_asg = n_tok * TOP_K
    flat_e = top_e.reshape(-1).astype(jnp.int32)
    idx = jnp.arange(n_asg, dtype=jnp.int32)
    order = jnp.argsort(flat_e * n_asg + idx)
    sorted_e = flat_e[order]
    counts = jnp.bincount(flat_e, length=N_EXPERTS).astype(jnp.int32)
    padded = (counts + MOE_BLOCK - 1) // MOE_BLOCK * MOE_BLOCK
    pad_end = jnp.cumsum(padded)
    pad_start = pad_end - padded
    start = jnp.cumsum(counts) - counts
    dest = pad_start[sorted_e] + idx - start[sorted_e]
    n_blocks = -(-n_asg // MOE_BLOCK) + N_EXPERTS
    n_rows = n_blocks * MOE_BLOCK
    tok = order // TOP_K
    x_rows = jnp.zeros((n_rows, d), h.dtype).at[dest].set(h[tok])
    g_rows = jnp.zeros((n_rows,), F32).at[dest].set(gates.reshape(-1)[order])
    t_rows = jnp.zeros((n_rows,), jnp.int32).at[dest].set(tok)
    block_e = jnp.minimum(
        jnp.searchsorted(pad_end, jnp.arange(n_blocks, dtype=jnp.int32) * MOE_BLOCK, side='right'),
        N_EXPERTS - 1)

    def expert_block(args):
        xb, e = args
        gu = xb @ w_gu[e] + b_gu[e]
        gate = jnp.minimum(gu[:, :D_EXPERT], SWIGLU_LIMIT)
        up = jnp.clip(gu[:, D_EXPERT:], -SWIGLU_LIMIT, SWIGLU_LIMIT)
        act = gate * jax.nn.sigmoid(gate * SWIGLU_ALPHA) * (up + 1.0)
        return act @ w_down[e] + b_down[e]

    out = lax.map(expert_block, (x_rows.reshape(n_blocks, MOE_BLOCK, d), block_e))
    out = out.reshape(n_rows, d).astype(F32) * g_rows[:, None]
    return jnp.zeros((n_tok, d), F32).at[t_rows].add(out).astype(h.dtype)


def encoder(x, norm1_g, w_in, rw_mu, gm_ln_g, gm_ln_b, gm_ws, gm_bs, rw_w0, rw_w2, rw_a0, rw_a2,
            rw_g2, rw_k_k, rw_k_a, rw_r_k, rw_gn_g, rw_gn_b, w_out, norm2_g, router_w, router_b,
            w_gu, b_gu, w_down, b_down, final_g):
    B, T, D = x.shape
    for l in range(DEPTH):
        h = rmsnorm(x, norm1_g[l])
        zin = h @ w_in[l]
        gm = gmlp_group(zin[..., :GM_WIDTH], zin[..., GM_WIDTH:2 * GM_WIDTH],
                        gm_ln_g[l], gm_ln_b[l], gm_ws[l], gm_bs[l])
        rw = rwkv_group(zin[..., 2 * GM_WIDTH:], rw_mu[l], rw_w0[l], rw_w2[l], rw_a0[l], rw_a2[l],
                        rw_g2[l], rw_k_k[l], rw_k_a[l], rw_r_k[l], rw_gn_g[l], rw_gn_b[l])
        x = x + jnp.concatenate([gm, rw.astype(gm.dtype)], axis=-1) @ w_out[l]
        h = rmsnorm(x, norm2_g[l])
        x = x + moe(h.reshape(B * T, D), router_w[l], router_b[l], w_gu[l], b_gu[l],
                    w_down[l], b_down[l]).reshape(B, T, D)
    return rmsnorm(x, final_g)


def setup_inputs(seed: int = 0) -> dict:
    key = jax.random.key(seed)
    ks = iter(jax.random.split(key, 40))

    def nrm(shape, scale):
        return jax.random.normal(next(ks), shape, F32) * scale

    L = DEPTH
    D = D_MODEL
    E = N_EXPERTS
    return {
        "x_prompt": nrm((BATCH, SEQ, D), 1.0),
        "x_sample": nrm((DEC_BATCH, DEC_SEQ, D), 1.0),
        "norm1_g": 1.0 + nrm((L, D), 0.02),
        "w_in": nrm((L, D, IN_COLS), D ** -0.5),
        "rw_mu": jax.random.uniform(next(ks), (L, RW_SHIFT_COLS), F32, 0.0, 1.0),
        "gm_ln_g": 1.0 + nrm((L, GM_WIDTH), 0.02),
        "gm_ln_b": nrm((L, GM_WIDTH), 0.02),
        "gm_ws": nrm((L, GM_HEADS, CHUNK, CHUNK), CHUNK ** -0.5),
        "gm_bs": 1.0 + nrm((L, GM_HEADS, CHUNK), 0.02),
        "rw_w0": jax.random.uniform(next(ks), (L, 2, RW_WIDTH), F32, -4.0, 0.0),
        "rw_w2": nrm((L, 2, DECAY_LORA, RW_WIDTH), 0.1 * DECAY_LORA ** -0.5),
        "rw_a0": nrm((L, 2, RW_WIDTH), 0.1),
        "rw_a2": nrm((L, 2, ICLR_LORA, RW_WIDTH), 0.1 * ICLR_LORA ** -0.5),
        "rw_g2": nrm((L, GATE_LORA, RW_WIDTH), GATE_LORA ** -0.5),
        "rw_k_k": 0.85 + nrm((L, RW_WIDTH), 0.02),
        "rw_k_a": 1.0 + nrm((L, RW_WIDTH), 0.02),
        "rw_r_k": nrm((L, RW_HEADS, RW_HEAD_DIM), 0.1),
        "rw_gn_g": 1.0 + nrm((L, RW_WIDTH), 0.02),
        "rw_gn_b": nrm((L, RW_WIDTH), 0.02),
        "w_out": nrm((L, MIX_WIDTH, D), MIX_WIDTH ** -0.5),
        "norm2_g": 1.0 + nrm((L, D), 0.02),
        "router_w": nrm((L, D, E), D ** -0.5),
        "router_b": nrm((L, E), 0.01),
        "w_gu": nrm((L, E, D, 2 * D_EXPERT), D ** -0.5),
        "b_gu": nrm((L, E, 2 * D_EXPERT), 0.01),
        "w_down": nrm((L, E, D_EXPERT, D), D_EXPERT ** -0.5),
        "b_down": nrm((L, E, D), 0.01),
        "final_g": 1.0 + nrm((D,), 0.02),
    }


def reference(x_prompt, x_sample, norm1_g, w_in, rw_mu, gm_ln_g, gm_ln_b, gm_ws, gm_bs, rw_w0, rw_w2,
              rw_a0, rw_a2, rw_g2, rw_k_k, rw_k_a, rw_r_k, rw_gn_g, rw_gn_b, w_out, norm2_g,
              router_w, router_b, w_gu, b_gu, w_down, b_down, final_g):
    weights = (norm1_g, w_in, rw_mu, gm_ln_g, gm_ln_b, gm_ws, gm_bs, rw_w0, rw_w2, rw_a0, rw_a2,
               rw_g2, rw_k_k, rw_k_a, rw_r_k, rw_gn_g, rw_gn_b, w_out, norm2_g, router_w, router_b,
               w_gu, b_gu, w_down, b_down, final_g)
    y_prompt = encoder(x_prompt, *weights)
    y_sample = encoder(x_sample, *weights)
    return (y_prompt, y_sample)
```

```python
import functools

import jax
import jax.numpy as jnp
from jax import lax
from jax.experimental import pallas as pl
from jax.experimental.pallas import tpu as pltpu

F32 = jnp.float32
BF16 = jnp.bfloat16

D_MODEL = 1024
GM_WIDTH = 512
RW_WIDTH = 512
GM_HEADS = 4
GM_HEAD_DIM = 128
GM_CHUNK = 128
RW_HEAD_DIM = 64
DECAY_LORA = 32
ICLR_LORA = 32
GATE_LORA = 96
LORA_COLS = DECAY_LORA + ICLR_LORA + GATE_LORA
LORA_PAD = 256
Z_COLS = 3 * RW_WIDTH + LORA_PAD
N_EXPERTS = 32
TOP_K = 4
D_EXPERT = 1024
SWIGLU_LIMIT = 7.0
SWIGLU_ALPHA = 1.702
RMS_EPS = 1e-5
LN_EPS = 1e-5
GN_EPS = 64e-5

LANES = 128
SUBLANES = 8
VMEM_LIMIT = 56 * 1024 * 1024

RW_CHUNK = 64
PAIR = 2 * RW_HEAD_DIM
N_PAIRS = RW_WIDTH // PAIR
MOE_ROWS = 256
NEG_BIG = -1e30


def _dot(a, b):
    return jnp.dot(a, b, preferred_element_type=F32)


def _dot_nt(a, b):
    return lax.dot_general(a, b, (((1,), (1,)), ((), ())), preferred_element_type=F32)


def _split(x):
    hi = x.astype(BF16)
    lo = (x - hi.astype(F32)).astype(BF16)
    return hi, lo


def _dot3(a, b_hi, b_lo):
    a_hi, a_lo = _split(a)
    return _dot(a_hi, b_hi) + _dot(a_lo, b_hi) + _dot(a_hi, b_lo)


def _dot2(a, b01):
    a_hi, a_lo = _split(a)
    return _dot(a_hi, b01) + _dot(a_lo, b01)


def _dot2_lhs01(a01, b):
    b_hi, b_lo = _split(b)
    return _dot(a01, b_hi) + _dot(a01, b_lo)


def _gelu_tanh(x):
    return 0.5 * x * (1.0 + jnp.tanh(0.7978845608028654 * (x + 0.044715 * (x * x * x))))


def _sigmoid(x):
    return 1.0 / (1.0 + jnp.exp(-x))


def _mix_in_body(x_ref, g1_ref, wgm_ref, wz_ref, lng_ref, lnb_ref, ws_ref, bs_ref, gm_ref, z_ref):
    x = x_ref[...]
    h = x * lax.rsqrt(jnp.mean(x * x, axis=-1, keepdims=True) + RMS_EPS) * g1_ref[...]
    hb = h.astype(BF16)
    z_ref[...] = _dot(hb, wz_ref[...])
    uv = _dot(hb, wgm_ref[...])
    u = _gelu_tanh(uv[:, :GM_WIDTH])
    v = _gelu_tanh(uv[:, GM_WIDTH:])
    mean = jnp.mean(v, axis=-1, keepdims=True)
    vc = v - mean
    var = jnp.mean(vc * vc, axis=-1, keepdims=True)
    vn = (vc * lax.rsqrt(var + LN_EPS) * lng_ref[...] + lnb_ref[...]).astype(BF16)
    tm = x.shape[0]
    for c in range(tm // GM_CHUNK):
        rows = slice(c * GM_CHUNK, (c + 1) * GM_CHUNK)
        for hd in range(GM_HEADS):
            cols = slice(hd * GM_HEAD_DIM, (hd + 1) * GM_HEAD_DIM)
            mixed = _dot(ws_ref[hd], vn[rows, cols]) + bs_ref[hd]
            gm_ref[rows, cols] = (u[rows, cols] * mixed).astype(BF16)


def _mix_in(x2d, p, tm):
    n = x2d.shape[0]
    full = lambda shape: pl.BlockSpec(shape, lambda i: (0,) * len(shape))
    return pl.pallas_call(
        _mix_in_body,
        grid=(n // tm,),
        in_specs=[
            pl.BlockSpec((tm, D_MODEL), lambda i: (i, 0)),
            full((1, D_MODEL)),
            full((D_MODEL, 2 * GM_WIDTH)),
            full((D_MODEL, Z_COLS)),
            full((1, GM_WIDTH)),
            full((1, GM_WIDTH)),
            full((GM_HEADS, GM_CHUNK, GM_CHUNK)),
            full((GM_HEADS, GM_CHUNK, GM_HEAD_DIM)),
        ],
        out_specs=[
            pl.BlockSpec((tm, GM_WIDTH), lambda i: (i, 0)),
            pl.BlockSpec((tm, Z_COLS), lambda i: (i, 0)),
        ],
        out_shape=[
            jax.ShapeDtypeStruct((n, GM_WIDTH), BF16),
            jax.ShapeDtypeStruct((n, Z_COLS), F32),
        ],
        compiler_params=pltpu.CompilerParams(vmem_limit_bytes=VMEM_LIMIT),
        name="mix_in",
    )(x2d, p["norm1_g"], p["w_gm"], p["w_z"], p["gm_ln_g"], p["gm_ln_b"], p["gm_ws"], p["gm_bs"])


def _rwkv_masks(reverse):
    c = RW_CHUNK
    ti = lax.broadcasted_iota(jnp.int32, (c, c), 0)
    tj = lax.broadcasted_iota(jnp.int32, (c, c), 1)
    incl = (tj >= ti) if reverse else (tj <= ti)
    lane =lax.broadcasted_iota(jnp.int32, (c, PAIR), 1)
    head0 = lane < RW_HEAD_DIM
    lane2 = lax.broadcasted_iota(jnp.int32, (c, 2 * c), 1)
    left = lane2 < c
    ti2 = lax.broadcasted_iota(jnp.int32, (c, 2 * c), 0)
    tj2 = jnp.where(left, lane2, lane2 - c)
    incl2 = (tj2 >= ti2) if reverse else (tj2 <= ti2)
    strict2 = (tj2 > ti2) if reverse else (tj2 < ti2)
    eye2 = jnp.where(tj2 == ti2, 1.0, 0.0).astype(F32)
    si = lax.broadcasted_iota(jnp.int32, (PAIR, PAIR), 0)
    sj = lax.broadcasted_iota(jnp.int32, (PAIR, PAIR), 1)
    same_head = (si // RW_HEAD_DIM) == (sj // RW_HEAD_DIM)
    return dict(
        incl2=incl2, strict2=strict2, head0=head0, left=left, eye2=eye2, same_head=same_head,
        tri=jnp.where(incl, 1.0, 0.0).astype(BF16),
    )


def _stack_heads(x, head0):
    return jnp.concatenate([jnp.where(head0, x, 0.0), jnp.where(head0, 0.0, x)], axis=0)


def _block_diag(x, left):
    return jnp.concatenate([jnp.where(left, x, 0.0), jnp.where(left, 0.0, x)], axis=0)


def _pair_chunk(state, r_t, a_t, b_t, k_t, b_h, k_h, v, w_tot, m):
    c = RW_CHUNK
    bf = lambda t: t.astype(BF16)
    rstack = jnp.concatenate([_stack_heads(b_t, m["head0"]), _stack_heads(k_t, m["head0"])], axis=0)
    ar = bf(jnp.concatenate([a_t, r_t], axis=0))
    acat = _dot_nt(ar, bf(rstack))
    a_ab = jnp.where(m["strict2"], acat[:c, :2 * c], 0.0)
    a_ak = jnp.where(m["strict2"], acat[:c, 2 * c:], 0.0)
    a_rb = jnp.where(m["incl2"], acat[c:, :2 * c], 0.0)
    a_rk = jnp.where(m["incl2"], acat[c:, 2 * c:], 0.0)
    p = m["eye2"] + a_ab
    lk = _dot(bf(a_ab), bf(_block_diag(a_ab, m["left"])))
    power = 2
    while 2 * power < c:
        res = _dot(bf(jnp.concatenate([p, lk], axis=0)), bf(_block_diag(lk, m["left"])))
        p = p + res[:c]
        lk = res[c:]
        power *= 2
    p = p + _dot(bf(p), bf(_block_diag(lk, m["left"])))
    sa = _dot_nt(ar, bf(state))
    vstack = _stack_heads(v, m["head0"])
    rhs = sa[:c] + _dot(bf(a_ak), bf(vstack))
    u = _dot(bf(p), bf(_stack_heads(rhs, m["head0"])))
    y = sa[c:] + _dot(bf(jnp.concatenate([a_rb, a_rk], axis=1)),
                      bf(jnp.concatenate([_stack_heads(u, m["head0"]), vstack], axis=0)))
    uv_t = jnp.concatenate([u, v], axis=0).T
    upd = _dot(bf(uv_t), bf(jnp.concatenate([b_h, k_h], axis=0)))
    new_state = state * w_tot + jnp.where(m["same_head"], upd, 0.0)
    return new_state, y


def _rwkv_body(reverse, n_chunks, bt, *refs):
    if reverse:
        (zc_ref, zp_ref, zn_ref, yf_ref, mu_ref, lw_hi_ref, lw_lo_ref, w0_ref, a0_ref, a0o_ref, kk_ref, ka_ref,
         rk_ref, bd_ref, gng_ref, gnb_ref, out_ref, state_ref) = refs
    else:
        (zc_ref, zp_ref, zn_ref, mu_ref, lw_hi_ref, lw_lo_ref, w0_ref, a0_ref, kk_ref, ka_ref,
         bd_ref, out_ref, state_ref) = refs
    c = RW_CHUNK
    ci = pl.program_id(1)
    cidx = (n_chunks - 1 - ci) if reverse else ci

    @pl.when(ci == 0)
    def _():
        state_ref[...] = jnp.zeros_like(state_ref)

    m = _rwkv_masks(reverse)
    row = lax.broadcasted_iota(jnp.int32, (c, Z_COLS), 0)
    lora_lane = lax.broadcasted_iota(jnp.int32, (c, LORA_PAD), 1)
    bd = bd_ref[...]
    for b in range(bt):
        z = zc_ref[b]
        prev_row = jnp.where(cidx > 0, zp_ref[b, SUBLANES - 1:SUBLANES, :], 0.0)
        next_row = jnp.where(cidx < n_chunks - 1, zn_ref[b, 0:1, :], 0.0)
        prev = jnp.where(row == 0, prev_row, pltpu.roll(z, 1, axis=0))
        nxt = jnp.where(row == c - 1, next_row, pltpu.roll(z, c - 1, axis=0))
        zf = z + (0.5 * (prev + nxt) - z) * mu_ref[...]
        r = zf[:, :RW_WIDTH]
        k = zf[:, RW_WIDTH:2 * RW_WIDTH]
        v = zf[:, 2 * RW_WIDTH:3 * RW_WIDTH]
        lo = zf[:, 3 * RW_WIDTH:]
        act = jnp.where(lora_lane < DECAY_LORA, jnp.tanh(lo),
                        jnp.where(lora_lane < DECAY_LORA + ICLR_LORA, lo, _sigmoid(lo)))
        proj = _dot3(act, lw_hi_ref[...], lw_lo_ref[...])
        w_in = w0_ref[...] + proj[:, :RW_WIDTH]
        w_log = -(jnp.maximum(-w_in, 0.0) + jnp.log(1.0 + jnp.exp(-jnp.abs(w_in)))) - 0.5
        logw = -jnp.exp(w_log)
        a_sig = _sigmoid(a0_ref[...] + proj[:, RW_WIDTH:2 * RW_WIDTH])
        kk = k * kk_ref[...]
        kk_norm = jnp.sqrt(_dot2(kk * kk, bd))
        kk = kk / jnp.maximum(kk_norm, 1e-12)
        kd = k * (1.0 + (a_sig - 1.0) * ka_ref[...])
        a_vec = -kk
        b_vec = kk * a_sig
        cum = _dot2_lhs01(m["tri"], logw)
        tot = cum[0:1, :] if reverse else cum[c - 1:c, :]
        w_inc = jnp.exp(cum)
        w_exc = jnp.exp(cum - logw)
        w_inv = jnp.exp(-cum)
        w_end = jnp.exp(tot - cum)
        w_tot = jnp.exp(tot)
        r_t = r * w_inc
        a_t = a_vec * w_exc
        b_t = b_vec * w_inv
        k_t = kd * w_inv
        b_h = b_vec * w_end
        k_h = kd * w_end
        ys = []
        for pi in range(N_PAIRS):
            ln = slice(pi * PAIR, (pi + 1) * PAIR)
            new_state, y = _pair_chunk(state_ref[b, pi], r_t[:, ln], a_t[:, ln], b_t[:, ln], k_t[:, ln],
                                       b_h[:, ln], k_h[:, ln], v[:, ln], w_tot[:, ln], m)
            state_ref[b, pi] = new_state
            ys.append(y)
        y = jnp.concatenate(ys, axis=1)
        if not reverse:
            out_ref[b] = y
        else:
            y = y + yf_ref[b]
            a_other = _sigmoid(a0o_ref[...] + proj[:, 2 * RW_WIDTH:3 * RW_WIDTH])
            kd_sum = kd + k * (1.0 + (a_other - 1.0) * ka_ref[...])
            bonus = _dot2(r * kd_sum * rk_ref[...], bd) * v
            inv_n = 1.0 / RW_HEAD_DIM
            mean = _dot2(y, bd) * inv_n
            yc = y - mean
            var = _dot2(yc * yc, bd) * inv_n
            yn = yc * lax.rsqrt(var + GN_EPS) * gng_ref[...] + gnb_ref[...]
            g = proj[:, 3 * RW_WIDTH:]
            out_ref[b] = ((yn + bonus) * g).astype(BF16)


def _rwkv_pass(z3, yf, p, reverse, bt):
    b_sz, t_len, _ = z3.shape
    c = RW_CHUNK
    n_chunks = t_len // c
    halo_per_chunk = c // SUBLANES
    n_halo = t_len // SUBLANES

    def cidx(ci):
        return (n_chunks - 1 - ci) if reverse else ci

    full = lambda shape: pl.BlockSpec(shape, lambda bi, ci: (0,) * len(shape))
    row512 = full((1, RW_WIDTH))
    d = "rev" if reverse else "fwd"
    n_proj = p["lora_hi_" + d].shape[1]
    in_specs = [
        pl.BlockSpec((bt, c, Z_COLS), lambda bi, ci: (bi, cidx(ci), 0)),
        pl.BlockSpec((bt, SUBLANES, Z_COLS), lambda bi, ci: (bi, jnp.maximum(cidx(ci) * halo_per_chunk - 1, 0), 0)),
        pl.BlockSpec((bt, SUBLANES, Z_COLS),
                     lambda bi, ci: (bi, jnp.minimum((cidx(ci) + 1) * halo_per_chunk, n_halo - 1), 0)),
    ]
    args = [z3, z3, z3]
    if reverse:
        in_specs.append(pl.BlockSpec((bt, c, RW_WIDTH), lambda bi, ci: (bi, cidx(ci), 0)))
        args.append(yf)
    in_specs += [full((1, Z_COLS)), full((LORA_PAD, n_proj)), full((LORA_PAD, n_proj)), row512, row512]
    args += [p["rw_mu"], p["lora_hi_" + d], p["lora_lo_" + d], p["w0_" + d], p["a0_" + d]]
    if reverse:
        in_specs.append(row512)
        args.append(p["a0_fwd"])
    in_specs += [row512, row512]
    args += [p["rw_k_k"], p["rw_k_a"]]
    if reverse:
        in_specs.append(row512)
        args.append(p["rw_r_k"])
    in_specs.append(full((RW_WIDTH, RW_WIDTH)))
    args.append(p["head_ones"])
    if reverse:
        in_specs += [row512, row512]
        args += [p["rw_gn_g"], p["rw_gn_b"]]
    return pl.pallas_call(
        functools.partial(_rwkv_body, reverse, n_chunks, bt),
        grid=(b_sz // bt, n_chunks),
        in_specs=in_specs,
        out_specs=pl.BlockSpec((bt, c, RW_WIDTH), lambda bi, ci: (bi, cidx(ci), 0)),
        out_shape=jax.ShapeDtypeStruct((b_sz, t_len, RW_WIDTH), BF16 if reverse else F32),
        scratch_shapes=[pltpu.VMEM((bt, N_PAIRS, PAIR, PAIR), F32)],
        compiler_params=pltpu.CompilerParams(vmem_limit_bytes=VMEM_LIMIT),
        name="rwkv_" + d,
    )(*args)


def _mix_out_body(x_ref, gm_ref, rw_ref, wo_ref, g2_ref, rw_hi_ref, rw_lo_ref, rb_ref,
                  x1_ref, h2_ref, route_ref, counts_ref, carry_ref):
    i = pl.program_id(0)

    @pl.when(i == 0)
    def _():
        carry_ref[...] = jnp.zeros_like(carry_ref)

    x1 = x_ref[...] + _dot(gm_ref[...], wo_ref[:GM_WIDTH, :]) + _dot(rw_ref[...], wo_ref[GM_WIDTH:, :])
    x1_ref[...] = x1
    h2 = x1 * lax.rsqrt(jnp.mean(x1 * x1, axis=-1, keepdims=True) + RMS_EPS) * g2_ref[...]
    h2_ref[...] = h2
    tm = x1.shape[0]
    logits = _dot3(h2, rw_hi_ref[...], rw_lo_ref[...]) + rb_ref[...]
    lane = lax.broadcasted_iota(jnp.int32, (tm, LANES), 1).astype(F32)
    work = logits
    vals, ids = [], []
    onehot = jnp.zeros((tm, LANES), F32)
    for _ in range(TOP_K):
        mx = jnp.max(work, axis=-1, keepdims=True)
        idx = jnp.min(jnp.where(work == mx, lane, float(LANES)), axis=-1, keepdims=True)
        hit = lane == idx
        vals.append(mx)
        ids.append(idx)
        onehot = jnp.where(hit, 1.0, onehot)
        work = jnp.where(hit, -jnp.inf, work)
    exps = [jnp.exp(vk - vals[0]) for vk in vals]
    denom = exps[0] + exps[1] + exps[2] + exps[3]
    ti = lax.broadcasted_iota(jnp.int32, (tm, tm), 0)
    tj = lax.broadcasted_iota(jnp.int32, (tm, tm), 1)
    before = jnp.where(tj < ti, 1.0, 0.0).astype(BF16)
    ranks = carry_ref[0:1, :] + _dot(before, onehot.astype(BF16))
    slab = jnp.zeros((tm, LANES), F32)
    for kk in range(TOP_K):
        slab = jnp.where(lane == kk, ids[kk], slab)
        slab = jnp.where(lane == TOP_K + kk, exps[kk] / denom, slab)
        rank_k = jnp.sum(jnp.where(lane == ids[kk], ranks, 0.0), axis=-1, keepdims=True)
        slab = jnp.where(lane == 2 * TOP_K + kk, rank_k, slab)
    route_ref[...] = slab
    new_counts = carry_ref[0:1, :] + jnp.sum(onehot, axis=0, keepdims=True)
    carry_ref[...] = jnp.broadcast_to(new_counts, carry_ref.shape)
    counts_ref[...] = jnp.broadcast_to(new_counts, counts_ref.shape)


def _mix_out(x2d, gm, rw, p, tm):
    n = x2d.shape[0]
    full = lambda shape: pl.BlockSpec(shape, lambda i: (0,) * len(shape))
    tile = lambda w: pl.BlockSpec((tm, w), lambda i: (i, 0))
    return pl.pallas_call(
        _mix_out_body,
        grid=(n // tm,),
        in_specs=[tile(D_MODEL), tile(GM_WIDTH), tile(RW_WIDTH), full((D_MODEL, D_MODEL)), full((1, D_MODEL)),
                  full((D_MODEL, LANES)), full((D_MODEL, LANES)), full((1, LANES))],
        out_specs=[tile(D_MODEL), tile(D_MODEL), tile(LANES), full((SUBLANES, LANES))],
        out_shape=[
            jax.ShapeDtypeStruct((n, D_MODEL), F32),
            jax.ShapeDtypeStruct((n, D_MODEL), F32),
            jax.ShapeDtypeStruct((n, LANES), F32),
            jax.ShapeDtypeStruct((SUBLANES, LANES), F32),
        ],
        scratch_shapes=[pltpu.VMEM((SUBLANES, LANES), F32)],
        compiler_params=pltpu.CompilerParams(vmem_limit_bytes=VMEM_LIMIT),
        name="mix_out",
    )(x2d, gm, rw, p["w_out"], p["norm2_g"], p["router_hi"], p["router_lo"], p["router_b"])


def _row_copy(h_hbm, xbuf, sem, slot, r, tok):
    return pltpu.make_async_copy(h_hbm.at[pl.ds(tok, 1), :], xbuf.at[slot, pl.ds(r, 1), :], sem.at[slot])


def _experts_body(blk_e_ref, n_used_ref, tok_ref, tok_next_ref, h_hbm, wgu_ref, bgu_ref, wd_ref, bd_ref,
                  out_ref, xbuf, sem):
    del blk_e_ref
    i = pl.program_id(0)
    n_used = n_used_ref[0]
    slot = i % 2

    def issue(tok_smem, to_slot):
        def body(r, carry):
            _row_copy(h_hbm, xbuf, sem, to_slot, r, tok_smem[0, 0, r]).start()
            return carry
        lax.fori_loop(0, MOE_ROWS, body, 0)

    @pl.when(i == 0)
    def _():
        issue(tok_ref, 0)

    @pl.when(i + 1 < n_used)
    def _():
        issue(tok_next_ref, 1 - slot)

    @pl.when(i < n_used)
    def _():
        def wait_body(r, carry):
            _row_copy(h_hbm, xbuf, sem, slot, r, 0).wait()
            return carry
        lax.fori_loop(0, MOE_ROWS, wait_body, 0)
        xb = xbuf[slot].astype(BF16)
        gu = _dot(xb, wgu_ref[0]) + bgu_ref[0]
        gate = jnp.minimum(gu[:, :D_EXPERT], SWIGLU_LIMIT)
        up = jnp.clip(gu[:, D_EXPERT:], -SWIGLU_LIMIT, SWIGLU_LIMIT)
        act = gate * _sigmoid(gate * SWIGLU_ALPHA) * (up + 1.0)
        out_ref[...] = _dot(act.astype(BF16), wd_ref[0]) + bd_ref[0]

    @pl.when(i >= n_used)
    def _():
        out_ref[...] = jnp.zeros_like(out_ref)


def _experts(h2, tok_rows, blk_e, n_used, p):
    n_blocks = blk_e.shape[0]
    tok3 = tok_rows.reshape(n_blocks, 1, MOE_ROWS)
    smem_blk = lambda fn: pl.BlockSpec((1, 1, MOE_ROWS), fn, memory_space=pltpu.SMEM)
    last = n_blocks - 1
    grid_spec = pltpu.PrefetchScalarGridSpec(
        num_scalar_prefetch=2,
        grid=(n_blocks,),
        in_specs=[
            smem_blk(lambda i, be, nu: (i, 0, 0)),
            smem_blk(lambda i, be, nu: (jnp.minimum(i + 1, last), 0, 0)),
            pl.BlockSpec(memory_space=pl.ANY),
            pl.BlockSpec((1, D_MODEL, 2 * D_EXPERT), lambda i, be, nu: (be[i], 0, 0)),
            pl.BlockSpec((1, 1, 2 * D_EXPERT), lambda i, be, nu: (be[i], 0, 0)),
            pl.BlockSpec((1, D_EXPERT, D_MODEL), lambda i, be, nu: (be[i], 0, 0)),
            pl.BlockSpec((1, 1, D_MODEL), lambda i, be, nu: (be[i], 0, 0)),
        ],
        out_specs=pl.BlockSpec((MOE_ROWS, D_MODEL), lambda i, be, nu: (i, 0)),
        scratch_shapes=[pltpu.VMEM((2, MOE_ROWS, D_MODEL), F32), pltpu.SemaphoreType.DMA((2,))],
    )
    return pl.pallas_call(
        _experts_body,
        grid_spec=grid_spec,
        out_shape=jax.ShapeDtypeStruct((n_blocks * MOE_ROWS, D_MODEL), F32),
        compiler_params=pltpu.CompilerParams(vmem_limit_bytes=VMEM_LIMIT),
        name="experts",
    )(blk_e, n_used, tok3, tok3, h2, p["w_gu"], p["b_gu"], p["w_down"], p["b_down"])


def _combine_copy(y_hbm, ybuf, sem, slot, j, row):
    return pltpu.make_async_copy(y_hbm.at[pl.ds(row, 1), :], ybuf.at[slot, pl.ds(j, 1), :], sem.at[slot])


def _combine_body(tm, n_tiles, dest_ref, dest_next_ref, y_hbm, x1_ref, gates_ref, fg_ref, out_ref, ybuf, sem):
    i = pl.program_id(0)
    slot = i % 2
    n_rows = TOP_K * tm

    def issue(dest_smem, to_slot):
        def body(j, carry):
            _combine_copy(y_hbm, ybuf, sem, to_slot, j, dest_smem[0, 0, j]).start()
            return carry
        lax.fori_loop(0, n_rows, body, 0)

    @pl.when(i == 0)
    def _():
        issue(dest_ref, 0)

    @pl.when(i + 1 < n_tiles)
    def _():
        issue(dest_next_ref, 1 - slot)

    def wait_body(j, carry):
        _combine_copy(y_hbm, ybuf, sem, slot, j, 0).wait()
        return carry
    lax.fori_loop(0, n_rows, wait_body, 0)
    acc = x1_ref[...]
    gates = gates_ref[...]
    for kk in range(TOP_K):
        acc = acc + gates[:, TOP_K + kk:TOP_K + kk + 1] * ybuf[slot, kk * tm:(kk + 1) * tm, :]
    out_ref[...] = acc * lax.rsqrt(jnp.mean(acc * acc, axis=-1, keepdims=True) + RMS_EPS) * fg_ref[...]


def _combine(y_rows, dest_km, x1, gates, final_g, tm):
    n = x1.shape[0]
    n_tiles = n // tm
    smem_blk = lambda fn: pl.BlockSpec((1, 1, TOP_K * tm), fn, memory_space=pltpu.SMEM)
    return pl.pallas_call(
        functools.partial(_combine_body, tm, n_tiles),
        grid=(n_tiles,),
        in_specs=[
            smem_blk(lambda i: (i, 0, 0)),
            smem_blk(lambda i: (jnp.minimum(i + 1, n_tiles - 1), 0, 0)),
            pl.BlockSpec(memory_space=pl.ANY),
            pl.BlockSpec((tm, D_MODEL), lambda i: (i, 0)),
            pl.BlockSpec((tm, LANES), lambda i: (i, 0)),
            pl.BlockSpec((1, D_MODEL), lambda i: (0, 0)),
        ],
        out_specs=pl.BlockSpec((tm, D_MODEL), lambda i: (i, 0)),
        out_shape=jax.ShapeDtypeStruct((n, D_MODEL), F32),
        scratch_shapes=[pltpu.VMEM((2, TOP_K * tm, D_MODEL), F32), pltpu.SemaphoreType.DMA((2,))],
        compiler_params=pltpu.CompilerParams(vmem_limit_bytes=VMEM_LIMIT),
        name="combine",
    )(dest_km, dest_km, y_rows, x1, gates, final_g)


def _prepare(norm1_g, w_in, rw_mu, gm_ln_g, gm_ln_b, gm_ws, gm_bs, rw_w0, rw_w2, rw_a0, rw_a2, rw_g2,
             rw_k_k, rw_k_a, rw_r_k, rw_gn_g, rw_gn_b, w_out, norm2_g, router_w, router_b,
             w_gu, b_gu, w_down, b_down, final_g):
    l = 0
    row = lambda t: t.reshape(1, -1).astype(F32)
    p = {}
    p["norm1_g"] = row(norm1_g[l])
    w = w_in[l]
    p["w_gm"] = w[:, :2 * GM_WIDTH].astype(BF16)
    pad_cols = LORA_PAD - LORA_COLS
    p["w_z"] = jnp.pad(w[:, 2 * GM_WIDTH:], ((0, 0), (0, pad_cols))).astype(BF16)
    p["rw_mu"] = jnp.pad(row(rw_mu[l]), ((0, 0), (0, pad_cols)))
    p["gm_ln_g"] = row(gm_ln_g[l])
    p["gm_ln_b"] = row(gm_ln_b[l])
    p["gm_ws"] = gm_ws[l].astype(BF16)
    p["gm_bs"] = jnp.broadcast_to(gm_bs[l][:, :, None], (GM_HEADS, GM_CHUNK, GM_HEAD_DIM)).astype(F32)
    zeros = lambda r, c: jnp.zeros((r, c), F32)
    o_a = DECAY_LORA
    o_g = DECAY_LORA + ICLR_LORA

    def lora_matrix(d, with_epilogue):
        blocks = [jnp.concatenate([rw_w2[l, d], zeros(LORA_PAD - DECAY_LORA, RW_WIDTH)], axis=0),
                  jnp.concatenate([zeros(o_a, RW_WIDTH), rw_a2[l, d], zeros(LORA_PAD - o_g, RW_WIDTH)], axis=0)]
        if with_epilogue:
            blocks.append(jnp.concatenate([zeros(o_a, RW_WIDTH), rw_a2[l, 1 - d], zeros(LORA_PAD - o_g, RW_WIDTH)],
                                          axis=0))
            blocks.append(jnp.concatenate([zeros(o_g, RW_WIDTH), rw_g2[l], zeros(LORA_PAD - LORA_COLS, RW_WIDTH)],
                                          axis=0))
        return jnp.concatenate(blocks, axis=1)

    for d, name in ((0, "fwd"), (1, "rev")):
        mat = lora_matrix(d, with_epilogue=(d == 1))
        hi = mat.astype(BF16)
        p["lora_hi_" + name] = hi
        p["lora_lo_" + name] = (mat - hi.astype(F32)).astype(BF16)
        p["w0_" + name] = row(rw_w0[l, d])
        p["a0_" + name] = row(rw_a0[l, d])
    p["rw_k_k"] = row(rw_k_k[l])
    p["rw_k_a"] = row(rw_k_a[l])
    p["rw_r_k"] = row(rw_r_k[l])
    p["rw_gn_g"] = row(rw_gn_g[l])
    p["rw_gn_b"] = row(rw_gn_b[l])
    ch = jnp.arange(RW_WIDTH) // RW_HEAD_DIM
    p["head_ones"] = (ch[:, None] == ch[None, :]).astype(BF16)
    p["w_out"] = w_out[l].astype(BF16)
    p["norm2_g"] = row(norm2_g[l])
    rw_pad = jnp.pad(router_w[l].astype(F32), ((0, 0), (0, LANES - N_EXPERTS)))
    hi = rw_pad.astype(BF16)
    p["router_hi"] = hi
    p["router_lo"] = (rw_pad - hi.astype(F32)).astype(BF16)
    p["router_b"] = jnp.pad(row(router_b[l]), ((0, 0), (0, LANES - N_EXPERTS)), constant_values=NEG_BIG)
    p["w_gu"] = w_gu[l].astype(BF16)
    p["b_gu"] = b_gu[l].reshape(N_EXPERTS, 1, 2 * D_EXPERT).astype(F32)
    p["w_down"] = w_down[l].astype(BF16)
    p["b_down"] = b_down[l].reshape(N_EXPERTS, 1, D_MODEL).astype(F32)
    p["final_g"] = row(final_g)
    return p


def _pick_tile(n, want):
    t = want
    while n % t:
        t //= 2
    return t


def _encoder(x, p):
    b_sz, t_len, d = x.shape
    n = b_sz * t_len
    x2d = x.reshape(n, d)
    tm = _pick_tile(n, 512)
    gm, z = _mix_in(x2d, p, tm)
    z3 = z.reshape(b_sz, t_len, Z_COLS)
    yf = _rwkv_pass(z3, None, p, reverse=False, bt=1)
    rw = _rwkv_pass(z3, yf, p, reverse=True, bt=1)
    tmo = _pick_tile(n, 256)
    x1, h2, route, counts = _mix_out(x2d, gm, rw.reshape(n, RW_WIDTH), p, tmo)
    ids = route[:, :TOP_K].astype(jnp.int32)
    ranks = route[:, 2 * TOP_K:3 * TOP_K].astype(jnp.int32)
    cnt = counts[0, :N_EXPERTS].astype(jnp.int32)
    padded = (cnt + MOE_ROWS - 1) // MOE_ROWS * MOE_ROWS
    pad_end = jnp.cumsum(padded)
    pad_start = pad_end - padded
    dest = pad_start[ids] + ranks
    n_blocks = n * TOP_K // MOE_ROWS + N_EXPERTS
    n_rows = n_blocks * MOE_ROWS
    tok = jnp.broadcast_to(jnp.arange(n, dtype=jnp.int32)[:, None], (n, TOP_K))
    tok_rows = jnp.zeros((n_rows,), jnp.int32).at[dest.reshape(-1)].set(tok.reshape(-1))
    blk_e = jnp.minimum(
        jnp.searchsorted(pad_end, jnp.arange(n_blocks, dtype=jnp.int32) * MOE_ROWS, side="right"),
        N_EXPERTS - 1).astype(jnp.int32)
    n_used = (pad_end[-1] // MOE_ROWS).astype(jnp.int32).reshape(1)
    y_rows = _experts(h2, tok_rows, blk_e, n_used, p)
    tmc = _pick_tile(n, 128)
    dest_km = dest.reshape(n // tmc, tmc, TOP_K).transpose(0, 2, 1).reshape(n // tmc, 1, TOP_K * tmc)
    out = _combine(y_rows, dest_km, x1, route, p["final_g"], tmc)
    return out.reshape(b_sz, t_len, d)


def kernel(x_prompt, x_sample, norm1_g, w_in, rw_mu, gm_ln_g, gm_ln_b, gm_ws, gm_bs, rw_w0, rw_w2, rw_a0, rw_a2,
           rw_g2, rw_k_k, rw_k_a, rw_r_k, rw_gn_g, rw_gn_b, w_out, norm2_g, router_w, router_b, w_gu, b_gu,
           w_down, b_down, final_g):
    p = _prepare(norm1_g, w_in, rw_mu, gm_ln_g, gm_ln_b, gm_ws, gm_bs, rw_w0, rw_w2, rw_a0, rw_a2, rw_g2,
                 rw_k_k, rw_k_a, rw_r_k, rw_gn_g, rw_gn_b, w_out, norm2_g, router_w, router_b,
                 w_gu, b_gu, w_down, b_down, final_g)
    return (_encoder(x_prompt, p), _encoder(x_sample, p))
```

```python
import functools

import jax
import jax.numpy as jnp
from jax import lax
from jax.experimental import pallas as pl
from jax.experimental.pallas import tpu as pltpu
from jax.experimental.pallas import tpu_sc as plsc

F32 = jnp.float32
BF16 = jnp.bfloat16

D_MODEL = 1024
GM_WIDTH = 512
RW_WIDTH = 512
GM_HEADS = 4
GM_HEAD_DIM = 128
GM_CHUNK = 128
RW_HEAD_DIM = 64
DECAY_LORA = 32
ICLR_LORA = 32
GATE_LORA = 96
LORA_COLS = DECAY_LORA + ICLR_LORA + GATE_LORA
LORA_PAD = 256
Z_COLS = 3 * RW_WIDTH + LORA_PAD
N_EXPERTS = 32
TOP_K = 4
D_EXPERT = 1024
SWIGLU_LIMIT = 7.0
SWIGLU_ALPHA = 1.702
RMS_EPS = 1e-5
LN_EPS = 1e-5
GN_EPS = 64e-5

LANES = 128
SUBLANES = 8
VMEM_LIMIT = 56 * 1024 * 1024

RW_CHUNK = 64
PAIR = 2 * RW_HEAD_DIM
N_PAIRS = RW_WIDTH // PAIR
RW_BATCH_TILE = 4
MOE_ROWS = 512
SC_CORES = 2
SC_SUBCORES = 16
SC_WINDOW = 32
NEG_BIG = -1e30


def _dot(a, b):
    return jnp.dot(a, b, preferred_element_type=F32)


def _dot_nt(a, b):
    return lax.dot_general(a, b, (((1,), (1,)), ((), ())), preferred_element_type=F32)


def _split(x):
    hi = x.astype(BF16)
    lo = (x - hi.astype(F32)).astype(BF16)
    return hi, lo


def _dot3(a, b_hi, b_lo):
    a_hi, a_lo = _split(a)
    return _dot(a_hi, b_hi) + _dot(a_lo, b_hi) + _dot(a_hi, b_lo)


def _dot2(a, b01):
    a_hi, a_lo = _split(a)
    return _dot(a_hi, b01) + _dot(a_lo, b01)


def _dot2_lhs01(a01, b):
    b_hi, b_lo = _split(b)
    return _dot(a01, b_hi) + _dot(a01, b_lo)


def _gelu_tanh(x):
    return 0.5 * x * (1.0 + jnp.tanh(0.7978845608028654 * (x + 0.044715 * (x * x * x))))


def _sigmoid(x):
    return 1.0 / (1.0 + jnp.exp(-x))


def _mix_in_body(x_ref, g1_ref, wgm_ref, wz_ref, lng_ref, lnb_ref, ws_ref, bs_ref, gm_ref, z_ref):
    x = x_ref[...]
    h = x * lax.rsqrt(jnp.mean(x * x, axis=-1, keepdims=True) + RMS_EPS) * g1_ref[...]
    hb = h.astype(BF16)
    z_ref[...] = _dot(hb, wz_ref[...])
    uv = _dot(hb, wgm_ref[...])
    u = _gelu_tanh(uv[:, :GM_WIDTH])
    v = _gelu_tanh(uv[:, GM_WIDTH:])
    mean = jnp.mean(v, axis=-1, keepdims=True)
    vc = v - mean
    var = jnp.mean(vc * vc, axis=-1, keepdims=True)
    vn = (vc * lax.rsqrt(var + LN_EPS) * lng_ref[...] + lnb_ref[...]).astype(BF16)
    tm = x.shape[0]
    for c in range(tm // GM_CHUNK):
        rows = slice(c * GM_CHUNK, (c + 1) * GM_CHUNK)
        for hd in range(GM_HEADS):
            cols = slice(hd * GM_HEAD_DIM, (hd + 1) * GM_HEAD_DIM)
            mixed = _dot(ws_ref[hd], vn[rows, cols]) + bs_ref[hd]
            gm_ref[rows, cols] = (u[rows, cols] * mixed).astype(BF16)


def _mix_in(x2d, p, tm):
    n = x2d.shape[0]
    full = lambda shape: pl.BlockSpec(shape, lambda i: (0,) * len(shape))
    return pl.pallas_call(
        _mix_in_body,
        grid=(n // tm,),
        in_specs=[
            pl.BlockSpec((tm, D_MODEL), lambda i: (i, 0)),
            full((1, D_MODEL)),
            full((D_MODEL, 2 * GM_WIDTH)),
            full((D_MODEL, Z_COLS)),
            full((1, GM_WIDTH)),
            full((1, GM_WIDTH)),
            full((GM_HEADS, GM_CHUNK, GM_CHUNK)),
            full((GM_HEADS, GM_CHUNK, GM_HEAD_DIM)),
        ],
        out_specs=[
            pl.BlockSpec((tm, GM_WIDTH), lambda i: (i, 0)),
            pl.BlockSpec((tm, Z_COLS), lambda i: (i, 0)),
        ],
        out_shape=[
            jax.ShapeDtypeStruct((n, GM_WIDTH), BF16),
            jax.ShapeDtypeStruct((n, Z_COLS), F32),
        ],
        compiler_params=pltpu.CompilerParams(vmem_limit_bytes=VMEM_LIMIT),
        name="mix_in",
    )(x2d, p["norm1_g"], p["w_gm"], p["w_z"], p["gm_ln_g"], p["gm_ln_b"], p["gm_ws"], p["gm_bs"])


def _rwkv_masks(reverse):
    c = RW_CHUNK
    ti = lax.broadcasted_iota(jnp.int32, (c, c), 0)
    tj = lax.broadcasted_iota(jnp.int32, (c, c), 1)
    incl = (tj >= ti) if reverse else (tj <= ti)
    lane =lax.broadcasted_iota(jnp.int32, (c, PAIR), 1)
    head0 = lane < RW_HEAD_DIM
    lane2 = lax.broadcasted_iota(jnp.int32, (c, 2 * c), 1)
    left = lane2 < c
    ti2 = lax.broadcasted_iota(jnp.int32, (c, 2 * c), 0)
    tj2 = jnp.where(left, lane2, lane2 - c)
    incl2 = (tj2 >= ti2) if reverse else (tj2 <= ti2)
    strict2 = (tj2 > ti2) if reverse else (tj2 < ti2)
    eye2 = jnp.where(tj2 == ti2, 1.0, 0.0).astype(F32)
    si = lax.broadcasted_iota(jnp.int32, (PAIR, PAIR), 0)
    sj = lax.broadcasted_iota(jnp.int32, (PAIR, PAIR), 1)
    same_head = (si // RW_HEAD_DIM) == (sj // RW_HEAD_DIM)
    return dict(
        incl2=incl2, strict2=strict2, head0=head0, left=left, eye2=eye2, same_head=same_head,
        tri=jnp.where(incl, 1.0, 0.0).astype(BF16),
    )


def _stack_heads(x, head0):
    return jnp.concatenate([jnp.where(head0, x, 0.0), jnp.where(head0, 0.0, x)], axis=0)


def _block_diag(x, left):
    return jnp.concatenate([jnp.where(left, x, 0.0), jnp.where(left, 0.0, x)], axis=0)


def _units_chunk(states, ops, m):
    c = RW_CHUNK
    units = range(len(ops))
    bf = lambda t: t.astype(BF16)
    head0, left = m["head0"], m["left"]
    ar = [bf(jnp.concatenate([o["a_t"], o["r_t"]], axis=0)) for o in ops]
    rstack = [bf(jnp.concatenate([_stack_heads(o["b_t"], head0), _stack_heads(o["k_t"], head0)], axis=0))
              for o in ops]
    acat = [_dot_nt(ar[i], rstack[i]) for i in units]
    sa = [_dot_nt(ar[i], bf(states[i])) for i in units]
    a_ab = [jnp.where(m["strict2"], acat[i][:c, :2 * c], 0.0) for i in units]
    a_ak = [bf(jnp.where(m["strict2"], acat[i][:c, 2 * c:], 0.0)) for i in units]
    a_r = [bf(jnp.concatenate([jnp.where(m["incl2"], acat[i][c:, :2 * c], 0.0),
                               jnp.where(m["incl2"], acat[i][c:, 2 * c:], 0.0)], axis=1)) for i in units]
    p = [m["eye2"] + a_ab[i] for i in units]
    lk = [_dot(bf(a_ab[i]), bf(_block_diag(a_ab[i], left))) for i in units]
    vstack = [bf(_stack_heads(o["v"], head0)) for o in ops]
    rhs = [sa[i][:c] + _dot(a_ak[i], vstack[i]) for i in units]
    power = 2
    while 2 * power < c:
        res = [_dot(bf(jnp.concatenate([p[i], lk[i]], axis=0)), bf(_block_diag(lk[i], left))) for i in units]
        p = [p[i] + res[i][:c] for i in units]
        lk = [res[i][c:] for i in units]
        power *= 2
    p = [p[i] + _dot(bf(p[i]), bf(_block_diag(lk[i], left))) for i in units]
    u = [_dot(bf(p[i]), bf(_stack_heads(rhs[i], head0))) for i in units]
    ys = [sa[i][c:] + _dot(a_r[i], jnp.concatenate([bf(_stack_heads(u[i], head0)), vstack[i]], axis=0))
          for i in units]
    upd = [_dot(bf(jnp.concatenate([u[i], ops[i]["v"]], axis=0).T),
                bf(jnp.concatenate([ops[i]["b_h"], ops[i]["k_h"]], axis=0))) for i in units]
    new_states = [states[i] * ops[i]["w_tot"] + jnp.where(m["same_head"], upd[i], 0.0) for i in units]
    return new_states, ys


def _rwkv_body(reverse, n_chunks, bt, *refs):
    if reverse:
        (zc_ref, zp_ref, zn_ref, yf_ref, mu_ref, lw_hi_ref, lw_lo_ref, w0_ref, a0_ref, a0o_ref, kk_ref, ka_ref,
         rk_ref, bd_ref, gng_ref, gnb_ref, out_ref, state_ref) = refs
    else:
        (zc_ref, zp_ref, zn_ref, mu_ref, lw_hi_ref, lw_lo_ref, w0_ref, a0_ref, kk_ref, ka_ref,
         bd_ref, out_ref, state_ref) = refs
    c = RW_CHUNK
    ci = pl.program_id(1)
    cidx = (n_chunks - 1 - ci) if reverse else ci

    @pl.when(ci == 0)
    def _():
        state_ref[...] = jnp.zeros_like(state_ref)

    rows = bt * c
    m = _rwkv_masks(reverse)
    row = lax.broadcasted_iota(jnp.int32, (c, Z_COLS), 0)
    lora_lane = lax.broadcasted_iota(jnp.int32, (rows, LORA_PAD), 1)
    bd = bd_ref[...]
    zs = []
    for b in range(bt):
        z = zc_ref[b]
        prev_row = jnp.where(cidx > 0, zp_ref[b, SUBLANES - 1:SUBLANES, :], 0.0)
        next_row = jnp.where(cidx < n_chunks - 1, zn_ref[b, 0:1, :], 0.0)
        prev = jnp.where(row == 0, prev_row, pltpu.roll(z, 1, axis=0))
        nxt = jnp.where(row == c - 1, next_row, pltpu.roll(z, c - 1, axis=0))
        zs.append(z + (0.5 * (prev + nxt) - z) * mu_ref[...])
    zf = jnp.concatenate(zs, axis=0)
    r = zf[:, :RW_WIDTH]
    k = zf[:, RW_WIDTH:2 * RW_WIDTH]
    v = zf[:, 2 * RW_WIDTH:3 * RW_WIDTH]
    lo = zf[:, 3 * RW_WIDTH:]
    act = jnp.where(lora_lane < DECAY_LORA, jnp.tanh(lo),
                    jnp.where(lora_lane < DECAY_LORA + ICLR_LORA, lo, _sigmoid(lo)))
    proj = _dot3(act, lw_hi_ref[...], lw_lo_ref[...])
    w_in = w0_ref[...] + proj[:, :RW_WIDTH]
    w_log = -(jnp.maximum(-w_in, 0.0) + jnp.log(1.0 + jnp.exp(-jnp.abs(w_in)))) - 0.5
    logw = -jnp.exp(w_log)
    a_sig = _sigmoid(a0_ref[...] + proj[:, RW_WIDTH:2 * RW_WIDTH])
    kk = k * kk_ref[...]
    kk_norm = jnp.sqrt(_dot2(kk * kk, bd))
    kk = kk / jnp.maximum(kk_norm, 1e-12)
    kd = k * (1.0 + (a_sig - 1.0) * ka_ref[...])
    a_vec = -kk
    b_vec = kk * a_sig
    ri = lax.broadcasted_iota(jnp.int32, (rows, rows), 0)
    rj = lax.broadcasted_iota(jnp.int32, (rows, rows), 1)
    ordered = (rj >= ri) if reverse else (rj <= ri)
    tri = jnp.where((ri // c) == (rj // c), jnp.where(ordered, 1.0, 0.0), 0.0).astype(BF16)
    cum = _dot2_lhs01(tri, logw)
    edge = 0 if reverse else c - 1
    tot = jnp.concatenate([jnp.broadcast_to(cum[b * c + edge:b * c + edge + 1, :], (c, RW_WIDTH))
                           for b in range(bt)], axis=0)
    w_inv = jnp.exp(-cum)
    w_end = jnp.exp(tot - cum)
    w_tot = jnp.exp(tot)
    full_ops = dict(r_t=r * jnp.exp(cum), a_t=a_vec * jnp.exp(cum - logw), b_t=b_vec * w_inv, k_t=kd * w_inv,
                    b_h=b_vec * w_end, k_h=kd * w_end, v=v)
    ops, states = [], []
    for b in range(bt):
        for pi in range(N_PAIRS):
            ln = slice(pi * PAIR, (pi + 1) * PAIR)
            unit = {name: t[b * c:(b + 1) * c, ln] for name, t in full_ops.items()}
            unit["w_tot"] = w_tot[b * c:b * c + 1, ln]
            ops.append(unit)
            states.append(state_ref[b, pi])
    new_states, ys = _units_chunk(states, ops, m)
    for b in range(bt):
        for pi in range(N_PAIRS):
            state_ref[b, pi] = new_states[b * N_PAIRS + pi]
    y = jnp.concatenate([jnp.concatenate(ys[b * N_PAIRS:(b + 1) * N_PAIRS], axis=1) for b in range(bt)], axis=0)
    if not reverse:
        out_ref[...] = y.reshape(bt, c, RW_WIDTH)
    else:
        y = y + yf_ref[...].reshape(rows, RW_WIDTH)
        a_other = _sigmoid(a0o_ref[...] + proj[:, 2 * RW_WIDTH:3 * RW_WIDTH])
        kd_sum = kd + k * (1.0 + (a_other - 1.0) * ka_ref[...])
        bonus = _dot2(r * kd_sum * rk_ref[...], bd) * v
        inv_n = 1.0 / RW_HEAD_DIM
        mean = _dot2(y, bd) * inv_n
        yc = y - mean
        var = _dot2(yc * yc, bd) * inv_n
        yn = yc * lax.rsqrt(var + GN_EPS) * gng_ref[...] + gnb_ref[...]
        g = proj[:, 3 * RW_WIDTH:]
        out_ref[...] = ((yn + bonus) * g).astype(BF16).reshape(bt, c, RW_WIDTH)


def _rwkv_pass(z3, yf, p, reverse, bt):
    b_sz, t_len, _ = z3.shape
    c = RW_CHUNK
    n_chunks = t_len // c
    halo_per_chunk = c // SUBLANES
    n_halo = t_len // SUBLANES

    def cidx(ci):
        return (n_chunks - 1 - ci) if reverse else ci

    full = lambda shape: pl.BlockSpec(shape, lambda bi, ci: (0,) * len(shape))
    row512 = full((1, RW_WIDTH))
    d = "rev" if reverse else "fwd"
    n_proj = p["lora_hi_" + d].shape[1]
    in_specs = [
        pl.BlockSpec((bt, c, Z_COLS), lambda bi, ci: (bi, cidx(ci), 0)),
        pl.BlockSpec((bt, SUBLANES, Z_COLS), lambda bi, ci: (bi, jnp.maximum(cidx(ci) * halo_per_chunk - 1, 0), 0)),
        pl.BlockSpec((bt, SUBLANES, Z_COLS),
                     lambda bi, ci: (bi, jnp.minimum((cidx(ci) + 1) * halo_per_chunk, n_halo - 1), 0)),
    ]
    args = [z3, z3, z3]
    if reverse:
        in_specs.append(pl.BlockSpec((bt, c, RW_WIDTH), lambda bi, ci: (bi, cidx(ci), 0)))
        args.append(yf)
    in_specs += [full((1, Z_COLS)), full((LORA_PAD, n_proj)), full((LORA_PAD, n_proj)), row512, row512]
    args += [p["rw_mu"], p["lora_hi_" + d], p["lora_lo_" + d], p["w0_" + d], p["a0_" + d]]
    if reverse:
        in_specs.append(row512)
        args.append(p["a0_fwd"])
    in_specs += [row512, row512]
    args += [p["rw_k_k"], p["rw_k_a"]]
    if reverse:
        in_specs.append(row512)
        args.append(p["rw_r_k"])
    in_specs.append(full((RW_WIDTH, RW_WIDTH)))
    args.append(p["head_ones"])
    if reverse:
        in_specs += [row512, row512]
        args += [p["rw_gn_g"], p["rw_gn_b"]]
    return pl.pallas_call(
        functools.partial(_rwkv_body, reverse, n_chunks, bt),
        grid=(b_sz // bt, n_chunks),
        in_specs=in_specs,
        out_specs=pl.BlockSpec((bt, c, RW_WIDTH), lambda bi, ci: (bi, cidx(ci), 0)),
        out_shape=jax.ShapeDtypeStruct((b_sz, t_len, RW_WIDTH), BF16 if reverse else F32),
        scratch_shapes=[pltpu.VMEM((bt, N_PAIRS, PAIR, PAIR), F32)],
        compiler_params=pltpu.CompilerParams(vmem_limit_bytes=VMEM_LIMIT),
        name="rwkv_" + d,
    )(*args)


def _mix_out_body(x_ref, gm_ref, rw_ref, wo_ref, g2_ref, rw_hi_ref, rw_lo_ref, rb_ref,
                  x1_ref, h2_ref, route_ref, counts_ref, carry_ref):
    i = pl.program_id(0)

    @pl.when(i == 0)
    def _():
        carry_ref[...] = jnp.zeros_like(carry_ref)

    x1 = x_ref[...] + _dot(gm_ref[...], wo_ref[:GM_WIDTH, :]) + _dot(rw_ref[...], wo_ref[GM_WIDTH:, :])
    x1_ref[...] = x1
    h2 = x1 * lax.rsqrt(jnp.mean(x1 * x1, axis=-1, keepdims=True) + RMS_EPS) * g2_ref[...]
    h2_ref[...] = h2
    tm = x1.shape[0]
    logits = _dot3(h2, rw_hi_ref[...], rw_lo_ref[...]) + rb_ref[...]
    lane = lax.broadcasted_iota(jnp.int32, (tm, LANES), 1).astype(F32)
    work = logits
    vals, ids = [], []
    onehot = jnp.zeros((tm, LANES), F32)
    for _ in range(TOP_K):
        mx = jnp.max(work, axis=-1, keepdims=True)
        idx = jnp.min(jnp.where(work == mx, lane, float(LANES)), axis=-1, keepdims=True)
        hit = lane == idx
        vals.append(mx)
        ids.append(idx)
        onehot = jnp.where(hit, 1.0, onehot)
        work = jnp.where(hit, -jnp.inf, work)
    exps = [jnp.exp(vk - vals[0]) for vk in vals]
    denom = exps[0] + exps[1] + exps[2] + exps[3]
    ti = lax.broadcasted_iota(jnp.int32, (tm, tm), 0)
    tj = lax.broadcasted_iota(jnp.int32, (tm, tm), 1)
    before = jnp.where(tj < ti, 1.0, 0.0).astype(BF16)
    ranks = carry_ref[0:1, :] + _dot(before, onehot.astype(BF16))
    slab = jnp.zeros((tm, LANES), F32)
    for kk in range(TOP_K):
        slab = jnp.where(lane == kk, ids[kk], slab)
        slab = jnp.where(lane == TOP_K + kk, exps[kk] / denom, slab)
        rank_k = jnp.sum(jnp.where(lane == ids[kk], ranks, 0.0), axis=-1, keepdims=True)
        slab = jnp.where(lane == 2 * TOP_K + kk, rank_k, slab)
    route_ref[...] = slab
    new_counts = carry_ref[0:1, :] + jnp.sum(onehot, axis=0, keepdims=True)
    carry_ref[...] = jnp.broadcast_to(new_counts, carry_ref.shape)
    counts_ref[...] = jnp.broadcast_to(new_counts, counts_ref.shape)


def _mix_out(x2d, gm, rw, p, tm):
    n = x2d.shape[0]
    full = lambda shape: pl.BlockSpec(shape, lambda i: (0,) * len(shape))
    tile = lambda w: pl.BlockSpec((tm, w), lambda i: (i, 0))
    return pl.pallas_call(
        _mix_out_body,
        grid=(n // tm,),
        in_specs=[tile(D_MODEL), tile(GM_WIDTH), tile(RW_WIDTH), full((D_MODEL, D_MODEL)), full((1, D_MODEL)),
                  full((D_MODEL, LANES)), full((D_MODEL, LANES)), full((1, LANES))],
        out_specs=[tile(D_MODEL), tile(D_MODEL), tile(LANES), full((SUBLANES, LANES))],
        out_shape=[
            jax.ShapeDtypeStruct((n, D_MODEL), F32),
            jax.ShapeDtypeStruct((n, D_MODEL), F32),
            jax.ShapeDtypeStruct((n, LANES), F32),
            jax.ShapeDtypeStruct((SUBLANES, LANES), F32),
        ],
        scratch_shapes=[pltpu.VMEM((SUBLANES, LANES), F32)],
        compiler_params=pltpu.CompilerParams(vmem_limit_bytes=VMEM_LIMIT),
        name="mix_out",
    )(x2d, gm, rw, p["w_out"], p["norm2_g"], p["router_hi"], p["router_lo"], p["router_b"])


def _sc_gather_rows(table, idx):
    n_idx = idx.shape[0]
    width = table.shape[1]
    n_workers = SC_CORES * SC_SUBCORES
    per_worker = n_idx // n_workers
    n_windows = per_worker // SC_WINDOW
    assert per_worker * n_workers == n_idx and n_windows * SC_WINDOW == per_worker
    mesh = plsc.VectorSubcoreMesh(core_axis_name="c", subcore_axis_name="s")

    @functools.partial(
        pl.kernel, mesh=mesh,
        out_type=jax.ShapeDtypeStruct((n_idx, width), table.dtype),
        scratch_types=[
            pltpu.VMEM((SC_WINDOW,), jnp.int32),
            pltpu.VMEM((SC_WINDOW, width), table.dtype),
            pltpu.SemaphoreType.DMA,
        ],
        name="sc_gather_rows",
    )
    def gather(table_hbm, idx_hbm, out_hbm, idx_v, rows_v, sem):
        worker = lax.axis_index("s") * SC_CORES + lax.axis_index("c")
        base = worker * per_worker

        @pl.loop(0, n_windows)
        def _(j):
            off = base + j * SC_WINDOW
            pltpu.sync_copy(idx_hbm.at[pl.ds(off, SC_WINDOW)], idx_v)
            pltpu.async_copy(table_hbm.at[idx_v], rows_v, sem).wait()
            pltpu.sync_copy(rows_v, out_hbm.at[pl.ds(off, SC_WINDOW)])

    return gather(table, idx)


def _experts_body(blk_e_ref, n_used_ref, x_ref, wgu_ref, bgu_ref, wd_ref, bd_ref, out_ref):
    del blk_e_ref
    i = pl.program_id(0)
    n_used = n_used_ref[0]

    @pl.when(i < n_used)
    def _():
        xb = x_ref[...].astype(BF16)
        gu = _dot(xb, wgu_ref[0]) + bgu_ref[0]
        gate = jnp.minimum(gu[:, :D_EXPERT], SWIGLU_LIMIT)
        up = jnp.clip(gu[:, D_EXPERT:], -SWIGLU_LIMIT, SWIGLU_LIMIT)
        act = gate * _sigmoid(gate * SWIGLU_ALPHA) * (up + 1.0)
        out_ref[...] = _dot(act.astype(BF16), wd_ref[0]) + bd_ref[0]

    @pl.when(i >= n_used)
    def _():
        out_ref[...] = jnp.zeros_like(out_ref)


def _experts(x_rows, blk_e, n_used, p):
    n_blocks = blk_e.shape[0]
    grid_spec = pltpu.PrefetchScalarGridSpec(
        num_scalar_prefetch=2,
        grid=(n_blocks,),
        in_specs=[
            pl.BlockSpec((MOE_ROWS, D_MODEL), lambda i, be, nu: (jnp.minimum(i, nu[0] - 1), 0)),
            pl.BlockSpec((1, D_MODEL, 2 * D_EXPERT), lambda i, be, nu: (be[i], 0, 0)),
            pl.BlockSpec((1, 1, 2 * D_EXPERT), lambda i, be, nu: (be[i], 0, 0)),
            pl.BlockSpec((1, D_EXPERT, D_MODEL), lambda i, be, nu: (be[i], 0, 0)),
            pl.BlockSpec((1, 1, D_MODEL), lambda i, be, nu: (be[i], 0, 0)),
        ],
        out_specs=pl.BlockSpec((MOE_ROWS, D_MODEL), lambda i, be, nu: (i, 0)),
    )
    return pl.pallas_call(
        _experts_body,
        grid_spec=grid_spec,
        out_shape=jax.ShapeDtypeStruct((n_blocks * MOE_ROWS, D_MODEL), F32),
        compiler_params=pltpu.CompilerParams(vmem_limit_bytes=VMEM_LIMIT),
        name="experts",
    )(blk_e, n_used, x_rows, p["w_gu"], p["b_gu"], p["w_down"], p["b_down"])


def _combine_body(yg_ref, x1_ref, gates_ref, fg_ref, out_ref):
    acc = x1_ref[...]
    gates = gates_ref[...]
    for kk in range(TOP_K):
        acc = acc + gates[:, TOP_K + kk:TOP_K + kk + 1] * yg_ref[kk]
    out_ref[...] = acc * lax.rsqrt(jnp.mean(acc * acc, axis=-1, keepdims=True) + RMS_EPS) * fg_ref[...]


def _combine(yg, x1, gates, final_g, tm):
    n = x1.shape[0]
    return pl.pallas_call(
        _combine_body,
        grid=(n // tm,),
        in_specs=[
            pl.BlockSpec((TOP_K, tm, D_MODEL), lambda i: (0, i, 0)),
            pl.BlockSpec((tm, D_MODEL), lambda i: (i, 0)),
            pl.BlockSpec((tm, LANES), lambda i: (i, 0)),
            pl.BlockSpec((1, D_MODEL), lambda i: (0, 0)),
        ],
        out_specs=pl.BlockSpec((tm, D_MODEL), lambda i: (i, 0)),
        out_shape=jax.ShapeDtypeStruct((n, D_MODEL), F32),
        compiler_params=pltpu.CompilerParams(vmem_limit_bytes=VMEM_LIMIT),
        name="combine",
    )(yg, x1, gates, final_g)


def _prepare(norm1_g, w_in, rw_mu, gm_ln_g, gm_ln_b, gm_ws, gm_bs, rw_w0, rw_w2, rw_a0, rw_a2, rw_g2,
             rw_k_k, rw_k_a, rw_r_k, rw_gn_g, rw_gn_b, w_out, norm2_g, router_w, router_b,
             w_gu, b_gu, w_down, b_down, final_g):
    l = 0
    row = lambda t: t.reshape(1, -1).astype(F32)
    p = {}
    p["norm1_g"] = row(norm1_g[l])
    w = w_in[l]
    p["w_gm"] = w[:, :2 * GM_WIDTH].astype(BF16)
    pad_cols = LORA_PAD - LORA_COLS
    p["w_z"] = jnp.pad(w[:, 2 * GM_WIDTH:], ((0, 0), (0, pad_cols))).astype(BF16)
    p["rw_mu"] = jnp.pad(row(rw_mu[l]), ((0, 0), (0, pad_cols)))
    p["gm_ln_g"] = row(gm_ln_g[l])
    p["gm_ln_b"] = row(gm_ln_b[l])
    p["gm_ws"] = gm_ws[l].astype(BF16)
    p["gm_bs"] = jnp.broadcast_to(gm_bs[l][:, :, None], (GM_HEADS, GM_CHUNK, GM_HEAD_DIM)).astype(F32)
    zeros = lambda r, c: jnp.zeros((r, c), F32)
    o_a = DECAY_LORA
    o_g = DECAY_LORA + ICLR_LORA

    def lora_matrix(d, with_epilogue):
        blocks = [jnp.concatenate([rw_w2[l, d], zeros(LORA_PAD - DECAY_LORA, RW_WIDTH)], axis=0),
                  jnp.concatenate([zeros(o_a, RW_WIDTH), rw_a2[l, d], zeros(LORA_PAD - o_g, RW_WIDTH)], axis=0)]
        if with_epilogue:
            blocks.append(jnp.concatenate([zeros(o_a, RW_WIDTH), rw_a2[l, 1 - d], zeros(LORA_PAD - o_g, RW_WIDTH)],
                                          axis=0))
            blocks.append(jnp.concatenate([zeros(o_g, RW_WIDTH), rw_g2[l], zeros(LORA_PAD - LORA_COLS, RW_WIDTH)],
                                          axis=0))
        return jnp.concatenate(blocks, axis=1)

    for d, name in ((0, "fwd"), (1, "rev")):
        mat = lora_matrix(d, with_epilogue=(d == 1))
        hi = mat.astype(BF16)
        p["lora_hi_" + name] = hi
        p["lora_lo_" + name] = (mat - hi.astype(F32)).astype(BF16)
        p["w0_" + name] = row(rw_w0[l, d])
        p["a0_" + name] = row(rw_a0[l, d])
    p["rw_k_k"] = row(rw_k_k[l])
    p["rw_k_a"] = row(rw_k_a[l])
    p["rw_r_k"] = row(rw_r_k[l])
    p["rw_gn_g"] = row(rw_gn_g[l])
    p["rw_gn_b"] = row(rw_gn_b[l])
    ch = jnp.arange(RW_WIDTH) // RW_HEAD_DIM
    p["head_ones"] = (ch[:, None] == ch[None, :]).astype(BF16)
    p["w_out"] = w_out[l].astype(BF16)
    p["norm2_g"] = row(norm2_g[l])
    rw_pad = jnp.pad(router_w[l].astype(F32), ((0, 0), (0, LANES - N_EXPERTS)))
    hi = rw_pad.astype(BF16)
    p["router_hi"] = hi
    p["router_lo"] = (rw_pad - hi.astype(F32)).astype(BF16)
    p["router_b"] = jnp.pad(row(router_b[l]), ((0, 0), (0, LANES - N_EXPERTS)), constant_values=NEG_BIG)
    p["w_gu"] = w_gu[l].astype(BF16)
    p["b_gu"] = b_gu[l].reshape(N_EXPERTS, 1, 2 * D_EXPERT).astype(F32)
    p["w_down"] = w_down[l].astype(BF16)
    p["b_down"] = b_down[l].reshape(N_EXPERTS, 1, D_MODEL).astype(F32)
    p["final_g"] = row(final_g)
    return p


def _pick_tile(n, want):
    t = want
    while n % t:
        t //= 2
    return t


def _encoder(x, p):
    b_sz, t_len, d = x.shape
    n = b_sz * t_len
    x2d = x.reshape(n, d)
    tm = _pick_tile(n, 512)
    gm, z = _mix_in(x2d, p, tm)
    z3 = z.reshape(b_sz, t_len, Z_COLS)
    yf = _rwkv_pass(z3, None, p, reverse=False, bt=RW_BATCH_TILE)
    rw = _rwkv_pass(z3, yf, p, reverse=True, bt=RW_BATCH_TILE)
    tmo = _pick_tile(n, 256)
    x1, h2, route, counts = _mix_out(x2d, gm, rw.reshape(n, RW_WIDTH), p, tmo)
    ids = route[:, :TOP_K].astype(jnp.int32)
    ranks = route[:, 2 * TOP_K:3 * TOP_K].astype(jnp.int32)
    cnt = counts[0, :N_EXPERTS].astype(jnp.int32)
    padded = (cnt + MOE_ROWS - 1) // MOE_ROWS * MOE_ROWS
    pad_end = jnp.cumsum(padded)
    pad_start = pad_end - padded
    dest = pad_start[ids] + ranks
    n_blocks = n * TOP_K // MOE_ROWS + N_EXPERTS
    n_rows = n_blocks * MOE_ROWS
    tok = jnp.broadcast_to(jnp.arange(n, dtype=jnp.int32)[:, None], (n, TOP_K))
    tok_rows = jnp.zeros((n_rows,), jnp.int32).at[dest.reshape(-1)].set(tok.reshape(-1))
    blk_start = jnp.arange(n_blocks, dtype=jnp.int32) * MOE_ROWS
    blk_e = jnp.minimum(jnp.sum((pad_end[None, :] <= blk_start[:, None]).astype(jnp.int32), axis=1), N_EXPERTS - 1)
    n_used = (pad_end[-1] // MOE_ROWS).astype(jnp.int32).reshape(1)
    x_rows = _sc_gather_rows(h2, tok_rows)
    y_rows = _experts(x_rows, blk_e, n_used, p)
    yg = _sc_gather_rows(y_rows, dest.T.reshape(-1)).reshape(TOP_K, n, D_MODEL)
    out = _combine(yg, x1, route, p["final_g"], _pick_tile(n, 256))
    return out.reshape(b_sz, t_len, d)


def kernel(x_prompt, x_sample, norm1_g, w_in, rw_mu, gm_ln_g, gm_ln_b, gm_ws, gm_bs, rw_w0, rw_w2, rw_a0, rw_a2,
           rw_g2, rw_k_k, rw_k_a, rw_r_k, rw_gn_g, rw_gn_b, w_out, norm2_g, router_w, router_b, w_gu, b_gu,
           w_down, b_down, final_g):
    p = _prepare(norm1_g, w_in, rw_mu, gm_ln_g, gm_ln_b, gm_ws, gm_bs, rw_w0, rw_w2, rw_a0, rw_a2, rw_g2,
                 rw_k_k, rw_k_a, rw_r_k, rw_gn_g, rw_gn_b, w_out, norm2_g, router_w, router_b,
                 w_gu, b_gu, w_down, b_down, final_g)
    return (_encoder(x_prompt, p), _encoder(x_sample, p))
```

```python
import functools

import jax
import jax.numpy as jnp
from jax import lax
from jax.experimental import pallas as pl
from jax.experimental.pallas import tpu as pltpu
from jax.experimental.pallas import tpu_sc as plsc

F32 = jnp.float32
BF16 = jnp.bfloat16

D_MODEL = 1024
GM_WIDTH = 512
RW_WIDTH = 512
GM_HEADS = 4
GM_HEAD_DIM = 128
GM_CHUNK = 128
RW_HEAD_DIM = 64
DECAY_LORA = 32
ICLR_LORA = 32
GATE_LORA = 96
LORA_COLS = DECAY_LORA + ICLR_LORA + GATE_LORA
LORA_PAD = 256
Z_COLS = 3 * RW_WIDTH + LORA_PAD
N_EXPERTS = 32
TOP_K = 4
D_EXPERT = 1024
SWIGLU_LIMIT = 7.0
SWIGLU_ALPHA = 1.702
RMS_EPS = 1e-5
LN_EPS = 1e-5
GN_EPS = 64e-5

LANES = 128
SUBLANES = 8
VMEM_LIMIT = 56 * 1024 * 1024

RW_CHUNK = 64
PAIR = 2 * RW_HEAD_DIM
N_PAIRS = RW_WIDTH // PAIR
RW_BATCH_TILE = 4
MOE_ROWS = 512
SC_CORES = 2
SC_SUBCORES = 16
SC_WINDOW = 32
SC_SCATTER_WINDOW = 64
NEG_BIG = -1e30
DECAY_SCALE = 0.6065306597126334


def _dot(a, b):
    return jnp.dot(a, b, preferred_element_type=F32)


def _dot_nt(a, b):
    return lax.dot_general(a, b, (((1,), (1,)), ((), ())), preferred_element_type=F32)


def _split(x):
    hi = x.astype(BF16)
    lo = (x - hi.astype(F32)).astype(BF16)
    return hi, lo


def _dot3(a, b_hi, b_lo):
    a_hi, a_lo = _split(a)
    return _dot(a_hi, b_hi) + _dot(a_lo, b_hi) + _dot(a_hi, b_lo)


def _dot2(a, b01):
    a_hi, a_lo = _split(a)
    return _dot(a_hi, b01) + _dot(a_lo, b01)


def _head_sums(x, pair_ones):
    rows = x.shape[0]
    n_tiles = x.shape[1] // LANES
    stacked = jnp.concatenate([x[:, t * LANES:(t + 1) * LANES] for t in range(n_tiles)], axis=0)
    sums = _dot2(stacked, pair_ones)
    return jnp.concatenate([sums[t * rows:(t + 1) * rows] for t in range(n_tiles)], axis=1)


def _dot2_lhs01(a01, b):
    b_hi, b_lo = _split(b)
    return _dot(a01, b_hi) + _dot(a01, b_lo)


def _gelu_tanh(x):
    return 0.5 * x * (1.0 + jnp.tanh(0.7978845608028654 * (x + 0.044715 * (x * x * x))))


def _sigmoid(x):
    return 1.0 / (1.0 + jnp.exp(-x))


def _mix_in_body(x_ref, g1_ref, wgm_ref, wz_ref, lng_ref, lnb_ref, ws_ref, bs_ref, gm_ref, z_ref):
    x = x_ref[...]
    h = x * lax.rsqrt(jnp.mean(x * x, axis=-1, keepdims=True) + RMS_EPS) * g1_ref[...]
    hb = h.astype(BF16)
    z_ref[...] = _dot(hb, wz_ref[...])
    uv = _dot(hb, wgm_ref[...])
    u = _gelu_tanh(uv[:, :GM_WIDTH])
    v = _gelu_tanh(uv[:, GM_WIDTH:])
    mean = jnp.mean(v, axis=-1, keepdims=True)
    vc = v - mean
    var = jnp.mean(vc * vc, axis=-1, keepdims=True)
    vn = (vc * lax.rsqrt(var + LN_EPS) * lng_ref[...] + lnb_ref[...]).astype(BF16)
    tm = x.shape[0]
    for c in range(tm // GM_CHUNK):
        rows = slice(c * GM_CHUNK, (c + 1) * GM_CHUNK)
        for hd in range(GM_HEADS):
            cols = slice(hd * GM_HEAD_DIM, (hd + 1) * GM_HEAD_DIM)
            mixed = _dot(ws_ref[hd], vn[rows, cols]) + bs_ref[hd]
            gm_ref[rows, cols] = (u[rows, cols] * mixed).astype(BF16)


def _mix_in(x2d, p, tm):
    n = x2d.shape[0]
    full = lambda shape: pl.BlockSpec(shape, lambda i: (0,) * len(shape))
    return pl.pallas_call(
        _mix_in_body,
        grid=(n // tm,),
        in_specs=[
            pl.BlockSpec((tm, D_MODEL), lambda i: (i, 0)),
            full((1, D_MODEL)),
            full((D_MODEL, 2 * GM_WIDTH)),
            full((D_MODEL, Z_COLS)),
            full((1, GM_WIDTH)),
            full((1, GM_WIDTH)),
            full((GM_HEADS, GM_CHUNK, GM_CHUNK)),
            full((GM_HEADS, GM_CHUNK, GM_HEAD_DIM)),
        ],
        out_specs=[
            pl.BlockSpec((tm, GM_WIDTH), lambda i: (i, 0)),
            pl.BlockSpec((tm, Z_COLS), lambda i: (i, 0)),
        ],
        out_shape=[
            jax.ShapeDtypeStruct((n, GM_WIDTH), BF16),
            jax.ShapeDtypeStruct((n, Z_COLS), F32),
        ],
        compiler_params=pltpu.CompilerParams(vmem_limit_bytes=VMEM_LIMIT),
        name="mix_in",
    )(x2d, p["norm1_g"], p["w_gm"], p["w_z"], p["gm_ln_g"], p["gm_ln_b"], p["gm_ws"], p["gm_bs"])


def _rwkv_masks(reverse):
    c = RW_CHUNK
    ti = lax.broadcasted_iota(jnp.int32, (c, c), 0)
    tj = lax.broadcasted_iota(jnp.int32, (c, c), 1)
    incl = (tj >= ti) if reverse else (tj <= ti)
    lane =lax.broadcasted_iota(jnp.int32, (c, PAIR), 1)
    head0 = lane < RW_HEAD_DIM
    lane2 = lax.broadcasted_iota(jnp.int32, (c, 2 * c), 1)
    left = lane2 < c
    ti2 = lax.broadcasted_iota(jnp.int32, (c, 2 * c), 0)
    tj2 = jnp.where(left, lane2, lane2 - c)
    incl2 = (tj2 >= ti2) if reverse else (tj2 <= ti2)
    strict2 = (tj2 > ti2) if reverse else (tj2 < ti2)
    eye2 = jnp.where(tj2 == ti2, 1.0, 0.0).astype(F32)
    si = lax.broadcasted_iota(jnp.int32, (PAIR, PAIR), 0)
    sj = lax.broadcasted_iota(jnp.int32, (PAIR, PAIR), 1)
    same_head = (si // RW_HEAD_DIM) == (sj // RW_HEAD_DIM)
    return dict(
        incl2=incl2, strict2=strict2, head0=head0, left=left, eye2=eye2, same_head=same_head,
        tri=jnp.where(incl, 1.0, 0.0).astype(BF16),
    )


def _stack_heads(x, head0):
    return jnp.concatenate([jnp.where(head0, x, 0.0), jnp.where(head0, 0.0, x)], axis=0)


def _block_diag(x, left):
    return jnp.concatenate([jnp.where(left, x, 0.0), jnp.where(left, 0.0, x)], axis=0)


def _units_chunk(states, ops, m):
    c = RW_CHUNK
    units = range(len(ops))
    bf = lambda t: t.astype(BF16)
    head0, left = m["head0"], m["left"]
    ar = [bf(jnp.concatenate([o["a_t"], o["r_t"]], axis=0)) for o in ops]
    rstack = [bf(jnp.concatenate([_stack_heads(o["b_t"], head0), _stack_heads(o["k_t"], head0)], axis=0))
              for o in ops]
    acat = [_dot_nt(ar[i], rstack[i]) for i in units]
    sa = [_dot_nt(ar[i], bf(states[i])) for i in units]
    a_ab = [jnp.where(m["strict2"], acat[i][:c, :2 * c], 0.0) for i in units]
    a_ak = [bf(jnp.where(m["strict2"], acat[i][:c, 2 * c:], 0.0)) for i in units]
    a_r = [bf(jnp.concatenate([jnp.where(m["incl2"], acat[i][c:, :2 * c], 0.0),
                               jnp.where(m["incl2"], acat[i][c:, 2 * c:], 0.0)], axis=1)) for i in units]
    p = [m["eye2"] + a_ab[i] for i in units]
    lk = [_dot(bf(a_ab[i]), bf(_block_diag(a_ab[i], left))) for i in units]
    vstack = [bf(_stack_heads(o["v"], head0)) for o in ops]
    rhs = [sa[i][:c] + _dot(a_ak[i], vstack[i]) for i in units]
    power = 2
    while 2 * power < c:
        res = [_dot(bf(jnp.concatenate([p[i], lk[i]], axis=0)), bf(_block_diag(lk[i], left))) for i in units]
        p = [p[i] + res[i][:c] for i in units]
        lk = [res[i][c:] for i in units]
        power *= 2
    p = [p[i] + _dot(bf(p[i]), bf(_block_diag(lk[i], left))) for i in units]
    u = [_dot(bf(p[i]), bf(_stack_heads(rhs[i], head0))) for i in units]
    ys = [sa[i][c:] + _dot(a_r[i], jnp.concatenate([bf(_stack_heads(u[i], head0)), vstack[i]], axis=0))
          for i in units]
    upd = [_dot(bf(jnp.concatenate([u[i], ops[i]["v"]], axis=0).T),
                bf(jnp.concatenate([ops[i]["b_h"], ops[i]["k_h"]], axis=0))) for i in units]
    new_states = [states[i] * ops[i]["w_tot"] + jnp.where(m["same_head"], upd[i], 0.0) for i in units]
    return new_states, ys


def _rwkv_body(reverse, n_chunks, bt, *refs):
    if reverse:
        (zc_ref, zp_ref, zn_ref, yf_ref, mu_ref, lw_hi_ref, lw_lo_ref, w0_ref, a0_ref, a0o_ref, kk_ref, ka_ref,
         rk_ref, bd_ref, gng_ref, gnb_ref, out_ref, state_ref) = refs
    else:
        (zc_ref, zp_ref, zn_ref, mu_ref, lw_hi_ref, lw_lo_ref, w0_ref, a0_ref, kk_ref, ka_ref,
         bd_ref, out_ref, state_ref) = refs
    c = RW_CHUNK
    ci = pl.program_id(1)
    cidx = (n_chunks - 1 - ci) if reverse else ci

    @pl.when(ci == 0)
    def _():
        state_ref[...] = jnp.zeros_like(state_ref)

    rows = bt * c
    m = _rwkv_masks(reverse)
    row = lax.broadcasted_iota(jnp.int32, (c, Z_COLS), 0)
    lora_lane = lax.broadcasted_iota(jnp.int32, (rows, LORA_PAD), 1)
    bd = bd_ref[...]
    zs = []
    for b in range(bt):
        z = zc_ref[b]
        prev_row = jnp.where(cidx > 0, zp_ref[b, SUBLANES - 1:SUBLANES, :], 0.0)
        next_row = jnp.where(cidx < n_chunks - 1, zn_ref[b, 0:1, :], 0.0)
        prev = jnp.where(row == 0, prev_row, pltpu.roll(z, 1, axis=0))
        nxt = jnp.where(row == c - 1, next_row, pltpu.roll(z, c - 1, axis=0))
        zs.append(z + (0.5 * (prev + nxt) - z) * mu_ref[...])
    zf = jnp.concatenate(zs, axis=0)
    r = zf[:, :RW_WIDTH]
    k = zf[:, RW_WIDTH:2 * RW_WIDTH]
    v = zf[:, 2 * RW_WIDTH:3 * RW_WIDTH]
    lo = zf[:, 3 * RW_WIDTH:]
    act = jnp.where(lora_lane < DECAY_LORA, jnp.tanh(lo),
                    jnp.where(lora_lane < DECAY_LORA + ICLR_LORA, lo, _sigmoid(lo)))
    proj = _dot3(act, lw_hi_ref[...], lw_lo_ref[...])
    w_in = w0_ref[...] + proj[:, :RW_WIDTH]
    logw = -DECAY_SCALE * _sigmoid(w_in)
    a_sig = _sigmoid(a0_ref[...] + proj[:, RW_WIDTH:2 * RW_WIDTH])
    kk = k * kk_ref[...]
    kk_norm = jnp.sqrt(_head_sums(kk * kk, bd))
    kk = kk / jnp.maximum(kk_norm, 1e-12)
    kd = k * (1.0 + (a_sig - 1.0) * ka_ref[...])
    a_vec = -kk
    b_vec = kk * a_sig
    ri = lax.broadcasted_iota(jnp.int32, (rows, rows), 0)
    rj = lax.broadcasted_iota(jnp.int32, (rows, rows), 1)
    ordered = (rj >= ri) if reverse else (rj <= ri)
    tri = jnp.where((ri // c) == (rj // c), jnp.where(ordered, 1.0, 0.0), 0.0).astype(BF16)
    cum = _dot2_lhs01(tri, logw)
    edge = 0 if reverse else c - 1
    tot = jnp.concatenate([jnp.broadcast_to(cum[b * c + edge:b * c + edge + 1, :], (c, RW_WIDTH))
                           for b in range(bt)], axis=0)
    w_inv = jnp.exp(-cum)
    w_end = jnp.exp(tot - cum)
    w_tot = jnp.exp(tot)
    full_ops = dict(r_t=r * jnp.exp(cum), a_t=a_vec * jnp.exp(cum - logw), b_t=b_vec * w_inv, k_t=kd * w_inv,
                    b_h=b_vec * w_end, k_h=kd * w_end, v=v)
    ops, states = [], []
    for b in range(bt):
        for pi in range(N_PAIRS):
            ln = slice(pi * PAIR, (pi + 1) * PAIR)
            unit = {name: t[b * c:(b + 1) * c, ln] for name, t in full_ops.items()}
            unit["w_tot"] = w_tot[b * c:b * c + 1, ln]
            ops.append(unit)
            states.append(state_ref[b, pi])
    new_states, ys = _units_chunk(states, ops, m)
    for b in range(bt):
        for pi in range(N_PAIRS):
            state_ref[b, pi] = new_states[b * N_PAIRS + pi]
    y = jnp.concatenate([jnp.concatenate(ys[b * N_PAIRS:(b + 1) * N_PAIRS], axis=1) for b in range(bt)], axis=0)
    if not reverse:
        out_ref[...] = y.reshape(bt, c, RW_WIDTH)
    else:
        y = y + yf_ref[...].reshape(rows, RW_WIDTH)
        a_other = _sigmoid(a0o_ref[...] + proj[:, 2 * RW_WIDTH:3 * RW_WIDTH])
        kd_sum = kd + k * (1.0 + (a_other - 1.0) * ka_ref[...])
        bonus = _head_sums(r * kd_sum * rk_ref[...], bd) * v
        inv_n = 1.0 / RW_HEAD_DIM
        mean = _head_sums(y, bd) * inv_n
        yc = y - mean
        var = _head_sums(yc * yc, bd) * inv_n
        yn = yc * lax.rsqrt(var + GN_EPS) * gng_ref[...] + gnb_ref[...]
        g = proj[:, 3 * RW_WIDTH:]
        out_ref[...] = ((yn + bonus) * g).astype(BF16).reshape(bt, c, RW_WIDTH)


def _rwkv_pass(z3, yf, p, reverse, bt):
    b_sz, t_len, _ = z3.shape
    c = RW_CHUNK
    n_chunks = t_len // c
    halo_per_chunk = c // SUBLANES
    n_halo = t_len // SUBLANES

    def cidx(ci):
        return (n_chunks - 1 - ci) if reverse else ci

    full = lambda shape: pl.BlockSpec(shape, lambda bi, ci: (0,) * len(shape))
    row512 = full((1, RW_WIDTH))
    d = "rev" if reverse else "fwd"
    n_proj = p["lora_hi_" + d].shape[1]
    in_specs = [
        pl.BlockSpec((bt, c, Z_COLS), lambda bi, ci: (bi, cidx(ci), 0)),
        pl.BlockSpec((bt, SUBLANES, Z_COLS), lambda bi, ci: (bi, jnp.maximum(cidx(ci) * halo_per_chunk - 1, 0), 0)),
        pl.BlockSpec((bt, SUBLANES, Z_COLS),
                     lambda bi, ci: (bi, jnp.minimum((cidx(ci) + 1) * halo_per_chunk, n_halo - 1), 0)),
    ]
    args = [z3, z3, z3]
    if reverse:
        in_specs.append(pl.BlockSpec((bt, c, RW_WIDTH), lambda bi, ci: (bi, cidx(ci), 0)))
        args.append(yf)
    in_specs += [full((1, Z_COLS)), full((LORA_PAD, n_proj)), full((LORA_PAD, n_proj)), row512, row512]
    args += [p["rw_mu"], p["lora_hi_" + d], p["lora_lo_" + d], p["w0_" + d], p["a0_" + d]]
    if reverse:
        in_specs.append(row512)
        args.append(p["a0_fwd"])
    in_specs += [row512, row512]
    args += [p["rw_k_k"], p["rw_k_a"]]
    if reverse:
        in_specs.append(row512)
        args.append(p["rw_r_k"])
    in_specs.append(full((PAIR, PAIR)))
    args.append(p["head_ones"])
    if reverse:
        in_specs += [row512, row512]
        args += [p["rw_gn_g"], p["rw_gn_b"]]
    return pl.pallas_call(
        functools.partial(_rwkv_body, reverse, n_chunks, bt),
        grid=(b_sz // bt, n_chunks),
        in_specs=in_specs,
        out_specs=pl.BlockSpec((bt, c, RW_WIDTH), lambda bi, ci: (bi, cidx(ci), 0)),
        out_shape=jax.ShapeDtypeStruct((b_sz, t_len, RW_WIDTH), BF16 if reverse else F32),
        scratch_shapes=[pltpu.VMEM((bt, N_PAIRS, PAIR, PAIR), F32)],
        compiler_params=pltpu.CompilerParams(vmem_limit_bytes=VMEM_LIMIT),
        name="rwkv_" + d,
    )(*args)


def _mix_out_body(x_ref, gm_ref, rw_ref, wo_ref, g2_ref, rw_hi_ref, rw_lo_ref, rb_ref,
                  x1_ref, h2_ref, route_ref, counts_ref, carry_ref):
    i = pl.program_id(0)

    @pl.when(i == 0)
    def _():
        carry_ref[...] = jnp.zeros_like(carry_ref)

    x1 = x_ref[...] + _dot(gm_ref[...], wo_ref[:GM_WIDTH, :]) + _dot(rw_ref[...], wo_ref[GM_WIDTH:, :])
    x1_ref[...] = x1
    h2 = x1 * lax.rsqrt(jnp.mean(x1 * x1, axis=-1, keepdims=True) + RMS_EPS) * g2_ref[...]
    h2_ref[...] = h2
    tm = x1.shape[0]
    logits = _dot3(h2, rw_hi_ref[...], rw_lo_ref[...]) + rb_ref[...]
    lane = lax.broadcasted_iota(jnp.int32, (tm, LANES), 1).astype(F32)
    work = logits
    vals, ids = [], []
    onehot = jnp.zeros((tm, LANES), F32)
    for _ in range(TOP_K):
        mx = jnp.max(work, axis=-1, keepdims=True)
        idx = jnp.min(jnp.where(work == mx, lane, float(LANES)), axis=-1, keepdims=True)
        hit = lane == idx
        vals.append(mx)
        ids.append(idx)
        onehot = jnp.where(hit, 1.0, onehot)
        work = jnp.where(hit, -jnp.inf, work)
    exps = [jnp.exp(vk - vals[0]) for vk in vals]
    denom = exps[0] + exps[1] + exps[2] + exps[3]
    ti = lax.broadcasted_iota(jnp.int32, (tm, tm), 0)
    tj = lax.broadcasted_iota(jnp.int32, (tm, tm), 1)
    before = jnp.where(tj < ti, 1.0, 0.0).astype(BF16)
    ranks = carry_ref[0:1, :] + _dot(before, onehot.astype(BF16))
    slab = jnp.zeros((tm, LANES), F32)
    for kk in range(TOP_K):
        slab = jnp.where(lane == kk, ids[kk], slab)
        slab = jnp.where(lane == TOP_K + kk, exps[kk] / denom, slab)
        rank_k = jnp.sum(jnp.where(lane == ids[kk], ranks, 0.0), axis=-1, keepdims=True)
        slab = jnp.where(lane == 2 * TOP_K + kk, rank_k, slab)
    route_ref[...] = slab
    new_counts = carry_ref[0:1, :] + jnp.sum(onehot, axis=0, keepdims=True)
    carry_ref[...] = jnp.broadcast_to(new_counts, carry_ref.shape)
    counts_ref[...] = jnp.broadcast_to(new_counts, counts_ref.shape)


def _mix_out(x2d, gm, rw, p, tm):
    n = x2d.shape[0]
    full = lambda shape: pl.BlockSpec(shape, lambda i: (0,) * len(shape))
    tile = lambda w: pl.BlockSpec((tm, w), lambda i: (i, 0))
    return pl.pallas_call(
        _mix_out_body,
        grid=(n // tm,),
        in_specs=[tile(D_MODEL), tile(GM_WIDTH), tile(RW_WIDTH), full((D_MODEL, D_MODEL)), full((1, D_MODEL)),
                  full((D_MODEL, LANES)), full((D_MODEL, LANES)), full((1, LANES))],
        out_specs=[tile(D_MODEL), tile(D_MODEL), tile(LANES), full((SUBLANES, LANES))],
        out_shape=[
            jax.ShapeDtypeStruct((n, D_MODEL), F32),
            jax.ShapeDtypeStruct((n, D_MODEL), F32),
            jax.ShapeDtypeStruct((n, LANES), F32),
            jax.ShapeDtypeStruct((SUBLANES, LANES), F32),
        ],
        scratch_shapes=[pltpu.VMEM((SUBLANES, LANES), F32)],
        compiler_params=pltpu.CompilerParams(vmem_limit_bytes=VMEM_LIMIT),
        name="mix_out",
    )(x2d, gm, rw, p["w_out"], p["norm2_g"], p["router_hi"], p["router_lo"], p["router_b"])


def _sc_gather_rows(table, idx):
    n_idx = idx.shape[0]
    width = table.shape[1]
    n_workers = SC_CORES * SC_SUBCORES
    per_worker = n_idx // n_workers
    n_windows = per_worker // SC_WINDOW
    assert per_worker * n_workers == n_idx and n_windows * SC_WINDOW == per_worker
    mesh = plsc.VectorSubcoreMesh(core_axis_name="c", subcore_axis_name="s")

    @functools.partial(
        pl.kernel, mesh=mesh,
        out_type=jax.ShapeDtypeStruct((n_idx, width), table.dtype),
        scratch_types=[
            pltpu.VMEM((SC_WINDOW,), jnp.int32),
            pltpu.VMEM((SC_WINDOW, width), table.dtype),
            pltpu.SemaphoreType.DMA,
        ],
        name="sc_gather_rows",
    )
    def gather(table_hbm, idx_hbm, out_hbm, idx_v, rows_v, sem):
        worker = lax.axis_index("s") * SC_CORES + lax.axis_index("c")
        base = worker * per_worker

        @pl.loop(0, n_windows)
        def _(j):
            off = base + j * SC_WINDOW
            pltpu.sync_copy(idx_hbm.at[pl.ds(off, SC_WINDOW)], idx_v)
            pltpu.async_copy(table_hbm.at[idx_v], rows_v, sem).wait()
            pltpu.sync_copy(rows_v, out_hbm.at[pl.ds(off, SC_WINDOW)])

    return gather(table, idx)


def _sc_scatter_rows(src, dest_w, n_out):
    n, width = src.shape
    n_windows, top_k, window = dest_w.shape
    n_workers = SC_CORES * SC_SUBCORES
    per_worker = n_windows // n_workers
    assert n_windows * window == n and per_worker * n_workers == n_windows
    mesh = plsc.VectorSubcoreMesh(core_axis_name="c", subcore_axis_name="s")

    @functools.partial(
        pl.kernel, mesh=mesh,
        out_type=jax.ShapeDtypeStruct((n_out, width), src.dtype),
        scratch_types=[
            pltpu.VMEM((top_k, window), jnp.int32),
            pltpu.VMEM((window, width), src.dtype),
        ],
        name="sc_scatter_rows",
    )
    def scatter(src_hbm, dest_hbm, out_hbm, idx_v, rows_v):
        worker = lax.axis_index("s") * SC_CORES + lax.axis_index("c")

        @pl.loop(0, per_worker)
        def _(j):
            g = worker * per_worker + j
            pltpu.sync_copy(dest_hbm.at[g], idx_v)
            pltpu.sync_copy(src_hbm.at[pl.ds(g * window, window)], rows_v)
            for k in range(top_k):
                pltpu.sync_copy(rows_v, out_hbm.at[idx_v.at[k]])

    return scatter(src, dest_w)


def _experts_body(blk_e_ref, n_used_ref, valid_ref, x_ref, wgu_ref, bgu_ref, wd_ref, bd_ref, out_ref,
                  wgu_bf, wd_bf):
    i = pl.program_id(0)
    n_used = n_used_ref[0]

    @pl.when((i == 0) | (blk_e_ref[i] != blk_e_ref[jnp.maximum(i - 1, 0)]))
    def _():
        wgu_bf[...] = wgu_ref[0].astype(BF16)
        wd_bf[...] = wd_ref[0].astype(BF16)

    @pl.when(i < n_used)
    def _():
        row = lax.broadcasted_iota(jnp.int32, (MOE_ROWS, 1), 0)
        xb = jnp.where(row < valid_ref[i], x_ref[...], 0.0).astype(BF16)
        gu = _dot(xb, wgu_bf[...]) + bgu_ref[0]
        gate = jnp.minimum(gu[:, :D_EXPERT], SWIGLU_LIMIT)
        up = jnp.clip(gu[:, D_EXPERT:], -SWIGLU_LIMIT, SWIGLU_LIMIT)
        act = gate * _sigmoid(gate * SWIGLU_ALPHA) * (up + 1.0)
        out_ref[...] = _dot(act.astype(BF16), wd_bf[...]) + bd_ref[0]

    @pl.when(i >= n_used)
    def _():
        out_ref[...] = jnp.zeros_like(out_ref)


def _experts(x_rows, blk_e, n_used, valid, p):
    n_blocks = blk_e.shape[0]
    grid_spec = pltpu.PrefetchScalarGridSpec(
        num_scalar_prefetch=3,
        grid=(n_blocks,),
        in_specs=[
            pl.BlockSpec((MOE_ROWS, D_MODEL), lambda i, be, nu, va: (jnp.minimum(i, nu[0] - 1), 0)),
            pl.BlockSpec((1, D_MODEL, 2 * D_EXPERT), lambda i, be, nu, va: (be[i], 0, 0)),
            pl.BlockSpec((1, 1, 2 * D_EXPERT), lambda i, be, nu, va: (be[i], 0, 0)),
            pl.BlockSpec((1, D_EXPERT, D_MODEL), lambda i, be, nu, va: (be[i], 0, 0)),
            pl.BlockSpec((1, 1, D_MODEL), lambda i, be, nu, va: (be[i], 0, 0)),
        ],
        out_specs=pl.BlockSpec((MOE_ROWS, D_MODEL), lambda i, be, nu, va: (i, 0)),
        scratch_shapes=[pltpu.VMEM((D_MODEL, 2 * D_EXPERT), BF16), pltpu.VMEM((D_EXPERT, D_MODEL), BF16)],
    )
    return pl.pallas_call(
        _experts_body,
        grid_spec=grid_spec,
        out_shape=jax.ShapeDtypeStruct((n_blocks * MOE_ROWS, D_MODEL), F32),
        compiler_params=pltpu.CompilerParams(vmem_limit_bytes=VMEM_LIMIT),
        name="experts",
    )(blk_e, n_used, valid, x_rows, p["w_gu"], p["b_gu"], p["w_down"], p["b_down"])


def _combine_body(yg_ref, x1_ref, gates_ref, fg_ref, out_ref):
    acc = x1_ref[...]
    gates = gates_ref[...]
    for kk in range(TOP_K):
        acc = acc + gates[:, TOP_K + kk:TOP_K + kk + 1] * yg_ref[kk]
    out_ref[...] = acc * lax.rsqrt(jnp.mean(acc * acc, axis=-1, keepdims=True) + RMS_EPS) * fg_ref[...]


def _combine(yg, x1, gates, final_g, tm):
    n = x1.shape[0]
    return pl.pallas_call(
        _combine_body,
        grid=(n // tm,),
        in_specs=[
            pl.BlockSpec((TOP_K, tm, D_MODEL), lambda i: (0, i, 0)),
            pl.BlockSpec((tm, D_MODEL), lambda i: (i, 0)),
            pl.BlockSpec((tm, LANES), lambda i: (i, 0)),
            pl.BlockSpec((1, D_MODEL), lambda i: (0, 0)),
        ],
        out_specs=pl.BlockSpec((tm, D_MODEL), lambda i: (i, 0)),
        out_shape=jax.ShapeDtypeStruct((n, D_MODEL), F32),
        compiler_params=pltpu.CompilerParams(vmem_limit_bytes=VMEM_LIMIT),
        name="combine",
    )(yg, x1, gates, final_g)


def _prepare(norm1_g, w_in, rw_mu, gm_ln_g, gm_ln_b, gm_ws, gm_bs, rw_w0, rw_w2, rw_a0, rw_a2, rw_g2,
             rw_k_k, rw_k_a, rw_r_k, rw_gn_g, rw_gn_b, w_out, norm2_g, router_w, router_b,
             w_gu, b_gu, w_down, b_down, final_g):
    l = 0
    row = lambda t: t.reshape(1, -1).astype(F32)
    p = {}
    p["norm1_g"] = row(norm1_g[l])
    w = w_in[l]
    p["w_gm"] = w[:, :2 * GM_WIDTH].astype(BF16)
    pad_cols = LORA_PAD - LORA_COLS
    p["w_z"] = jnp.pad(w[:, 2 * GM_WIDTH:], ((0, 0), (0, pad_cols))).astype(BF16)
    p["rw_mu"] = jnp.pad(row(rw_mu[l]), ((0, 0), (0, pad_cols)))
    p["gm_ln_g"] = row(gm_ln_g[l])
    p["gm_ln_b"] = row(gm_ln_b[l])
    p["gm_ws"] = gm_ws[l].astype(BF16)
    p["gm_bs"] = jnp.broadcast_to(gm_bs[l][:, :, None], (GM_HEADS, GM_CHUNK, GM_HEAD_DIM)).astype(F32)
    zeros = lambda r, c: jnp.zeros((r, c), F32)
    o_a = DECAY_LORA
    o_g = DECAY_LORA + ICLR_LORA

    def lora_matrix(d, with_epilogue):
        blocks = [jnp.concatenate([rw_w2[l, d], zeros(LORA_PAD - DECAY_LORA, RW_WIDTH)], axis=0),
                  jnp.concatenate([zeros(o_a, RW_WIDTH), rw_a2[l, d], zeros(LORA_PAD - o_g, RW_WIDTH)], axis=0)]
        if with_epilogue:
            blocks.append(jnp.concatenate([zeros(o_a, RW_WIDTH), rw_a2[l, 1 - d], zeros(LORA_PAD - o_g, RW_WIDTH)],
                                          axis=0))
            blocks.append(jnp.concatenate([zeros(o_g, RW_WIDTH), rw_g2[l], zeros(LORA_PAD - LORA_COLS, RW_WIDTH)],
                                          axis=0))
        return jnp.concatenate(blocks, axis=1)

    for d, name in ((0, "fwd"), (1, "rev")):
        mat = lora_matrix(d, with_epilogue=(d == 1))
        hi = mat.astype(BF16)
        p["lora_hi_" + name] = hi
        p["lora_lo_" + name] = (mat - hi.astype(F32)).astype(BF16)
        p["w0_" + name] = row(rw_w0[l, d])
        p["a0_" + name] = row(rw_a0[l, d])
    p["rw_k_k"] = row(rw_k_k[l])
    p["rw_k_a"] = row(rw_k_a[l])
    p["rw_r_k"] = row(rw_r_k[l])
    p["rw_gn_g"] = row(rw_gn_g[l])
    p["rw_gn_b"] = row(rw_gn_b[l])
    ch = jnp.arange(PAIR) // RW_HEAD_DIM
    p["head_ones"] = (ch[:, None] == ch[None, :]).astype(BF16)
    p["w_out"] = w_out[l].astype(BF16)
    p["norm2_g"] = row(norm2_g[l])
    rw_pad = jnp.pad(router_w[l].astype(F32), ((0, 0), (0, LANES - N_EXPERTS)))
    hi = rw_pad.astype(BF16)
    p["router_hi"] = hi
    p["router_lo"] = (rw_pad - hi.astype(F32)).astype(BF16)
    p["router_b"] = jnp.pad(row(router_b[l]), ((0, 0), (0, LANES - N_EXPERTS)), constant_values=NEG_BIG)
    p["w_gu"] = w_gu[l]
    p["b_gu"] = b_gu[l].reshape(N_EXPERTS, 1, 2 * D_EXPERT).astype(F32)
    p["w_down"] = w_down[l]
    p["b_down"] = b_down[l].reshape(N_EXPERTS, 1, D_MODEL).astype(F32)
    p["final_g"] = row(final_g)
    return p


def _pick_tile(n, want):
    t = want
    while n % t:
        t //= 2
    return t


def _encoder(x, p):
    b_sz, t_len, d = x.shape
    n = b_sz * t_len
    x2d = x.reshape(n, d)
    tm = _pick_tile(n, 512)
    gm, z = _mix_in(x2d, p, tm)
    z3 = z.reshape(b_sz, t_len, Z_COLS)
    yf = _rwkv_pass(z3, None, p, reverse=False, bt=RW_BATCH_TILE)
    rw = _rwkv_pass(z3, yf, p, reverse=True, bt=RW_BATCH_TILE)
    tmo = _pick_tile(n, 256)
    x1, h2, route, counts = _mix_out(x2d, gm, rw.reshape(n, RW_WIDTH), p, tmo)
    ids = route[:, :TOP_K].astype(jnp.int32)
    ranks = route[:, 2 * TOP_K:3 * TOP_K].astype(jnp.int32)
    cnt = counts[0, :N_EXPERTS].astype(jnp.int32)
    padded = (cnt + MOE_ROWS - 1) // MOE_ROWS * MOE_ROWS
    pad_end = jnp.cumsum(padded)
    pad_start = pad_end - padded
    dest = pad_start[ids] + ranks
    n_blocks = n * TOP_K // MOE_ROWS + N_EXPERTS
    blk_start = jnp.arange(n_blocks, dtype=jnp.int32) * MOE_ROWS
    blk_e = jnp.minimum(jnp.sum((pad_end[None, :] <= blk_start[:, None]).astype(jnp.int32), axis=1), N_EXPERTS - 1)
    n_used = (pad_end[-1] // MOE_ROWS).astype(jnp.int32).reshape(1)
    valid = jnp.clip(cnt[blk_e] - (blk_start - pad_start[blk_e]), 0, MOE_ROWS).astype(jnp.int32)
    window = min(SC_SCATTER_WINDOW, n // (SC_CORES * SC_SUBCORES))
    dest_w = dest.reshape(n // window, window, TOP_K).transpose(0, 2, 1)
    x_rows = _sc_scatter_rows(h2, dest_w, n_blocks * MOE_ROWS)
    y_rows = _experts(x_rows, blk_e, n_used, valid, p)
    yg = _sc_gather_rows(y_rows, dest.T.reshape(-1)).reshape(TOP_K, n, D_MODEL)
    out = _combine(yg, x1, route, p["final_g"], _pick_tile(n, 256))
    return out.reshape(b_sz, t_len, d)


def kernel(x_prompt, x_sample, norm1_g, w_in, rw_mu, gm_ln_g, gm_ln_b, gm_ws, gm_bs, rw_w0, rw_w2, rw_a0, rw_a2,
           rw_g2, rw_k_k, rw_k_a, rw_r_k, rw_gn_g, rw_gn_b, w_out, norm2_g, router_w, router_b, w_gu, b_gu,
           w_down, b_down, final_g):
    p = _prepare(norm1_g, w_in, rw_mu, gm_ln_g, gm_ln_b, gm_ws, gm_bs, rw_w0, rw_w2, rw_a0, rw_a2, rw_g2,
                 rw_k_k, rw_k_a, rw_r_k, rw_gn_g, rw_gn_b, w_out, norm2_g, router_w, router_b,
                 w_gu, b_gu, w_down, b_down, final_g)
    return (_encoder(x_prompt, p), _encoder(x_sample, p))
```

```python
import functools

import jax
import jax.numpy as jnp
from jax import lax
from jax.experimental import pallas as pl
from jax.experimental.pallas import tpu as pltpu
from jax.experimental.pallas import tpu_sc as plsc

F32 = jnp.float32
BF16 = jnp.bfloat16

D_MODEL = 1024
GM_WIDTH = 512
RW_WIDTH = 512
GM_HEADS = 4
GM_HEAD_DIM = 128
GM_CHUNK = 128
RW_HEAD_DIM = 64
DECAY_LORA = 32
ICLR_LORA = 32
GATE_LORA = 96
LORA_COLS = DECAY_LORA + ICLR_LORA + GATE_LORA
LORA_PAD = 256
Z_COLS = 3 * RW_WIDTH + LORA_PAD
N_EXPERTS = 32
TOP_K = 4
D_EXPERT = 1024
SWIGLU_LIMIT = 7.0
SWIGLU_ALPHA = 1.702
RMS_EPS = 1e-5
LN_EPS = 1e-5
GN_EPS = 64e-5

LANES = 128
SUBLANES = 8
VMEM_LIMIT = 56 * 1024 * 1024

RW_CHUNK = 64
PAIR = 2 * RW_HEAD_DIM
N_PAIRS = RW_WIDTH // PAIR
RW_BATCH_TILE = 4
MOE_ROWS = 512
SC_CORES = 2
SC_SUBCORES = 16
SC_WINDOW = 32
SC_SCATTER_WINDOW = 64
ROUTE_ROWS = 16
DECAY_SCALE = 0.6065306597126334


def _dot(a, b):
    return jnp.dot(a, b, preferred_element_type=F32)


def _dot_nt(a, b):
    return lax.dot_general(a, b, (((1,), (1,)), ((), ())), preferred_element_type=F32)


def _split(x):
    hi = x.astype(BF16)
    lo = (x - hi.astype(F32)).astype(BF16)
    return hi, lo


def _dot3(a, b_hi, b_lo):
    a_hi, a_lo = _split(a)
    return _dot(a_hi, b_hi) + _dot(a_lo, b_hi) + _dot(a_hi, b_lo)


def _dot2(a, b01):
    a_hi, a_lo = _split(a)
    return _dot(a_hi, b01) + _dot(a_lo, b01)


def _head_sums(x, pair_ones):
    rows = x.shape[0]
    n_tiles = x.shape[1] // LANES
    stacked = jnp.concatenate([x[:, t * LANES:(t + 1) * LANES] for t in range(n_tiles)], axis=0)
    sums = _dot2(stacked, pair_ones)
    return jnp.concatenate([sums[t * rows:(t + 1) * rows] for t in range(n_tiles)], axis=1)


def _dot2_lhs01(a01, b):
    b_hi, b_lo = _split(b)
    return _dot(a01, b_hi) + _dot(a01, b_lo)


def _gelu_tanh(x):
    return 0.5 * x * (1.0 + jnp.tanh(0.7978845608028654 * (x + 0.044715 * (x * x * x))))


def _sigmoid(x):
    return 1.0 / (1.0 + jnp.exp(-x))


def _mix_in_body(x_ref, g1_ref, wgm_ref, wz_ref, lng_ref, lnb_ref, ws_ref, bs_ref, gm_ref, z_ref):
    x = x_ref[...]
    h = x * lax.rsqrt(jnp.mean(x * x, axis=-1, keepdims=True) + RMS_EPS) * g1_ref[...]
    hb = h.astype(BF16)
    uv = _dot(hb, wgm_ref[...])
    z_ref[...] = _dot(hb, wz_ref[...])
    u = _gelu_tanh(uv[:, :GM_WIDTH])
    v = _gelu_tanh(uv[:, GM_WIDTH:])
    mean = jnp.mean(v, axis=-1, keepdims=True)
    vc = v - mean
    var = jnp.mean(vc * vc, axis=-1, keepdims=True)
    vn = (vc * lax.rsqrt(var + LN_EPS) * lng_ref[...] + lnb_ref[...]).astype(BF16)
    tm = x.shape[0]
    for c in range(tm // GM_CHUNK):
        rows = slice(c * GM_CHUNK, (c + 1) * GM_CHUNK)
        for hd in range(GM_HEADS):
            cols = slice(hd * GM_HEAD_DIM, (hd + 1) * GM_HEAD_DIM)
            mixed = _dot(ws_ref[hd], vn[rows, cols]) + bs_ref[hd]
            gm_ref[rows, cols] = (u[rows, cols] * mixed).astype(BF16)


def _mix_in(x2d, p, tm):
    n = x2d.shape[0]
    full = lambda shape: pl.BlockSpec(shape, lambda i: (0,) * len(shape))
    return pl.pallas_call(
        _mix_in_body,
        grid=(n // tm,),
        in_specs=[
            pl.BlockSpec((tm, D_MODEL), lambda i: (i, 0)),
            full((1, D_MODEL)),
            full((D_MODEL, 2 * GM_WIDTH)),
            full((D_MODEL, Z_COLS)),
            full((1, GM_WIDTH)),
            full((1, GM_WIDTH)),
            full((GM_HEADS, GM_CHUNK, GM_CHUNK)),
            full((GM_HEADS, GM_CHUNK, GM_HEAD_DIM)),
        ],
        out_specs=[
            pl.BlockSpec((tm, GM_WIDTH), lambda i: (i, 0)),
            pl.BlockSpec((tm, Z_COLS), lambda i: (i, 0)),
        ],
        out_shape=[
            jax.ShapeDtypeStruct((n, GM_WIDTH), BF16),
            jax.ShapeDtypeStruct((n, Z_COLS), F32),
        ],
        compiler_params=pltpu.CompilerParams(vmem_limit_bytes=VMEM_LIMIT),
        name="mix_in",
    )(x2d, p["norm1_g"], p["w_gm"], p["w_z"], p["gm_ln_g"], p["gm_ln_b"], p["gm_ws"], p["gm_bs"])


def _rwkv_masks(reverse):
    c = RW_CHUNK
    ti = lax.broadcasted_iota(jnp.int32, (c, c), 0)
    tj = lax.broadcasted_iota(jnp.int32, (c, c), 1)
    incl = (tj >= ti) if reverse else (tj <= ti)
    lane =lax.broadcasted_iota(jnp.int32, (c, PAIR), 1)
    head0 = lane < RW_HEAD_DIM
    lane2 = lax.broadcasted_iota(jnp.int32, (c, 2 * c), 1)
    left = lane2 < c
    ti2 = lax.broadcasted_iota(jnp.int32, (c, 2 * c), 0)
    tj2 = jnp.where(left, lane2, lane2 - c)
    incl2 = (tj2 >= ti2) if reverse else (tj2 <= ti2)
    strict2 = (tj2 > ti2) if reverse else (tj2 < ti2)
    eye2 = jnp.where(tj2 == ti2, 1.0, 0.0).astype(F32)
    si = lax.broadcasted_iota(jnp.int32, (PAIR, PAIR), 0)
    sj = lax.broadcasted_iota(jnp.int32, (PAIR, PAIR), 1)
    same_head = (si // RW_HEAD_DIM) == (sj // RW_HEAD_DIM)
    return dict(
        incl2=incl2, strict2=strict2, head0=head0, left=left, eye2=eye2, same_head=same_head,
        tri=jnp.where(incl, 1.0, 0.0).astype(BF16),
    )


def _stack_heads(x, head0):
    return jnp.concatenate([jnp.where(head0, x, 0.0), jnp.where(head0, 0.0, x)], axis=0)


def _block_diag(x, left):
    return jnp.concatenate([jnp.where(left, x, 0.0), jnp.where(left, 0.0, x)], axis=0)


def _units_chunk(states, ops, m):
    c = RW_CHUNK
    units = range(len(ops))
    bf = lambda t: t.astype(BF16)
    head0, left = m["head0"], m["left"]
    ar = [bf(jnp.concatenate([o["a_t"], o["r_t"]], axis=0)) for o in ops]
    rstack = [bf(jnp.concatenate([_stack_heads(o["b_t"], head0), _stack_heads(o["k_t"], head0)], axis=0))
              for o in ops]
    acat = [_dot_nt(ar[i], rstack[i]) for i in units]
    sa = [_dot_nt(ar[i], bf(states[i])) for i in units]
    a_ab = [jnp.where(m["strict2"], acat[i][:c, :2 * c], 0.0) for i in units]
    a_ak = [bf(jnp.where(m["strict2"], acat[i][:c, 2 * c:], 0.0)) for i in units]
    a_r = [bf(jnp.concatenate([jnp.where(m["incl2"], acat[i][c:, :2 * c], 0.0),
                               jnp.where(m["incl2"], acat[i][c:, 2 * c:], 0.0)], axis=1)) for i in units]
    p = [m["eye2"] + a_ab[i] for i in units]
    lk = [_dot(bf(a_ab[i]), bf(_block_diag(a_ab[i], left))) for i in units]
    vstack = [bf(_stack_heads(o["v"], head0)) for o in ops]
    rhs = [sa[i][:c] + _dot(a_ak[i], vstack[i]) for i in units]
    power = 2
    while 2 * power < c:
        res = [_dot(bf(jnp.concatenate([p[i], lk[i]], axis=0)), bf(_block_diag(lk[i], left))) for i in units]
        p = [p[i] + res[i][:c] for i in units]
        lk = [res[i][c:] for i in units]
        power *= 2
    p = [p[i] + _dot(bf(p[i]), bf(_block_diag(lk[i], left))) for i in units]
    u = [_dot(bf(p[i]), bf(_stack_heads(rhs[i], head0))) for i in units]
    ys = [sa[i][c:] + _dot(a_r[i], jnp.concatenate([bf(_stack_heads(u[i], head0)), vstack[i]], axis=0))
          for i in units]
    upd = [_dot(bf(jnp.concatenate([u[i], ops[i]["v"]], axis=0).T),
                bf(jnp.concatenate([ops[i]["b_h"], ops[i]["k_h"]], axis=0))) for i in units]
    new_states = [states[i] * ops[i]["w_tot"] + jnp.where(m["same_head"], upd[i], 0.0) for i in units]
    return new_states, ys


def _rwkv_body(reverse, n_chunks, bt, *refs):
    if reverse:
        (zc_ref, zp_ref, zn_ref, yf_ref, mu_ref, lw_hi_ref, lw_lo_ref, w0_ref, a0_ref, a0o_ref, kk_ref, ka_ref,
         rk_ref, bd_ref, gng_ref, gnb_ref, out_ref, state_ref) = refs
    else:
        (zc_ref, zp_ref, zn_ref, mu_ref, lw_hi_ref, lw_lo_ref, w0_ref, a0_ref, kk_ref, ka_ref,
         bd_ref, out_ref, state_ref) = refs
    c = RW_CHUNK
    ci = pl.program_id(1)
    cidx = (n_chunks - 1 - ci) if reverse else ci

    @pl.when(ci == 0)
    def _():
        state_ref[...] = jnp.zeros_like(state_ref)

    rows = bt * c
    m = _rwkv_masks(reverse)
    row8 = lax.broadcasted_iota(jnp.int32, (SUBLANES, Z_COLS), 0)
    lora_lane = lax.broadcasted_iota(jnp.int32, (rows, LORA_PAD), 1)
    bd = bd_ref[...]
    keep = mu_ref[0:1, :]
    mix = mu_ref[1:2, :]
    zs = []
    for b in range(bt):
        z = zc_ref[b]
        prev_row = jnp.where(cidx > 0, zp_ref[b, SUBLANES - 1:SUBLANES, :], 0.0)
        next_row = jnp.where(cidx < n_chunks - 1, zn_ref[b, 0:1, :], 0.0)
        down = pltpu.roll(z, 1, axis=0)
        up = pltpu.roll(z, c - 1, axis=0)
        prev = jnp.concatenate([jnp.where(row8 == 0, prev_row, down[:SUBLANES]), down[SUBLANES:]], axis=0)
        nxt = jnp.concatenate([up[:c - SUBLANES], jnp.where(row8 == SUBLANES - 1, next_row, up[c - SUBLANES:])],
                              axis=0)
        zs.append(z * keep + (prev + nxt) * mix)
    zf = jnp.concatenate(zs, axis=0)
    r = zf[:, :RW_WIDTH]
    k = zf[:, RW_WIDTH:2 * RW_WIDTH]
    v = zf[:, 2 * RW_WIDTH:3 * RW_WIDTH]
    lo = zf[:, 3 * RW_WIDTH:]
    act = jnp.where(lora_lane < DECAY_LORA, jnp.tanh(lo),
                    jnp.where(lora_lane < DECAY_LORA + ICLR_LORA, lo, _sigmoid(lo)))
    proj = _dot3(act, lw_hi_ref[...], lw_lo_ref[...])
    w_in = w0_ref[...] + proj[:, :RW_WIDTH]
    logw = -DECAY_SCALE * _sigmoid(w_in)
    a_sig = _sigmoid(a0_ref[...] + proj[:, RW_WIDTH:2 * RW_WIDTH])
    kk = k * kk_ref[...]
    kk_norm = jnp.sqrt(_head_sums(kk * kk, bd))
    kk = kk / jnp.maximum(kk_norm, 1e-12)
    kd = k * (1.0 + (a_sig - 1.0) * ka_ref[...])
    a_vec = -kk
    b_vec = kk * a_sig
    ri = lax.broadcasted_iota(jnp.int32, (rows, rows), 0)
    rj = lax.broadcasted_iota(jnp.int32, (rows, rows), 1)
    ordered = (rj >= ri) if reverse else (rj <= ri)
    tri = jnp.where((ri // c) == (rj // c), jnp.where(ordered, 1.0, 0.0), 0.0).astype(BF16)
    cum = _dot2_lhs01(tri, logw)
    edge = 0 if reverse else c - 1
    tot = jnp.concatenate([jnp.broadcast_to(cum[b * c + edge:b * c + edge + 1, :], (c, RW_WIDTH))
                           for b in range(bt)], axis=0)
    w_inv = jnp.exp(-cum)
    w_end = jnp.exp(tot - cum)
    w_tot = jnp.exp(tot)
    full_ops = dict(r_t=r * jnp.exp(cum), a_t=a_vec * jnp.exp(cum - logw), b_t=b_vec * w_inv, k_t=kd * w_inv,
                    b_h=b_vec * w_end, k_h=kd * w_end, v=v)
    ops, states = [], []
    for b in range(bt):
        for pi in range(N_PAIRS):
            ln = slice(pi * PAIR, (pi + 1) * PAIR)
            unit = {name: t[b * c:(b + 1) * c, ln] for name, t in full_ops.items()}
            unit["w_tot"] = w_tot[b * c:b * c + 1, ln]
            ops.append(unit)
            states.append(state_ref[b, pi])
    new_states, ys = _units_chunk(states, ops, m)
    for b in range(bt):
        for pi in range(N_PAIRS):
            state_ref[b, pi] = new_states[b * N_PAIRS + pi]
    y = jnp.concatenate([jnp.concatenate(ys[b * N_PAIRS:(b + 1) * N_PAIRS], axis=1) for b in range(bt)], axis=0)
    if not reverse:
        out_ref[...] = y.reshape(bt, c, RW_WIDTH)
    else:
        y = y + yf_ref[...].reshape(rows, RW_WIDTH)
        a_other = _sigmoid(a0o_ref[...] + proj[:, 2 * RW_WIDTH:3 * RW_WIDTH])
        kd_sum = kd + k * (1.0 + (a_other - 1.0) * ka_ref[...])
        bonus = _head_sums(r * kd_sum * rk_ref[...], bd) * v
        inv_n = 1.0 / RW_HEAD_DIM
        mean = _head_sums(y, bd) * inv_n
        yc = y - mean
        var = _head_sums(yc * yc, bd) * inv_n
        yn = yc * lax.rsqrt(var + GN_EPS) * gng_ref[...] + gnb_ref[...]
        g = proj[:, 3 * RW_WIDTH:]
        out_ref[...] = ((yn + bonus) * g).astype(BF16).reshape(bt, c, RW_WIDTH)


def _rwkv_pass(z3, yf, p, reverse, bt):
    b_sz, t_len, _ = z3.shape
    c = RW_CHUNK
    n_chunks = t_len // c
    halo_per_chunk = c // SUBLANES
    n_halo = t_len // SUBLANES

    def cidx(ci):
        return (n_chunks - 1 - ci) if reverse else ci

    full = lambda shape: pl.BlockSpec(shape, lambda bi, ci: (0,) * len(shape))
    row512 = full((1, RW_WIDTH))
    d = "rev" if reverse else "fwd"
    n_proj = p["lora_hi_" + d].shape[1]
    in_specs = [
        pl.BlockSpec((bt, c, Z_COLS), lambda bi, ci: (bi, cidx(ci), 0)),
        pl.BlockSpec((bt, SUBLANES, Z_COLS), lambda bi, ci: (bi, jnp.maximum(cidx(ci) * halo_per_chunk - 1, 0), 0)),
        pl.BlockSpec((bt, SUBLANES, Z_COLS),
                     lambda bi, ci: (bi, jnp.minimum((cidx(ci) + 1) * halo_per_chunk, n_halo - 1), 0)),
    ]
    args = [z3, z3, z3]
    if reverse:
        in_specs.append(pl.BlockSpec((bt, c, RW_WIDTH), lambda bi, ci: (bi, cidx(ci), 0)))
        args.append(yf)
    in_specs += [full((2, Z_COLS)), full((LORA_PAD, n_proj)), full((LORA_PAD, n_proj)), row512, row512]
    args += [p["rw_mu"], p["lora_hi_" + d], p["lora_lo_" + d], p["w0_" + d], p["a0_" + d]]
    if reverse:
        in_specs.append(row512)
        args.append(p["a0_fwd"])
    in_specs += [row512, row512]
    args += [p["rw_k_k"], p["rw_k_a"]]
    if reverse:
        in_specs.append(row512)
        args.append(p["rw_r_k"])
    in_specs.append(full((PAIR, PAIR)))
    args.append(p["head_ones"])
    if reverse:
        in_specs += [row512, row512]
        args += [p["rw_gn_g"], p["rw_gn_b"]]
    return pl.pallas_call(
        functools.partial(_rwkv_body, reverse, n_chunks, bt),
        grid=(b_sz // bt, n_chunks),
        in_specs=in_specs,
        out_specs=pl.BlockSpec((bt, c, RW_WIDTH), lambda bi, ci: (bi, cidx(ci), 0)),
        out_shape=jax.ShapeDtypeStruct((b_sz, t_len, RW_WIDTH), BF16 if reverse else F32),
        scratch_shapes=[pltpu.VMEM((bt, N_PAIRS, PAIR, PAIR), F32)],
        compiler_params=pltpu.CompilerParams(vmem_limit_bytes=VMEM_LIMIT),
        name="rwkv_" + d,
    )(*args)


def _mix_out_body(x_ref, gm_ref, rw_ref, wo_ref, g2_ref, rw_hi_ref, rw_lo_ref, rb_ref, earlier_ref,
                  x1_ref, h2_ref, route_ref, route_t_ref, counts_ref, carry_ref):
    i = pl.program_id(0)

    @pl.when(i == 0)
    def _():
        carry_ref[...] = jnp.zeros_like(carry_ref)

    x1 = x_ref[...] + _dot(gm_ref[...], wo_ref[:GM_WIDTH, :]) + _dot(rw_ref[...], wo_ref[GM_WIDTH:, :])
    x1_ref[...] = x1
    h2 = x1 * lax.rsqrt(jnp.mean(x1 * x1, axis=-1, keepdims=True) + RMS_EPS) * g2_ref[...]
    h2_ref[...] = h2
    tm = x1.shape[0]
    h_hi, h_lo = _split(h2)
    logits = (_dot_nt(rw_hi_ref[...], h_hi) + _dot_nt(rw_hi_ref[...], h_lo) + _dot_nt(rw_lo_ref[...], h_hi)
              + rb_ref[...])
    expert = lax.broadcasted_iota(jnp.int32, (N_EXPERTS, tm), 0).astype(F32)
    work = logits
    vals, ids, hits = [], [], []
    onehot = jnp.zeros((N_EXPERTS, tm), F32)
    for _ in range(TOP_K):
        mx = jnp.max(work, axis=0, keepdims=True)
        idx = jnp.min(jnp.where(work == mx, expert, float(N_EXPERTS)), axis=0, keepdims=True)
        hit = expert == idx
        vals.append(mx)
        ids.append(idx)
        hits.append(hit)
        onehot = jnp.where(hit, 1.0, onehot)
        work = jnp.where(hit, -jnp.inf, work)
    exps = [jnp.exp(vk - vals[0]) for vk in vals]
    denom = exps[0] + exps[1] + exps[2] + exps[3]
    carry = carry_ref[:, 0:1]
    ranks = carry + _dot(onehot.astype(BF16), earlier_ref[...])
    rank_rows = [jnp.sum(jnp.where(hit, ranks, 0.0), axis=0, keepdims=True) for hit in hits]
    route_t = jnp.concatenate(ids + [e / denom for e in exps] + rank_rows
                              + [jnp.zeros((ROUTE_ROWS - 3 * TOP_K, tm), F32)], axis=0)
    route_t_ref[...] = route_t
    route_ref[...] = jnp.concatenate([route_t, jnp.zeros((LANES - ROUTE_ROWS, tm), F32)], axis=0).T
    new_counts = carry + jnp.sum(onehot, axis=1, keepdims=True)
    carry_ref[...] = jnp.broadcast_to(new_counts, carry_ref.shape)
    counts_ref[...] = jnp.broadcast_to(new_counts, counts_ref.shape)


def _mix_out(x2d, gm, rw, p, tm):
    n = x2d.shape[0]
    full = lambda shape: pl.BlockSpec(shape, lambda i: (0,) * len(shape))
    tile = lambda w: pl.BlockSpec((tm, w), lambda i: (i, 0))
    earlier = (jnp.arange(tm)[:, None] < jnp.arange(tm)[None, :]).astype(BF16)
    return pl.pallas_call(
        _mix_out_body,
        grid=(n // tm,),
        in_specs=[tile(D_MODEL), tile(GM_WIDTH), tile(RW_WIDTH), full((D_MODEL, D_MODEL)), full((1, D_MODEL)),
                  full((N_EXPERTS, D_MODEL)), full((N_EXPERTS, D_MODEL)), full((N_EXPERTS, 1)), full((tm, tm))],
        out_specs=[tile(D_MODEL), tile(D_MODEL), tile(LANES), pl.BlockSpec((ROUTE_ROWS, tm), lambda i: (0, i)),
                   full((N_EXPERTS, LANES))],
        out_shape=[
            jax.ShapeDtypeStruct((n, D_MODEL), F32),
            jax.ShapeDtypeStruct((n, D_MODEL), F32),
            jax.ShapeDtypeStruct((n, LANES), F32),
            jax.ShapeDtypeStruct((ROUTE_ROWS, n), F32),
            jax.ShapeDtypeStruct((N_EXPERTS, LANES), F32),
        ],
        scratch_shapes=[pltpu.VMEM((N_EXPERTS, LANES), F32)],
        compiler_params=pltpu.CompilerParams(vmem_limit_bytes=VMEM_LIMIT),
        name="mix_out",
    )(x2d, gm, rw, p["w_out"], p["norm2_g"], p["router_hi"], p["router_lo"], p["router_b"], earlier)


def _sc_gather_rows(table, idx):
    n_idx = idx.shape[0]
    width = table.shape[1]
    n_workers = SC_CORES * SC_SUBCORES
    per_worker = n_idx // n_workers
    n_windows = per_worker // SC_WINDOW
    assert per_worker * n_workers == n_idx and n_windows * SC_WINDOW == per_worker
    mesh = plsc.VectorSubcoreMesh(core_axis_name="c", subcore_axis_name="s")

    @functools.partial(
        pl.kernel, mesh=mesh,
        out_type=jax.ShapeDtypeStruct((n_idx, width), table.dtype),
        scratch_types=[
            pltpu.VMEM((SC_WINDOW,), jnp.int32),
            pltpu.VMEM((SC_WINDOW, width), table.dtype),
            pltpu.SemaphoreType.DMA,
        ],
        name="sc_gather_rows",
    )
    def gather(table_hbm, idx_hbm, out_hbm, idx_v, rows_v, sem):
        worker = lax.axis_index("s") * SC_CORES + lax.axis_index("c")
        base = worker * per_worker

        @pl.loop(0, n_windows)
        def _(j):
            off = base + j * SC_WINDOW
            pltpu.sync_copy(idx_hbm.at[pl.ds(off, SC_WINDOW)], idx_v)
            pltpu.async_copy(table_hbm.at[idx_v], rows_v, sem).wait()
            pltpu.sync_copy(rows_v, out_hbm.at[pl.ds(off, SC_WINDOW)])

    return gather(table, idx)


def _sc_scatter_rows(src, dest_w, n_out):
    n, width = src.shape
    n_windows, top_k, window = dest_w.shape
    n_workers = SC_CORES * SC_SUBCORES
    per_worker = n_windows // n_workers
    assert n_windows * window == n and per_worker * n_workers == n_windows
    mesh = plsc.VectorSubcoreMesh(core_axis_name="c", subcore_axis_name="s")

    @functools.partial(
        pl.kernel, mesh=mesh,
        out_type=jax.ShapeDtypeStruct((n_out, width), src.dtype),
        scratch_types=[
            pltpu.VMEM((top_k, window), jnp.int32),
            pltpu.VMEM((window, width), src.dtype),
        ],
        name="sc_scatter_rows",
    )
    def scatter(src_hbm, dest_hbm, out_hbm, idx_v, rows_v):
        worker = lax.axis_index("s") * SC_CORES + lax.axis_index("c")

        @pl.loop(0, per_worker)
        def _(j):
            g = worker * per_worker + j
            pltpu.sync_copy(dest_hbm.at[g], idx_v)
            pltpu.sync_copy(src_hbm.at[pl.ds(g * window, window)], rows_v)
            for k in range(top_k):
                pltpu.sync_copy(rows_v, out_hbm.at[idx_v.at[k]])

    return scatter(src, dest_w)


def _experts_body(blk_e_ref, n_used_ref, valid_ref, x_ref, wgu_ref, bgu_ref, wd_ref, bd_ref, out_ref,
                  wgu_bf, wd_bf):
    i = pl.program_id(0)
    n_used = n_used_ref[0]

    @pl.when((i == 0) | (blk_e_ref[i] != blk_e_ref[jnp.maximum(i - 1, 0)]))
    def _():
        wgu_bf[...] = wgu_ref[0].astype(BF16)
        wd_bf[...] = wd_ref[0].astype(BF16)

    @pl.when(i < n_used)
    def _():
        row = lax.broadcasted_iota(jnp.int32, (MOE_ROWS, 1), 0)
        xb = jnp.where(row < valid_ref[i], x_ref[...], 0.0).astype(BF16)
        gu = _dot(xb, wgu_bf[...]) + bgu_ref[0]
        gate = jnp.minimum(gu[:, :D_EXPERT], SWIGLU_LIMIT)
        up = jnp.clip(gu[:, D_EXPERT:], -SWIGLU_LIMIT, SWIGLU_LIMIT)
        act = gate * _sigmoid(gate * SWIGLU_ALPHA) * (up + 1.0)
        out_ref[...] = _dot(act.astype(BF16), wd_bf[...]) + bd_ref[0]

    @pl.when(i >= n_used)
    def _():
        out_ref[...] = jnp.zeros_like(out_ref)


def _experts(x_rows, blk_e, n_used, valid, p):
    n_blocks = blk_e.shape[0]
    grid_spec = pltpu.PrefetchScalarGridSpec(
        num_scalar_prefetch=3,
        grid=(n_blocks,),
        in_specs=[
            pl.BlockSpec((MOE_ROWS, D_MODEL), lambda i, be, nu, va: (jnp.minimum(i, nu[0] - 1), 0)),
            pl.BlockSpec((1, D_MODEL, 2 * D_EXPERT), lambda i, be, nu, va: (be[i], 0, 0)),
            pl.BlockSpec((1, 1, 2 * D_EXPERT), lambda i, be, nu, va: (be[i], 0, 0)),
            pl.BlockSpec((1, D_EXPERT, D_MODEL), lambda i, be, nu, va: (be[i], 0, 0)),
            pl.BlockSpec((1, 1, D_MODEL), lambda i, be, nu, va: (be[i], 0, 0)),
        ],
        out_specs=pl.BlockSpec((MOE_ROWS, D_MODEL), lambda i, be, nu, va: (i, 0)),
        scratch_shapes=[pltpu.VMEM((D_MODEL, 2 * D_EXPERT), BF16), pltpu.VMEM((D_EXPERT, D_MODEL), BF16)],
    )
    return pl.pallas_call(
        _experts_body,
        grid_spec=grid_spec,
        out_shape=jax.ShapeDtypeStruct((n_blocks * MOE_ROWS, D_MODEL), F32),
        compiler_params=pltpu.CompilerParams(vmem_limit_bytes=VMEM_LIMIT),
        name="experts",
    )(blk_e, n_used, valid, x_rows, p["w_gu"], p["b_gu"], p["w_down"], p["b_down"])


def _combine_body(yg_ref, x1_ref, gates_ref, fg_ref, out_ref):
    acc = x1_ref[...]
    gates = gates_ref[...]
    for kk in range(TOP_K):
        acc = acc + gates[:, TOP_K + kk:TOP_K + kk + 1] * yg_ref[kk]
    out_ref[...] = acc * lax.rsqrt(jnp.mean(acc * acc, axis=-1, keepdims=True) + RMS_EPS) * fg_ref[...]


def _combine(yg, x1, gates, final_g, tm):
    n = x1.shape[0]
    return pl.pallas_call(
        _combine_body,
        grid=(n // tm,),
        in_specs=[
            pl.BlockSpec((TOP_K, tm, D_MODEL), lambda i: (0, i, 0)),
            pl.BlockSpec((tm, D_MODEL), lambda i: (i, 0)),
            pl.BlockSpec((tm, LANES), lambda i: (i, 0)),
            pl.BlockSpec((1, D_MODEL), lambda i: (0, 0)),
        ],
        out_specs=pl.BlockSpec((tm, D_MODEL), lambda i: (i, 0)),
        out_shape=jax.ShapeDtypeStruct((n, D_MODEL), F32),
        compiler_params=pltpu.CompilerParams(vmem_limit_bytes=VMEM_LIMIT),
        name="combine",
    )(yg, x1, gates, final_g)


def _prepare(norm1_g, w_in, rw_mu, gm_ln_g, gm_ln_b, gm_ws, gm_bs, rw_w0, rw_w2, rw_a0, rw_a2, rw_g2,
             rw_k_k, rw_k_a, rw_r_k, rw_gn_g, rw_gn_b, w_out, norm2_g, router_w, router_b,
             w_gu, b_gu, w_down, b_down, final_g):
    l = 0
    row = lambda t: t.reshape(1, -1).astype(F32)
    p = {}
    p["norm1_g"] = row(norm1_g[l])
    w = w_in[l]
    p["w_gm"] = w[:, :2 * GM_WIDTH].astype(BF16)
    pad_cols = LORA_PAD - LORA_COLS
    p["w_z"] = jnp.pad(w[:, 2 * GM_WIDTH:], ((0, 0), (0, pad_cols))).astype(BF16)
    mu = jnp.pad(row(rw_mu[l]), ((0, 0), (0, pad_cols)))
    p["rw_mu"] = jnp.concatenate([1.0 - mu, 0.5 * mu], axis=0)
    p["gm_ln_g"] = row(gm_ln_g[l])
    p["gm_ln_b"] = row(gm_ln_b[l])
    p["gm_ws"] = gm_ws[l].astype(BF16)
    p["gm_bs"] = jnp.broadcast_to(gm_bs[l][:, :, None], (GM_HEADS, GM_CHUNK, GM_HEAD_DIM)).astype(F32)
    zeros = lambda r, c: jnp.zeros((r, c), F32)
    o_a = DECAY_LORA
    o_g = DECAY_LORA + ICLR_LORA

    def lora_matrix(d, with_epilogue):
        blocks = [jnp.concatenate([rw_w2[l, d], zeros(LORA_PAD - DECAY_LORA, RW_WIDTH)], axis=0),
                  jnp.concatenate([zeros(o_a, RW_WIDTH), rw_a2[l, d], zeros(LORA_PAD - o_g, RW_WIDTH)], axis=0)]
        if with_epilogue:
            blocks.append(jnp.concatenate([zeros(o_a, RW_WIDTH), rw_a2[l, 1 - d], zeros(LORA_PAD - o_g, RW_WIDTH)],
                                          axis=0))
            blocks.append(jnp.concatenate([zeros(o_g, RW_WIDTH), rw_g2[l], zeros(LORA_PAD - LORA_COLS, RW_WIDTH)],
                                          axis=0))
        return jnp.concatenate(blocks, axis=1)

    for d, name in ((0, "fwd"), (1, "rev")):
        mat = lora_matrix(d, with_epilogue=(d == 1))
        hi = mat.astype(BF16)
        p["lora_hi_" + name] = hi
        p["lora_lo_" + name] = (mat - hi.astype(F32)).astype(BF16)
        p["w0_" + name] = row(rw_w0[l, d])
        p["a0_" + name] = row(rw_a0[l, d])
    p["rw_k_k"] = row(rw_k_k[l])
    p["rw_k_a"] = row(rw_k_a[l])
    p["rw_r_k"] = row(rw_r_k[l])
    p["rw_gn_g"] = row(rw_gn_g[l])
    p["rw_gn_b"] = row(rw_gn_b[l])
    ch = jnp.arange(PAIR) // RW_HEAD_DIM
    p["head_ones"] = (ch[:, None] == ch[None, :]).astype(BF16)
    p["w_out"] = w_out[l].astype(BF16)
    p["norm2_g"] = row(norm2_g[l])
    rw_t = router_w[l].astype(F32).T
    hi = rw_t.astype(BF16)
    p["router_hi"] = hi
    p["router_lo"] = (rw_t - hi.astype(F32)).astype(BF16)
    p["router_b"] = router_b[l].astype(F32).reshape(N_EXPERTS, 1)
    p["w_gu"] = w_gu[l]
    p["b_gu"] = b_gu[l].reshape(N_EXPERTS, 1, 2 * D_EXPERT).astype(F32)
    p["w_down"] = w_down[l]
    p["b_down"] = b_down[l].reshape(N_EXPERTS, 1, D_MODEL).astype(F32)
    p["final_g"] = row(final_g)
    return p


def _pick_tile(n, want):
    t = want
    while n % t:
        t //= 2
    return t


def _encoder(x, p):
    b_sz, t_len, d = x.shape
    n = b_sz * t_len
    x2d = x.reshape(n, d)
    tm = _pick_tile(n, 512)
    gm, z = _mix_in(x2d, p, tm)
    z3 = z.reshape(b_sz, t_len, Z_COLS)
    yf = _rwkv_pass(z3, None, p, reverse=False, bt=RW_BATCH_TILE)
    rw = _rwkv_pass(z3, yf, p, reverse=True, bt=RW_BATCH_TILE)
    tmo = _pick_tile(n, 512)
    x1, h2, route, route_t, counts = _mix_out(x2d, gm, rw.reshape(n, RW_WIDTH), p, tmo)
    ids = route_t[:TOP_K].astype(jnp.int32)
    ranks = route_t[2 * TOP_K:3 * TOP_K].astype(jnp.int32)
    cnt = counts[:, 0].astype(jnp.int32)
    padded = (cnt + MOE_ROWS - 1) // MOE_ROWS * MOE_ROWS
    pad_end = jnp.cumsum(padded)
    pad_start = pad_end - padded
    dest = pad_start[ids] + ranks
    n_blocks = n * TOP_K // MOE_ROWS + N_EXPERTS
    blk_start = jnp.arange(n_blocks, dtype=jnp.int32) * MOE_ROWS
    blk_e = jnp.minimum(jnp.sum((pad_end[None, :] <= blk_start[:, None]).astype(jnp.int32), axis=1), N_EXPERTS - 1)
    n_used = (pad_end[-1] // MOE_ROWS).astype(jnp.int32).reshape(1)
    valid = jnp.clip(cnt[blk_e] - (blk_start - pad_start[blk_e]), 0, MOE_ROWS).astype(jnp.int32)
    window = min(SC_SCATTER_WINDOW, n // (SC_CORES * SC_SUBCORES))
    dest_w = dest.reshape(TOP_K, n // window, window).transpose(1, 0, 2)
    x_rows = _sc_scatter_rows(h2, dest_w, n_blocks * MOE_ROWS)
    y_rows = _experts(x_rows, blk_e, n_used, valid, p)
    yg = _sc_gather_rows(y_rows, dest.reshape(-1)).reshape(TOP_K, n, D_MODEL)
    out = _combine(yg, x1, route, p["final_g"], _pick_tile(n, 256))
    return out.reshape(b_sz, t_len, d)


def kernel(x_prompt, x_sample, norm1_g, w_in, rw_mu, gm_ln_g, gm_ln_b, gm_ws, gm_bs, rw_w0, rw_w2, rw_a0, rw_a2,
           rw_g2, rw_k_k, rw_k_a, rw_r_k, rw_gn_g, rw_gn_b, w_out, norm2_g, router_w, router_b, w_gu, b_gu,
           w_down, b_down, final_g):
    p = _prepare(norm1_g, w_in, rw_mu, gm_ln_g, gm_ln_b, gm_ws, gm_bs, rw_w0, rw_w2, rw_a0, rw_a2, rw_g2,
                 rw_k_k, rw_k_a, rw_r_k, rw_gn_g, rw_gn_b, w_out, norm2_g, router_w, router_b,
                 w_gu, b_gu, w_down, b_down, final_g)
    return (_encoder(x_prompt, p), _encoder(x_sample, p))
```

```python
import functools

import jax
import jax.numpy as jnp
from jax import lax
from jax.experimental import pallas as pl
from jax.experimental.pallas import tpu as pltpu
from jax.experimental.pallas import tpu_sc as plsc

F32 = jnp.float32
BF16 = jnp.bfloat16

D_MODEL = 1024
GM_WIDTH = 512
RW_WIDTH = 512
GM_HEADS = 4
GM_HEAD_DIM = 128
GM_CHUNK = 128
RW_HEAD_DIM = 64
DECAY_LORA = 32
ICLR_LORA = 32
GATE_LORA = 96
LORA_COLS = DECAY_LORA + ICLR_LORA + GATE_LORA
LORA_PAD = 256
Z_COLS = 3 * RW_WIDTH + LORA_PAD
N_EXPERTS = 32
TOP_K = 4
D_EXPERT = 1024
SWIGLU_LIMIT = 7.0
SWIGLU_ALPHA = 1.702
RMS_EPS = 1e-5
LN_EPS = 1e-5
GN_EPS = 64e-5

LANES = 128
SUBLANES = 8
VMEM_LIMIT = 56 * 1024 * 1024

RW_CHUNK = 64
PAIR = 2 * RW_HEAD_DIM
N_PAIRS = RW_WIDTH // PAIR
RW_BATCH_TILE = 4
MOE_ROWS = 512
SC_CORES = 2
SC_SUBCORES = 16
SC_WINDOW = 64
SC_SCATTER_WINDOW = 128
ROUTE_ROWS = 16
DECAY_SCALE = 0.6065306597126334


def _dot(a, b):
    return jnp.dot(a, b, preferred_element_type=F32)


def _dot_nt(a, b):
    return lax.dot_general(a, b, (((1,), (1,)), ((), ())), preferred_element_type=F32)


def _split(x):
    hi = x.astype(BF16)
    lo = (x - hi.astype(F32)).astype(BF16)
    return hi, lo


def _dot3(a, b_hi, b_lo):
    a_hi, a_lo = _split(a)
    return _dot(a_hi, b_hi) + _dot(a_lo, b_hi) + _dot(a_hi, b_lo)


def _dot2(a, b01):
    a_hi, a_lo = _split(a)
    return _dot(a_hi, b01) + _dot(a_lo, b01)


def _head_sums(x, pair_ones):
    rows = x.shape[0]
    n_tiles = x.shape[1] // LANES
    stacked = jnp.concatenate([x[:, t * LANES:(t + 1) * LANES] for t in range(n_tiles)], axis=0)
    sums = _dot2(stacked, pair_ones)
    return jnp.concatenate([sums[t * rows:(t + 1) * rows] for t in range(n_tiles)], axis=1)


def _dot2_lhs01(a01, b):
    b_hi, b_lo = _split(b)
    return _dot(a01, b_hi) + _dot(a01, b_lo)


def _pack_bf16_halves(x):
    w = x.shape[1] // 2
    hi = lax.bitcast_convert_type(x[:, :w].astype(BF16).astype(F32), jnp.int32)
    lo = lax.bitcast_convert_type(x[:, w:].astype(BF16).astype(F32), jnp.int32)
    return hi | lax.shift_right_logical(lo, 16)


def _unpack_bf16_halves(words):
    left = lax.bitcast_convert_type(words & jnp.int32(-65536), F32)
    right = lax.bitcast_convert_type(lax.shift_left(words, 16), F32)
    return left, right


def _gelu_tanh(x):
    return 0.5 * x * (1.0 + jnp.tanh(0.7978845608028654 * (x + 0.044715 * (x * x * x))))


def _sigmoid(x):
    return 1.0 / (1.0 + jnp.exp(-x))


def _mix_in_body(x_ref, g1_ref, wgm_ref, wz_ref, lng_ref, lnb_ref, ws_ref, bs_ref, gm_ref, z_ref):
    x = x_ref[...]
    h = x * lax.rsqrt(jnp.mean(x * x, axis=-1, keepdims=True) + RMS_EPS) * g1_ref[...]
    hb = h.astype(BF16)
    uv = _dot(hb, wgm_ref[...])
    z_ref[...] = _dot(hb, wz_ref[...])
    u = _gelu_tanh(uv[:, :GM_WIDTH])
    v = _gelu_tanh(uv[:, GM_WIDTH:])
    mean = jnp.mean(v, axis=-1, keepdims=True)
    vc = v - mean
    var = jnp.mean(vc * vc, axis=-1, keepdims=True)
    vn = (vc * lax.rsqrt(var + LN_EPS) * lng_ref[...] + lnb_ref[...]).astype(BF16)
    tm = x.shape[0]
    for c in range(tm // GM_CHUNK):
        rows = slice(c * GM_CHUNK, (c + 1) * GM_CHUNK)
        for hd in range(GM_HEADS):
            cols = slice(hd * GM_HEAD_DIM, (hd + 1) * GM_HEAD_DIM)
            mixed = _dot(ws_ref[hd], vn[rows, cols]) + bs_ref[hd]
            gm_ref[rows, cols] = (u[rows, cols] * mixed).astype(BF16)


def _mix_in(x2d, p, tm):
    n = x2d.shape[0]
    full = lambda shape: pl.BlockSpec(shape, lambda i: (0,) * len(shape))
    return pl.pallas_call(
        _mix_in_body,
        grid=(n // tm,),
        in_specs=[
            pl.BlockSpec((tm, D_MODEL), lambda i: (i, 0)),
            full((1, D_MODEL)),
            full((D_MODEL, 2 * GM_WIDTH)),
            full((D_MODEL, Z_COLS)),
            full((1, GM_WIDTH)),
            full((1, GM_WIDTH)),
            full((GM_HEADS, GM_CHUNK, GM_CHUNK)),
            full((GM_HEADS, GM_CHUNK, GM_HEAD_DIM)),
        ],
        out_specs=[
            pl.BlockSpec((tm, GM_WIDTH), lambda i: (i, 0)),
            pl.BlockSpec((tm, Z_COLS), lambda i: (i, 0)),
        ],
        out_shape=[
            jax.ShapeDtypeStruct((n, GM_WIDTH), BF16),
            jax.ShapeDtypeStruct((n, Z_COLS), F32),
        ],
        compiler_params=pltpu.CompilerParams(vmem_limit_bytes=VMEM_LIMIT),
        name="mix_in",
    )(x2d, p["norm1_g"], p["w_gm"], p["w_z"], p["gm_ln_g"], p["gm_ln_b"], p["gm_ws"], p["gm_bs"])


def _rwkv_masks(reverse):
    c = RW_CHUNK
    ti = lax.broadcasted_iota(jnp.int32, (c, c), 0)
    tj = lax.broadcasted_iota(jnp.int32, (c, c), 1)
    incl = (tj >= ti) if reverse else (tj <= ti)
    lane =lax.broadcasted_iota(jnp.int32, (c, PAIR), 1)
    head0 = lane < RW_HEAD_DIM
    lane2 = lax.broadcasted_iota(jnp.int32, (c, 2 * c), 1)
    left = lane2 < c
    ti2 = lax.broadcasted_iota(jnp.int32, (c, 2 * c), 0)
    tj2 = jnp.where(left, lane2, lane2 - c)
    incl2 = (tj2 >= ti2) if reverse else (tj2 <= ti2)
    strict2 = (tj2 > ti2) if reverse else (tj2 < ti2)
    eye2 = jnp.where(tj2 == ti2, 1.0, 0.0).astype(F32)
    si = lax.broadcasted_iota(jnp.int32, (PAIR, PAIR), 0)
    sj = lax.broadcasted_iota(jnp.int32, (PAIR, PAIR), 1)
    same_head = (si // RW_HEAD_DIM) == (sj // RW_HEAD_DIM)
    return dict(
        incl2=incl2, strict2=strict2, head0=head0, left=left, eye2=eye2, same_head=same_head,
        tri=jnp.where(incl, 1.0, 0.0).astype(BF16),
    )


def _stack_heads(x, head0):
    return jnp.concatenate([jnp.where(head0, x, 0.0), jnp.where(head0, 0.0, x)], axis=0)


def _block_diag(x, left):
    return jnp.concatenate([jnp.where(left, x, 0.0), jnp.where(left, 0.0, x)], axis=0)


def _units_chunk(states, ops, m):
    c = RW_CHUNK
    units = range(len(ops))
    bf = lambda t: t.astype(BF16)
    head0, left = m["head0"], m["left"]
    ar = [bf(jnp.concatenate([o["a_t"], o["r_t"]], axis=0)) for o in ops]
    rstack = [bf(jnp.concatenate([_stack_heads(o["b_t"], head0), _stack_heads(o["k_t"], head0)], axis=0))
              for o in ops]
    acat = [_dot_nt(ar[i], rstack[i]) for i in units]
    sa = [_dot_nt(ar[i], bf(states[i])) for i in units]
    a_ab = [jnp.where(m["strict2"], acat[i][:c, :2 * c], 0.0) for i in units]
    a_ak = [bf(jnp.where(m["strict2"], acat[i][:c, 2 * c:], 0.0)) for i in units]
    a_r = [bf(jnp.concatenate([jnp.where(m["incl2"], acat[i][c:, :2 * c], 0.0),
                               jnp.where(m["incl2"], acat[i][c:, 2 * c:], 0.0)], axis=1)) for i in units]
    p = [m["eye2"] + a_ab[i] for i in units]
    lk = [_dot(bf(a_ab[i]), bf(_block_diag(a_ab[i], left))) for i in units]
    vstack = [bf(_stack_heads(o["v"], head0)) for o in ops]
    rhs = [sa[i][:c] + _dot(a_ak[i], vstack[i]) for i in units]
    power = 2
    while 2 * power < c:
        res = [_dot(bf(jnp.concatenate([p[i], lk[i]], axis=0)), bf(_block_diag(lk[i], left))) for i in units]
        p = [p[i] + res[i][:c] for i in units]
        lk = [res[i][c:] for i in units]
        power *= 2
    p = [p[i] + _dot(bf(p[i]), bf(_block_diag(lk[i], left))) for i in units]
    u = [_dot(bf(p[i]), bf(_stack_heads(rhs[i], head0))) for i in units]
    ys = [sa[i][c:] + _dot(a_r[i], jnp.concatenate([bf(_stack_heads(u[i], head0)), vstack[i]], axis=0))
          for i in units]
    upd = [_dot(bf(jnp.concatenate([u[i], ops[i]["v"]], axis=0).T),
                bf(jnp.concatenate([ops[i]["b_h"], ops[i]["k_h"]], axis=0))) for i in units]
    new_states = [states[i] * ops[i]["w_tot"] + jnp.where(m["same_head"], upd[i], 0.0) for i in units]
    return new_states, ys


def _rwkv_body(reverse, n_chunks, bt, *refs):
    if reverse:
        (zc_ref, zp_ref, zn_ref, yf_ref, mu_ref, lw_hi_ref, lw_lo_ref, w0_ref, a0_ref, a0o_ref, kk_ref, ka_ref,
         rk_ref, bd_ref, gng_ref, gnb_ref, out_ref, state_ref) = refs
    else:
        (zc_ref, zp_ref, zn_ref, mu_ref, lw_hi_ref, lw_lo_ref, w0_ref, a0_ref, kk_ref, ka_ref,
         bd_ref, out_ref, state_ref) = refs
    c = RW_CHUNK
    ci = pl.program_id(1)
    cidx = (n_chunks - 1 - ci) if reverse else ci

    @pl.when(ci == 0)
    def _():
        state_ref[...] = jnp.zeros_like(state_ref)

    rows = bt * c
    m = _rwkv_masks(reverse)
    row8 = lax.broadcasted_iota(jnp.int32, (SUBLANES, Z_COLS), 0)
    lora_lane = lax.broadcasted_iota(jnp.int32, (rows, LORA_PAD), 1)
    bd = bd_ref[...]
    keep = mu_ref[0:1, :]
    mix = mu_ref[1:2, :]
    zs = []
    for b in range(bt):
        z = zc_ref[b]
        prev_row = jnp.where(cidx > 0, zp_ref[b, SUBLANES - 1:SUBLANES, :], 0.0)
        next_row = jnp.where(cidx < n_chunks - 1, zn_ref[b, 0:1, :], 0.0)
        down = pltpu.roll(z, 1, axis=0)
        up = pltpu.roll(z, c - 1, axis=0)
        prev = jnp.concatenate([jnp.where(row8 == 0, prev_row, down[:SUBLANES]), down[SUBLANES:]], axis=0)
        nxt = jnp.concatenate([up[:c - SUBLANES], jnp.where(row8 == SUBLANES - 1, next_row, up[c - SUBLANES:])],
                              axis=0)
        zs.append(z * keep + (prev + nxt) * mix)
    zf = jnp.concatenate(zs, axis=0)
    r = zf[:, :RW_WIDTH]
    k = zf[:, RW_WIDTH:2 * RW_WIDTH]
    v = zf[:, 2 * RW_WIDTH:3 * RW_WIDTH]
    lo = zf[:, 3 * RW_WIDTH:]
    act = jnp.where(lora_lane < DECAY_LORA, jnp.tanh(lo),
                    jnp.where(lora_lane < DECAY_LORA + ICLR_LORA, lo, _sigmoid(lo)))
    proj = _dot3(act, lw_hi_ref[...], lw_lo_ref[...])
    w_in = w0_ref[...] + proj[:, :RW_WIDTH]
    logw = -DECAY_SCALE * _sigmoid(w_in)
    a_sig = _sigmoid(a0_ref[...] + proj[:, RW_WIDTH:2 * RW_WIDTH])
    kk = k * kk_ref[...]
    kk_norm = jnp.sqrt(_head_sums(kk * kk, bd))
    kk = kk / jnp.maximum(kk_norm, 1e-12)
    kd = k * (1.0 + (a_sig - 1.0) * ka_ref[...])
    a_vec = -kk
    b_vec = kk * a_sig
    ri = lax.broadcasted_iota(jnp.int32, (rows, rows), 0)
    rj = lax.broadcasted_iota(jnp.int32, (rows, rows), 1)
    ordered = (rj >= ri) if reverse else (rj <= ri)
    tri = jnp.where((ri // c) == (rj // c), jnp.where(ordered, 1.0, 0.0), 0.0).astype(BF16)
    cum = _dot2_lhs01(tri, logw)
    edge = 0 if reverse else c - 1
    tot = jnp.concatenate([jnp.broadcast_to(cum[b * c + edge:b * c + edge + 1, :], (c, RW_WIDTH))
                           for b in range(bt)], axis=0)
    w_inv = jnp.exp(-cum)
    w_end = jnp.exp(tot - cum)
    w_tot = jnp.exp(tot)
    full_ops = dict(r_t=r * jnp.exp(cum), a_t=a_vec * jnp.exp(cum - logw), b_t=b_vec * w_inv, k_t=kd * w_inv,
                    b_h=b_vec * w_end, k_h=kd * w_end, v=v)
    ops, states = [], []
    for b in range(bt):
        for pi in range(N_PAIRS):
            ln = slice(pi * PAIR, (pi + 1) * PAIR)
            unit = {name: t[b * c:(b + 1) * c, ln] for name, t in full_ops.items()}
            unit["w_tot"] = w_tot[b * c:b * c + 1, ln]
            ops.append(unit)
            states.append(state_ref[b, pi])
    new_states, ys = _units_chunk(states, ops, m)
    for b in range(bt):
        for pi in range(N_PAIRS):
            state_ref[b, pi] = new_states[b * N_PAIRS + pi]
    y = jnp.concatenate([jnp.concatenate(ys[b * N_PAIRS:(b + 1) * N_PAIRS], axis=1) for b in range(bt)], axis=0)
    if not reverse:
        out_ref[...] = y.reshape(bt, c, RW_WIDTH)
    else:
        y = y + yf_ref[...].reshape(rows, RW_WIDTH)
        a_other = _sigmoid(a0o_ref[...] + proj[:, 2 * RW_WIDTH:3 * RW_WIDTH])
        kd_sum = kd + k * (1.0 + (a_other - 1.0) * ka_ref[...])
        bonus = _head_sums(r * kd_sum * rk_ref[...], bd) * v
        inv_n = 1.0 / RW_HEAD_DIM
        mean = _head_sums(y, bd) * inv_n
        yc = y - mean
        var = _head_sums(yc * yc, bd) * inv_n
        yn = yc * lax.rsqrt(var + GN_EPS) * gng_ref[...] + gnb_ref[...]
        g = proj[:, 3 * RW_WIDTH:]
        out_ref[...] = ((yn + bonus) * g).astype(BF16).reshape(bt, c, RW_WIDTH)


def _rwkv_pass(z3, yf, p, reverse, bt):
    b_sz, t_len, _ = z3.shape
    c = RW_CHUNK
    n_chunks = t_len // c
    halo_per_chunk = c // SUBLANES
    n_halo = t_len // SUBLANES

    def cidx(ci):
        return (n_chunks - 1 - ci) if reverse else ci

    full = lambda shape: pl.BlockSpec(shape, lambda bi, ci: (0,) * len(shape))
    row512 = full((1, RW_WIDTH))
    d = "rev" if reverse else "fwd"
    n_proj = p["lora_hi_" + d].shape[1]
    in_specs = [
        pl.BlockSpec((bt, c, Z_COLS), lambda bi, ci: (bi, cidx(ci), 0)),
        pl.BlockSpec((bt, SUBLANES, Z_COLS), lambda bi, ci: (bi, jnp.maximum(cidx(ci) * halo_per_chunk - 1, 0), 0)),
        pl.BlockSpec((bt, SUBLANES, Z_COLS),
                     lambda bi, ci: (bi, jnp.minimum((cidx(ci) + 1) * halo_per_chunk, n_halo - 1), 0)),
    ]
    args = [z3, z3, z3]
    if reverse:
        in_specs.append(pl.BlockSpec((bt, c, RW_WIDTH), lambda bi, ci: (bi, cidx(ci), 0)))
        args.append(yf)
    in_specs += [full((2, Z_COLS)), full((LORA_PAD, n_proj)), full((LORA_PAD, n_proj)), row512, row512]
    args += [p["rw_mu"], p["lora_hi_" + d], p["lora_lo_" + d], p["w0_" + d], p["a0_" + d]]
    if reverse:
        in_specs.append(row512)
        args.append(p["a0_fwd"])
    in_specs += [row512, row512]
    args += [p["rw_k_k"], p["rw_k_a"]]
    if reverse:
        in_specs.append(row512)
        args.append(p["rw_r_k"])
    in_specs.append(full((PAIR, PAIR)))
    args.append(p["head_ones"])
    if reverse:
        in_specs += [row512, row512]
        args += [p["rw_gn_g"], p["rw_gn_b"]]
    return pl.pallas_call(
        functools.partial(_rwkv_body, reverse, n_chunks, bt),
        grid=(b_sz // bt, n_chunks),
        in_specs=in_specs,
        out_specs=pl.BlockSpec((bt, c, RW_WIDTH), lambda bi, ci: (bi, cidx(ci), 0)),
        out_shape=jax.ShapeDtypeStruct((b_sz, t_len, RW_WIDTH), BF16 if reverse else F32),
        scratch_shapes=[pltpu.VMEM((bt, N_PAIRS, PAIR, PAIR), F32)],
        compiler_params=pltpu.CompilerParams(vmem_limit_bytes=VMEM_LIMIT),
        name="rwkv_" + d,
    )(*args)


def _mix_out_body(x_ref, gm_ref, rw_ref, wo_ref, g2_ref, rw_hi_ref, rw_lo_ref, rb_ref, earlier_ref,
                  x1_ref, h2_ref, route_ref, route_t_ref, counts_ref, carry_ref):
    i = pl.program_id(0)

    @pl.when(i == 0)
    def _():
        carry_ref[...] = jnp.zeros_like(carry_ref)

    x1 = x_ref[...] + _dot(gm_ref[...], wo_ref[:GM_WIDTH, :]) + _dot(rw_ref[...], wo_ref[GM_WIDTH:, :])
    x1_ref[...] = x1
    h2 = x1 * lax.rsqrt(jnp.mean(x1 * x1, axis=-1, keepdims=True) + RMS_EPS) * g2_ref[...]
    h2_ref[...] = _pack_bf16_halves(h2)
    tm = x1.shape[0]
    h_hi, h_lo = _split(h2)
    logits = (_dot_nt(rw_hi_ref[...], h_hi) + _dot_nt(rw_hi_ref[...], h_lo) + _dot_nt(rw_lo_ref[...], h_hi)
              + rb_ref[...])
    expert = lax.broadcasted_iota(jnp.int32, (N_EXPERTS, tm), 0).astype(F32)
    work = logits
    vals, ids, hits = [], [], []
    onehot = jnp.zeros((N_EXPERTS, tm), F32)
    for _ in range(TOP_K):
        mx = jnp.max(work, axis=0, keepdims=True)
        idx = jnp.min(jnp.where(work == mx, expert, float(N_EXPERTS)), axis=0, keepdims=True)
        hit = expert == idx
        vals.append(mx)
        ids.append(idx)
        hits.append(hit)
        onehot = jnp.where(hit, 1.0, onehot)
        work = jnp.where(hit, -jnp.inf, work)
    exps = [jnp.exp(vk - vals[0]) for vk in vals]
    denom = exps[0] + exps[1] + exps[2] + exps[3]
    carry = carry_ref[:, 0:1]
    ranks = carry + _dot(onehot.astype(BF16), earlier_ref[...])
    rank_rows = [jnp.sum(jnp.where(hit, ranks, 0.0), axis=0, keepdims=True) for hit in hits]
    route_t = jnp.concatenate(ids + [e / denom for e in exps] + rank_rows
                              + [jnp.zeros((ROUTE_ROWS - 3 * TOP_K, tm), F32)], axis=0)
    route_t_ref[...] = route_t
    route_ref[...] = jnp.concatenate([route_t, jnp.zeros((LANES - ROUTE_ROWS, tm), F32)], axis=0).T
    new_counts = carry + jnp.sum(onehot, axis=1, keepdims=True)
    carry_ref[...] = jnp.broadcast_to(new_counts, carry_ref.shape)
    counts_ref[...] = jnp.broadcast_to(new_counts, counts_ref.shape)


def _mix_out(x2d, gm, rw, p, tm):
    n = x2d.shape[0]
    full = lambda shape: pl.BlockSpec(shape, lambda i: (0,) * len(shape))
    tile = lambda w: pl.BlockSpec((tm, w), lambda i: (i, 0))
    earlier = (jnp.arange(tm)[:, None] < jnp.arange(tm)[None, :]).astype(BF16)
    return pl.pallas_call(
        _mix_out_body,
        grid=(n // tm,),
        in_specs=[tile(D_MODEL), tile(GM_WIDTH), tile(RW_WIDTH), full((D_MODEL, D_MODEL)), full((1, D_MODEL)),
                  full((N_EXPERTS, D_MODEL)), full((N_EXPERTS, D_MODEL)), full((N_EXPERTS, 1)), full((tm, tm))],
        out_specs=[tile(D_MODEL), tile(D_MODEL // 2), tile(LANES), pl.BlockSpec((ROUTE_ROWS, tm), lambda i: (0, i)),
                   full((N_EXPERTS, LANES))],
        out_shape=[
            jax.ShapeDtypeStruct((n, D_MODEL), F32),
            jax.ShapeDtypeStruct((n, D_MODEL // 2), jnp.int32),
            jax.ShapeDtypeStruct((n, LANES), F32),
            jax.ShapeDtypeStruct((ROUTE_ROWS, n), F32),
            jax.ShapeDtypeStruct((N_EXPERTS, LANES), F32),
        ],
        scratch_shapes=[pltpu.VMEM((N_EXPERTS, LANES), F32)],
        compiler_params=pltpu.CompilerParams(vmem_limit_bytes=VMEM_LIMIT),
        name="mix_out",
    )(x2d, gm, rw, p["w_out"], p["norm2_g"], p["router_hi"], p["router_lo"], p["router_b"], earlier)


def _sc_gather_rows(table, idx):
    n_idx = idx.shape[0]
    width = table.shape[1]
    n_workers = SC_CORES * SC_SUBCORES
    per_worker = n_idx // n_workers
    n_windows = per_worker // SC_WINDOW
    assert per_worker * n_workers == n_idx and n_windows * SC_WINDOW == per_worker
    mesh = plsc.VectorSubcoreMesh(core_axis_name="c", subcore_axis_name="s")

    @functools.partial(
        pl.kernel, mesh=mesh,
        out_type=jax.ShapeDtypeStruct((n_idx, width), table.dtype),
        scratch_types=[
            pltpu.VMEM((SC_WINDOW,), jnp.int32),
            pltpu.VMEM((SC_WINDOW, width), table.dtype),
            pltpu.SemaphoreType.DMA,
        ],
        name="sc_gather_rows",
    )
    def gather(table_hbm, idx_hbm, out_hbm, idx_v, rows_v, sem):
        worker = lax.axis_index("s") * SC_CORES + lax.axis_index("c")
        base = worker * per_worker

        @pl.loop(0, n_windows)
        def _(j):
            off = base + j * SC_WINDOW
            pltpu.sync_copy(idx_hbm.at[pl.ds(off, SC_WINDOW)], idx_v)
            pltpu.async_copy(table_hbm.at[idx_v], rows_v, sem).wait()
            pltpu.sync_copy(rows_v, out_hbm.at[pl.ds(off, SC_WINDOW)])

    return gather(table, idx)


def _sc_scatter_rows(src, dest_w, n_out):
    n, width = src.shape
    n_windows, top_k, window = dest_w.shape
    n_workers = SC_CORES * SC_SUBCORES
    per_worker = n_windows // n_workers
    assert n_windows * window == n and per_worker * n_workers == n_windows
    mesh = plsc.VectorSubcoreMesh(core_axis_name="c", subcore_axis_name="s")

    @functools.partial(
        pl.kernel, mesh=mesh,
        out_type=jax.ShapeDtypeStruct((n_out, width), src.dtype),
        scratch_types=[
            pltpu.VMEM((top_k, window), jnp.int32),
            pltpu.VMEM((window, width), src.dtype),
        ],
        name="sc_scatter_rows",
    )
    def scatter(src_hbm, dest_hbm, out_hbm, idx_v, rows_v):
        worker = lax.axis_index("s") * SC_CORES + lax.axis_index("c")

        @pl.loop(0, per_worker)
        def _(j):
            g = worker * per_worker + j
            pltpu.sync_copy(dest_hbm.at[g], idx_v)
            pltpu.sync_copy(src_hbm.at[pl.ds(g * window, window)], rows_v)
            for k in range(top_k):
                pltpu.sync_copy(rows_v, out_hbm.at[idx_v.at[k]])

    return scatter(src, dest_w)


def _experts_body(blk_e_ref, n_used_ref, valid_ref, x_ref, wgu_ref, bgu_ref, wd_ref, bd_ref, out_ref,
                  wgu_bf, wd_bf):
    i = pl.program_id(0)
    n_used = n_used_ref[0]

    @pl.when((i == 0) | (blk_e_ref[i] != blk_e_ref[jnp.maximum(i - 1, 0)]))
    def _():
        wgu_bf[...] = wgu_ref[0].astype(BF16)
        wd_bf[...] = wd_ref[0].astype(BF16)

    @pl.when(i < n_used)
    def _():
        row = lax.broadcasted_iota(jnp.int32, (MOE_ROWS, 1), 0)
        x_left, x_right = _unpack_bf16_halves(jnp.where(row < valid_ref[i], x_ref[...], 0))
        half = D_MODEL // 2
        gu = (_dot(x_left.astype(BF16), wgu_bf[:half, :]) + _dot(x_right.astype(BF16), wgu_bf[half:, :])
              + bgu_ref[0])
        gate = jnp.minimum(gu[:, :D_EXPERT], SWIGLU_LIMIT)
        up = jnp.clip(gu[:, D_EXPERT:], -SWIGLU_LIMIT, SWIGLU_LIMIT)
        act = gate * _sigmoid(gate * SWIGLU_ALPHA) * (up + 1.0)
        out_ref[...] = _pack_bf16_halves(_dot(act.astype(BF16), wd_bf[...]) + bd_ref[0])

    @pl.when(i >= n_used)
    def _():
        out_ref[...] = jnp.zeros_like(out_ref)


def _experts(x_rows, blk_e, n_used, valid, p):
    n_blocks = blk_e.shape[0]
    grid_spec = pltpu.PrefetchScalarGridSpec(
        num_scalar_prefetch=3,
        grid=(n_blocks,),
        in_specs=[
            pl.BlockSpec((MOE_ROWS, D_MODEL // 2), lambda i, be, nu, va: (jnp.minimum(i, nu[0] - 1), 0)),
            pl.BlockSpec((1, D_MODEL, 2 * D_EXPERT), lambda i, be, nu, va: (be[i], 0, 0)),
            pl.BlockSpec((1, 1, 2 * D_EXPERT), lambda i, be, nu, va: (be[i], 0, 0)),
            pl.BlockSpec((1, D_EXPERT, D_MODEL), lambda i, be, nu, va: (be[i], 0, 0)),
            pl.BlockSpec((1, 1, D_MODEL), lambda i, be, nu, va: (be[i], 0, 0)),
        ],
        out_specs=pl.BlockSpec((MOE_ROWS, D_MODEL // 2), lambda i, be, nu, va: (i, 0)),
        scratch_shapes=[pltpu.VMEM((D_MODEL, 2 * D_EXPERT), BF16), pltpu.VMEM((D_EXPERT, D_MODEL), BF16)],
    )
    return pl.pallas_call(
        _experts_body,
        grid_spec=grid_spec,
        out_shape=jax.ShapeDtypeStruct((n_blocks * MOE_ROWS, D_MODEL // 2), jnp.int32),
        compiler_params=pltpu.CompilerParams(vmem_limit_bytes=VMEM_LIMIT),
        name="experts",
    )(blk_e, n_used, valid, x_rows, p["w_gu"], p["b_gu"], p["w_down"], p["b_down"])


def _combine_body(yg_ref, x1_ref, gates_ref, fg_ref, out_ref):
    gates = gates_ref[...]
    half = D_MODEL // 2
    acc_left = x1_ref[:, :half]
    acc_right = x1_ref[:, half:]
    for kk in range(TOP_K):
        y_left, y_right = _unpack_bf16_halves(yg_ref[kk])
        gate = gates[:, TOP_K + kk:TOP_K + kk + 1]
        acc_left = acc_left + gate * y_left
        acc_right = acc_right + gate * y_right
    acc = jnp.concatenate([acc_left, acc_right], axis=1)
    out_ref[...] = acc * lax.rsqrt(jnp.mean(acc * acc, axis=-1, keepdims=True) + RMS_EPS) * fg_ref[...]


def _combine(yg, x1, gates, final_g, tm):
    n = x1.shape[0]
    return pl.pallas_call(
        _combine_body,
        grid=(n // tm,),
        in_specs=[
            pl.BlockSpec((TOP_K, tm, D_MODEL // 2), lambda i: (0, i, 0)),
            pl.BlockSpec((tm, D_MODEL), lambda i: (i, 0)),
            pl.BlockSpec((tm, LANES), lambda i: (i, 0)),
            pl.BlockSpec((1, D_MODEL), lambda i: (0, 0)),
        ],
        out_specs=pl.BlockSpec((tm, D_MODEL), lambda i: (i, 0)),
        out_shape=jax.ShapeDtypeStruct((n, D_MODEL), F32),
        compiler_params=pltpu.CompilerParams(vmem_limit_bytes=VMEM_LIMIT),
        name="combine",
    )(yg, x1, gates, final_g)


def _prepare(norm1_g, w_in, rw_mu, gm_ln_g, gm_ln_b, gm_ws, gm_bs, rw_w0, rw_w2, rw_a0, rw_a2, rw_g2,
             rw_k_k, rw_k_a, rw_r_k, rw_gn_g, rw_gn_b, w_out, norm2_g, router_w, router_b,
             w_gu, b_gu, w_down, b_down, final_g):
    l = 0
    row = lambda t: t.reshape(1, -1).astype(F32)
    p = {}
    p["norm1_g"] = row(norm1_g[l])
    w = w_in[l]
    p["w_gm"] = w[:, :2 * GM_WIDTH].astype(BF16)
    pad_cols = LORA_PAD - LORA_COLS
    p["w_z"] = jnp.pad(w[:, 2 * GM_WIDTH:], ((0, 0), (0, pad_cols))).astype(BF16)
    mu = jnp.pad(row(rw_mu[l]), ((0, 0), (0, pad_cols)))
    p["rw_mu"] = jnp.concatenate([1.0 - mu, 0.5 * mu], axis=0)
    p["gm_ln_g"] = row(gm_ln_g[l])
    p["gm_ln_b"] = row(gm_ln_b[l])
    p["gm_ws"] = gm_ws[l].astype(BF16)
    p["gm_bs"] = jnp.broadcast_to(gm_bs[l][:, :, None], (GM_HEADS, GM_CHUNK, GM_HEAD_DIM)).astype(F32)
    zeros = lambda r, c: jnp.zeros((r, c), F32)
    o_a = DECAY_LORA
    o_g = DECAY_LORA + ICLR_LORA

    def lora_matrix(d, with_epilogue):
        blocks = [jnp.concatenate([rw_w2[l, d], zeros(LORA_PAD - DECAY_LORA, RW_WIDTH)], axis=0),
                  jnp.concatenate([zeros(o_a, RW_WIDTH), rw_a2[l, d], zeros(LORA_PAD - o_g, RW_WIDTH)], axis=0)]
        if with_epilogue:
            blocks.append(jnp.concatenate([zeros(o_a, RW_WIDTH), rw_a2[l, 1 - d], zeros(LORA_PAD - o_g, RW_WIDTH)],
                                          axis=0))
            blocks.append(jnp.concatenate([zeros(o_g, RW_WIDTH), rw_g2[l], zeros(LORA_PAD - LORA_COLS, RW_WIDTH)],
                                          axis=0))
        return jnp.concatenate(blocks, axis=1)

    for d, name in ((0, "fwd"), (1, "rev")):
        mat = lora_matrix(d, with_epilogue=(d == 1))
        hi = mat.astype(BF16)
        p["lora_hi_" + name] = hi
        p["lora_lo_" + name] = (mat - hi.astype(F32)).astype(BF16)
        p["w0_" + name] = row(rw_w0[l, d])
        p["a0_" + name] = row(rw_a0[l, d])
    p["rw_k_k"] = row(rw_k_k[l])
    p["rw_k_a"] = row(rw_k_a[l])
    p["rw_r_k"] = row(rw_r_k[l])
    p["rw_gn_g"] = row(rw_gn_g[l])
    p["rw_gn_b"] = row(rw_gn_b[l])
    ch = jnp.arange(PAIR) // RW_HEAD_DIM
    p["head_ones"] = (ch[:, None] == ch[None, :]).astype(BF16)
    p["w_out"] = w_out[l].astype(BF16)
    p["norm2_g"] = row(norm2_g[l])
    rw_t = router_w[l].astype(F32).T
    hi = rw_t.astype(BF16)
    p["router_hi"] = hi
    p["router_lo"] = (rw_t - hi.astype(F32)).astype(BF16)
    p["router_b"] = router_b[l].astype(F32).reshape(N_EXPERTS, 1)
    p["w_gu"] = w_gu[l]
    p["b_gu"] = b_gu[l].reshape(N_EXPERTS, 1, 2 * D_EXPERT).astype(F32)
    p["w_down"] = w_down[l]
    p["b_down"] = b_down[l].reshape(N_EXPERTS, 1, D_MODEL).astype(F32)
    p["final_g"] = row(final_g)
    return p


def _pick_tile(n, want):
    t = want
    while n % t:
        t //= 2
    return t


def _encoder(x, p):
    b_sz, t_len, d = x.shape
    n = b_sz * t_len
    x2d = x.reshape(n, d)
    tm = _pick_tile(n, 512)
    gm, z = _mix_in(x2d, p, tm)
    z3 = z.reshape(b_sz, t_len, Z_COLS)
    yf = _rwkv_pass(z3, None, p, reverse=False, bt=RW_BATCH_TILE)
    rw = _rwkv_pass(z3, yf, p, reverse=True, bt=RW_BATCH_TILE)
    tmo = _pick_tile(n, 512)
    x1, h2, route, route_t, counts = _mix_out(x2d, gm, rw.reshape(n, RW_WIDTH), p, tmo)
    ids = route_t[:TOP_K].astype(jnp.int32)
    ranks = route_t[2 * TOP_K:3 * TOP_K].astype(jnp.int32)
    cnt = counts[:, 0].astype(jnp.int32)
    padded = (cnt + MOE_ROWS - 1) // MOE_ROWS * MOE_ROWS
    pad_end = jnp.cumsum(padded)
    pad_start = pad_end - padded
    expert_ids = jnp.arange(N_EXPERTS, dtype=jnp.int32)
    start_of = jnp.sum(jnp.where(ids[..., None] == expert_ids, pad_start, 0), axis=-1)
    dest = start_of + ranks
    n_blocks = n * TOP_K // MOE_ROWS + N_EXPERTS
    blk_start = jnp.arange(n_blocks, dtype=jnp.int32) * MOE_ROWS
    blk_e = jnp.minimum(jnp.sum((pad_end[None, :] <= blk_start[:, None]).astype(jnp.int32), axis=1), N_EXPERTS - 1)
    n_used = (pad_end[-1] // MOE_ROWS).astype(jnp.int32).reshape(1)
    blk_onehot = blk_e[:, None] == expert_ids
    blk_cnt = jnp.sum(jnp.where(blk_onehot, cnt, 0), axis=-1)
    blk_first = jnp.sum(jnp.where(blk_onehot, pad_start, 0), axis=-1)
    valid = jnp.clip(blk_cnt - (blk_start - blk_first), 0, MOE_ROWS).astype(jnp.int32)
    window = min(SC_SCATTER_WINDOW, n // (SC_CORES * SC_SUBCORES))
    dest_w = dest.reshape(TOP_K, n // window, window).transpose(1, 0, 2)
    x_rows = _sc_scatter_rows(h2, dest_w, n_blocks * MOE_ROWS)
    y_rows = _experts(x_rows, blk_e, n_used, valid, p)
    yg = _sc_gather_rows(y_rows, dest.reshape(-1)).reshape(TOP_K, n, D_MODEL // 2)
    out = _combine(yg, x1, route, p["final_g"], _pick_tile(n, 256))
    return out.reshape(b_sz, t_len, d)


def kernel(x_prompt, x_sample, norm1_g, w_in, rw_mu, gm_ln_g, gm_ln_b, gm_ws, gm_bs, rw_w0, rw_w2, rw_a0, rw_a2,
           rw_g2, rw_k_k, rw_k_a, rw_r_k, rw_gn_g, rw_gn_b, w_out, norm2_g, router_w, router_b, w_gu, b_gu,
           w_down, b_down, final_g):
    p = _prepare(norm1_g, w_in, rw_mu, gm_ln_g, gm_ln_b, gm_ws, gm_bs, rw_w0, rw_w2, rw_a0, rw_a2, rw_g2,
                 rw_k_k, rw_k_a, rw_r_k, rw_gn_g, rw_gn_b, w_out, norm2_g, router_w, router_b,
                 w_gu, b_gu, w_down, b_down, final_g)
    return (_encoder(x_prompt, p), _encoder(x_sample, p))
```

```python
import functools

import jax
import jax.numpy as jnp
from jax import lax
from jax.experimental import pallas as pl
from jax.experimental.pallas import tpu as pltpu
from jax.experimental.pallas import tpu_sc as plsc

F32 = jnp.float32
BF16 = jnp.bfloat16

D_MODEL = 1024
GM_WIDTH = 512
RW_WIDTH = 512
GM_HEADS = 4
GM_HEAD_DIM = 128
GM_CHUNK = 128
RW_HEAD_DIM = 64
DECAY_LORA = 32
ICLR_LORA = 32
GATE_LORA = 96
LORA_COLS = DECAY_LORA + ICLR_LORA + GATE_LORA
LORA_PAD = 256
Z_COLS = 3 * RW_WIDTH + LORA_PAD
N_EXPERTS = 32
TOP_K = 4
D_EXPERT = 1024
SWIGLU_LIMIT = 7.0
SWIGLU_ALPHA = 1.702
RMS_EPS = 1e-5
LN_EPS = 1e-5
GN_EPS = 64e-5

LANES = 128
SUBLANES = 8
VMEM_LIMIT = 56 * 1024 * 1024

RW_CHUNK = 64
PAIR = 2 * RW_HEAD_DIM
N_PAIRS = RW_WIDTH // PAIR
RW_BATCH_TILE = 4
MOE_ROWS = 512
MOE_SUB_BLOCKS = 4
SC_CORES = 2
SC_SUBCORES = 16
SC_WINDOW = 64
SC_SCATTER_WINDOW = 128
ROUTE_ROWS = 16
DECAY_SCALE = 0.6065306597126334
KK_NORM_FLOOR_SQ = 1e-24


def _dot(a, b):
    return jnp.dot(a, b, preferred_element_type=F32)


def _dot_nt(a, b):
    return lax.dot_general(a, b, (((1,), (1,)), ((), ())), preferred_element_type=F32)


def _split(x):
    hi = x.astype(BF16)
    lo = (x - hi.astype(F32)).astype(BF16)
    return hi, lo


def _dot3(a, b_hi, b_lo):
    a_hi, a_lo = _split(a)
    return _dot(a_hi, b_hi) + _dot(a_lo, b_hi) + _dot(a_hi, b_lo)


def _dot2(a, b01):
    a_hi, a_lo = _split(a)
    return _dot(a_hi, b01) + _dot(a_lo, b01)


def _head_sums(x, pair_ones):
    rows = x.shape[0]
    n_tiles = x.shape[1] // LANES
    stacked = jnp.concatenate([x[:, t * LANES:(t + 1) * LANES] for t in range(n_tiles)], axis=0)
    sums = _dot2(stacked, pair_ones)
    return jnp.concatenate([sums[t * rows:(t + 1) * rows] for t in range(n_tiles)], axis=1)


def _dot2_lhs01(a01, b):
    b_hi, b_lo = _split(b)
    return _dot(a01, b_hi) + _dot(a01, b_lo)


def _pack_bf16_halves(x):
    w = x.shape[1] // 2
    hi = lax.bitcast_convert_type(x[:, :w].astype(BF16).astype(F32), jnp.int32)
    lo = lax.bitcast_convert_type(x[:, w:].astype(BF16).astype(F32), jnp.int32)
    return hi | lax.shift_right_logical(lo, 16)


def _unpack_bf16_halves(words):
    left = lax.bitcast_convert_type(words & jnp.int32(-65536), F32)
    right = lax.bitcast_convert_type(lax.shift_left(words, 16), F32)
    return left, right


def _gelu_tanh(x):
    return 0.5 * x * (1.0 + jnp.tanh(0.7978845608028654 * (x + 0.044715 * (x * x * x))))


def _sigmoid(x):
    return 1.0 / (1.0 + jnp.exp(-x))


def _mix_in_body(x_ref, g1_ref, wgm_ref, wz_ref, lng_ref, lnb_ref, ws_ref, bs_ref, gm_ref, z_ref):
    x = x_ref[...]
    h = x * lax.rsqrt(jnp.mean(x * x, axis=-1, keepdims=True) + RMS_EPS) * g1_ref[...]
    hb = h.astype(BF16)
    uv = _dot(hb, wgm_ref[...])
    z_ref[...] = _dot(hb, wz_ref[...])
    u = _gelu_tanh(uv[:, :GM_WIDTH])
    v = _gelu_tanh(uv[:, GM_WIDTH:])
    mean = jnp.mean(v, axis=-1, keepdims=True)
    vc = v - mean
    var = jnp.mean(vc * vc, axis=-1, keepdims=True)
    vn = (vc * lax.rsqrt(var + LN_EPS) * lng_ref[...] + lnb_ref[...]).astype(BF16)
    tm = x.shape[0]
    for c in range(tm // GM_CHUNK):
        rows = slice(c * GM_CHUNK, (c + 1) * GM_CHUNK)
        for hd in range(GM_HEADS):
            cols = slice(hd * GM_HEAD_DIM, (hd + 1) * GM_HEAD_DIM)
            mixed = _dot(ws_ref[hd], vn[rows, cols]) + bs_ref[hd]
            gm_ref[rows, cols] = (u[rows, cols] * mixed).astype(BF16)


def _mix_in(x2d, p, tm):
    n = x2d.shape[0]
    full = lambda shape: pl.BlockSpec(shape, lambda i: (0,) * len(shape))
    return pl.pallas_call(
        _mix_in_body,
        grid=(n // tm,),
        in_specs=[
            pl.BlockSpec((tm, D_MODEL), lambda i: (i, 0)),
            full((1, D_MODEL)),
            full((D_MODEL, 2 * GM_WIDTH)),
            full((D_MODEL, Z_COLS)),
            full((1, GM_WIDTH)),
            full((1, GM_WIDTH)),
            full((GM_HEADS, GM_CHUNK, GM_CHUNK)),
            full((GM_HEADS, GM_CHUNK, GM_HEAD_DIM)),
        ],
        out_specs=[
            pl.BlockSpec((tm, GM_WIDTH), lambda i: (i, 0)),
            pl.BlockSpec((tm, Z_COLS), lambda i: (i, 0)),
        ],
        out_shape=[
            jax.ShapeDtypeStruct((n, GM_WIDTH), BF16),
            jax.ShapeDtypeStruct((n, Z_COLS), F32),
        ],
        compiler_params=pltpu.CompilerParams(vmem_limit_bytes=VMEM_LIMIT),
        name="mix_in",
    )(x2d, p["norm1_g"], p["w_gm"], p["w_z"], p["gm_ln_g"], p["gm_ln_b"], p["gm_ws"], p["gm_bs"])


def _rwkv_masks(reverse):
    c = RW_CHUNK
    ti = lax.broadcasted_iota(jnp.int32, (c, c), 0)
    tj = lax.broadcasted_iota(jnp.int32, (c, c), 1)
    incl = (tj >= ti) if reverse else (tj <= ti)
    lane =lax.broadcasted_iota(jnp.int32, (c, PAIR), 1)
    head0 = lane < RW_HEAD_DIM
    lane2 = lax.broadcasted_iota(jnp.int32, (c, 2 * c), 1)
    left = lane2 < c
    ti2 = lax.broadcasted_iota(jnp.int32, (c, 2 * c), 0)
    tj2 = jnp.where(left, lane2, lane2 - c)
    incl2 = (tj2 >= ti2) if reverse else (tj2 <= ti2)
    strict2 = (tj2 > ti2) if reverse else (tj2 < ti2)
    eye2 = jnp.where(tj2 == ti2, 1.0, 0.0).astype(F32)
    si = lax.broadcasted_iota(jnp.int32, (PAIR, PAIR), 0)
    sj = lax.broadcasted_iota(jnp.int32, (PAIR, PAIR), 1)
    same_head = (si // RW_HEAD_DIM) == (sj // RW_HEAD_DIM)
    return dict(
        incl2=incl2, strict2=strict2, head0=head0, left=left, eye2=eye2, same_head=same_head,
        tri=jnp.where(incl, 1.0, 0.0).astype(BF16),
    )


def _stack_heads(x, head0):
    return jnp.concatenate([jnp.where(head0, x, 0.0), jnp.where(head0, 0.0, x)], axis=0)


def _block_diag(x, left):
    return jnp.concatenate([jnp.where(left, x, 0.0), jnp.where(left, 0.0, x)], axis=0)


def _units_chunk(states, ops, m):
    c = RW_CHUNK
    units = range(len(ops))
    bf = lambda t: t.astype(BF16)
    head0, left = m["head0"], m["left"]
    ar = [bf(jnp.concatenate([o["a_t"], o["r_t"]], axis=0)) for o in ops]
    rstack = [bf(jnp.concatenate([_stack_heads(o["b_t"], head0), _stack_heads(o["k_t"], head0)], axis=0))
              for o in ops]
    acat = [_dot_nt(ar[i], rstack[i]) for i in units]
    sa = [_dot_nt(ar[i], bf(states[i])) for i in units]
    a_ab = [jnp.where(m["strict2"], acat[i][:c, :2 * c], 0.0) for i in units]
    a_ak = [bf(jnp.where(m["strict2"], acat[i][:c, 2 * c:], 0.0)) for i in units]
    a_r = [bf(jnp.concatenate([jnp.where(m["incl2"], acat[i][c:, :2 * c], 0.0),
                               jnp.where(m["incl2"], acat[i][c:, 2 * c:], 0.0)], axis=1)) for i in units]
    p = [m["eye2"] + a_ab[i] for i in units]
    lk = [_dot(bf(a_ab[i]), bf(_block_diag(a_ab[i], left))) for i in units]
    vstack = [bf(_stack_heads(o["v"], head0)) for o in ops]
    rhs = [sa[i][:c] + _dot(a_ak[i], vstack[i]) for i in units]
    power = 2
    while 2 * power < c:
        res = [_dot(bf(jnp.concatenate([p[i], lk[i]], axis=0)), bf(_block_diag(lk[i], left))) for i in units]
        p = [p[i] + res[i][:c] for i in units]
        lk = [res[i][c:] for i in units]
        power *= 2
    p = [p[i] + _dot(bf(p[i]), bf(_block_diag(lk[i], left))) for i in units]
    u = [_dot(bf(p[i]), bf(_stack_heads(rhs[i], head0))) for i in units]
    ys = [sa[i][c:] + _dot(a_r[i], jnp.concatenate([bf(_stack_heads(u[i], head0)), vstack[i]], axis=0))
          for i in units]
    upd = [_dot(bf(jnp.concatenate([u[i], ops[i]["v"]], axis=0).T),
                bf(jnp.concatenate([ops[i]["b_h"], ops[i]["k_h"]], axis=0))) for i in units]
    new_states = [states[i] * ops[i]["w_tot"] + jnp.where(m["same_head"], upd[i], 0.0) for i in units]
    return new_states, ys


def _rwkv_body(reverse, n_chunks, bt, *refs):
    if reverse:
        (zc_ref, zp_ref, zn_ref, yf_ref, mu_ref, lw_hi_ref, lw_lo_ref, w0_ref, a0_ref, a0o_ref, kk_ref, ka_ref,
         rk_ref, bd_ref, gng_ref, gnb_ref, out_ref, state_ref) = refs
    else:
        (zc_ref, zp_ref, zn_ref, mu_ref, lw_hi_ref, lw_lo_ref, w0_ref, a0_ref, kk_ref, ka_ref,
         bd_ref, out_ref, state_ref) = refs
    c = RW_CHUNK
    ci = pl.program_id(1)
    cidx = (n_chunks - 1 - ci) if reverse else ci

    @pl.when(ci == 0)
    def _():
        state_ref[...] = jnp.zeros_like(state_ref)

    rows = bt * c
    m = _rwkv_masks(reverse)
    row8 = lax.broadcasted_iota(jnp.int32, (SUBLANES, Z_COLS), 0)
    lora_lane = lax.broadcasted_iota(jnp.int32, (rows, LORA_PAD), 1)
    bd = bd_ref[...]
    keep = mu_ref[0:1, :]
    mix = mu_ref[1:2, :]
    zs = []
    for b in range(bt):
        z = zc_ref[b]
        prev_row = jnp.where(cidx > 0, zp_ref[b, SUBLANES - 1:SUBLANES, :], 0.0)
        next_row = jnp.where(cidx < n_chunks - 1, zn_ref[b, 0:1, :], 0.0)
        down = pltpu.roll(z, 1, axis=0)
        up = pltpu.roll(z, c - 1, axis=0)
        prev = jnp.concatenate([jnp.where(row8 == 0, prev_row, down[:SUBLANES]), down[SUBLANES:]], axis=0)
        nxt = jnp.concatenate([up[:c - SUBLANES], jnp.where(row8 == SUBLANES - 1, next_row, up[c - SUBLANES:])],
                              axis=0)
        zs.append(z * keep + (prev + nxt) * mix)
    zf = jnp.concatenate(zs, axis=0)
    r = zf[:, :RW_WIDTH]
    k = zf[:, RW_WIDTH:2 * RW_WIDTH]
    v = zf[:, 2 * RW_WIDTH:3 * RW_WIDTH]
    lo = zf[:, 3 * RW_WIDTH:]
    act = jnp.where(lora_lane < DECAY_LORA, jnp.tanh(lo),
                    jnp.where(lora_lane < DECAY_LORA + ICLR_LORA, lo, _sigmoid(lo)))
    proj = _dot3(act, lw_hi_ref[...], lw_lo_ref[...])
    w_in = w0_ref[...] + proj[:, :RW_WIDTH]
    logw = -DECAY_SCALE * _sigmoid(w_in)
    a_sig = _sigmoid(a0_ref[...] + proj[:, RW_WIDTH:2 * RW_WIDTH])
    kk = k * kk_ref[...]
    kk = kk * lax.rsqrt(jnp.maximum(_head_sums(kk * kk, bd), KK_NORM_FLOOR_SQ))
    kd = k * (1.0 + (a_sig - 1.0) * ka_ref[...])
    a_vec = -kk
    b_vec = kk * a_sig
    ri = lax.broadcasted_iota(jnp.int32, (rows, rows), 0)
    rj = lax.broadcasted_iota(jnp.int32, (rows, rows), 1)
    ordered = (rj >= ri) if reverse else (rj <= ri)
    tri = jnp.where((ri // c) == (rj // c), jnp.where(ordered, 1.0, 0.0), 0.0).astype(BF16)
    cum = _dot2_lhs01(tri, logw)
    edge = 0 if reverse else c - 1
    tot = jnp.concatenate([jnp.broadcast_to(cum[b * c + edge:b * c + edge + 1, :], (c, RW_WIDTH))
                           for b in range(bt)], axis=0)
    w_inv = jnp.exp(-cum)
    w_end = jnp.exp(tot - cum)
    w_tot = jnp.exp(tot)
    full_ops = dict(r_t=r * jnp.exp(cum), a_t=a_vec * jnp.exp(cum - logw), b_t=b_vec * w_inv, k_t=kd * w_inv,
                    b_h=b_vec * w_end, k_h=kd * w_end, v=v)
    ops, states = [], []
    for b in range(bt):
        for pi in range(N_PAIRS):
            ln = slice(pi * PAIR, (pi + 1) * PAIR)
            unit = {name: t[b * c:(b + 1) * c, ln] for name, t in full_ops.items()}
            unit["w_tot"] = w_tot[b * c:b * c + 1, ln]
            ops.append(unit)
            states.append(state_ref[b, pi])
    new_states, ys = _units_chunk(states, ops, m)
    for b in range(bt):
        for pi in range(N_PAIRS):
            state_ref[b, pi] = new_states[b * N_PAIRS + pi]
    y = jnp.concatenate([jnp.concatenate(ys[b * N_PAIRS:(b + 1) * N_PAIRS], axis=1) for b in range(bt)], axis=0)
    if not reverse:
        out_ref[...] = y.reshape(bt, c, RW_WIDTH)
    else:
        y = y + yf_ref[...].reshape(rows, RW_WIDTH)
        a_other = _sigmoid(a0o_ref[...] + proj[:, 2 * RW_WIDTH:3 * RW_WIDTH])
        kd_sum = kd + k * (1.0 + (a_other - 1.0) * ka_ref[...])
        bonus = _head_sums(r * kd_sum * rk_ref[...], bd) * v
        inv_n = 1.0 / RW_HEAD_DIM
        mean = _head_sums(y, bd) * inv_n
        yc = y - mean
        var = _head_sums(yc * yc, bd) * inv_n
        yn = yc * lax.rsqrt(var + GN_EPS) * gng_ref[...] + gnb_ref[...]
        g = proj[:, 3 * RW_WIDTH:]
        out_ref[...] = ((yn + bonus) * g).astype(BF16).reshape(bt, c, RW_WIDTH)


def _rwkv_pass(z3, yf, p, reverse, bt):
    b_sz, t_len, _ = z3.shape
    c = RW_CHUNK
    n_chunks = t_len // c
    halo_per_chunk = c // SUBLANES
    n_halo = t_len // SUBLANES

    def cidx(ci):
        return (n_chunks - 1 - ci) if reverse else ci

    full = lambda shape: pl.BlockSpec(shape, lambda bi, ci: (0,) * len(shape))
    row512 = full((1, RW_WIDTH))
    d = "rev" if reverse else "fwd"
    n_proj = p["lora_hi_" + d].shape[1]
    in_specs = [
        pl.BlockSpec((bt, c, Z_COLS), lambda bi, ci: (bi, cidx(ci), 0)),
        pl.BlockSpec((bt, SUBLANES, Z_COLS), lambda bi, ci: (bi, jnp.maximum(cidx(ci) * halo_per_chunk - 1, 0), 0)),
        pl.BlockSpec((bt, SUBLANES, Z_COLS),
                     lambda bi, ci: (bi, jnp.minimum((cidx(ci) + 1) * halo_per_chunk, n_halo - 1), 0)),
    ]
    args = [z3, z3, z3]
    if reverse:
        in_specs.append(pl.BlockSpec((bt, c, RW_WIDTH), lambda bi, ci: (bi, cidx(ci), 0)))
        args.append(yf)
    in_specs += [full((2, Z_COLS)), full((LORA_PAD, n_proj)), full((LORA_PAD, n_proj)), row512, row512]
    args += [p["rw_mu"], p["lora_hi_" + d], p["lora_lo_" + d], p["w0_" + d], p["a0_" + d]]
    if reverse:
        in_specs.append(row512)
        args.append(p["a0_fwd"])
    in_specs += [row512, row512]
    args += [p["rw_k_k"], p["rw_k_a"]]
    if reverse:
        in_specs.append(row512)
        args.append(p["rw_r_k"])
    in_specs.append(full((PAIR, PAIR)))
    args.append(p["head_ones"])
    if reverse:
        in_specs += [row512, row512]
        args += [p["rw_gn_g"], p["rw_gn_b"]]
    return pl.pallas_call(
        functools.partial(_rwkv_body, reverse, n_chunks, bt),
        grid=(b_sz // bt, n_chunks),
        in_specs=in_specs,
        out_specs=pl.BlockSpec((bt, c, RW_WIDTH), lambda bi, ci: (bi, cidx(ci), 0)),
        out_shape=jax.ShapeDtypeStruct((b_sz, t_len, RW_WIDTH), BF16 if reverse else F32),
        scratch_shapes=[pltpu.VMEM((bt, N_PAIRS, PAIR, PAIR), F32)],
        compiler_params=pltpu.CompilerParams(vmem_limit_bytes=VMEM_LIMIT),
        name="rwkv_" + d,
    )(*args)


def _mix_out_body(x_ref, gm_ref, rw_ref, wo_ref, g2_ref, rw_hi_ref, rw_lo_ref, rb_ref, earlier_ref,
                  x1_ref, h2_ref, route_ref, route_t_ref, counts_ref, carry_ref):
    i = pl.program_id(0)

    @pl.when(i == 0)
    def _():
        carry_ref[...] = jnp.zeros_like(carry_ref)

    x1 = x_ref[...] + _dot(gm_ref[...], wo_ref[:GM_WIDTH, :]) + _dot(rw_ref[...], wo_ref[GM_WIDTH:, :])
    x1_ref[...] = x1
    h2 = x1 * lax.rsqrt(jnp.mean(x1 * x1, axis=-1, keepdims=True) + RMS_EPS) * g2_ref[...]
    h2_ref[...] = _pack_bf16_halves(h2)
    tm = x1.shape[0]
    h_hi, h_lo = _split(h2)
    logits = (_dot_nt(rw_hi_ref[...], h_hi) + _dot_nt(rw_hi_ref[...], h_lo) + _dot_nt(rw_lo_ref[...], h_hi)
              + rb_ref[...])
    expert = lax.broadcasted_iota(jnp.int32, (N_EXPERTS, tm), 0).astype(F32)
    work = logits
    vals, ids, hits = [], [], []
    onehot = jnp.zeros((N_EXPERTS, tm), F32)
    for _ in range(TOP_K):
        mx = jnp.max(work, axis=0, keepdims=True)
        idx = jnp.min(jnp.where(work == mx, expert, float(N_EXPERTS)), axis=0, keepdims=True)
        hit = expert == idx
        vals.append(mx)
        ids.append(idx)
        hits.append(hit)
        onehot = jnp.where(hit, 1.0, onehot)
        work = jnp.where(hit, -jnp.inf, work)
    exps = [jnp.exp(vk - vals[0]) for vk in vals]
    denom = exps[0] + exps[1] + exps[2] + exps[3]
    carry = carry_ref[:, 0:1]
    ranks = carry + _dot(onehot.astype(BF16), earlier_ref[...])
    rank_rows = [jnp.sum(jnp.where(hit, ranks, 0.0), axis=0, keepdims=True) for hit in hits]
    route_t = jnp.concatenate(ids + [e / denom for e in exps] + rank_rows
                              + [jnp.zeros((ROUTE_ROWS - 3 * TOP_K, tm), F32)], axis=0)
    route_t_ref[...] = route_t
    route_ref[...] = jnp.concatenate([route_t, jnp.zeros((LANES - ROUTE_ROWS, tm), F32)], axis=0).T
    new_counts = carry + jnp.sum(onehot, axis=1, keepdims=True)
    carry_ref[...] = jnp.broadcast_to(new_counts, carry_ref.shape)
    counts_ref[...] = jnp.broadcast_to(new_counts, counts_ref.shape)


def _mix_out(x2d, gm, rw, p, tm):
    n = x2d.shape[0]
    full = lambda shape: pl.BlockSpec(shape, lambda i: (0,) * len(shape))
    tile = lambda w: pl.BlockSpec((tm, w), lambda i: (i, 0))
    earlier = (jnp.arange(tm)[:, None] < jnp.arange(tm)[None, :]).astype(BF16)
    return pl.pallas_call(
        _mix_out_body,
        grid=(n // tm,),
        in_specs=[tile(D_MODEL), tile(GM_WIDTH), tile(RW_WIDTH), full((D_MODEL, D_MODEL)), full((1, D_MODEL)),
                  full((N_EXPERTS, D_MODEL)), full((N_EXPERTS, D_MODEL)), full((N_EXPERTS, 1)), full((tm, tm))],
        out_specs=[tile(D_MODEL), tile(D_MODEL // 2), tile(LANES), pl.BlockSpec((ROUTE_ROWS, tm), lambda i: (0, i)),
                   full((N_EXPERTS, LANES))],
        out_shape=[
            jax.ShapeDtypeStruct((n, D_MODEL), F32),
            jax.ShapeDtypeStruct((n, D_MODEL // 2), jnp.int32),
            jax.ShapeDtypeStruct((n, LANES), F32),
            jax.ShapeDtypeStruct((ROUTE_ROWS, n), F32),
            jax.ShapeDtypeStruct((N_EXPERTS, LANES), F32),
        ],
        scratch_shapes=[pltpu.VMEM((N_EXPERTS, LANES), F32)],
        compiler_params=pltpu.CompilerParams(vmem_limit_bytes=VMEM_LIMIT),
        name="mix_out",
    )(x2d, gm, rw, p["w_out"], p["norm2_g"], p["router_hi"], p["router_lo"], p["router_b"], earlier)


def _sc_gather_rows(table, idx):
    n_idx = idx.shape[0]
    width = table.shape[1]
    n_workers = SC_CORES * SC_SUBCORES
    per_worker = n_idx // n_workers
    n_windows = per_worker // SC_WINDOW
    assert per_worker * n_workers == n_idx and n_windows * SC_WINDOW == per_worker
    mesh = plsc.VectorSubcoreMesh(core_axis_name="c", subcore_axis_name="s")

    @functools.partial(
        pl.kernel, mesh=mesh,
        out_type=jax.ShapeDtypeStruct((n_idx, width), table.dtype),
        scratch_types=[
            pltpu.VMEM((SC_WINDOW,), jnp.int32),
            pltpu.VMEM((SC_WINDOW, width), table.dtype),
            pltpu.SemaphoreType.DMA,
        ],
        name="sc_gather_rows",
    )
    def gather(table_hbm, idx_hbm, out_hbm, idx_v, rows_v, sem):
        worker = lax.axis_index("s") * SC_CORES + lax.axis_index("c")
        base = worker * per_worker

        @pl.loop(0, n_windows)
        def _(j):
            off = base + j * SC_WINDOW
            pltpu.sync_copy(idx_hbm.at[pl.ds(off, SC_WINDOW)], idx_v)
            pltpu.async_copy(table_hbm.at[idx_v], rows_v, sem).wait()
            pltpu.sync_copy(rows_v, out_hbm.at[pl.ds(off, SC_WINDOW)])

    return gather(table, idx)


def _sc_scatter_rows(src, dest_w, n_out):
    n, width = src.shape
    n_windows, top_k, window = dest_w.shape
    n_workers = SC_CORES * SC_SUBCORES
    per_worker = n_windows // n_workers
    assert n_windows * window == n and per_worker * n_workers == n_windows
    mesh = plsc.VectorSubcoreMesh(core_axis_name="c", subcore_axis_name="s")

    @functools.partial(
        pl.kernel, mesh=mesh,
        out_type=jax.ShapeDtypeStruct((n_out, width), src.dtype),
        scratch_types=[
            pltpu.VMEM((top_k, window), jnp.int32),
            pltpu.VMEM((window, width), src.dtype),
        ],
        name="sc_scatter_rows",
    )
    def scatter(src_hbm, dest_hbm, out_hbm, idx_v, rows_v):
        worker = lax.axis_index("s") * SC_CORES + lax.axis_index("c")

        @pl.loop(0, per_worker)
        def _(j):
            g = worker * per_worker + j
            pltpu.sync_copy(dest_hbm.at[g], idx_v)
            pltpu.sync_copy(src_hbm.at[pl.ds(g * window, window)], rows_v)
            for k in range(top_k):
                pltpu.sync_copy(rows_v, out_hbm.at[idx_v.at[k]])

    return scatter(src, dest_w)


def _experts_body(blk_e_ref, n_used_ref, valid_ref, x_ref, wgu_ref, bgu_ref, wd_ref, bd_ref, out_ref,
                  wgu_bf, wd_bf):
    i = pl.program_id(0)
    n_used = n_used_ref[0]

    @pl.when((i == 0) | (blk_e_ref[i] != blk_e_ref[jnp.maximum(i - 1, 0)]))
    def _():
        wgu_bf[...] = wgu_ref[0].astype(BF16)
        wd_bf[...] = wd_ref[0].astype(BF16)

    @pl.when(i < n_used)
    def _():
        half = D_MODEL // 2
        sub = MOE_ROWS // MOE_SUB_BLOCKS
        row = lax.broadcasted_iota(jnp.int32, (sub, 1), 0)

        def gate_up(s):
            words = jnp.where(row + s * sub < valid_ref[i], x_ref[s * sub:(s + 1) * sub, :], 0)
            x_left, x_right = _unpack_bf16_halves(words)
            return (_dot(x_left.astype(BF16), wgu_bf[:half, :]) + _dot(x_right.astype(BF16), wgu_bf[half:, :])
                    + bgu_ref[0])

        def down(s, gu):
            gate = jnp.minimum(gu[:, :D_EXPERT], SWIGLU_LIMIT)
            up = jnp.clip(gu[:, D_EXPERT:], -SWIGLU_LIMIT, SWIGLU_LIMIT)
            act = gate * _sigmoid(gate * SWIGLU_ALPHA) * (up + 1.0)
            out_ref[s * sub:(s + 1) * sub, :] = _pack_bf16_halves(_dot(act.astype(BF16), wd_bf[...]) + bd_ref[0])

        pending = gate_up(0)
        for s in range(MOE_SUB_BLOCKS):
            following = gate_up(s + 1) if s + 1 < MOE_SUB_BLOCKS else None
            down(s, pending)
            pending = following

    @pl.when(i >= n_used)
    def _():
        out_ref[...] = jnp.zeros_like(out_ref)


def _experts(x_rows, blk_e, n_used, valid, p):
    n_blocks = blk_e.shape[0]
    grid_spec = pltpu.PrefetchScalarGridSpec(
        num_scalar_prefetch=3,
        grid=(n_blocks,),
        in_specs=[
            pl.BlockSpec((MOE_ROWS, D_MODEL // 2), lambda i, be, nu, va: (jnp.minimum(i, nu[0] - 1), 0)),
            pl.BlockSpec((1, D_MODEL, 2 * D_EXPERT), lambda i, be, nu, va: (be[i], 0, 0)),
            pl.BlockSpec((1, 1, 2 * D_EXPERT), lambda i, be, nu, va: (be[i], 0, 0)),
            pl.BlockSpec((1, D_EXPERT, D_MODEL), lambda i, be, nu, va: (be[i], 0, 0)),
            pl.BlockSpec((1, 1, D_MODEL), lambda i, be, nu, va: (be[i], 0, 0)),
        ],
        out_specs=pl.BlockSpec((MOE_ROWS, D_MODEL // 2), lambda i, be, nu, va: (i, 0)),
        scratch_shapes=[pltpu.VMEM((D_MODEL, 2 * D_EXPERT), BF16), pltpu.VMEM((D_EXPERT, D_MODEL), BF16)],
    )
    return pl.pallas_call(
        _experts_body,
        grid_spec=grid_spec,
        out_shape=jax.ShapeDtypeStruct((n_blocks * MOE_ROWS, D_MODEL // 2), jnp.int32),
        compiler_params=pltpu.CompilerParams(vmem_limit_bytes=VMEM_LIMIT),
        name="experts",
    )(blk_e, n_used, valid, x_rows, p["w_gu"], p["b_gu"], p["w_down"], p["b_down"])


def _combine_body(yg_ref, x1_ref, gates_ref, fg_ref, out_ref):
    gates = gates_ref[...]
    half = D_MODEL // 2
    acc_left = x1_ref[:, :half]
    acc_right = x1_ref[:, half:]
    for kk in range(TOP_K):
        y_left, y_right = _unpack_bf16_halves(yg_ref[kk])
        gate = gates[:, TOP_K + kk:TOP_K + kk + 1]
        acc_left = acc_left + gate * y_left
        acc_right = acc_right + gate * y_right
    acc = jnp.concatenate([acc_left, acc_right], axis=1)
    out_ref[...] = acc * lax.rsqrt(jnp.mean(acc * acc, axis=-1, keepdims=True) + RMS_EPS) * fg_ref[...]


def _combine(yg, x1, gates, final_g, tm):
    n = x1.shape[0]
    return pl.pallas_call(
        _combine_body,
        grid=(n // tm,),
        in_specs=[
            pl.BlockSpec((TOP_K, tm, D_MODEL // 2), lambda i: (0, i, 0)),
            pl.BlockSpec((tm, D_MODEL), lambda i: (i, 0)),
            pl.BlockSpec((tm, LANES), lambda i: (i, 0)),
            pl.BlockSpec((1, D_MODEL), lambda i: (0, 0)),
        ],
        out_specs=pl.BlockSpec((tm, D_MODEL), lambda i: (i, 0)),
        out_shape=jax.ShapeDtypeStruct((n, D_MODEL), F32),
        compiler_params=pltpu.CompilerParams(vmem_limit_bytes=VMEM_LIMIT),
        name="combine",
    )(yg, x1, gates, final_g)


def _prepare(norm1_g, w_in, rw_mu, gm_ln_g, gm_ln_b, gm_ws, gm_bs, rw_w0, rw_w2, rw_a0, rw_a2, rw_g2,
             rw_k_k, rw_k_a, rw_r_k, rw_gn_g, rw_gn_b, w_out, norm2_g, router_w, router_b,
             w_gu, b_gu, w_down, b_down, final_g):
    l = 0
    row = lambda t: t.reshape(1, -1).astype(F32)
    p = {}
    p["norm1_g"] = row(norm1_g[l])
    w = w_in[l]
    p["w_gm"] = w[:, :2 * GM_WIDTH].astype(BF16)
    pad_cols = LORA_PAD - LORA_COLS
    p["w_z"] = jnp.pad(w[:, 2 * GM_WIDTH:], ((0, 0), (0, pad_cols))).astype(BF16)
    mu = jnp.pad(row(rw_mu[l]), ((0, 0), (0, pad_cols)))
    p["rw_mu"] = jnp.concatenate([1.0 - mu, 0.5 * mu], axis=0)
    p["gm_ln_g"] = row(gm_ln_g[l])
    p["gm_ln_b"] = row(gm_ln_b[l])
    p["gm_ws"] = gm_ws[l].astype(BF16)
    p["gm_bs"] = jnp.broadcast_to(gm_bs[l][:, :, None], (GM_HEADS, GM_CHUNK, GM_HEAD_DIM)).astype(F32)
    zeros = lambda r, c: jnp.zeros((r, c), F32)
    o_a = DECAY_LORA
    o_g = DECAY_LORA + ICLR_LORA

    def lora_matrix(d, with_epilogue):
        blocks = [jnp.concatenate([rw_w2[l, d], zeros(LORA_PAD - DECAY_LORA, RW_WIDTH)], axis=0),
                  jnp.concatenate([zeros(o_a, RW_WIDTH), rw_a2[l, d], zeros(LORA_PAD - o_g, RW_WIDTH)], axis=0)]
        if with_epilogue:
            blocks.append(jnp.concatenate([zeros(o_a, RW_WIDTH), rw_a2[l, 1 - d], zeros(LORA_PAD - o_g, RW_WIDTH)],
                                          axis=0))
            blocks.append(jnp.concatenate([zeros(o_g, RW_WIDTH), rw_g2[l], zeros(LORA_PAD - LORA_COLS, RW_WIDTH)],
                                          axis=0))
        return jnp.concatenate(blocks, axis=1)

    for d, name in ((0, "fwd"), (1, "rev")):
        mat = lora_matrix(d, with_epilogue=(d == 1))
        hi = mat.astype(BF16)
        p["lora_hi_" + name] = hi
        p["lora_lo_" + name] = (mat - hi.astype(F32)).astype(BF16)
        p["w0_" + name] = row(rw_w0[l, d])
        p["a0_" + name] = row(rw_a0[l, d])
    p["rw_k_k"] = row(rw_k_k[l])
    p["rw_k_a"] = row(rw_k_a[l])
    p["rw_r_k"] = row(rw_r_k[l])
    p["rw_gn_g"] = row(rw_gn_g[l])
    p["rw_gn_b"] = row(rw_gn_b[l])
    ch = jnp.arange(PAIR) // RW_HEAD_DIM
    p["head_ones"] = (ch[:, None] == ch[None, :]).astype(BF16)
    p["w_out"] = w_out[l].astype(BF16)
    p["norm2_g"] = row(norm2_g[l])
    rw_t = router_w[l].astype(F32).T
    hi = rw_t.astype(BF16)
    p["router_hi"] = hi
    p["router_lo"] = (rw_t - hi.astype(F32)).astype(BF16)
    p["router_b"] = router_b[l].astype(F32).reshape(N_EXPERTS, 1)
    p["w_gu"] = w_gu[l]
    p["b_gu"] = b_gu[l].reshape(N_EXPERTS, 1, 2 * D_EXPERT).astype(F32)
    p["w_down"] = w_down[l]
    p["b_down"] = b_down[l].reshape(N_EXPERTS, 1, D_MODEL).astype(F32)
    p["final_g"] = row(final_g)
    return p


def _pick_tile(n, want):
    t = want
    while n % t:
        t //= 2
    return t


def _encoder(x, p):
    b_sz, t_len, d = x.shape
    n = b_sz * t_len
    x2d = x.reshape(n, d)
    tm = _pick_tile(n, 512)
    gm, z = _mix_in(x2d, p, tm)
    z3 = z.reshape(b_sz, t_len, Z_COLS)
    yf = _rwkv_pass(z3, None, p, reverse=False, bt=RW_BATCH_TILE)
    rw = _rwkv_pass(z3, yf, p, reverse=True, bt=RW_BATCH_TILE)
    tmo = _pick_tile(n, 512)
    x1, h2, route, route_t, counts = _mix_out(x2d, gm, rw.reshape(n, RW_WIDTH), p, tmo)
    ids = route_t[:TOP_K].astype(jnp.int32)
    ranks = route_t[2 * TOP_K:3 * TOP_K].astype(jnp.int32)
    cnt = counts[:, 0].astype(jnp.int32)
    padded = (cnt + MOE_ROWS - 1) // MOE_ROWS * MOE_ROWS
    pad_end = jnp.cumsum(padded)
    pad_start = pad_end - padded
    expert_ids = jnp.arange(N_EXPERTS, dtype=jnp.int32)
    start_of = jnp.sum(jnp.where(ids[..., None] == expert_ids, pad_start, 0), axis=-1)
    dest = start_of + ranks
    n_blocks = n * TOP_K // MOE_ROWS + N_EXPERTS
    blk_start = jnp.arange(n_blocks, dtype=jnp.int32) * MOE_ROWS
    blk_e = jnp.minimum(jnp.sum((pad_end[None, :] <= blk_start[:, None]).astype(jnp.int32), axis=1), N_EXPERTS - 1)
    n_used = (pad_end[-1] // MOE_ROWS).astype(jnp.int32).reshape(1)
    blk_onehot = blk_e[:, None] == expert_ids
    blk_cnt = jnp.sum(jnp.where(blk_onehot, cnt, 0), axis=-1)
    blk_first = jnp.sum(jnp.where(blk_onehot, pad_start, 0), axis=-1)
    valid = jnp.clip(blk_cnt - (blk_start - blk_first), 0, MOE_ROWS).astype(jnp.int32)
    window = min(SC_SCATTER_WINDOW, n // (SC_CORES * SC_SUBCORES))
    dest_w = dest.reshape(TOP_K, n // window, window).transpose(1, 0, 2)
    x_rows = _sc_scatter_rows(h2, dest_w, n_blocks * MOE_ROWS)
    y_rows = _experts(x_rows, blk_e, n_used, valid, p)
    yg = _sc_gather_rows(y_rows, dest.reshape(-1)).reshape(TOP_K, n, D_MODEL // 2)
    out = _combine(yg, x1, route, p["final_g"], _pick_tile(n, 512))
    return out.reshape(b_sz, t_len, d)


def kernel(x_prompt, x_sample, norm1_g, w_in, rw_mu, gm_ln_g, gm_ln_b, gm_ws, gm_bs, rw_w0, rw_w2, rw_a0, rw_a2,
           rw_g2, rw_k_k, rw_k_a, rw_r_k, rw_gn_g, rw_gn_b, w_out, norm2_g, router_w, router_b, w_gu, b_gu,
           w_down, b_down, final_g):
    p = _prepare(norm1_g, w_in, rw_mu, gm_ln_g, gm_ln_b, gm_ws, gm_bs, rw_w0, rw_w2, rw_a0, rw_a2, rw_g2,
                 rw_k_k, rw_k_a, rw_r_k, rw_gn_g, rw_gn_b, w_out, norm2_g, router_w, router_b,
                 w_gu, b_gu, w_down, b_down, final_g)
    return (_encoder(x_prompt, p), _encoder(x_sample, p))
```

```python
import functools

import jax
import jax.numpy as jnp
from jax import lax
from jax.experimental import pallas as pl
from jax.experimental.pallas import tpu as pltpu
from jax.experimental.pallas import tpu_sc as plsc

F32 = jnp.float32
BF16 = jnp.bfloat16

D_MODEL = 1024
GM_WIDTH = 512
RW_WIDTH = 512
GM_HEADS = 4
GM_HEAD_DIM = 128
GM_CHUNK = 128
RW_HEAD_DIM = 64
DECAY_LORA = 32
ICLR_LORA = 32
GATE_LORA = 96
LORA_COLS = DECAY_LORA + ICLR_LORA + GATE_LORA
LORA_PAD = 256
Z_COLS = 3 * RW_WIDTH + LORA_PAD
N_EXPERTS = 32
TOP_K = 4
D_EXPERT = 1024
SWIGLU_LIMIT = 7.0
SWIGLU_ALPHA = 1.702
RMS_EPS = 1e-5
LN_EPS = 1e-5
GN_EPS = 64e-5

LANES = 128
SUBLANES = 8
VMEM_LIMIT = 56 * 1024 * 1024

RW_CHUNK = 64
PAIR = 2 * RW_HEAD_DIM
N_PAIRS = RW_WIDTH // PAIR
RW_BATCH_TILE = 4
MOE_ROWS = 512
SC_CORES = 2
SC_SUBCORES = 16
SC_WINDOW = 64
SC_SCATTER_WINDOW = 128
ROUTE_ROWS = 16
DECAY_SCALE = 0.6065306597126334
KK_NORM_FLOOR_SQ = 1e-24


def _dot(a, b):
    return jnp.dot(a, b, preferred_element_type=F32)


def _dot_nt(a, b):
    return lax.dot_general(a, b, (((1,), (1,)), ((), ())), preferred_element_type=F32)


def _split(x):
    hi = x.astype(BF16)
    lo = (x - hi.astype(F32)).astype(BF16)
    return hi, lo


def _dot3(a, b_hi, b_lo):
    a_hi, a_lo = _split(a)
    return _dot(a_hi, b_hi) + _dot(a_lo, b_hi) + _dot(a_hi, b_lo)


def _dot2(a, b01):
    a_hi, a_lo = _split(a)
    return _dot(a_hi, b01) + _dot(a_lo, b01)


def _head_sums(x, pair_ones):
    rows = x.shape[0]
    n_tiles = x.shape[1] // LANES
    stacked = jnp.concatenate([x[:, t * LANES:(t + 1) * LANES] for t in range(n_tiles)], axis=0)
    sums = _dot2(stacked, pair_ones)
    return jnp.concatenate([sums[t * rows:(t + 1) * rows] for t in range(n_tiles)], axis=1)


def _dot2_lhs01(a01, b):
    b_hi, b_lo = _split(b)
    return _dot(a01, b_hi) + _dot(a01, b_lo)


def _pack_bf16_halves(x):
    w = x.shape[1] // 2
    hi = lax.bitcast_convert_type(x[:, :w].astype(BF16).astype(F32), jnp.int32)
    lo = lax.bitcast_convert_type(x[:, w:].astype(BF16).astype(F32), jnp.int32)
    return hi | lax.shift_right_logical(lo, 16)


def _unpack_bf16_halves(words):
    left = lax.bitcast_convert_type(words & jnp.int32(-65536), F32)
    right = lax.bitcast_convert_type(lax.shift_left(words, 16), F32)
    return left, right


def _gelu_tanh(x):
    return 0.5 * x * (1.0 + jnp.tanh(0.7978845608028654 * (x + 0.044715 * (x * x * x))))


def _sigmoid(x):
    return 1.0 / (1.0 + jnp.exp(-x))


def _mix_in_body(tiles_per_seq, x_ref, xp_ref, xn_ref, g1_ref, wgm_ref, wz_ref, mu_ref, lng_ref, lnb_ref, ws_ref,
                 bs_ref, gm_ref, z_ref):
    i = pl.program_id(0)
    tm = x_ref.shape[0]

    def normed(x):
        return (x * lax.rsqrt(jnp.mean(x * x, axis=-1, keepdims=True) + RMS_EPS) * g1_ref[...]).astype(BF16)

    hb = normed(x_ref[...])
    uv = _dot(hb, wgm_ref[...])
    z = _dot(hb, wz_ref[...])
    halo = _dot(normed(jnp.concatenate([xp_ref[...], xn_ref[...]], axis=0)), wz_ref[...])
    prev_row = jnp.where(i % tiles_per_seq > 0, halo[SUBLANES - 1:SUBLANES, :], 0.0)
    next_row = jnp.where(i % tiles_per_seq < tiles_per_seq - 1, halo[SUBLANES:SUBLANES + 1, :], 0.0)
    row8 = lax.broadcasted_iota(jnp.int32, (SUBLANES, Z_COLS), 0)
    down = pltpu.roll(z, 1, axis=0)
    up = pltpu.roll(z, tm - 1, axis=0)
    prev = jnp.concatenate([jnp.where(row8 == 0, prev_row, down[:SUBLANES]), down[SUBLANES:]], axis=0)
    nxt = jnp.concatenate([up[:tm - SUBLANES], jnp.where(row8 == SUBLANES - 1, next_row, up[tm - SUBLANES:])],
                          axis=0)
    z_ref[...] = z * mu_ref[0:1, :] + (prev + nxt) * mu_ref[1:2, :]
    u = _gelu_tanh(uv[:, :GM_WIDTH])
    v = _gelu_tanh(uv[:, GM_WIDTH:])
    mean = jnp.mean(v, axis=-1, keepdims=True)
    vc = v - mean
    var = jnp.mean(vc * vc, axis=-1, keepdims=True)
    vn = (vc * lax.rsqrt(var + LN_EPS) * lng_ref[...] + lnb_ref[...]).astype(BF16)
    for c in range(tm // GM_CHUNK):
        rows = slice(c * GM_CHUNK, (c + 1) * GM_CHUNK)
        for hd in range(GM_HEADS):
            cols = slice(hd * GM_HEAD_DIM, (hd + 1) * GM_HEAD_DIM)
            mixed = _dot(ws_ref[hd], vn[rows, cols]) + bs_ref[hd]
            gm_ref[rows, cols] = (u[rows, cols] * mixed).astype(BF16)


def _mix_in(x2d, p, tm, t_len):
    n = x2d.shape[0]
    assert t_len % tm == 0
    halo_per_tile = tm // SUBLANES
    n_halo = n // SUBLANES
    full = lambda shape: pl.BlockSpec(shape, lambda i: (0,) * len(shape))
    return pl.pallas_call(
        functools.partial(_mix_in_body, t_len // tm),
        grid=(n // tm,),
        in_specs=[
            pl.BlockSpec((tm, D_MODEL), lambda i: (i, 0)),
            pl.BlockSpec((SUBLANES, D_MODEL), lambda i: (jnp.maximum(i * halo_per_tile - 1, 0), 0)),
            pl.BlockSpec((SUBLANES, D_MODEL), lambda i: (jnp.minimum((i + 1) * halo_per_tile, n_halo - 1), 0)),
            full((1, D_MODEL)),
            full((D_MODEL, 2 * GM_WIDTH)),
            full((D_MODEL, Z_COLS)),
            full((2, Z_COLS)),
            full((1, GM_WIDTH)),
            full((1, GM_WIDTH)),
            full((GM_HEADS, GM_CHUNK, GM_CHUNK)),
            full((GM_HEADS, GM_CHUNK, GM_HEAD_DIM)),
        ],
        out_specs=[
            pl.BlockSpec((tm, GM_WIDTH), lambda i: (i, 0)),
            pl.BlockSpec((tm, Z_COLS), lambda i: (i, 0)),
        ],
        out_shape=[
            jax.ShapeDtypeStruct((n, GM_WIDTH), BF16),
            jax.ShapeDtypeStruct((n, Z_COLS), F32),
        ],
        compiler_params=pltpu.CompilerParams(vmem_limit_bytes=VMEM_LIMIT),
        name="mix_in",
    )(x2d, x2d, x2d, p["norm1_g"], p["w_gm"], p["w_z"], p["rw_mu"], p["gm_ln_g"], p["gm_ln_b"], p["gm_ws"],
      p["gm_bs"])


def _rwkv_masks(reverse):
    c = RW_CHUNK
    ti = lax.broadcasted_iota(jnp.int32, (c, c), 0)
    tj = lax.broadcasted_iota(jnp.int32, (c, c), 1)
    incl = (tj >= ti) if reverse else (tj <= ti)
    lane =lax.broadcasted_iota(jnp.int32, (c, PAIR), 1)
    head0 = lane < RW_HEAD_DIM
    lane2 = lax.broadcasted_iota(jnp.int32, (c, 2 * c), 1)
    left = lane2 < c
    ti2 = lax.broadcasted_iota(jnp.int32, (c, 2 * c), 0)
    tj2 = jnp.where(left, lane2, lane2 - c)
    incl2 = (tj2 >= ti2) if reverse else (tj2 <= ti2)
    strict2 = (tj2 > ti2) if reverse else (tj2 < ti2)
    eye2 = jnp.where(tj2 == ti2, 1.0, 0.0).astype(F32)
    si = lax.broadcasted_iota(jnp.int32, (PAIR, PAIR), 0)
    sj = lax.broadcasted_iota(jnp.int32, (PAIR, PAIR), 1)
    same_head = (si // RW_HEAD_DIM) == (sj // RW_HEAD_DIM)
    return dict(
        incl2=incl2, strict2=strict2, head0=head0, left=left, eye2=eye2, same_head=same_head,
        tri=jnp.where(incl, 1.0, 0.0).astype(BF16),
    )


def _stack_heads(x, head0):
    return jnp.concatenate([jnp.where(head0, x, 0.0), jnp.where(head0, 0.0, x)], axis=0)


def _block_diag(x, left):
    return jnp.concatenate([jnp.where(left, x, 0.0), jnp.where(left, 0.0, x)], axis=0)


def _units_chunk(states, ops, m):
    c = RW_CHUNK
    units = range(len(ops))
    bf = lambda t: t.astype(BF16)
    head0, left = m["head0"], m["left"]
    ar = [bf(jnp.concatenate([o["a_t"], o["r_t"]], axis=0)) for o in ops]
    rstack = [bf(jnp.concatenate([_stack_heads(o["b_t"], head0), _stack_heads(o["k_t"], head0)], axis=0))
              for o in ops]
    acat = [_dot_nt(ar[i], rstack[i]) for i in units]
    sa = [_dot_nt(ar[i], bf(states[i])) for i in units]
    a_ab = [jnp.where(m["strict2"], acat[i][:c, :2 * c], 0.0) for i in units]
    a_ak = [bf(jnp.where(m["strict2"], acat[i][:c, 2 * c:], 0.0)) for i in units]
    a_r = [bf(jnp.concatenate([jnp.where(m["incl2"], acat[i][c:, :2 * c], 0.0),
                               jnp.where(m["incl2"], acat[i][c:, 2 * c:], 0.0)], axis=1)) for i in units]
    p = [m["eye2"] + a_ab[i] for i in units]
    lk = [_dot(bf(a_ab[i]), bf(_block_diag(a_ab[i], left))) for i in units]
    vstack = [bf(_stack_heads(o["v"], head0)) for o in ops]
    rhs = [sa[i][:c] + _dot(a_ak[i], vstack[i]) for i in units]
    power = 2
    while 2 * power < c:
        res = [_dot(bf(jnp.concatenate([p[i], lk[i]], axis=0)), bf(_block_diag(lk[i], left))) for i in units]
        p = [p[i] + res[i][:c] for i in units]
        lk = [res[i][c:] for i in units]
        power *= 2
    p = [p[i] + _dot(bf(p[i]), bf(_block_diag(lk[i], left))) for i in units]
    u = [_dot(bf(p[i]), bf(_stack_heads(rhs[i], head0))) for i in units]
    ys = [sa[i][c:] + _dot(a_r[i], jnp.concatenate([bf(_stack_heads(u[i], head0)), vstack[i]], axis=0))
          for i in units]
    upd = [_dot(bf(jnp.concatenate([u[i], ops[i]["v"]], axis=0).T),
                bf(jnp.concatenate([ops[i]["b_h"], ops[i]["k_h"]], axis=0))) for i in units]
    new_states = [states[i] * ops[i]["w_tot"] + jnp.where(m["same_head"], upd[i], 0.0) for i in units]
    return new_states, ys


def _rwkv_body(reverse, bt, *refs):
    if reverse:
        (zc_ref, yf_ref, lw_hi_ref, lw_lo_ref, w0_ref, a0_ref, a0o_ref, kk_ref, ka_ref,
         rk_ref, bd_ref, gng_ref, gnb_ref, out_ref, state_ref) = refs
    else:
        (zc_ref, lw_hi_ref, lw_lo_ref, w0_ref, a0_ref, kk_ref, ka_ref,
         bd_ref, out_ref, state_ref) = refs
    c = RW_CHUNK

    @pl.when(pl.program_id(1) == 0)
    def _():
        state_ref[...] = jnp.zeros_like(state_ref)

    rows = bt * c
    m = _rwkv_masks(reverse)
    lora_lane = lax.broadcasted_iota(jnp.int32, (rows, LORA_PAD), 1)
    bd = bd_ref[...]
    zf = zc_ref[...].reshape(rows, Z_COLS)
    r = zf[:, :RW_WIDTH]
    k = zf[:, RW_WIDTH:2 * RW_WIDTH]
    v = zf[:, 2 * RW_WIDTH:3 * RW_WIDTH]
    lo = zf[:, 3 * RW_WIDTH:]
    act = jnp.where(lora_lane < DECAY_LORA, jnp.tanh(lo),
                    jnp.where(lora_lane < DECAY_LORA + ICLR_LORA, lo, _sigmoid(lo)))
    proj = _dot3(act, lw_hi_ref[...], lw_lo_ref[...])
    w_in = w0_ref[...] + proj[:, :RW_WIDTH]
    logw = -DECAY_SCALE * _sigmoid(w_in)
    a_sig = _sigmoid(a0_ref[...] + proj[:, RW_WIDTH:2 * RW_WIDTH])
    kk = k * kk_ref[...]
    kk = kk * lax.rsqrt(jnp.maximum(_head_sums(kk * kk, bd), KK_NORM_FLOOR_SQ))
    kd = k * (1.0 + (a_sig - 1.0) * ka_ref[...])
    a_vec = -kk
    b_vec = kk * a_sig
    ri = lax.broadcasted_iota(jnp.int32, (rows, rows), 0)
    rj = lax.broadcasted_iota(jnp.int32, (rows, rows), 1)
    ordered = (rj >= ri) if reverse else (rj <= ri)
    tri = jnp.where((ri // c) == (rj // c), jnp.where(ordered, 1.0, 0.0), 0.0).astype(BF16)
    cum = _dot2_lhs01(tri, logw)
    edge = 0 if reverse else c - 1
    tot = jnp.concatenate([jnp.broadcast_to(cum[b * c + edge:b * c + edge + 1, :], (c, RW_WIDTH))
                           for b in range(bt)], axis=0)
    w_inv = jnp.exp(-cum)
    w_end = jnp.exp(tot - cum)
    w_tot = jnp.exp(tot)
    full_ops = dict(r_t=r * jnp.exp(cum), a_t=a_vec * jnp.exp(cum - logw), b_t=b_vec * w_inv, k_t=kd * w_inv,
                    b_h=b_vec * w_end, k_h=kd * w_end, v=v)
    ops, states = [], []
    for b in range(bt):
        for pi in range(N_PAIRS):
            ln = slice(pi * PAIR, (pi + 1) * PAIR)
            unit = {name: t[b * c:(b + 1) * c, ln] for name, t in full_ops.items()}
            unit["w_tot"] = w_tot[b * c:b * c + 1, ln]
            ops.append(unit)
            states.append(state_ref[b, pi])
    new_states, ys = _units_chunk(states, ops, m)
    for b in range(bt):
        for pi in range(N_PAIRS):
            state_ref[b, pi] = new_states[b * N_PAIRS + pi]
    y = jnp.concatenate([jnp.concatenate(ys[b * N_PAIRS:(b + 1) * N_PAIRS], axis=1) for b in range(bt)], axis=0)
    if not reverse:
        out_ref[...] = y.reshape(bt, c, RW_WIDTH)
    else:
        y = y + yf_ref[...].reshape(rows, RW_WIDTH)
        a_other = _sigmoid(a0o_ref[...] + proj[:, 2 * RW_WIDTH:3 * RW_WIDTH])
        kd_sum = kd + k * (1.0 + (a_other - 1.0) * ka_ref[...])
        bonus = _head_sums(r * kd_sum * rk_ref[...], bd) * v
        inv_n = 1.0 / RW_HEAD_DIM
        mean = _head_sums(y, bd) * inv_n
        yc = y - mean
        var = _head_sums(yc * yc, bd) * inv_n
        yn = yc * lax.rsqrt(var + GN_EPS) * gng_ref[...] + gnb_ref[...]
        g = proj[:, 3 * RW_WIDTH:]
        out_ref[...] = ((yn + bonus) * g).astype(BF16).reshape(bt, c, RW_WIDTH)


def _rwkv_pass(z3, yf, p, reverse, bt):
    b_sz, t_len, _ = z3.shape
    c = RW_CHUNK
    n_chunks = t_len // c

    def cidx(ci):
        return (n_chunks - 1 - ci) if reverse else ci

    full = lambda shape: pl.BlockSpec(shape, lambda bi, ci: (0,) * len(shape))
    row512 = full((1, RW_WIDTH))
    d = "rev" if reverse else "fwd"
    n_proj = p["lora_hi_" + d].shape[1]
    in_specs = [pl.BlockSpec((bt, c, Z_COLS), lambda bi, ci: (bi, cidx(ci), 0))]
    args = [z3]
    if reverse:
        in_specs.append(pl.BlockSpec((bt, c, RW_WIDTH), lambda bi, ci: (bi, cidx(ci), 0)))
        args.append(yf)
    in_specs += [full((LORA_PAD, n_proj)), full((LORA_PAD, n_proj)), row512, row512]
    args += [p["lora_hi_" + d], p["lora_lo_" + d], p["w0_" + d], p["a0_" + d]]
    if reverse:
        in_specs.append(row512)
        args.append(p["a0_fwd"])
    in_specs += [row512, row512]
    args += [p["rw_k_k"], p["rw_k_a"]]
    if reverse:
        in_specs.append(row512)
        args.append(p["rw_r_k"])
    in_specs.append(full((PAIR, PAIR)))
    args.append(p["head_ones"])
    if reverse:
        in_specs += [row512, row512]
        args += [p["rw_gn_g"], p["rw_gn_b"]]
    return pl.pallas_call(
        functools.partial(_rwkv_body, reverse, bt),
        grid=(b_sz // bt, n_chunks),
        in_specs=in_specs,
        out_specs=pl.BlockSpec((bt, c, RW_WIDTH), lambda bi, ci: (bi, cidx(ci), 0)),
        out_shape=jax.ShapeDtypeStruct((b_sz, t_len, RW_WIDTH), BF16 if reverse else F32),
        scratch_shapes=[pltpu.VMEM((bt, N_PAIRS, PAIR, PAIR), F32)],
        compiler_params=pltpu.CompilerParams(vmem_limit_bytes=VMEM_LIMIT),
        name="rwkv_" + d,
    )(*args)


def _mix_out_body(x_ref, gm_ref, rw_ref, wo_ref, g2_ref, rw_hi_ref, rw_lo_ref, rb_ref, earlier_ref,
                  x1_ref, h2_ref, route_ref, route_t_ref, counts_ref, carry_ref):
    i = pl.program_id(0)

    @pl.when(i == 0)
    def _():
        carry_ref[...] = jnp.zeros_like(carry_ref)

    x1 = x_ref[...] + _dot(gm_ref[...], wo_ref[:GM_WIDTH, :]) + _dot(rw_ref[...], wo_ref[GM_WIDTH:, :])
    x1_ref[...] = x1
    h2 = x1 * lax.rsqrt(jnp.mean(x1 * x1, axis=-1, keepdims=True) + RMS_EPS) * g2_ref[...]
    h2_ref[...] = _pack_bf16_halves(h2)
    tm = x1.shape[0]
    h_hi, h_lo = _split(h2)
    logits = (_dot_nt(rw_hi_ref[...], h_hi) + _dot_nt(rw_hi_ref[...], h_lo) + _dot_nt(rw_lo_ref[...], h_hi)
              + rb_ref[...])
    expert = lax.broadcasted_iota(jnp.int32, (N_EXPERTS, tm), 0).astype(F32)
    work = logits
    vals, ids, hits = [], [], []
    onehot = jnp.zeros((N_EXPERTS, tm), F32)
    for _ in range(TOP_K):
        mx = jnp.max(work, axis=0, keepdims=True)
        idx = jnp.min(jnp.where(work == mx, expert, float(N_EXPERTS)), axis=0, keepdims=True)
        hit = expert == idx
        vals.append(mx)
        ids.append(idx)
        hits.append(hit)
        onehot = jnp.where(hit, 1.0, onehot)
        work = jnp.where(hit, -jnp.inf, work)
    exps = [jnp.exp(vk - vals[0]) for vk in vals]
    denom = exps[0] + exps[1] + exps[2] + exps[3]
    carry = carry_ref[:, 0:1]
    ranks = carry + _dot(onehot.astype(BF16), earlier_ref[...])
    rank_rows = [jnp.sum(jnp.where(hit, ranks, 0.0), axis=0, keepdims=True) for hit in hits]
    route_t = jnp.concatenate(ids + [e / denom for e in exps] + rank_rows
                              + [jnp.zeros((ROUTE_ROWS - 3 * TOP_K, tm), F32)], axis=0)
    route_t_ref[...] = route_t
    route_ref[...] = jnp.concatenate([route_t, jnp.zeros((LANES - ROUTE_ROWS, tm), F32)], axis=0).T
    new_counts = carry + jnp.sum(onehot, axis=1, keepdims=True)
    carry_ref[...] = jnp.broadcast_to(new_counts, carry_ref.shape)
    counts_ref[...] = jnp.broadcast_to(new_counts, counts_ref.shape)


def _mix_out(x2d, gm, rw, p, tm):
    n = x2d.shape[0]
    full = lambda shape: pl.BlockSpec(shape, lambda i: (0,) * len(shape))
    tile = lambda w: pl.BlockSpec((tm, w), lambda i: (i, 0))
    earlier = (jnp.arange(tm)[:, None] < jnp.arange(tm)[None, :]).astype(BF16)
    return pl.pallas_call(
        _mix_out_body,
        grid=(n // tm,),
        in_specs=[tile(D_MODEL), tile(GM_WIDTH), tile(RW_WIDTH), full((D_MODEL, D_MODEL)), full((1, D_MODEL)),
                  full((N_EXPERTS, D_MODEL)), full((N_EXPERTS, D_MODEL)), full((N_EXPERTS, 1)), full((tm, tm))],
        out_specs=[tile(D_MODEL), tile(D_MODEL // 2), tile(LANES), pl.BlockSpec((ROUTE_ROWS, tm), lambda i: (0, i)),
                   full((N_EXPERTS, LANES))],
        out_shape=[
            jax.ShapeDtypeStruct((n, D_MODEL), F32),
            jax.ShapeDtypeStruct((n, D_MODEL // 2), jnp.int32),
            jax.ShapeDtypeStruct((n, LANES), F32),
            jax.ShapeDtypeStruct((ROUTE_ROWS, n), F32),
            jax.ShapeDtypeStruct((N_EXPERTS, LANES), F32),
        ],
        scratch_shapes=[pltpu.VMEM((N_EXPERTS, LANES), F32)],
        compiler_params=pltpu.CompilerParams(vmem_limit_bytes=VMEM_LIMIT),
        name="mix_out",
    )(x2d, gm, rw, p["w_out"], p["norm2_g"], p["router_hi"], p["router_lo"], p["router_b"], earlier)


def _sc_gather_rows(table, idx):
    n_idx = idx.shape[0]
    width = table.shape[1]
    n_workers = SC_CORES * SC_SUBCORES
    per_worker = n_idx // n_workers
    n_windows = per_worker // SC_WINDOW
    assert per_worker * n_workers == n_idx and n_windows * SC_WINDOW == per_worker
    mesh = plsc.VectorSubcoreMesh(core_axis_name="c", subcore_axis_name="s")

    @functools.partial(
        pl.kernel, mesh=mesh,
        out_type=jax.ShapeDtypeStruct((n_idx, width), table.dtype),
        scratch_types=[
            pltpu.VMEM((SC_WINDOW,), jnp.int32),
            pltpu.VMEM((SC_WINDOW, width), table.dtype),
            pltpu.SemaphoreType.DMA,
        ],
        name="sc_gather_rows",
    )
    def gather(table_hbm, idx_hbm, out_hbm, idx_v, rows_v, sem):
        worker = lax.axis_index("s") * SC_CORES + lax.axis_index("c")
        base = worker * per_worker

        @pl.loop(0, n_windows)
        def _(j):
            off = base + j * SC_WINDOW
            pltpu.sync_copy(idx_hbm.at[pl.ds(off, SC_WINDOW)], idx_v)
            pltpu.async_copy(table_hbm.at[idx_v], rows_v, sem).wait()
            pltpu.sync_copy(rows_v, out_hbm.at[pl.ds(off, SC_WINDOW)])

    return gather(table, idx)


def _sc_scatter_rows(src, dest_w, n_out):
    n, width = src.shape
    n_windows, top_k, window = dest_w.shape
    n_workers = SC_CORES * SC_SUBCORES
    per_worker = n_windows // n_workers
    assert n_windows * window == n and per_worker * n_workers == n_windows
    mesh = plsc.VectorSubcoreMesh(core_axis_name="c", subcore_axis_name="s")

    @functools.partial(
        pl.kernel, mesh=mesh,
        out_type=jax.ShapeDtypeStruct((n_out, width), src.dtype),
        scratch_types=[
            pltpu.VMEM((top_k, window), jnp.int32),
            pltpu.VMEM((window, width), src.dtype),
        ],
        name="sc_scatter_rows",
    )
    def scatter(src_hbm, dest_hbm, out_hbm, idx_v, rows_v):
        worker = lax.axis_index("s") * SC_CORES + lax.axis_index("c")

        @pl.loop(0, per_worker)
        def _(j):
            g = worker * per_worker + j
            pltpu.sync_copy(dest_hbm.at[g], idx_v)
            pltpu.sync_copy(src_hbm.at[pl.ds(g * window, window)], rows_v)
            for k in range(top_k):
                pltpu.sync_copy(rows_v, out_hbm.at[idx_v.at[k]])

    return scatter(src, dest_w)


def _experts_body(blk_e_ref, n_used_ref, valid_ref, x_ref, wgu_ref, bgu_ref, wd_ref, bd_ref, out_ref,
                  wgu_bf, wd_bf):
    i = pl.program_id(0)
    n_used = n_used_ref[0]

    @pl.when((i == 0) | (blk_e_ref[i] != blk_e_ref[jnp.maximum(i - 1, 0)]))
    def _():
        wgu_bf[...] = wgu_ref[0].astype(BF16)
        wd_bf[...] = wd_ref[0].astype(BF16)

    @pl.when(i < n_used)
    def _():
        row = lax.broadcasted_iota(jnp.int32, (MOE_ROWS, 1), 0)
        x_left, x_right = _unpack_bf16_halves(jnp.where(row < valid_ref[i], x_ref[...], 0))
        half = D_MODEL // 2
        gu = (_dot(x_left.astype(BF16), wgu_bf[:half, :]) + _dot(x_right.astype(BF16), wgu_bf[half:, :])
              + bgu_ref[0])
        gate = jnp.minimum(gu[:, :D_EXPERT], SWIGLU_LIMIT)
        up = jnp.clip(gu[:, D_EXPERT:], -SWIGLU_LIMIT, SWIGLU_LIMIT)
        act = gate * _sigmoid(gate * SWIGLU_ALPHA) * (up + 1.0)
        out_ref[...] = _pack_bf16_halves(_dot(act.astype(BF16), wd_bf[...]) + bd_ref[0])

    @pl.when(i >= n_used)
    def _():
        out_ref[...] = jnp.zeros_like(out_ref)


def _experts(x_rows, blk_e, n_used, valid, p):
    n_blocks = blk_e.shape[0]
    grid_spec = pltpu.PrefetchScalarGridSpec(
        num_scalar_prefetch=3,
        grid=(n_blocks,),
        in_specs=[
            pl.BlockSpec((MOE_ROWS, D_MODEL // 2), lambda i, be, nu, va: (jnp.minimum(i, nu[0] - 1), 0)),
            pl.BlockSpec((1, D_MODEL, 2 * D_EXPERT), lambda i, be, nu, va: (be[i], 0, 0)),
            pl.BlockSpec((1, 1, 2 * D_EXPERT), lambda i, be, nu, va: (be[i], 0, 0)),
            pl.BlockSpec((1, D_EXPERT, D_MODEL), lambda i, be, nu, va: (be[i], 0, 0)),
            pl.BlockSpec((1, 1, D_MODEL), lambda i, be, nu, va: (be[i], 0, 0)),
        ],
        out_specs=pl.BlockSpec((MOE_ROWS, D_MODEL // 2), lambda i, be, nu, va: (i, 0)),
        scratch_shapes=[pltpu.VMEM((D_MODEL, 2 * D_EXPERT), BF16), pltpu.VMEM((D_EXPERT, D_MODEL), BF16)],
    )
    return pl.pallas_call(
        _experts_body,
        grid_spec=grid_spec,
        out_shape=jax.ShapeDtypeStruct((n_blocks * MOE_ROWS, D_MODEL // 2), jnp.int32),
        compiler_params=pltpu.CompilerParams(vmem_limit_bytes=VMEM_LIMIT),
        name="experts",
    )(blk_e, n_used, valid, x_rows, p["w_gu"], p["b_gu"], p["w_down"], p["b_down"])


def _combine_body(yg_ref, x1_ref, gates_ref, fg_ref, out_ref):
    gates = gates_ref[...]
    half = D_MODEL // 2
    acc_left = x1_ref[:, :half]
    acc_right = x1_ref[:, half:]
    for kk in range(TOP_K):
        y_left, y_right = _unpack_bf16_halves(yg_ref[kk])
        gate = gates[:, TOP_K + kk:TOP_K + kk + 1]
        acc_left = acc_left + gate * y_left
        acc_right = acc_right + gate * y_right
    acc = jnp.concatenate([acc_left, acc_right], axis=1)
    out_ref[...] = acc * lax.rsqrt(jnp.mean(acc * acc, axis=-1, keepdims=True) + RMS_EPS) * fg_ref[...]


def _combine(yg, x1, gates, final_g, tm):
    n = x1.shape[0]
    return pl.pallas_call(
        _combine_body,
        grid=(n // tm,),
        in_specs=[
            pl.BlockSpec((TOP_K, tm, D_MODEL // 2), lambda i: (0, i, 0)),
            pl.BlockSpec((tm, D_MODEL), lambda i: (i, 0)),
            pl.BlockSpec((tm, LANES), lambda i: (i, 0)),
            pl.BlockSpec((1, D_MODEL), lambda i: (0, 0)),
        ],
        out_specs=pl.BlockSpec((tm, D_MODEL), lambda i: (i, 0)),
        out_shape=jax.ShapeDtypeStruct((n, D_MODEL), F32),
        compiler_params=pltpu.CompilerParams(vmem_limit_bytes=VMEM_LIMIT),
        name="combine",
    )(yg, x1, gates, final_g)


def _prepare(norm1_g, w_in, rw_mu, gm_ln_g, gm_ln_b, gm_ws, gm_bs, rw_w0, rw_w2, rw_a0, rw_a2, rw_g2,
             rw_k_k, rw_k_a, rw_r_k, rw_gn_g, rw_gn_b, w_out, norm2_g, router_w, router_b,
             w_gu, b_gu, w_down, b_down, final_g):
    l = 0
    row = lambda t: t.reshape(1, -1).astype(F32)
    p = {}
    p["norm1_g"] = row(norm1_g[l])
    w = w_in[l]
    p["w_gm"] = w[:, :2 * GM_WIDTH].astype(BF16)
    pad_cols = LORA_PAD - LORA_COLS
    p["w_z"] = jnp.pad(w[:, 2 * GM_WIDTH:], ((0, 0), (0, pad_cols))).astype(BF16)
    mu = jnp.pad(row(rw_mu[l]), ((0, 0), (0, pad_cols)))
    p["rw_mu"] = jnp.concatenate([1.0 - mu, 0.5 * mu], axis=0)
    p["gm_ln_g"] = row(gm_ln_g[l])
    p["gm_ln_b"] = row(gm_ln_b[l])
    p["gm_ws"] = gm_ws[l].astype(BF16)
    p["gm_bs"] = jnp.broadcast_to(gm_bs[l][:, :, None], (GM_HEADS, GM_CHUNK, GM_HEAD_DIM)).astype(F32)
    zeros = lambda r, c: jnp.zeros((r, c), F32)
    o_a = DECAY_LORA
    o_g = DECAY_LORA + ICLR_LORA

    def lora_matrix(d, with_epilogue):
        blocks = [jnp.concatenate([rw_w2[l, d], zeros(LORA_PAD - DECAY_LORA, RW_WIDTH)], axis=0),
                  jnp.concatenate([zeros(o_a, RW_WIDTH), rw_a2[l, d], zeros(LORA_PAD - o_g, RW_WIDTH)], axis=0)]
        if with_epilogue:
            blocks.append(jnp.concatenate([zeros(o_a, RW_WIDTH), rw_a2[l, 1 - d], zeros(LORA_PAD - o_g, RW_WIDTH)],
                                          axis=0))
            blocks.append(jnp.concatenate([zeros(o_g, RW_WIDTH), rw_g2[l], zeros(LORA_PAD - LORA_COLS, RW_WIDTH)],
                                          axis=0))
        return jnp.concatenate(blocks, axis=1)

    for d, name in ((0, "fwd"), (1, "rev")):
        mat = lora_matrix(d, with_epilogue=(d == 1))
        hi = mat.astype(BF16)
        p["lora_hi_" + name] = hi
        p["lora_lo_" + name] = (mat - hi.astype(F32)).astype(BF16)
        p["w0_" + name] = row(rw_w0[l, d])
        p["a0_" + name] = row(rw_a0[l, d])
    p["rw_k_k"] = row(rw_k_k[l])
    p["rw_k_a"] = row(rw_k_a[l])
    p["rw_r_k"] = row(rw_r_k[l])
    p["rw_gn_g"] = row(rw_gn_g[l])
    p["rw_gn_b"] = row(rw_gn_b[l])
    ch = jnp.arange(PAIR) // RW_HEAD_DIM
    p["head_ones"] = (ch[:, None] == ch[None, :]).astype(BF16)
    p["w_out"] = w_out[l].astype(BF16)
    p["norm2_g"] = row(norm2_g[l])
    rw_t = router_w[l].astype(F32).T
    hi = rw_t.astype(BF16)
    p["router_hi"] = hi
    p["router_lo"] = (rw_t - hi.astype(F32)).astype(BF16)
    p["router_b"] = router_b[l].astype(F32).reshape(N_EXPERTS, 1)
    p["w_gu"] = w_gu[l]
    p["b_gu"] = b_gu[l].reshape(N_EXPERTS, 1, 2 * D_EXPERT).astype(F32)
    p["w_down"] = w_down[l]
    p["b_down"] = b_down[l].reshape(N_EXPERTS, 1, D_MODEL).astype(F32)
    p["final_g"] = row(final_g)
    return p


def _pick_tile(n, want):
    t = want
    while n % t:
        t //= 2
    return t


def _encoder(x, p):
    b_sz, t_len, d = x.shape
    n = b_sz * t_len
    x2d = x.reshape(n, d)
    gm, z = _mix_in(x2d, p, _pick_tile(t_len, 512), t_len)
    z3 = z.reshape(b_sz, t_len, Z_COLS)
    yf = _rwkv_pass(z3, None, p, reverse=False, bt=RW_BATCH_TILE)
    rw = _rwkv_pass(z3, yf, p, reverse=True, bt=RW_BATCH_TILE)
    tmo = _pick_tile(n, 512)
    x1, h2, route, route_t, counts = _mix_out(x2d, gm, rw.reshape(n, RW_WIDTH), p, tmo)
    ids = route_t[:TOP_K].astype(jnp.int32)
    ranks = route_t[2 * TOP_K:3 * TOP_K].astype(jnp.int32)
    cnt = counts[:, 0].astype(jnp.int32)
    padded = (cnt + MOE_ROWS - 1) // MOE_ROWS * MOE_ROWS
    pad_end = jnp.cumsum(padded)
    pad_start = pad_end - padded
    expert_ids = jnp.arange(N_EXPERTS, dtype=jnp.int32)
    start_of = jnp.sum(jnp.where(ids[..., None] == expert_ids, pad_start, 0), axis=-1)
    dest = start_of + ranks
    n_blocks = n * TOP_K // MOE_ROWS + N_EXPERTS
    blk_start = jnp.arange(n_blocks, dtype=jnp.int32) * MOE_ROWS
    blk_e = jnp.minimum(jnp.sum((pad_end[None, :] <= blk_start[:, None]).astype(jnp.int32), axis=1), N_EXPERTS - 1)
    n_used = (pad_end[-1] // MOE_ROWS).astype(jnp.int32).reshape(1)
    blk_onehot = blk_e[:, None] == expert_ids
    blk_cnt = jnp.sum(jnp.where(blk_onehot, cnt, 0), axis=-1)
    blk_first = jnp.sum(jnp.where(blk_onehot, pad_start, 0), axis=-1)
    valid = jnp.clip(blk_cnt - (blk_start - blk_first), 0, MOE_ROWS).astype(jnp.int32)
    window = min(SC_SCATTER_WINDOW, n // (SC_CORES * SC_SUBCORES))
    dest_w = dest.reshape(TOP_K, n // window, window).transpose(1, 0, 2)
    x_rows = _sc_scatter_rows(h2, dest_w, n_blocks * MOE_ROWS)
    y_rows = _experts(x_rows, blk_e, n_used, valid, p)
    yg = _sc_gather_rows(y_rows, dest.reshape(-1)).reshape(TOP_K, n, D_MODEL // 2)
    out = _combine(yg, x1, route, p["final_g"], _pick_tile(n, 512))
    return out.reshape(b_sz, t_len, d)


def kernel(x_prompt, x_sample, norm1_g, w_in, rw_mu, gm_ln_g, gm_ln_b, gm_ws, gm_bs, rw_w0, rw_w2, rw_a0, rw_a2,
           rw_g2, rw_k_k, rw_k_a, rw_r_k, rw_gn_g, rw_gn_b, w_out, norm2_g, router_w, router_b, w_gu, b_gu,
           w_down, b_down, final_g):
    p = _prepare(norm1_g, w_in, rw_mu, gm_ln_g, gm_ln_b, gm_ws, gm_bs, rw_w0, rw_w2, rw_a0, rw_a2, rw_g2,
                 rw_k_k, rw_k_a, rw_r_k, rw_gn_g, rw_gn_b, w_out, norm2_g, router_w, router_b,
                 w_gu, b_gu, w_down, b_down, final_g)
    return (_encoder(x_prompt, p), _encoder(x_sample, p))
```

```python
import functools

import jax
import jax.numpy as jnp
from jax import lax
from jax.experimental import pallas as pl
from jax.experimental.pallas import tpu as pltpu
from jax.experimental.pallas import tpu_sc as plsc

F32 = jnp.float32
BF16 = jnp.bfloat16

D_MODEL = 1024
GM_WIDTH = 512
RW_WIDTH = 512
GM_HEADS = 4
GM_HEAD_DIM = 128
GM_CHUNK = 128
RW_HEAD_DIM = 64
DECAY_LORA = 32
ICLR_LORA = 32
GATE_LORA = 96
LORA_COLS = DECAY_LORA + ICLR_LORA + GATE_LORA
LORA_PAD = 256
Z_COLS = 3 * RW_WIDTH + LORA_PAD
Z_CHUNK = 256
N_EXPERTS = 32
TOP_K = 4
D_EXPERT = 1024
SWIGLU_LIMIT = 7.0
SWIGLU_ALPHA = 1.702
RMS_EPS = 1e-5
LN_EPS = 1e-5
GN_EPS = 64e-5

LANES = 128
SUBLANES = 8
VMEM_LIMIT = 56 * 1024 * 1024

RW_CHUNK = 64
PAIR = 2 * RW_HEAD_DIM
N_PAIRS = RW_WIDTH // PAIR
RW_BATCH_TILE = 4
MOE_ROWS = 512
MOE_CHUNK = 256
SC_CORES = 2
SC_SUBCORES = 16
SC_WINDOW = 64
SC_SCATTER_WINDOW = 128
ROUTE_ROWS = 16
DECAY_SCALE = 0.6065306597126334
KK_NORM_FLOOR_SQ = 1e-24


def _dot(a, b):
    return jnp.dot(a, b, preferred_element_type=F32)


def _dot_nt(a, b):
    return lax.dot_general(a, b, (((1,), (1,)), ((), ())), preferred_element_type=F32)


def _split(x):
    hi = x.astype(BF16)
    lo = (x - hi.astype(F32)).astype(BF16)
    return hi, lo


def _dot3(a, b_hi, b_lo):
    a_hi, a_lo = _split(a)
    return _dot(a_hi, b_hi) + _dot(a_lo, b_hi) + _dot(a_hi, b_lo)


def _dot2(a, b01):
    a_hi, a_lo = _split(a)
    return _dot(a_hi, b01) + _dot(a_lo, b01)


def _head_sums(x, pair_ones):
    rows = x.shape[0]
    n_tiles = x.shape[1] // LANES
    stacked = jnp.concatenate([x[:, t * LANES:(t + 1) * LANES] for t in range(n_tiles)], axis=0)
    sums = _dot2(stacked, pair_ones)
    return jnp.concatenate([sums[t * rows:(t + 1) * rows] for t in range(n_tiles)], axis=1)


def _dot2_lhs01(a01, b):
    b_hi, b_lo = _split(b)
    return _dot(a01, b_hi) + _dot(a01, b_lo)


def _pack_bf16_halves(x):
    w = x.shape[1] // 2
    hi = lax.bitcast_convert_type(x[:, :w].astype(BF16).astype(F32), jnp.int32)
    lo = lax.bitcast_convert_type(x[:, w:].astype(BF16).astype(F32), jnp.int32)
    return hi | lax.shift_right_logical(lo, 16)


def _unpack_bf16_halves(words):
    left = lax.bitcast_convert_type(words & jnp.int32(-65536), F32)
    right = lax.bitcast_convert_type(lax.shift_left(words, 16), F32)
    return left, right


def _gelu_tanh(x):
    return 0.5 * x * (1.0 + jnp.tanh(0.7978845608028654 * (x + 0.044715 * (x * x * x))))


def _sigmoid(x):
    return 1.0 / (1.0 + jnp.exp(-x))


def _mix_in_body(tiles_per_seq, x_ref, xp_ref, xn_ref, g1_ref, wgm_ref, wz_ref, mu_ref, lng_ref, lnb_ref, ws_ref,
                 bs_ref, gm_ref, z_ref):
    i = pl.program_id(0)
    tm = x_ref.shape[0]

    def normed(x):
        return (x * lax.rsqrt(jnp.mean(x * x, axis=-1, keepdims=True) + RMS_EPS) * g1_ref[...]).astype(BF16)

    hb = normed(x_ref[...])
    uv = _dot(hb, wgm_ref[...])
    hb_halo = normed(jnp.concatenate([xp_ref[...], xn_ref[...]], axis=0))
    has_prev = i % tiles_per_seq > 0
    has_next = i % tiles_per_seq < tiles_per_seq - 1
    row8 = lax.broadcasted_iota(jnp.int32, (SUBLANES, Z_CHUNK), 0)
    for j in range(Z_COLS // Z_CHUNK):
        cols = slice(j * Z_CHUNK, (j + 1) * Z_CHUNK)
        z = _dot(hb, wz_ref[:, cols])
        halo = _dot(hb_halo, wz_ref[:, cols])
        prev_row = jnp.where(has_prev, halo[SUBLANES - 1:SUBLANES, :], 0.0)
        next_row = jnp.where(has_next, halo[SUBLANES:SUBLANES + 1, :], 0.0)
        down = pltpu.roll(z, 1, axis=0)
        up = pltpu.roll(z, tm - 1, axis=0)
        prev = jnp.concatenate([jnp.where(row8 == 0, prev_row, down[:SUBLANES]), down[SUBLANES:]], axis=0)
        nxt = jnp.concatenate([up[:tm - SUBLANES], jnp.where(row8 == SUBLANES - 1, next_row, up[tm - SUBLANES:])],
                              axis=0)
        z_ref[:, cols] = z * mu_ref[0:1, cols] + (prev + nxt) * mu_ref[1:2, cols]
    u = _gelu_tanh(uv[:, :GM_WIDTH])
    v = _gelu_tanh(uv[:, GM_WIDTH:])
    mean = jnp.mean(v, axis=-1, keepdims=True)
    vc = v - mean
    var = jnp.mean(vc * vc, axis=-1, keepdims=True)
    vn = (vc * lax.rsqrt(var + LN_EPS) * lng_ref[...] + lnb_ref[...]).astype(BF16)
    for c in range(tm // GM_CHUNK):
        rows = slice(c * GM_CHUNK, (c + 1) * GM_CHUNK)
        for hd in range(GM_HEADS):
            cols = slice(hd * GM_HEAD_DIM, (hd + 1) * GM_HEAD_DIM)
            mixed = _dot(ws_ref[hd], vn[rows, cols]) + bs_ref[hd]
            gm_ref[rows, cols] = (u[rows, cols] * mixed).astype(BF16)


def _mix_in(x2d, p, tm, t_len):
    n = x2d.shape[0]
    assert t_len % tm == 0
    halo_per_tile = tm // SUBLANES
    n_halo = n // SUBLANES
    full = lambda shape: pl.BlockSpec(shape, lambda i: (0,) * len(shape))
    return pl.pallas_call(
        functools.partial(_mix_in_body, t_len // tm),
        grid=(n // tm,),
        in_specs=[
            pl.BlockSpec((tm, D_MODEL), lambda i: (i, 0)),
            pl.BlockSpec((SUBLANES, D_MODEL), lambda i: (jnp.maximum(i * halo_per_tile - 1, 0), 0)),
            pl.BlockSpec((SUBLANES, D_MODEL), lambda i: (jnp.minimum((i + 1) * halo_per_tile, n_halo - 1), 0)),
            full((1, D_MODEL)),
            full((D_MODEL, 2 * GM_WIDTH)),
            full((D_MODEL, Z_COLS)),
            full((2, Z_COLS)),
            full((1, GM_WIDTH)),
            full((1, GM_WIDTH)),
            full((GM_HEADS, GM_CHUNK, GM_CHUNK)),
            full((GM_HEADS, GM_CHUNK, GM_HEAD_DIM)),
        ],
        out_specs=[
            pl.BlockSpec((tm, GM_WIDTH), lambda i: (i, 0)),
            pl.BlockSpec((tm, Z_COLS), lambda i: (i, 0)),
        ],
        out_shape=[
            jax.ShapeDtypeStruct((n, GM_WIDTH), BF16),
            jax.ShapeDtypeStruct((n, Z_COLS), F32),
        ],
        compiler_params=pltpu.CompilerParams(vmem_limit_bytes=VMEM_LIMIT),
        name="mix_in",
    )(x2d, x2d, x2d, p["norm1_g"], p["w_gm"], p["w_z"], p["rw_mu"], p["gm_ln_g"], p["gm_ln_b"], p["gm_ws"],
      p["gm_bs"])


def _rwkv_masks(reverse):
    c = RW_CHUNK
    ti = lax.broadcasted_iota(jnp.int32, (c, c), 0)
    tj = lax.broadcasted_iota(jnp.int32, (c, c), 1)
    incl = (tj >= ti) if reverse else (tj <= ti)
    lane =lax.broadcasted_iota(jnp.int32, (c, PAIR), 1)
    head0 = lane < RW_HEAD_DIM
    lane2 = lax.broadcasted_iota(jnp.int32, (c, 2 * c), 1)
    left = lane2 < c
    ti2 = lax.broadcasted_iota(jnp.int32, (c, 2 * c), 0)
    tj2 = jnp.where(left, lane2, lane2 - c)
    incl2 = (tj2 >= ti2) if reverse else (tj2 <= ti2)
    strict2 = (tj2 > ti2) if reverse else (tj2 < ti2)
    eye2 = jnp.where(tj2 == ti2, 1.0, 0.0).astype(F32)
    si = lax.broadcasted_iota(jnp.int32, (PAIR, PAIR), 0)
    sj = lax.broadcasted_iota(jnp.int32, (PAIR, PAIR), 1)
    same_head = (si // RW_HEAD_DIM) == (sj // RW_HEAD_DIM)
    return dict(
        incl2=incl2, strict2=strict2, head0=head0, left=left, eye2=eye2, same_head=same_head,
        tri=jnp.where(incl, 1.0, 0.0).astype(BF16),
    )


def _stack_heads(x, head0):
    return jnp.concatenate([jnp.where(head0, x, 0.0), jnp.where(head0, 0.0, x)], axis=0)


def _block_diag(x, left):
    return jnp.concatenate([jnp.where(left, x, 0.0), jnp.where(left, 0.0, x)], axis=0)


def _units_chunk(states, ops, m):
    c = RW_CHUNK
    units = range(len(ops))
    bf = lambda t: t.astype(BF16)
    head0, left = m["head0"], m["left"]
    ar = [bf(jnp.concatenate([o["a_t"], o["r_t"]], axis=0)) for o in ops]
    rstack = [bf(jnp.concatenate([_stack_heads(o["b_t"], head0), _stack_heads(o["k_t"], head0)], axis=0))
              for o in ops]
    acat = [_dot_nt(ar[i], rstack[i]) for i in units]
    sa = [_dot_nt(ar[i], bf(states[i])) for i in units]
    a_ab = [jnp.where(m["strict2"], acat[i][:c, :2 * c], 0.0) for i in units]
    a_ak = [bf(jnp.where(m["strict2"], acat[i][:c, 2 * c:], 0.0)) for i in units]
    a_r = [bf(jnp.concatenate([jnp.where(m["incl2"], acat[i][c:, :2 * c], 0.0),
                               jnp.where(m["incl2"], acat[i][c:, 2 * c:], 0.0)], axis=1)) for i in units]
    p = [m["eye2"] + a_ab[i] for i in units]
    lk = [_dot(bf(a_ab[i]), bf(_block_diag(a_ab[i], left))) for i in units]
    vstack = [bf(_stack_heads(o["v"], head0)) for o in ops]
    rhs = [sa[i][:c] + _dot(a_ak[i], vstack[i]) for i in units]
    power = 2
    while 2 * power < c:
        res = [_dot(bf(jnp.concatenate([p[i], lk[i]], axis=0)), bf(_block_diag(lk[i], left))) for i in units]
        p = [p[i] + res[i][:c] for i in units]
        lk = [res[i][c:] for i in units]
        power *= 2
    p = [p[i] + _dot(bf(p[i]), bf(_block_diag(lk[i], left))) for i in units]
    u = [_dot(bf(p[i]), bf(_stack_heads(rhs[i], head0))) for i in units]
    ys = [sa[i][c:] + _dot(a_r[i], jnp.concatenate([bf(_stack_heads(u[i], head0)), vstack[i]], axis=0))
          for i in units]
    upd = [_dot(bf(jnp.concatenate([u[i], ops[i]["v"]], axis=0).T),
                bf(jnp.concatenate([ops[i]["b_h"], ops[i]["k_h"]], axis=0))) for i in units]
    new_states = [states[i] * ops[i]["w_tot"] + jnp.where(m["same_head"], upd[i], 0.0) for i in units]
    return new_states, ys


def _rwkv_body(reverse, bt, *refs):
    if reverse:
        (zc_ref, yf_ref, lw_hi_ref, lw_lo_ref, w0_ref, a0_ref, a0o_ref, kk_ref, ka_ref,
         rk_ref, bd_ref, gng_ref, gnb_ref, out_ref, state_ref) = refs
    else:
        (zc_ref, lw_hi_ref, lw_lo_ref, w0_ref, a0_ref, kk_ref, ka_ref,
         bd_ref, out_ref, state_ref) = refs
    c = RW_CHUNK

    @pl.when(pl.program_id(1) == 0)
    def _():
        state_ref[...] = jnp.zeros_like(state_ref)

    rows = bt * c
    m = _rwkv_masks(reverse)
    lora_lane = lax.broadcasted_iota(jnp.int32, (rows, LORA_PAD), 1)
    bd = bd_ref[...]
    zf = zc_ref[...].reshape(rows, Z_COLS)
    r = zf[:, :RW_WIDTH]
    k = zf[:, RW_WIDTH:2 * RW_WIDTH]
    v = zf[:, 2 * RW_WIDTH:3 * RW_WIDTH]
    lo = zf[:, 3 * RW_WIDTH:]
    act = jnp.where(lora_lane < DECAY_LORA, jnp.tanh(lo),
                    jnp.where(lora_lane < DECAY_LORA + ICLR_LORA, lo, _sigmoid(lo)))
    proj = _dot3(act, lw_hi_ref[...], lw_lo_ref[...])
    w_in = w0_ref[...] + proj[:, :RW_WIDTH]
    logw = -DECAY_SCALE * _sigmoid(w_in)
    a_sig = _sigmoid(a0_ref[...] + proj[:, RW_WIDTH:2 * RW_WIDTH])
    kk = k * kk_ref[...]
    kk = kk * lax.rsqrt(jnp.maximum(_head_sums(kk * kk, bd), KK_NORM_FLOOR_SQ))
    kd = k * (1.0 + (a_sig - 1.0) * ka_ref[...])
    a_vec = -kk
    b_vec = kk * a_sig
    ri = lax.broadcasted_iota(jnp.int32, (rows, rows), 0)
    rj = lax.broadcasted_iota(jnp.int32, (rows, rows), 1)
    ordered = (rj >= ri) if reverse else (rj <= ri)
    tri = jnp.where((ri // c) == (rj // c), jnp.where(ordered, 1.0, 0.0), 0.0).astype(BF16)
    cum = _dot2_lhs01(tri, logw)
    edge = 0 if reverse else c - 1
    tot = jnp.concatenate([jnp.broadcast_to(cum[b * c + edge:b * c + edge + 1, :], (c, RW_WIDTH))
                           for b in range(bt)], axis=0)
    w_inv = jnp.exp(-cum)
    w_end = jnp.exp(tot - cum)
    w_tot = jnp.exp(tot)
    full_ops = dict(r_t=r * jnp.exp(cum), a_t=a_vec * jnp.exp(cum - logw), b_t=b_vec * w_inv, k_t=kd * w_inv,
                    b_h=b_vec * w_end, k_h=kd * w_end, v=v)
    ops, states = [], []
    for b in range(bt):
        for pi in range(N_PAIRS):
            ln = slice(pi * PAIR, (pi + 1) * PAIR)
            unit = {name: t[b * c:(b + 1) * c, ln] for name, t in full_ops.items()}
            unit["w_tot"] = w_tot[b * c:b * c + 1, ln]
            ops.append(unit)
            states.append(state_ref[b, pi])
    new_states, ys = _units_chunk(states, ops, m)
    for b in range(bt):
        for pi in range(N_PAIRS):
            state_ref[b, pi] = new_states[b * N_PAIRS + pi]
    y = jnp.concatenate([jnp.concatenate(ys[b * N_PAIRS:(b + 1) * N_PAIRS], axis=1) for b in range(bt)], axis=0)
    if not reverse:
        out_ref[...] = y.reshape(bt, c, RW_WIDTH)
    else:
        y = y + yf_ref[...].reshape(rows, RW_WIDTH)
        a_other = _sigmoid(a0o_ref[...] + proj[:, 2 * RW_WIDTH:3 * RW_WIDTH])
        kd_sum = kd + k * (1.0 + (a_other - 1.0) * ka_ref[...])
        bonus = _head_sums(r * kd_sum * rk_ref[...], bd) * v
        inv_n = 1.0 / RW_HEAD_DIM
        mean = _head_sums(y, bd) * inv_n
        yc = y - mean
        var = _head_sums(yc * yc, bd) * inv_n
        yn = yc * lax.rsqrt(var + GN_EPS) * gng_ref[...] + gnb_ref[...]
        g = proj[:, 3 * RW_WIDTH:]
        out_ref[...] = ((yn + bonus) * g).astype(BF16).reshape(bt, c, RW_WIDTH)


def _rwkv_pass(z3, yf, p, reverse, bt):
    b_sz, t_len, _ = z3.shape
    c = RW_CHUNK
    n_chunks = t_len // c

    def cidx(ci):
        return (n_chunks - 1 - ci) if reverse else ci

    full = lambda shape: pl.BlockSpec(shape, lambda bi, ci: (0,) * len(shape))
    row512 = full((1, RW_WIDTH))
    d = "rev" if reverse else "fwd"
    n_proj = p["lora_hi_" + d].shape[1]
    in_specs = [pl.BlockSpec((bt, c, Z_COLS), lambda bi, ci: (bi, cidx(ci), 0))]
    args = [z3]
    if reverse:
        in_specs.append(pl.BlockSpec((bt, c, RW_WIDTH), lambda bi, ci: (bi, cidx(ci), 0)))
        args.append(yf)
    in_specs += [full((LORA_PAD, n_proj)), full((LORA_PAD, n_proj)), row512, row512]
    args += [p["lora_hi_" + d], p["lora_lo_" + d], p["w0_" + d], p["a0_" + d]]
    if reverse:
        in_specs.append(row512)
        args.append(p["a0_fwd"])
    in_specs += [row512, row512]
    args += [p["rw_k_k"], p["rw_k_a"]]
    if reverse:
        in_specs.append(row512)
        args.append(p["rw_r_k"])
    in_specs.append(full((PAIR, PAIR)))
    args.append(p["head_ones"])
    if reverse:
        in_specs += [row512, row512]
        args += [p["rw_gn_g"], p["rw_gn_b"]]
    return pl.pallas_call(
        functools.partial(_rwkv_body, reverse, bt),
        grid=(b_sz // bt, n_chunks),
        in_specs=in_specs,
        out_specs=pl.BlockSpec((bt, c, RW_WIDTH), lambda bi, ci: (bi, cidx(ci), 0)),
        out_shape=jax.ShapeDtypeStruct((b_sz, t_len, RW_WIDTH), BF16 if reverse else F32),
        scratch_shapes=[pltpu.VMEM((bt, N_PAIRS, PAIR, PAIR), F32)],
        compiler_params=pltpu.CompilerParams(vmem_limit_bytes=VMEM_LIMIT),
        name="rwkv_" + d,
    )(*args)


def _mix_out_body(x_ref, gm_ref, rw_ref, wo_ref, g2_ref, rw_hi_ref, rw_lo_ref, rb_ref, earlier_ref,
                  x1_ref, h2_ref, route_ref, route_t_ref, counts_ref, carry_ref):
    i = pl.program_id(0)

    @pl.when(i == 0)
    def _():
        carry_ref[...] = jnp.zeros_like(carry_ref)

    x1 = x_ref[...] + _dot(gm_ref[...], wo_ref[:GM_WIDTH, :]) + _dot(rw_ref[...], wo_ref[GM_WIDTH:, :])
    x1_ref[...] = x1
    h2 = x1 * lax.rsqrt(jnp.mean(x1 * x1, axis=-1, keepdims=True) + RMS_EPS) * g2_ref[...]
    h2_ref[...] = _pack_bf16_halves(h2)
    tm = x1.shape[0]
    h_hi, h_lo = _split(h2)
    logits = (_dot_nt(rw_hi_ref[...], h_hi) + _dot_nt(rw_hi_ref[...], h_lo) + _dot_nt(rw_lo_ref[...], h_hi)
              + rb_ref[...])
    expert = lax.broadcasted_iota(jnp.int32, (N_EXPERTS, tm), 0).astype(F32)
    work = logits
    vals, ids, hits = [], [], []
    onehot = jnp.zeros((N_EXPERTS, tm), F32)
    for _ in range(TOP_K):
        mx = jnp.max(work, axis=0, keepdims=True)
        idx = jnp.min(jnp.where(work == mx, expert, float(N_EXPERTS)), axis=0, keepdims=True)
        hit = expert == idx
        vals.append(mx)
        ids.append(idx)
        hits.append(hit)
        onehot = jnp.where(hit, 1.0, onehot)
        work = jnp.where(hit, -jnp.inf, work)
    exps = [jnp.exp(vk - vals[0]) for vk in vals]
    denom = exps[0] + exps[1] + exps[2] + exps[3]
    carry = carry_ref[:, 0:1]
    ranks = carry + _dot(onehot.astype(BF16), earlier_ref[...])
    rank_rows = [jnp.sum(jnp.where(hit, ranks, 0.0), axis=0, keepdims=True) for hit in hits]
    route_t = jnp.concatenate(ids + [e / denom for e in exps] + rank_rows
                              + [jnp.zeros((ROUTE_ROWS - 3 * TOP_K, tm), F32)], axis=0)
    route_t_ref[...] = route_t
    route_ref[...] = jnp.concatenate([route_t, jnp.zeros((LANES - ROUTE_ROWS, tm), F32)], axis=0).T
    new_counts = carry + jnp.sum(onehot, axis=1, keepdims=True)
    carry_ref[...] = jnp.broadcast_to(new_counts, carry_ref.shape)
    counts_ref[...] = jnp.broadcast_to(new_counts, counts_ref.shape)


def _mix_out(x2d, gm, rw, p, tm):
    n = x2d.shape[0]
    full = lambda shape: pl.BlockSpec(shape, lambda i: (0,) * len(shape))
    tile = lambda w: pl.BlockSpec((tm, w), lambda i: (i, 0))
    earlier = (jnp.arange(tm)[:, None] < jnp.arange(tm)[None, :]).astype(BF16)
    return pl.pallas_call(
        _mix_out_body,
        grid=(n // tm,),
        in_specs=[tile(D_MODEL), tile(GM_WIDTH), tile(RW_WIDTH), full((D_MODEL, D_MODEL)), full((1, D_MODEL)),
                  full((N_EXPERTS, D_MODEL)), full((N_EXPERTS, D_MODEL)), full((N_EXPERTS, 1)), full((tm, tm))],
        out_specs=[tile(D_MODEL), tile(D_MODEL // 2), tile(LANES), pl.BlockSpec((ROUTE_ROWS, tm), lambda i: (0, i)),
                   full((N_EXPERTS, LANES))],
        out_shape=[
            jax.ShapeDtypeStruct((n, D_MODEL), F32),
            jax.ShapeDtypeStruct((n, D_MODEL // 2), jnp.int32),
            jax.ShapeDtypeStruct((n, LANES), F32),
            jax.ShapeDtypeStruct((ROUTE_ROWS, n), F32),
            jax.ShapeDtypeStruct((N_EXPERTS, LANES), F32),
        ],
        scratch_shapes=[pltpu.VMEM((N_EXPERTS, LANES), F32)],
        compiler_params=pltpu.CompilerParams(vmem_limit_bytes=VMEM_LIMIT),
        name="mix_out",
    )(x2d, gm, rw, p["w_out"], p["norm2_g"], p["router_hi"], p["router_lo"], p["router_b"], earlier)


def _sc_gather_rows(table, idx):
    n_idx = idx.shape[0]
    width = table.shape[1]
    n_workers = SC_CORES * SC_SUBCORES
    per_worker = n_idx // n_workers
    n_windows = per_worker // SC_WINDOW
    assert per_worker * n_workers == n_idx and n_windows * SC_WINDOW == per_worker
    mesh = plsc.VectorSubcoreMesh(core_axis_name="c", subcore_axis_name="s")

    @functools.partial(
        pl.kernel, mesh=mesh,
        out_type=jax.ShapeDtypeStruct((n_idx, width), table.dtype),
        scratch_types=[
            pltpu.VMEM((SC_WINDOW,), jnp.int32),
            pltpu.VMEM((SC_WINDOW, width), table.dtype),
            pltpu.SemaphoreType.DMA,
        ],
        name="sc_gather_rows",
    )
    def gather(table_hbm, idx_hbm, out_hbm, idx_v, rows_v, sem):
        worker = lax.axis_index("s") * SC_CORES + lax.axis_index("c")
        base = worker * per_worker

        @pl.loop(0, n_windows)
        def _(j):
            off = base + j * SC_WINDOW
            pltpu.sync_copy(idx_hbm.at[pl.ds(off, SC_WINDOW)], idx_v)
            pltpu.async_copy(table_hbm.at[idx_v], rows_v, sem).wait()
            pltpu.sync_copy(rows_v, out_hbm.at[pl.ds(off, SC_WINDOW)])

    return gather(table, idx)


def _sc_scatter_rows(src, dest_w, n_out):
    n, width = src.shape
    n_windows, top_k, window = dest_w.shape
    n_workers = SC_CORES * SC_SUBCORES
    per_worker = n_windows // n_workers
    assert n_windows * window == n and per_worker * n_workers == n_windows
    mesh = plsc.VectorSubcoreMesh(core_axis_name="c", subcore_axis_name="s")

    @functools.partial(
        pl.kernel, mesh=mesh,
        out_type=jax.ShapeDtypeStruct((n_out, width), src.dtype),
        scratch_types=[
            pltpu.VMEM((top_k, window), jnp.int32),
            pltpu.VMEM((window, width), src.dtype),
        ],
        name="sc_scatter_rows",
    )
    def scatter(src_hbm, dest_hbm, out_hbm, idx_v, rows_v):
        worker = lax.axis_index("s") * SC_CORES + lax.axis_index("c")

        @pl.loop(0, per_worker)
        def _(j):
            g = worker * per_worker + j
            pltpu.sync_copy(dest_hbm.at[g], idx_v)
            pltpu.sync_copy(src_hbm.at[pl.ds(g * window, window)], rows_v)
            for k in range(top_k):
                pltpu.sync_copy(rows_v, out_hbm.at[idx_v.at[k]])

    return scatter(src, dest_w)


def _experts_body(blk_e_ref, n_used_ref, valid_ref, x_ref, wgu_ref, bgu_ref, wd_ref, bd_ref, out_ref,
                  wgu_bf, wd_bf):
    i = pl.program_id(0)
    n_used = n_used_ref[0]

    @pl.when((i == 0) | (blk_e_ref[i] != blk_e_ref[jnp.maximum(i - 1, 0)]))
    def _():
        wgu_bf[...] = wgu_ref[0].astype(BF16)
        wd_bf[...] = wd_ref[0].astype(BF16)

    @pl.when(i < n_used)
    def _():
        row = lax.broadcasted_iota(jnp.int32, (MOE_ROWS, 1), 0)
        x_left, x_right = _unpack_bf16_halves(jnp.where(row < valid_ref[i], x_ref[...], 0))
        x_left = x_left.astype(BF16)
        x_right = x_right.astype(BF16)
        half = D_MODEL // 2

        def project(cols):
            return _dot(x_left, wgu_bf[:half, cols]) + _dot(x_right, wgu_bf[half:, cols]) + bgu_ref[0, :, cols]

        def gate_up(j):
            return (project(slice(j * MOE_CHUNK, (j + 1) * MOE_CHUNK)),
                    project(slice(D_EXPERT + j * MOE_CHUNK, D_EXPERT + (j + 1) * MOE_CHUNK)))

        n_chunks = D_EXPERT // MOE_CHUNK
        y = bd_ref[0]
        pending = gate_up(0)
        for j in range(n_chunks):
            following = gate_up(j + 1) if j + 1 < n_chunks else None
            gate = jnp.minimum(pending[0], SWIGLU_LIMIT)
            up = jnp.clip(pending[1], -SWIGLU_LIMIT, SWIGLU_LIMIT)
            act = gate * _sigmoid(gate * SWIGLU_ALPHA) * (up + 1.0)
            y = y + _dot(act.astype(BF16), wd_bf[j * MOE_CHUNK:(j + 1) * MOE_CHUNK, :])
            pending = following
        out_ref[...] = _pack_bf16_halves(y)

    @pl.when(i >= n_used)
    def _():
        out_ref[...] = jnp.zeros_like(out_ref)


def _experts(x_rows, blk_e, n_used, valid, p):
    n_blocks = blk_e.shape[0]
    grid_spec = pltpu.PrefetchScalarGridSpec(
        num_scalar_prefetch=3,
        grid=(n_blocks,),
        in_specs=[
            pl.BlockSpec((MOE_ROWS, D_MODEL // 2), lambda i, be, nu, va: (jnp.minimum(i, nu[0] - 1), 0)),
            pl.BlockSpec((1, D_MODEL, 2 * D_EXPERT), lambda i, be, nu, va: (be[i], 0, 0)),
            pl.BlockSpec((1, 1, 2 * D_EXPERT), lambda i, be, nu, va: (be[i], 0, 0)),
            pl.BlockSpec((1, D_EXPERT, D_MODEL), lambda i, be, nu, va: (be[i], 0, 0)),
            pl.BlockSpec((1, 1, D_MODEL), lambda i, be, nu, va: (be[i], 0, 0)),
        ],
        out_specs=pl.BlockSpec((MOE_ROWS, D_MODEL // 2), lambda i, be, nu, va: (i, 0)),
        scratch_shapes=[pltpu.VMEM((D_MODEL, 2 * D_EXPERT), BF16), pltpu.VMEM((D_EXPERT, D_MODEL), BF16)],
    )
    return pl.pallas_call(
        _experts_body,
        grid_spec=grid_spec,
        out_shape=jax.ShapeDtypeStruct((n_blocks * MOE_ROWS, D_MODEL // 2), jnp.int32),
        compiler_params=pltpu.CompilerParams(vmem_limit_bytes=VMEM_LIMIT),
        name="experts",
    )(blk_e, n_used, valid, x_rows, p["w_gu"], p["b_gu"], p["w_down"], p["b_down"])


def _combine_body(yg_ref, x1_ref, gates_ref, fg_ref, out_ref):
    gates = gates_ref[...]
    half = D_MODEL // 2
    acc_left = x1_ref[:, :half]
    acc_right = x1_ref[:, half:]
    for kk in range(TOP_K):
        y_left, y_right = _unpack_bf16_halves(yg_ref[kk])
        gate = gates[:, TOP_K + kk:TOP_K + kk + 1]
        acc_left = acc_left + gate * y_left
        acc_right = acc_right + gate * y_right
    acc = jnp.concatenate([acc_left, acc_right], axis=1)
    out_ref[...] = acc * lax.rsqrt(jnp.mean(acc * acc, axis=-1, keepdims=True) + RMS_EPS) * fg_ref[...]


def _combine(yg, x1, gates, final_g, tm):
    n = x1.shape[0]
    return pl.pallas_call(
        _combine_body,
        grid=(n // tm,),
        in_specs=[
            pl.BlockSpec((TOP_K, tm, D_MODEL // 2), lambda i: (0, i, 0)),
            pl.BlockSpec((tm, D_MODEL), lambda i: (i, 0)),
            pl.BlockSpec((tm, LANES), lambda i: (i, 0)),
            pl.BlockSpec((1, D_MODEL), lambda i: (0, 0)),
        ],
        out_specs=pl.BlockSpec((tm, D_MODEL), lambda i: (i, 0)),
        out_shape=jax.ShapeDtypeStruct((n, D_MODEL), F32),
        compiler_params=pltpu.CompilerParams(vmem_limit_bytes=VMEM_LIMIT),
        name="combine",
    )(yg, x1, gates, final_g)


def _prepare(norm1_g, w_in, rw_mu, gm_ln_g, gm_ln_b, gm_ws, gm_bs, rw_w0, rw_w2, rw_a0, rw_a2, rw_g2,
             rw_k_k, rw_k_a, rw_r_k, rw_gn_g, rw_gn_b, w_out, norm2_g, router_w, router_b,
             w_gu, b_gu, w_down, b_down, final_g):
    l = 0
    row = lambda t: t.reshape(1, -1).astype(F32)
    p = {}
    p["norm1_g"] = row(norm1_g[l])
    w = w_in[l]
    p["w_gm"] = w[:, :2 * GM_WIDTH].astype(BF16)
    pad_cols = LORA_PAD - LORA_COLS
    p["w_z"] = jnp.pad(w[:, 2 * GM_WIDTH:], ((0, 0), (0, pad_cols))).astype(BF16)
    mu = jnp.pad(row(rw_mu[l]), ((0, 0), (0, pad_cols)))
    p["rw_mu"] = jnp.concatenate([1.0 - mu, 0.5 * mu], axis=0)
    p["gm_ln_g"] = row(gm_ln_g[l])
    p["gm_ln_b"] = row(gm_ln_b[l])
    p["gm_ws"] = gm_ws[l].astype(BF16)
    p["gm_bs"] = jnp.broadcast_to(gm_bs[l][:, :, None], (GM_HEADS, GM_CHUNK, GM_HEAD_DIM)).astype(F32)
    zeros = lambda r, c: jnp.zeros((r, c), F32)
    o_a = DECAY_LORA
    o_g = DECAY_LORA + ICLR_LORA

    def lora_matrix(d, with_epilogue):
        blocks = [jnp.concatenate([rw_w2[l, d], zeros(LORA_PAD - DECAY_LORA, RW_WIDTH)], axis=0),
                  jnp.concatenate([zeros(o_a, RW_WIDTH), rw_a2[l, d], zeros(LORA_PAD - o_g, RW_WIDTH)], axis=0)]
        if with_epilogue:
            blocks.append(jnp.concatenate([zeros(o_a, RW_WIDTH), rw_a2[l, 1 - d], zeros(LORA_PAD - o_g, RW_WIDTH)],
                                          axis=0))
            blocks.append(jnp.concatenate([zeros(o_g, RW_WIDTH), rw_g2[l], zeros(LORA_PAD - LORA_COLS, RW_WIDTH)],
                                          axis=0))
        return jnp.concatenate(blocks, axis=1)

    for d, name in ((0, "fwd"), (1, "rev")):
        mat = lora_matrix(d, with_epilogue=(d == 1))
        hi = mat.astype(BF16)
        p["lora_hi_" + name] = hi
        p["lora_lo_" + name] = (mat - hi.astype(F32)).astype(BF16)
        p["w0_" + name] = row(rw_w0[l, d])
        p["a0_" + name] = row(rw_a0[l, d])
    p["rw_k_k"] = row(rw_k_k[l])
    p["rw_k_a"] = row(rw_k_a[l])
    p["rw_r_k"] = row(rw_r_k[l])
    p["rw_gn_g"] = row(rw_gn_g[l])
    p["rw_gn_b"] = row(rw_gn_b[l])
    ch = jnp.arange(PAIR) // RW_HEAD_DIM
    p["head_ones"] = (ch[:, None] == ch[None, :]).astype(BF16)
    p["w_out"] = w_out[l].astype(BF16)
    p["norm2_g"] = row(norm2_g[l])
    rw_t = router_w[l].astype(F32).T
    hi = rw_t.astype(BF16)
    p["router_hi"] = hi
    p["router_lo"] = (rw_t - hi.astype(F32)).astype(BF16)
    p["router_b"] = router_b[l].astype(F32).reshape(N_EXPERTS, 1)
    p["w_gu"] = w_gu[l]
    p["b_gu"] = b_gu[l].reshape(N_EXPERTS, 1, 2 * D_EXPERT).astype(F32)
    p["w_down"] = w_down[l]
    p["b_down"] = b_down[l].reshape(N_EXPERTS, 1, D_MODEL).astype(F32)
    p["final_g"] = row(final_g)
    return p


def _pick_tile(n, want):
    t = want
    while n % t:
        t //= 2
    return t


def _encoder(x, p):
    b_sz, t_len, d = x.shape
    n = b_sz * t_len
    x2d = x.reshape(n, d)
    gm, z = _mix_in(x2d, p, _pick_tile(t_len, 512), t_len)
    z3 = z.reshape(b_sz, t_len, Z_COLS)
    yf = _rwkv_pass(z3, None, p, reverse=False, bt=RW_BATCH_TILE)
    rw = _rwkv_pass(z3, yf, p, reverse=True, bt=RW_BATCH_TILE)
    tmo = _pick_tile(n, 512)
    x1, h2, route, route_t, counts = _mix_out(x2d, gm, rw.reshape(n, RW_WIDTH), p, tmo)
    ids = route_t[:TOP_K].astype(jnp.int32)
    ranks = route_t[2 * TOP_K:3 * TOP_K].astype(jnp.int32)
    cnt = counts[:, 0].astype(jnp.int32)
    padded = (cnt + MOE_ROWS - 1) // MOE_ROWS * MOE_ROWS
    pad_end = jnp.cumsum(padded)
    pad_start = pad_end - padded
    expert_ids = jnp.arange(N_EXPERTS, dtype=jnp.int32)
    start_of = jnp.sum(jnp.where(ids[..., None] == expert_ids, pad_start, 0), axis=-1)
    dest = start_of + ranks
    n_blocks = n * TOP_K // MOE_ROWS + N_EXPERTS
    blk_start = jnp.arange(n_blocks, dtype=jnp.int32) * MOE_ROWS
    blk_e = jnp.minimum(jnp.sum((pad_end[None, :] <= blk_start[:, None]).astype(jnp.int32), axis=1), N_EXPERTS - 1)
    n_used = (pad_end[-1] // MOE_ROWS).astype(jnp.int32).reshape(1)
    blk_onehot = blk_e[:, None] == expert_ids
    blk_cnt = jnp.sum(jnp.where(blk_onehot, cnt, 0), axis=-1)
    blk_first = jnp.sum(jnp.where(blk_onehot, pad_start, 0), axis=-1)
    valid = jnp.clip(blk_cnt - (blk_start - blk_first), 0, MOE_ROWS).astype(jnp.int32)
    window = min(SC_SCATTER_WINDOW, n // (SC_CORES * SC_SUBCORES))
    dest_w = dest.reshape(TOP_K, n // window, window).transpose(1, 0, 2)
    x_rows = _sc_scatter_rows(h2, dest_w, n_blocks * MOE_ROWS)
    y_rows = _experts(x_rows, blk_e, n_used, valid, p)
    yg = _sc_gather_rows(y_rows, dest.reshape(-1)).reshape(TOP_K, n, D_MODEL // 2)
    out = _combine(yg, x1, route, p["final_g"], _pick_tile(n, 512))
    return out.reshape(b_sz, t_len, d)


def kernel(x_prompt, x_sample, norm1_g, w_in, rw_mu, gm_ln_g, gm_ln_b, gm_ws, gm_bs, rw_w0, rw_w2, rw_a0, rw_a2,
           rw_g2, rw_k_k, rw_k_a, rw_r_k, rw_gn_g, rw_gn_b, w_out, norm2_g, router_w, router_b, w_gu, b_gu,
           w_down, b_down, final_g):
    p = _prepare(norm1_g, w_in, rw_mu, gm_ln_g, gm_ln_b, gm_ws, gm_bs, rw_w0, rw_w2, rw_a0, rw_a2, rw_g2,
                 rw_k_k, rw_k_a, rw_r_k, rw_gn_g, rw_gn_b, w_out, norm2_g, router_w, router_b,
                 w_gu, b_gu, w_down, b_down, final_g)
    return (_encoder(x_prompt, p), _encoder(x_sample, p))
```

```python
import functools

import jax
import jax.numpy as jnp
from jax import lax
from jax.experimental import pallas as pl
from jax.experimental.pallas import tpu as pltpu
from jax.experimental.pallas import tpu_sc as plsc

F32 = jnp.float32
BF16 = jnp.bfloat16

D_MODEL = 1024
GM_WIDTH = 512
RW_WIDTH = 512
GM_HEADS = 4
GM_HEAD_DIM = 128
GM_CHUNK = 128
RW_HEAD_DIM = 64
DECAY_LORA = 32
ICLR_LORA = 32
GATE_LORA = 96
LORA_COLS = DECAY_LORA + ICLR_LORA + GATE_LORA
LORA_PAD = 256
Z_COLS = 3 * RW_WIDTH + LORA_PAD
Z_CHUNK = 256
N_EXPERTS = 32
TOP_K = 4
D_EXPERT = 1024
SWIGLU_LIMIT = 7.0
SWIGLU_ALPHA = 1.702
RMS_EPS = 1e-5
LN_EPS = 1e-5
GN_EPS = 64e-5

LANES = 128
SUBLANES = 8
VMEM_LIMIT = 56 * 1024 * 1024

RW_CHUNK = 64
PAIR = 2 * RW_HEAD_DIM
N_PAIRS = RW_WIDTH // PAIR
RW_BATCH_TILE = 4
MIX_OUT_SUB = 128
MOE_ROWS = 512
SC_CORES = 2
SC_SUBCORES = 16
SC_WINDOW = 64
SC_SCATTER_WINDOW = 128
ROUTE_ROWS = 16
DECAY_SCALE = 0.6065306597126334
KK_NORM_FLOOR_SQ = 1e-24


def _dot(a, b):
    return jnp.dot(a, b, preferred_element_type=F32)


def _dot_nt(a, b):
    return lax.dot_general(a, b, (((1,), (1,)), ((), ())), preferred_element_type=F32)


def _split(x):
    hi = x.astype(BF16)
    lo = (x - hi.astype(F32)).astype(BF16)
    return hi, lo


def _dot3(a, b_hi, b_lo):
    a_hi, a_lo = _split(a)
    return _dot(a_hi, b_hi) + _dot(a_lo, b_hi) + _dot(a_hi, b_lo)


def _dot2(a, b01):
    a_hi, a_lo = _split(a)
    return _dot(a_hi, b01) + _dot(a_lo, b01)


def _head_sums(x, pair_ones):
    rows = x.shape[0]
    n_tiles = x.shape[1] // LANES
    stacked = jnp.concatenate([x[:, t * LANES:(t + 1) * LANES] for t in range(n_tiles)], axis=0)
    sums = _dot2(stacked, pair_ones)
    return jnp.concatenate([sums[t * rows:(t + 1) * rows] for t in range(n_tiles)], axis=1)


def _dot2_lhs01(a01, b):
    b_hi, b_lo = _split(b)
    return _dot(a01, b_hi) + _dot(a01, b_lo)


def _pack_bf16_halves(x):
    w = x.shape[1] // 2
    hi = lax.bitcast_convert_type(x[:, :w].astype(BF16).astype(F32), jnp.int32)
    lo = lax.bitcast_convert_type(x[:, w:].astype(BF16).astype(F32), jnp.int32)
    return hi | lax.shift_right_logical(lo, 16)


def _unpack_bf16_halves(words):
    left = lax.bitcast_convert_type(words & jnp.int32(-65536), F32)
    right = lax.bitcast_convert_type(lax.shift_left(words, 16), F32)
    return left, right


def _gelu_tanh(x):
    return 0.5 * x * (1.0 + jnp.tanh(0.7978845608028654 * (x + 0.044715 * (x * x * x))))


def _sigmoid(x):
    return 1.0 / (1.0 + jnp.exp(-x))


def _mix_in_body(tiles_per_seq, x_ref, xp_ref, xn_ref, g1_ref, wgm_ref, wz_ref, mu_ref, lng_ref, lnb_ref, ws_ref,
                 bs_ref, gm_ref, z_ref):
    i = pl.program_id(0)
    tm = x_ref.shape[0]

    def normed(x):
        return (x * lax.rsqrt(jnp.mean(x * x, axis=-1, keepdims=True) + RMS_EPS) * g1_ref[...]).astype(BF16)

    hb = normed(x_ref[...])
    uv = _dot(hb, wgm_ref[...])
    hb_halo = normed(jnp.concatenate([xp_ref[...], xn_ref[...]], axis=0))
    has_prev = i % tiles_per_seq > 0
    has_next = i % tiles_per_seq < tiles_per_seq - 1
    row8 = lax.broadcasted_iota(jnp.int32, (SUBLANES, Z_CHUNK), 0)
    for j in range(Z_COLS // Z_CHUNK):
        cols = slice(j * Z_CHUNK, (j + 1) * Z_CHUNK)
        z = _dot(hb, wz_ref[:, cols])
        halo = _dot(hb_halo, wz_ref[:, cols])
        prev_row = jnp.where(has_prev, halo[SUBLANES - 1:SUBLANES, :], 0.0)
        next_row = jnp.where(has_next, halo[SUBLANES:SUBLANES + 1, :], 0.0)
        down = pltpu.roll(z, 1, axis=0)
        up = pltpu.roll(z, tm - 1, axis=0)
        prev = jnp.concatenate([jnp.where(row8 == 0, prev_row, down[:SUBLANES]), down[SUBLANES:]], axis=0)
        nxt = jnp.concatenate([up[:tm - SUBLANES], jnp.where(row8 == SUBLANES - 1, next_row, up[tm - SUBLANES:])],
                              axis=0)
        z_ref[:, cols] = z * mu_ref[0:1, cols] + (prev + nxt) * mu_ref[1:2, cols]
    u = _gelu_tanh(uv[:, :GM_WIDTH])
    v = _gelu_tanh(uv[:, GM_WIDTH:])
    mean = jnp.mean(v, axis=-1, keepdims=True)
    vc = v - mean
    var = jnp.mean(vc * vc, axis=-1, keepdims=True)
    vn = (vc * lax.rsqrt(var + LN_EPS) * lng_ref[...] + lnb_ref[...]).astype(BF16)
    for c in range(tm // GM_CHUNK):
        rows = slice(c * GM_CHUNK, (c + 1) * GM_CHUNK)
        for hd in range(GM_HEADS):
            cols = slice(hd * GM_HEAD_DIM, (hd + 1) * GM_HEAD_DIM)
            mixed = _dot(ws_ref[hd], vn[rows, cols]) + bs_ref[hd]
            gm_ref[rows, cols] = (u[rows, cols] * mixed).astype(BF16)


def _mix_in(x2d, p, tm, t_len):
    n = x2d.shape[0]
    assert t_len % tm == 0
    halo_per_tile = tm // SUBLANES
    n_halo = n // SUBLANES
    full = lambda shape: pl.BlockSpec(shape, lambda i: (0,) * len(shape))
    return pl.pallas_call(
        functools.partial(_mix_in_body, t_len // tm),
        grid=(n // tm,),
        in_specs=[
            pl.BlockSpec((tm, D_MODEL), lambda i: (i, 0)),
            pl.BlockSpec((SUBLANES, D_MODEL), lambda i: (jnp.maximum(i * halo_per_tile - 1, 0), 0)),
            pl.BlockSpec((SUBLANES, D_MODEL), lambda i: (jnp.minimum((i + 1) * halo_per_tile, n_halo - 1), 0)),
            full((1, D_MODEL)),
            full((D_MODEL, 2 * GM_WIDTH)),
            full((D_MODEL, Z_COLS)),
            full((2, Z_COLS)),
            full((1, GM_WIDTH)),
            full((1, GM_WIDTH)),
            full((GM_HEADS, GM_CHUNK, GM_CHUNK)),
            full((GM_HEADS, GM_CHUNK, GM_HEAD_DIM)),
        ],
        out_specs=[
            pl.BlockSpec((tm, GM_WIDTH), lambda i: (i, 0)),
            pl.BlockSpec((tm, Z_COLS), lambda i: (i, 0)),
        ],
        out_shape=[
            jax.ShapeDtypeStruct((n, GM_WIDTH), BF16),
            jax.ShapeDtypeStruct((n, Z_COLS), F32),
        ],
        compiler_params=pltpu.CompilerParams(vmem_limit_bytes=VMEM_LIMIT),
        name="mix_in",
    )(x2d, x2d, x2d, p["norm1_g"], p["w_gm"], p["w_z"], p["rw_mu"], p["gm_ln_g"], p["gm_ln_b"], p["gm_ws"],
      p["gm_bs"])


def _rwkv_masks(reverse):
    c = RW_CHUNK
    ti = lax.broadcasted_iota(jnp.int32, (c, c), 0)
    tj = lax.broadcasted_iota(jnp.int32, (c, c), 1)
    incl = (tj >= ti) if reverse else (tj <= ti)
    lane =lax.broadcasted_iota(jnp.int32, (c, PAIR), 1)
    head0 = lane < RW_HEAD_DIM
    lane2 = lax.broadcasted_iota(jnp.int32, (c, 2 * c), 1)
    left = lane2 < c
    ti2 = lax.broadcasted_iota(jnp.int32, (c, 2 * c), 0)
    tj2 = jnp.where(left, lane2, lane2 - c)
    incl2 = (tj2 >= ti2) if reverse else (tj2 <= ti2)
    strict2 = (tj2 > ti2) if reverse else (tj2 < ti2)
    eye2 = jnp.where(tj2 == ti2, 1.0, 0.0).astype(F32)
    si = lax.broadcasted_iota(jnp.int32, (PAIR, PAIR), 0)
    sj = lax.broadcasted_iota(jnp.int32, (PAIR, PAIR), 1)
    same_head = (si // RW_HEAD_DIM) == (sj // RW_HEAD_DIM)
    return dict(
        incl2=incl2, strict2=strict2, head0=head0, left=left, eye2=eye2, same_head=same_head,
        tri=jnp.where(incl, 1.0, 0.0).astype(BF16),
    )


def _stack_heads(x, head0):
    return jnp.concatenate([jnp.where(head0, x, 0.0), jnp.where(head0, 0.0, x)], axis=0)


def _block_diag(x, left):
    return jnp.concatenate([jnp.where(left, x, 0.0), jnp.where(left, 0.0, x)], axis=0)


def _units_chunk(states, ops, m):
    c = RW_CHUNK
    units = range(len(ops))
    bf = lambda t: t.astype(BF16)
    head0, left = m["head0"], m["left"]
    ar = [bf(jnp.concatenate([o["a_t"], o["r_t"]], axis=0)) for o in ops]
    rstack = [bf(jnp.concatenate([_stack_heads(o["b_t"], head0), _stack_heads(o["k_t"], head0)], axis=0))
              for o in ops]
    acat = [_dot_nt(ar[i], rstack[i]) for i in units]
    sa = [_dot_nt(ar[i], bf(states[i])) for i in units]
    a_ab = [jnp.where(m["strict2"], acat[i][:c, :2 * c], 0.0) for i in units]
    a_ak = [bf(jnp.where(m["strict2"], acat[i][:c, 2 * c:], 0.0)) for i in units]
    a_r = [bf(jnp.concatenate([jnp.where(m["incl2"], acat[i][c:, :2 * c], 0.0),
                               jnp.where(m["incl2"], acat[i][c:, 2 * c:], 0.0)], axis=1)) for i in units]
    p = [m["eye2"] + a_ab[i] for i in units]
    lk = [_dot(bf(a_ab[i]), bf(_block_diag(a_ab[i], left))) for i in units]
    vstack = [bf(_stack_heads(o["v"], head0)) for o in ops]
    rhs = [sa[i][:c] + _dot(a_ak[i], vstack[i]) for i in units]
    power = 2
    while 2 * power < c:
        res = [_dot(bf(jnp.concatenate([p[i], lk[i]], axis=0)), bf(_block_diag(lk[i], left))) for i in units]
        p = [p[i] + res[i][:c] for i in units]
        lk = [res[i][c:] for i in units]
        power *= 2
    p = [p[i] + _dot(bf(p[i]), bf(_block_diag(lk[i], left))) for i in units]
    u = [_dot(bf(p[i]), bf(_stack_heads(rhs[i], head0))) for i in units]
    ys = [sa[i][c:] + _dot(a_r[i], jnp.concatenate([bf(_stack_heads(u[i], head0)), vstack[i]], axis=0))
          for i in units]
    upd = [_dot(bf(jnp.concatenate([u[i], ops[i]["v"]], axis=0).T),
                bf(jnp.concatenate([ops[i]["b_h"], ops[i]["k_h"]], axis=0))) for i in units]
    new_states = [states[i] * ops[i]["w_tot"] + jnp.where(m["same_head"], upd[i], 0.0) for i in units]
    return new_states, ys


def _rwkv_body(reverse, bt, *refs):
    if reverse:
        (zc_ref, yf_ref, lw_hi_ref, lw_lo_ref, w0_ref, a0_ref, a0o_ref, kk_ref, ka_ref,
         rk_ref, bd_ref, gng_ref, gnb_ref, out_ref, state_ref) = refs
    else:
        (zc_ref, lw_hi_ref, lw_lo_ref, w0_ref, a0_ref, kk_ref, ka_ref,
         bd_ref, out_ref, state_ref) = refs
    c = RW_CHUNK

    @pl.when(pl.program_id(1) == 0)
    def _():
        state_ref[...] = jnp.zeros_like(state_ref)

    rows = bt * c
    m = _rwkv_masks(reverse)
    lora_lane = lax.broadcasted_iota(jnp.int32, (rows, LORA_PAD), 1)
    bd = bd_ref[...]
    zf = zc_ref[...].reshape(rows, Z_COLS)
    r = zf[:, :RW_WIDTH]
    k = zf[:, RW_WIDTH:2 * RW_WIDTH]
    v = zf[:, 2 * RW_WIDTH:3 * RW_WIDTH]
    lo = zf[:, 3 * RW_WIDTH:]
    act = jnp.where(lora_lane < DECAY_LORA, jnp.tanh(lo),
                    jnp.where(lora_lane < DECAY_LORA + ICLR_LORA, lo, _sigmoid(lo)))
    proj = _dot3(act, lw_hi_ref[...], lw_lo_ref[...])
    w_in = w0_ref[...] + proj[:, :RW_WIDTH]
    logw = -DECAY_SCALE * _sigmoid(w_in)
    a_sig = _sigmoid(a0_ref[...] + proj[:, RW_WIDTH:2 * RW_WIDTH])
    kk = k * kk_ref[...]
    kk = kk * lax.rsqrt(jnp.maximum(_head_sums(kk * kk, bd), KK_NORM_FLOOR_SQ))
    kd = k * (1.0 + (a_sig - 1.0) * ka_ref[...])
    a_vec = -kk
    b_vec = kk * a_sig
    ri = lax.broadcasted_iota(jnp.int32, (rows, rows), 0)
    rj = lax.broadcasted_iota(jnp.int32, (rows, rows), 1)
    ordered = (rj >= ri) if reverse else (rj <= ri)
    tri = jnp.where((ri // c) == (rj // c), jnp.where(ordered, 1.0, 0.0), 0.0).astype(BF16)
    cum = _dot2_lhs01(tri, logw)
    edge = 0 if reverse else c - 1
    tot = jnp.concatenate([jnp.broadcast_to(cum[b * c + edge:b * c + edge + 1, :], (c, RW_WIDTH))
                           for b in range(bt)], axis=0)
    w_inv = jnp.exp(-cum)
    w_end = jnp.exp(tot - cum)
    w_tot = jnp.exp(tot)
    full_ops = dict(r_t=r * jnp.exp(cum), a_t=a_vec * jnp.exp(cum - logw), b_t=b_vec * w_inv, k_t=kd * w_inv,
                    b_h=b_vec * w_end, k_h=kd * w_end, v=v)
    ops, states = [], []
    for b in range(bt):
        for pi in range(N_PAIRS):
            ln = slice(pi * PAIR, (pi + 1) * PAIR)
            unit = {name: t[b * c:(b + 1) * c, ln] for name, t in full_ops.items()}
            unit["w_tot"] = w_tot[b * c:b * c + 1, ln]
            ops.append(unit)
            states.append(state_ref[b, pi])
    new_states, ys = _units_chunk(states, ops, m)
    for b in range(bt):
        for pi in range(N_PAIRS):
            state_ref[b, pi] = new_states[b * N_PAIRS + pi]
    y = jnp.concatenate([jnp.concatenate(ys[b * N_PAIRS:(b + 1) * N_PAIRS], axis=1) for b in range(bt)], axis=0)
    if not reverse:
        out_ref[...] = y.reshape(bt, c, RW_WIDTH)
    else:
        y = y + yf_ref[...].reshape(rows, RW_WIDTH)
        a_other = _sigmoid(a0o_ref[...] + proj[:, 2 * RW_WIDTH:3 * RW_WIDTH])
        kd_sum = kd + k * (1.0 + (a_other - 1.0) * ka_ref[...])
        bonus = _head_sums(r * kd_sum * rk_ref[...], bd) * v
        inv_n = 1.0 / RW_HEAD_DIM
        mean = _head_sums(y, bd) * inv_n
        yc = y - mean
        var = _head_sums(yc * yc, bd) * inv_n
        yn = yc * lax.rsqrt(var + GN_EPS) * gng_ref[...] + gnb_ref[...]
        g = proj[:, 3 * RW_WIDTH:]
        out_ref[...] = ((yn + bonus) * g).astype(BF16).reshape(bt, c, RW_WIDTH)


def _rwkv_pass(z3, yf, p, reverse, bt):
    b_sz, t_len, _ = z3.shape
    c = RW_CHUNK
    n_chunks = t_len // c

    def cidx(ci):
        return (n_chunks - 1 - ci) if reverse else ci

    full = lambda shape: pl.BlockSpec(shape, lambda bi, ci: (0,) * len(shape))
    row512 = full((1, RW_WIDTH))
    d = "rev" if reverse else "fwd"
    n_proj = p["lora_hi_" + d].shape[1]
    in_specs = [pl.BlockSpec((bt, c, Z_COLS), lambda bi, ci: (bi, cidx(ci), 0))]
    args = [z3]
    if reverse:
        in_specs.append(pl.BlockSpec((bt, c, RW_WIDTH), lambda bi, ci: (bi, cidx(ci), 0)))
        args.append(yf)
    in_specs += [full((LORA_PAD, n_proj)), full((LORA_PAD, n_proj)), row512, row512]
    args += [p["lora_hi_" + d], p["lora_lo_" + d], p["w0_" + d], p["a0_" + d]]
    if reverse:
        in_specs.append(row512)
        args.append(p["a0_fwd"])
    in_specs += [row512, row512]
    args += [p["rw_k_k"], p["rw_k_a"]]
    if reverse:
        in_specs.append(row512)
        args.append(p["rw_r_k"])
    in_specs.append(full((PAIR, PAIR)))
    args.append(p["head_ones"])
    if reverse:
        in_specs += [row512, row512]
        args += [p["rw_gn_g"], p["rw_gn_b"]]
    return pl.pallas_call(
        functools.partial(_rwkv_body, reverse, bt),
        grid=(b_sz // bt, n_chunks),
        in_specs=in_specs,
        out_specs=pl.BlockSpec((bt, c, RW_WIDTH), lambda bi, ci: (bi, cidx(ci), 0)),
        out_shape=jax.ShapeDtypeStruct((b_sz, t_len, RW_WIDTH), BF16 if reverse else F32),
        scratch_shapes=[pltpu.VMEM((bt, N_PAIRS, PAIR, PAIR), F32)],
        compiler_params=pltpu.CompilerParams(vmem_limit_bytes=VMEM_LIMIT),
        name="rwkv_" + d,
    )(*args)


def _mix_out_body(x_ref, gm_ref, rw_ref, wo_ref, g2_ref, rw_hi_ref, rw_lo_ref, rb_ref, earlier_ref,
                  x1_ref, h2_ref, route_ref, route_t_ref, counts_ref, carry_ref):
    i = pl.program_id(0)

    @pl.when(i == 0)
    def _():
        carry_ref[...] = jnp.zeros_like(carry_ref)

    sub = earlier_ref.shape[0]
    subs = range(x_ref.shape[0] // sub)
    rows = [slice(s * sub, (s + 1) * sub) for s in subs]
    x1 = [x_ref[r, :] + _dot(gm_ref[r, :], wo_ref[:GM_WIDTH, :]) + _dot(rw_ref[r, :], wo_ref[GM_WIDTH:, :])
          for r in rows]
    h2 = [x1[s] * lax.rsqrt(jnp.mean(x1[s] * x1[s], axis=-1, keepdims=True) + RMS_EPS) * g2_ref[...] for s in subs]
    for s in subs:
        x1_ref[rows[s], :] = x1[s]
        h2_ref[rows[s], :] = _pack_bf16_halves(h2[s])
    split = [_split(h) for h in h2]
    logits = [_dot_nt(rw_hi_ref[...], hi) + _dot_nt(rw_hi_ref[...], lo) + _dot_nt(rw_lo_ref[...], hi) + rb_ref[...]
              for hi, lo in split]
    expert = lax.broadcasted_iota(jnp.int32, (N_EXPERTS, sub), 0).astype(F32)
    work = logits
    vals, ids, hits = [[] for _ in subs], [[] for _ in subs], [[] for _ in subs]
    onehot = [jnp.zeros((N_EXPERTS, sub), F32) for _ in subs]
    for _ in range(TOP_K):
        for s in subs:
            mx = jnp.max(work[s], axis=0, keepdims=True)
            idx = jnp.min(jnp.where(work[s] == mx, expert, float(N_EXPERTS)), axis=0, keepdims=True)
            hit = expert == idx
            vals[s].append(mx)
            ids[s].append(idx)
            hits[s].append(hit)
            onehot[s] = jnp.where(hit, 1.0, onehot[s])
        work = [jnp.where(hits[s][-1], -jnp.inf, work[s]) for s in subs]
    earlier_counts = [_dot(onehot[s].astype(BF16), earlier_ref[...]) for s in subs]
    carry = carry_ref[:, 0:1]
    for s in subs:
        exps = [jnp.exp(vk - vals[s][0]) for vk in vals[s]]
        denom = exps[0] + exps[1] + exps[2] + exps[3]
        ranks = carry + earlier_counts[s]
        rank_rows = [jnp.sum(jnp.where(hit, ranks, 0.0), axis=0, keepdims=True) for hit in hits[s]]
        route_t = jnp.concatenate(ids[s] + [e / denom for e in exps] + rank_rows
                                  + [jnp.zeros((ROUTE_ROWS - 3 * TOP_K, sub), F32)], axis=0)
        route_t_ref[:, rows[s]] = route_t
        route_ref[rows[s], :] = jnp.concatenate([route_t, jnp.zeros((LANES - ROUTE_ROWS, sub), F32)], axis=0).T
        carry = carry + jnp.sum(onehot[s], axis=1, keepdims=True)
    carry_ref[...] = jnp.broadcast_to(carry, carry_ref.shape)
    counts_ref[...] = jnp.broadcast_to(carry, counts_ref.shape)


def _mix_out(x2d, gm, rw, p, tm):
    n = x2d.shape[0]
    full = lambda shape: pl.BlockSpec(shape, lambda i: (0,) * len(shape))
    tile = lambda w: pl.BlockSpec((tm, w), lambda i: (i, 0))
    sub = min(MIX_OUT_SUB, tm)
    earlier = (jnp.arange(sub)[:, None] < jnp.arange(sub)[None, :]).astype(BF16)
    return pl.pallas_call(
        _mix_out_body,
        grid=(n // tm,),
        in_specs=[tile(D_MODEL), tile(GM_WIDTH), tile(RW_WIDTH), full((D_MODEL, D_MODEL)), full((1, D_MODEL)),
                  full((N_EXPERTS, D_MODEL)), full((N_EXPERTS, D_MODEL)), full((N_EXPERTS, 1)), full((sub, sub))],
        out_specs=[tile(D_MODEL), tile(D_MODEL // 2), tile(LANES), pl.BlockSpec((ROUTE_ROWS, tm), lambda i: (0, i)),
                   full((N_EXPERTS, LANES))],
        out_shape=[
            jax.ShapeDtypeStruct((n, D_MODEL), F32),
            jax.ShapeDtypeStruct((n, D_MODEL // 2), jnp.int32),
            jax.ShapeDtypeStruct((n, LANES), F32),
            jax.ShapeDtypeStruct((ROUTE_ROWS, n), F32),
            jax.ShapeDtypeStruct((N_EXPERTS, LANES), F32),
        ],
        scratch_shapes=[pltpu.VMEM((N_EXPERTS, LANES), F32)],
        compiler_params=pltpu.CompilerParams(vmem_limit_bytes=VMEM_LIMIT),
        name="mix_out",
    )(x2d, gm, rw, p["w_out"], p["norm2_g"], p["router_hi"], p["router_lo"], p["router_b"], earlier)


def _sc_gather_rows(table, idx):
    n_idx = idx.shape[0]
    width = table.shape[1]
    n_workers = SC_CORES * SC_SUBCORES
    per_worker = n_idx // n_workers
    n_windows = per_worker // SC_WINDOW
    assert per_worker * n_workers == n_idx and n_windows * SC_WINDOW == per_worker
    mesh = plsc.VectorSubcoreMesh(core_axis_name="c", subcore_axis_name="s")

    @functools.partial(
        pl.kernel, mesh=mesh,
        out_type=jax.ShapeDtypeStruct((n_idx, width), table.dtype),
        scratch_types=[
            pltpu.VMEM((SC_WINDOW,), jnp.int32),
            pltpu.VMEM((SC_WINDOW, width), table.dtype),
            pltpu.SemaphoreType.DMA,
        ],
        name="sc_gather_rows",
    )
    def gather(table_hbm, idx_hbm, out_hbm, idx_v, rows_v, sem):
        worker = lax.axis_index("s") * SC_CORES + lax.axis_index("c")
        base = worker * per_worker

        @pl.loop(0, n_windows)
        def _(j):
            off = base + j * SC_WINDOW
            pltpu.sync_copy(idx_hbm.at[pl.ds(off, SC_WINDOW)], idx_v)
            pltpu.async_copy(table_hbm.at[idx_v], rows_v, sem).wait()
            pltpu.sync_copy(rows_v, out_hbm.at[pl.ds(off, SC_WINDOW)])

    return gather(table, idx)


def _sc_scatter_rows(src, dest_w, n_out):
    n, width = src.shape
    n_windows, top_k, window = dest_w.shape
    n_workers = SC_CORES * SC_SUBCORES
    per_worker = n_windows // n_workers
    assert n_windows * window == n and per_worker * n_workers == n_windows
    mesh = plsc.VectorSubcoreMesh(core_axis_name="c", subcore_axis_name="s")

    @functools.partial(
        pl.kernel, mesh=mesh,
        out_type=jax.ShapeDtypeStruct((n_out, width), src.dtype),
        scratch_types=[
            pltpu.VMEM((top_k, window), jnp.int32),
            pltpu.VMEM((window, width), src.dtype),
        ],
        name="sc_scatter_rows",
    )
    def scatter(src_hbm, dest_hbm, out_hbm, idx_v, rows_v):
        worker = lax.axis_index("s") * SC_CORES + lax.axis_index("c")

        @pl.loop(0, per_worker)
        def _(j):
            g = worker * per_worker + j
            pltpu.sync_copy(dest_hbm.at[g], idx_v)
            pltpu.sync_copy(src_hbm.at[pl.ds(g * window, window)], rows_v)
            for k in range(top_k):
                pltpu.sync_copy(rows_v, out_hbm.at[idx_v.at[k]])

    return scatter(src, dest_w)


def _experts_body(blk_e_ref, n_used_ref, valid_ref, x_ref, wgu_ref, bgu_ref, wd_ref, bd_ref, out_ref,
                  wgu_bf, wd_bf):
    i = pl.program_id(0)
    n_used = n_used_ref[0]

    @pl.when((i == 0) | (blk_e_ref[i] != blk_e_ref[jnp.maximum(i - 1, 0)]))
    def _():
        wgu_bf[...] = wgu_ref[0].astype(BF16)
        wd_bf[...] = wd_ref[0].astype(BF16)

    @pl.when(i < n_used)
    def _():
        row = lax.broadcasted_iota(jnp.int32, (MOE_ROWS, 1), 0)
        x_left, x_right = _unpack_bf16_halves(jnp.where(row < valid_ref[i], x_ref[...], 0))
        half = D_MODEL // 2
        gu = (_dot(x_left.astype(BF16), wgu_bf[:half, :]) + _dot(x_right.astype(BF16), wgu_bf[half:, :])
              + bgu_ref[0])
        gate = jnp.minimum(gu[:, :D_EXPERT], SWIGLU_LIMIT)
        up = jnp.clip(gu[:, D_EXPERT:], -SWIGLU_LIMIT, SWIGLU_LIMIT)
        act = gate * _sigmoid(gate * SWIGLU_ALPHA) * (up + 1.0)
        out_ref[...] = _pack_bf16_halves(_dot(act.astype(BF16), wd_bf[...]) + bd_ref[0])

    @pl.when(i >= n_used)
    def _():
        out_ref[...] = jnp.zeros_like(out_ref)


def _experts(x_rows, blk_e, n_used, valid, p):
    n_blocks = blk_e.shape[0]
    grid_spec = pltpu.PrefetchScalarGridSpec(
        num_scalar_prefetch=3,
        grid=(n_blocks,),
        in_specs=[
            pl.BlockSpec((MOE_ROWS, D_MODEL // 2), lambda i, be, nu, va: (jnp.minimum(i, nu[0] - 1), 0)),
            pl.BlockSpec((1, D_MODEL, 2 * D_EXPERT), lambda i, be, nu, va: (be[i], 0, 0)),
            pl.BlockSpec((1, 1, 2 * D_EXPERT), lambda i, be, nu, va: (be[i], 0, 0)),
            pl.BlockSpec((1, D_EXPERT, D_MODEL), lambda i, be, nu, va: (be[i], 0, 0)),
            pl.BlockSpec((1, 1, D_MODEL), lambda i, be, nu, va: (be[i], 0, 0)),
        ],
        out_specs=pl.BlockSpec((MOE_ROWS, D_MODEL // 2), lambda i, be, nu, va: (i, 0)),
        scratch_shapes=[pltpu.VMEM((D_MODEL, 2 * D_EXPERT), BF16), pltpu.VMEM((D_EXPERT, D_MODEL), BF16)],
    )
    return pl.pallas_call(
        _experts_body,
        grid_spec=grid_spec,
        out_shape=jax.ShapeDtypeStruct((n_blocks * MOE_ROWS, D_MODEL // 2), jnp.int32),
        compiler_params=pltpu.CompilerParams(vmem_limit_bytes=VMEM_LIMIT),
        name="experts",
    )(blk_e, n_used, valid, x_rows, p["w_gu"], p["b_gu"], p["w_down"], p["b_down"])


def _combine_body(yg_ref, x1_ref, gates_ref, fg_ref, out_ref):
    gates = gates_ref[...]
    half = D_MODEL // 2
    acc_left = x1_ref[:, :half]
    acc_right = x1_ref[:, half:]
    for kk in range(TOP_K):
        y_left, y_right = _unpack_bf16_halves(yg_ref[kk])
        gate = gates[:, TOP_K + kk:TOP_K + kk + 1]
        acc_left = acc_left + gate * y_left
        acc_right = acc_right + gate * y_right
    acc = jnp.concatenate([acc_left, acc_right], axis=1)
    out_ref[...] = acc * lax.rsqrt(jnp.mean(acc * acc, axis=-1, keepdims=True) + RMS_EPS) * fg_ref[...]


def _combine(yg, x1, gates, final_g, tm):
    n = x1.shape[0]
    return pl.pallas_call(
        _combine_body,
        grid=(n // tm,),
        in_specs=[
            pl.BlockSpec((TOP_K, tm, D_MODEL // 2), lambda i: (0, i, 0)),
            pl.BlockSpec((tm, D_MODEL), lambda i: (i, 0)),
            pl.BlockSpec((tm, LANES), lambda i: (i, 0)),
            pl.BlockSpec((1, D_MODEL), lambda i: (0, 0)),
        ],
        out_specs=pl.BlockSpec((tm, D_MODEL), lambda i: (i, 0)),
        out_shape=jax.ShapeDtypeStruct((n, D_MODEL), F32),
        compiler_params=pltpu.CompilerParams(vmem_limit_bytes=VMEM_LIMIT),
        name="combine",
    )(yg, x1, gates, final_g)


def _prepare(norm1_g, w_in, rw_mu, gm_ln_g, gm_ln_b, gm_ws, gm_bs, rw_w0, rw_w2, rw_a0, rw_a2, rw_g2,
             rw_k_k, rw_k_a, rw_r_k, rw_gn_g, rw_gn_b, w_out, norm2_g, router_w, router_b,
             w_gu, b_gu, w_down, b_down, final_g):
    l = 0
    row = lambda t: t.reshape(1, -1).astype(F32)
    p = {}
    p["norm1_g"] = row(norm1_g[l])
    w = w_in[l]
    p["w_gm"] = w[:, :2 * GM_WIDTH].astype(BF16)
    pad_cols = LORA_PAD - LORA_COLS
    p["w_z"] = jnp.pad(w[:, 2 * GM_WIDTH:], ((0, 0), (0, pad_cols))).astype(BF16)
    mu = jnp.pad(row(rw_mu[l]), ((0, 0), (0, pad_cols)))
    p["rw_mu"] = jnp.concatenate([1.0 - mu, 0.5 * mu], axis=0)
    p["gm_ln_g"] = row(gm_ln_g[l])
    p["gm_ln_b"] = row(gm_ln_b[l])
    p["gm_ws"] = gm_ws[l].astype(BF16)
    p["gm_bs"] = jnp.broadcast_to(gm_bs[l][:, :, None], (GM_HEADS, GM_CHUNK, GM_HEAD_DIM)).astype(F32)
    zeros = lambda r, c: jnp.zeros((r, c), F32)
    o_a = DECAY_LORA
    o_g = DECAY_LORA + ICLR_LORA

    def lora_matrix(d, with_epilogue):
        blocks = [jnp.concatenate([rw_w2[l, d], zeros(LORA_PAD - DECAY_LORA, RW_WIDTH)], axis=0),
                  jnp.concatenate([zeros(o_a, RW_WIDTH), rw_a2[l, d], zeros(LORA_PAD - o_g, RW_WIDTH)], axis=0)]
        if with_epilogue:
            blocks.append(jnp.concatenate([zeros(o_a, RW_WIDTH), rw_a2[l, 1 - d], zeros(LORA_PAD - o_g, RW_WIDTH)],
                                          axis=0))
            blocks.append(jnp.concatenate([zeros(o_g, RW_WIDTH), rw_g2[l], zeros(LORA_PAD - LORA_COLS, RW_WIDTH)],
                                          axis=0))
        return jnp.concatenate(blocks, axis=1)

    for d, name in ((0, "fwd"), (1, "rev")):
        mat = lora_matrix(d, with_epilogue=(d == 1))
        hi = mat.astype(BF16)
        p["lora_hi_" + name] = hi
        p["lora_lo_" + name] = (mat - hi.astype(F32)).astype(BF16)
        p["w0_" + name] = row(rw_w0[l, d])
        p["a0_" + name] = row(rw_a0[l, d])
    p["rw_k_k"] = row(rw_k_k[l])
    p["rw_k_a"] = row(rw_k_a[l])
    p["rw_r_k"] = row(rw_r_k[l])
    p["rw_gn_g"] = row(rw_gn_g[l])
    p["rw_gn_b"] = row(rw_gn_b[l])
    ch = jnp.arange(PAIR) // RW_HEAD_DIM
    p["head_ones"] = (ch[:, None] == ch[None, :]).astype(BF16)
    p["w_out"] = w_out[l].astype(BF16)
    p["norm2_g"] = row(norm2_g[l])
    rw_t = router_w[l].astype(F32).T
    hi = rw_t.astype(BF16)
    p["router_hi"] = hi
    p["router_lo"] = (rw_t - hi.astype(F32)).astype(BF16)
    p["router_b"] = router_b[l].astype(F32).reshape(N_EXPERTS, 1)
    p["w_gu"] = w_gu[l]
    p["b_gu"] = b_gu[l].reshape(N_EXPERTS, 1, 2 * D_EXPERT).astype(F32)
    p["w_down"] = w_down[l]
    p["b_down"] = b_down[l].reshape(N_EXPERTS, 1, D_MODEL).astype(F32)
    p["final_g"] = row(final_g)
    return p


def _pick_tile(n, want):
    t = want
    while n % t:
        t //= 2
    return t


def _encoder(x, p):
    b_sz, t_len, d = x.shape
    n = b_sz * t_len
    x2d = x.reshape(n, d)
    gm, z = _mix_in(x2d, p, _pick_tile(t_len, 512), t_len)
    z3 = z.reshape(b_sz, t_len, Z_COLS)
    yf = _rwkv_pass(z3, None, p, reverse=False, bt=RW_BATCH_TILE)
    rw = _rwkv_pass(z3, yf, p, reverse=True, bt=RW_BATCH_TILE)
    tmo = _pick_tile(n, 1024)
    x1, h2, route, route_t, counts = _mix_out(x2d, gm, rw.reshape(n, RW_WIDTH), p, tmo)
    ids = route_t[:TOP_K].astype(jnp.int32)
    ranks = route_t[2 * TOP_K:3 * TOP_K].astype(jnp.int32)
    cnt = counts[:, 0].astype(jnp.int32)
    padded = (cnt + MOE_ROWS - 1) // MOE_ROWS * MOE_ROWS
    pad_end = jnp.cumsum(padded)
    pad_start = pad_end - padded
    expert_ids = jnp.arange(N_EXPERTS, dtype=jnp.int32)
    start_of = jnp.sum(jnp.where(ids[..., None] == expert_ids, pad_start, 0), axis=-1)
    dest = start_of + ranks
    n_blocks = n * TOP_K // MOE_ROWS + N_EXPERTS
    blk_start = jnp.arange(n_blocks, dtype=jnp.int32) * MOE_ROWS
    blk_e = jnp.minimum(jnp.sum((pad_end[None, :] <= blk_start[:, None]).astype(jnp.int32), axis=1), N_EXPERTS - 1)
    n_used = (pad_end[-1] // MOE_ROWS).astype(jnp.int32).reshape(1)
    blk_onehot = blk_e[:, None] == expert_ids
    blk_cnt = jnp.sum(jnp.where(blk_onehot, cnt, 0), axis=-1)
    blk_first = jnp.sum(jnp.where(blk_onehot, pad_start, 0), axis=-1)
    valid = jnp.clip(blk_cnt - (blk_start - blk_first), 0, MOE_ROWS).astype(jnp.int32)
    window = min(SC_SCATTER_WINDOW, n // (SC_CORES * SC_SUBCORES))
    dest_w = dest.reshape(TOP_K, n // window, window).transpose(1, 0, 2)
    x_rows = _sc_scatter_rows(h2, dest_w, n_blocks * MOE_ROWS)
    y_rows = _experts(x_rows, blk_e, n_used, valid, p)
    yg = _sc_gather_rows(y_rows, dest.reshape(-1)).reshape(TOP_K, n, D_MODEL // 2)
    out = _combine(yg, x1, route, p["final_g"], _pick_tile(n, 512))
    return out.reshape(b_sz, t_len, d)


def kernel(x_prompt, x_sample, norm1_g, w_in, rw_mu, gm_ln_g, gm_ln_b, gm_ws, gm_bs, rw_w0, rw_w2, rw_a0, rw_a2,
           rw_g2, rw_k_k, rw_k_a, rw_r_k, rw_gn_g, rw_gn_b, w_out, norm2_g, router_w, router_b, w_gu, b_gu,
           w_down, b_down, final_g):
    p = _prepare(norm1_g, w_in, rw_mu, gm_ln_g, gm_ln_b, gm_ws, gm_bs, rw_w0, rw_w2, rw_a0, rw_a2, rw_g2,
                 rw_k_k, rw_k_a, rw_r_k, rw_gn_g, rw_gn_b, w_out, norm2_g, router_w, router_b,
                 w_gu, b_gu, w_down, b_down, final_g)
    return (_encoder(x_prompt, p), _encoder(x_sample, p))
```

```python
import functools

import jax
import jax.numpy as jnp
from jax import lax
from jax.experimental import pallas as pl
from jax.experimental.pallas import tpu as pltpu
from jax.experimental.pallas import tpu_sc as plsc

F32 = jnp.float32
BF16 = jnp.bfloat16

D_MODEL = 1024
GM_WIDTH = 512
RW_WIDTH = 512
GM_HEADS = 4
GM_HEAD_DIM = 128
GM_CHUNK = 128
RW_HEAD_DIM = 64
DECAY_LORA = 32
ICLR_LORA = 32
GATE_LORA = 96
LORA_COLS = DECAY_LORA + ICLR_LORA + GATE_LORA
LORA_PAD = 256
Z_COLS = 3 * RW_WIDTH + LORA_PAD
Z_CHUNK = 256
N_EXPERTS = 32
TOP_K = 4
D_EXPERT = 1024
SWIGLU_LIMIT = 7.0
SWIGLU_ALPHA = 1.702
RMS_EPS = 1e-5
LN_EPS = 1e-5
GN_EPS = 64e-5

LANES = 128
SUBLANES = 8
VMEM_LIMIT = 56 * 1024 * 1024

RW_CHUNK = 64
PAIR = 2 * RW_HEAD_DIM
N_PAIRS = RW_WIDTH // PAIR
RW_BATCH_TILE = 4
MIX_OUT_SUB = 128
MOE_ROWS = 512
SC_CORES = 2
SC_SUBCORES = 16
SC_WINDOW = 64
SC_SCATTER_WINDOW = 128
ROUTE_ROWS = 16
DECAY_SCALE = 0.6065306597126334
KK_NORM_FLOOR_SQ = 1e-24


def _dot(a, b):
    return jnp.dot(a, b, preferred_element_type=F32)


def _dot_nt(a, b):
    return lax.dot_general(a, b, (((1,), (1,)), ((), ())), preferred_element_type=F32)


def _split(x):
    hi = x.astype(BF16)
    lo = (x - hi.astype(F32)).astype(BF16)
    return hi, lo


def _dot3(a, b_hi, b_lo):
    a_hi, a_lo = _split(a)
    return _dot(a_hi, b_hi) + _dot(a_lo, b_hi) + _dot(a_hi, b_lo)


def _dot2(a, b01):
    a_hi, a_lo = _split(a)
    return _dot(a_hi, b01) + _dot(a_lo, b01)


def _head_sums(x, pair_ones):
    rows = x.shape[0]
    n_tiles = x.shape[1] // LANES
    stacked = jnp.concatenate([x[:, t * LANES:(t + 1) * LANES] for t in range(n_tiles)], axis=0)
    sums = _dot2(stacked, pair_ones)
    return jnp.concatenate([sums[t * rows:(t + 1) * rows] for t in range(n_tiles)], axis=1)


def _dot2_lhs01(a01, b):
    b_hi, b_lo = _split(b)
    return _dot(a01, b_hi) + _dot(a01, b_lo)


def _pack_bf16_halves(x):
    w = x.shape[1] // 2
    hi = lax.bitcast_convert_type(x[:, :w].astype(BF16).astype(F32), jnp.int32)
    lo = lax.bitcast_convert_type(x[:, w:].astype(BF16).astype(F32), jnp.int32)
    return hi | lax.shift_right_logical(lo, 16)


def _unpack_bf16_halves(words):
    left = lax.bitcast_convert_type(words & jnp.int32(-65536), F32)
    right = lax.bitcast_convert_type(lax.shift_left(words, 16), F32)
    return left, right


def _gelu_tanh(x):
    return 0.5 * x * (1.0 + jnp.tanh(0.7978845608028654 * (x + 0.044715 * (x * x * x))))


def _sigmoid(x):
    return 1.0 / (1.0 + jnp.exp(-x))


def _mix_in_body(tiles_per_seq, x_ref, xp_ref, xn_ref, g1_ref, wgm_ref, wz_ref, mu_ref, lng_ref, lnb_ref, ws_ref,
                 bs_ref, gm_ref, z_ref):
    i = pl.program_id(0)
    tm = x_ref.shape[0]

    def normed(x):
        return (x * lax.rsqrt(jnp.mean(x * x, axis=-1, keepdims=True) + RMS_EPS) * g1_ref[...]).astype(BF16)

    hb = normed(x_ref[...])
    uv = _dot(hb, wgm_ref[...])
    hb_halo = normed(jnp.concatenate([xp_ref[...], xn_ref[...]], axis=0))
    has_prev = i % tiles_per_seq > 0
    has_next = i % tiles_per_seq < tiles_per_seq - 1
    row8 = lax.broadcasted_iota(jnp.int32, (SUBLANES, Z_CHUNK), 0)
    for j in range(Z_COLS // Z_CHUNK):
        cols = slice(j * Z_CHUNK, (j + 1) * Z_CHUNK)
        z = _dot(hb, wz_ref[:, cols])
        halo = _dot(hb_halo, wz_ref[:, cols])
        prev_row = jnp.where(has_prev, halo[SUBLANES - 1:SUBLANES, :], 0.0)
        next_row = jnp.where(has_next, halo[SUBLANES:SUBLANES + 1, :], 0.0)
        down = pltpu.roll(z, 1, axis=0)
        up = pltpu.roll(z, tm - 1, axis=0)
        prev = jnp.concatenate([jnp.where(row8 == 0, prev_row, down[:SUBLANES]), down[SUBLANES:]], axis=0)
        nxt = jnp.concatenate([up[:tm - SUBLANES], jnp.where(row8 == SUBLANES - 1, next_row, up[tm - SUBLANES:])],
                              axis=0)
        z_ref[:, cols] = z * mu_ref[0:1, cols] + (prev + nxt) * mu_ref[1:2, cols]
    u = _gelu_tanh(uv[:, :GM_WIDTH])
    v = _gelu_tanh(uv[:, GM_WIDTH:])
    mean = jnp.mean(v, axis=-1, keepdims=True)
    vc = v - mean
    var = jnp.mean(vc * vc, axis=-1, keepdims=True)
    vn = (vc * lax.rsqrt(var + LN_EPS) * lng_ref[...] + lnb_ref[...]).astype(BF16)
    for c in range(tm // GM_CHUNK):
        rows = slice(c * GM_CHUNK, (c + 1) * GM_CHUNK)
        for hd in range(GM_HEADS):
            cols = slice(hd * GM_HEAD_DIM, (hd + 1) * GM_HEAD_DIM)
            mixed = _dot(ws_ref[hd], vn[rows, cols]) + bs_ref[hd]
            gm_ref[rows, cols] = (u[rows, cols] * mixed).astype(BF16)


def _mix_in(x2d, p, tm, t_len):
    n = x2d.shape[0]
    assert t_len % tm == 0
    halo_per_tile = tm // SUBLANES
    n_halo = n // SUBLANES
    full = lambda shape: pl.BlockSpec(shape, lambda i: (0,) * len(shape))
    return pl.pallas_call(
        functools.partial(_mix_in_body, t_len // tm),
        grid=(n // tm,),
        in_specs=[
            pl.BlockSpec((tm, D_MODEL), lambda i: (i, 0)),
            pl.BlockSpec((SUBLANES, D_MODEL), lambda i: (jnp.maximum(i * halo_per_tile - 1, 0), 0)),
            pl.BlockSpec((SUBLANES, D_MODEL), lambda i: (jnp.minimum((i + 1) * halo_per_tile, n_halo - 1), 0)),
            full((1, D_MODEL)),
            full((D_MODEL, 2 * GM_WIDTH)),
            full((D_MODEL, Z_COLS)),
            full((2, Z_COLS)),
            full((1, GM_WIDTH)),
            full((1, GM_WIDTH)),
            full((GM_HEADS, GM_CHUNK, GM_CHUNK)),
            full((GM_HEADS, GM_CHUNK, GM_HEAD_DIM)),
        ],
        out_specs=[
            pl.BlockSpec((tm, GM_WIDTH), lambda i: (i, 0)),
            pl.BlockSpec((tm, Z_COLS), lambda i: (i, 0)),
        ],
        out_shape=[
            jax.ShapeDtypeStruct((n, GM_WIDTH), BF16),
            jax.ShapeDtypeStruct((n, Z_COLS), F32),
        ],
        compiler_params=pltpu.CompilerParams(vmem_limit_bytes=VMEM_LIMIT),
        name="mix_in",
    )(x2d, x2d, x2d, p["norm1_g"], p["w_gm"], p["w_z"], p["rw_mu"], p["gm_ln_g"], p["gm_ln_b"], p["gm_ws"],
      p["gm_bs"])


def _rwkv_masks(reverse):
    c = RW_CHUNK
    ti = lax.broadcasted_iota(jnp.int32, (c, c), 0)
    tj = lax.broadcasted_iota(jnp.int32, (c, c), 1)
    incl = (tj >= ti) if reverse else (tj <= ti)
    lane =lax.broadcasted_iota(jnp.int32, (c, PAIR), 1)
    head0 = lane < RW_HEAD_DIM
    lane2 = lax.broadcasted_iota(jnp.int32, (c, 2 * c), 1)
    left = lane2 < c
    ti2 = lax.broadcasted_iota(jnp.int32, (c, 2 * c), 0)
    tj2 = jnp.where(left, lane2, lane2 - c)
    incl2 = (tj2 >= ti2) if reverse else (tj2 <= ti2)
    strict2 = (tj2 > ti2) if reverse else (tj2 < ti2)
    eye2 = jnp.where(tj2 == ti2, 1.0, 0.0).astype(F32)
    si = lax.broadcasted_iota(jnp.int32, (PAIR, PAIR), 0)
    sj = lax.broadcasted_iota(jnp.int32, (PAIR, PAIR), 1)
    same_head = (si // RW_HEAD_DIM) == (sj // RW_HEAD_DIM)
    return dict(
        incl2=incl2, strict2=strict2, head0=head0, left=left, eye2=eye2, same_head=same_head,
        tri=jnp.where(incl, 1.0, 0.0).astype(BF16),
    )


def _stack_heads(x, head0):
    return jnp.concatenate([jnp.where(head0, x, 0.0), jnp.where(head0, 0.0, x)], axis=0)


def _block_diag(x, left):
    return jnp.concatenate([jnp.where(left, x, 0.0), jnp.where(left, 0.0, x)], axis=0)


def _units_chunk(states, ops, m):
    c = RW_CHUNK
    units = range(len(ops))
    bf = lambda t: t.astype(BF16)
    head0, left = m["head0"], m["left"]
    ar = [bf(jnp.concatenate([o["a_t"], o["r_t"]], axis=0)) for o in ops]
    rstack = [bf(jnp.concatenate([_stack_heads(o["b_t"], head0), _stack_heads(o["k_t"], head0)], axis=0))
              for o in ops]
    acat = [_dot_nt(ar[i], rstack[i]) for i in units]
    sa = [_dot_nt(ar[i], bf(states[i])) for i in units]
    a_ab = [jnp.where(m["strict2"], acat[i][:c, :2 * c], 0.0) for i in units]
    a_ak = [bf(jnp.where(m["strict2"], acat[i][:c, 2 * c:], 0.0)) for i in units]
    a_r = [bf(jnp.concatenate([jnp.where(m["incl2"], acat[i][c:, :2 * c], 0.0),
                               jnp.where(m["incl2"], acat[i][c:, 2 * c:], 0.0)], axis=1)) for i in units]
    p = [m["eye2"] + a_ab[i] for i in units]
    lk = [_dot(bf(a_ab[i]), bf(_block_diag(a_ab[i], left))) for i in units]
    vstack = [bf(_stack_heads(o["v"], head0)) for o in ops]
    rhs = [sa[i][:c] + _dot(a_ak[i], vstack[i]) for i in units]
    power = 2
    while 2 * power < c:
        res = [_dot(bf(jnp.concatenate([p[i], lk[i]], axis=0)), bf(_block_diag(lk[i], left))) for i in units]
        p = [p[i] + res[i][:c] for i in units]
        lk = [res[i][c:] for i in units]
        power *= 2
    p = [p[i] + _dot(bf(p[i]), bf(_block_diag(lk[i], left))) for i in units]
    u = [_dot(bf(p[i]), bf(_stack_heads(rhs[i], head0))) for i in units]
    ys = [sa[i][c:] + _dot(a_r[i], jnp.concatenate([bf(_stack_heads(u[i], head0)), vstack[i]], axis=0))
          for i in units]
    upd = [_dot(bf(jnp.concatenate([u[i], ops[i]["v"]], axis=0).T),
                bf(jnp.concatenate([ops[i]["b_h"], ops[i]["k_h"]], axis=0))) for i in units]
    new_states = [states[i] * ops[i]["w_tot"] + jnp.where(m["same_head"], upd[i], 0.0) for i in units]
    return new_states, ys


def _rwkv_body(reverse, bt, *refs):
    if reverse:
        (zc_ref, yf_ref, lw_hi_ref, lw_lo_ref, w0_ref, a0_ref, a0o_ref, kk_ref, ka_ref,
         rk_ref, bd_ref, gng_ref, gnb_ref, out_ref, state_ref) = refs
    else:
        (zc_ref, lw_hi_ref, lw_lo_ref, w0_ref, a0_ref, kk_ref, ka_ref,
         bd_ref, out_ref, state_ref) = refs
    c = RW_CHUNK

    @pl.when(pl.program_id(1) == 0)
    def _():
        state_ref[...] = jnp.zeros_like(state_ref)

    rows = bt * c
    m = _rwkv_masks(reverse)
    lora_lane = lax.broadcasted_iota(jnp.int32, (rows, LORA_PAD), 1)
    bd = bd_ref[...]
    zf = zc_ref[...].reshape(rows, Z_COLS)
    r = zf[:, :RW_WIDTH]
    k = zf[:, RW_WIDTH:2 * RW_WIDTH]
    v = zf[:, 2 * RW_WIDTH:3 * RW_WIDTH]
    lo = zf[:, 3 * RW_WIDTH:]
    act = jnp.where(lora_lane < DECAY_LORA, jnp.tanh(lo),
                    jnp.where(lora_lane < DECAY_LORA + ICLR_LORA, lo, _sigmoid(lo)))
    proj = _dot3(act, lw_hi_ref[...], lw_lo_ref[...])
    w_in = w0_ref[...] + proj[:, :RW_WIDTH]
    logw = -DECAY_SCALE * _sigmoid(w_in)
    a_sig = _sigmoid(a0_ref[...] + proj[:, RW_WIDTH:2 * RW_WIDTH])
    kk = k * kk_ref[...]
    kk = kk * lax.rsqrt(jnp.maximum(_head_sums(kk * kk, bd), KK_NORM_FLOOR_SQ))
    kd = k * (1.0 + (a_sig - 1.0) * ka_ref[...])
    a_vec = -kk
    b_vec = kk * a_sig
    ri = lax.broadcasted_iota(jnp.int32, (rows, rows), 0)
    rj = lax.broadcasted_iota(jnp.int32, (rows, rows), 1)
    ordered = (rj >= ri) if reverse else (rj <= ri)
    tri = jnp.where((ri // c) == (rj // c), jnp.where(ordered, 1.0, 0.0), 0.0).astype(BF16)
    cum = _dot2_lhs01(tri, logw)
    edge = 0 if reverse else c - 1
    tot = jnp.concatenate([jnp.broadcast_to(cum[b * c + edge:b * c + edge + 1, :], (c, RW_WIDTH))
                           for b in range(bt)], axis=0)
    w_inv = jnp.exp(-cum)
    w_end = jnp.exp(tot - cum)
    w_tot = jnp.exp(tot)
    full_ops = dict(r_t=r * jnp.exp(cum), a_t=a_vec * jnp.exp(cum - logw), b_t=b_vec * w_inv, k_t=kd * w_inv,
                    b_h=b_vec * w_end, k_h=kd * w_end, v=v)
    ops, states = [], []
    for b in range(bt):
        for pi in range(N_PAIRS):
            ln = slice(pi * PAIR, (pi + 1) * PAIR)
            unit = {name: t[b * c:(b + 1) * c, ln] for name, t in full_ops.items()}
            unit["w_tot"] = w_tot[b * c:b * c + 1, ln]
            ops.append(unit)
            states.append(state_ref[b, pi])
    new_states, ys = _units_chunk(states, ops, m)
    for b in range(bt):
        for pi in range(N_PAIRS):
            state_ref[b, pi] = new_states[b * N_PAIRS + pi]
    y = jnp.concatenate([jnp.concatenate(ys[b * N_PAIRS:(b + 1) * N_PAIRS], axis=1) for b in range(bt)], axis=0)
    if not reverse:
        out_ref[...] = y.reshape(bt, c, RW_WIDTH)
    else:
        y = y + yf_ref[...].reshape(rows, RW_WIDTH)
        a_other = _sigmoid(a0o_ref[...] + proj[:, 2 * RW_WIDTH:3 * RW_WIDTH])
        kd_sum = kd + k * (1.0 + (a_other - 1.0) * ka_ref[...])
        bonus = _head_sums(r * kd_sum * rk_ref[...], bd) * v
        inv_n = 1.0 / RW_HEAD_DIM
        mean = _head_sums(y, bd) * inv_n
        yc = y - mean
        var = _head_sums(yc * yc, bd) * inv_n
        yn = yc * lax.rsqrt(var + GN_EPS) * gng_ref[...] + gnb_ref[...]
        g = proj[:, 3 * RW_WIDTH:]
        out_ref[...] = ((yn + bonus) * g).astype(BF16).reshape(bt, c, RW_WIDTH)


def _rwkv_pass(z3, yf, p, reverse, bt):
    b_sz, t_len, _ = z3.shape
    c = RW_CHUNK
    n_chunks = t_len // c

    def cidx(ci):
        return (n_chunks - 1 - ci) if reverse else ci

    full = lambda shape: pl.BlockSpec(shape, lambda bi, ci: (0,) * len(shape))
    row512 = full((1, RW_WIDTH))
    d = "rev" if reverse else "fwd"
    n_proj = p["lora_hi_" + d].shape[1]
    in_specs = [pl.BlockSpec((bt, c, Z_COLS), lambda bi, ci: (bi, cidx(ci), 0))]
    args = [z3]
    if reverse:
        in_specs.append(pl.BlockSpec((bt, c, RW_WIDTH), lambda bi, ci: (bi, cidx(ci), 0)))
        args.append(yf)
    in_specs += [full((LORA_PAD, n_proj)), full((LORA_PAD, n_proj)), row512, row512]
    args += [p["lora_hi_" + d], p["lora_lo_" + d], p["w0_" + d], p["a0_" + d]]
    if reverse:
        in_specs.append(row512)
        args.append(p["a0_fwd"])
    in_specs += [row512, row512]
    args += [p["rw_k_k"], p["rw_k_a"]]
    if reverse:
        in_specs.append(row512)
        args.append(p["rw_r_k"])
    in_specs.append(full((PAIR, PAIR)))
    args.append(p["head_ones"])
    if reverse:
        in_specs += [row512, row512]
        args += [p["rw_gn_g"], p["rw_gn_b"]]
    return pl.pallas_call(
        functools.partial(_rwkv_body, reverse, bt),
        grid=(b_sz // bt, n_chunks),
        in_specs=in_specs,
        out_specs=pl.BlockSpec((bt, c, RW_WIDTH), lambda bi, ci: (bi, cidx(ci), 0)),
        out_shape=jax.ShapeDtypeStruct((b_sz, t_len, RW_WIDTH), BF16 if reverse else F32),
        scratch_shapes=[pltpu.VMEM((bt, N_PAIRS, PAIR, PAIR), F32)],
        compiler_params=pltpu.CompilerParams(vmem_limit_bytes=VMEM_LIMIT),
        name="rwkv_" + d,
    )(*args)


def _mix_out_body(x_ref, gm_ref, rw_ref, wo_ref, g2_ref, rw_hi_ref, rw_lo_ref, rb_ref, earlier_ref,
                  x1_ref, h2_ref, route_ref, route_t_ref, counts_ref, carry_ref):
    i = pl.program_id(0)

    @pl.when(i == 0)
    def _():
        carry_ref[...] = jnp.zeros_like(carry_ref)

    sub = earlier_ref.shape[0]
    subs = range(x_ref.shape[0] // sub)
    rows = [slice(s * sub, (s + 1) * sub) for s in subs]
    x1 = [x_ref[r, :] + _dot(gm_ref[r, :], wo_ref[:GM_WIDTH, :]) + _dot(rw_ref[r, :], wo_ref[GM_WIDTH:, :])
          for r in rows]
    h2 = [x1[s] * lax.rsqrt(jnp.mean(x1[s] * x1[s], axis=-1, keepdims=True) + RMS_EPS) * g2_ref[...] for s in subs]
    for s in subs:
        x1_ref[rows[s], :] = x1[s]
        h2_ref[rows[s], :] = _pack_bf16_halves(h2[s])
    split = [_split(h) for h in h2]
    logits = [_dot_nt(rw_hi_ref[...], hi) + _dot_nt(rw_hi_ref[...], lo) + _dot_nt(rw_lo_ref[...], hi) + rb_ref[...]
              for hi, lo in split]
    expert = lax.broadcasted_iota(jnp.int32, (N_EXPERTS, sub), 0).astype(F32)
    work = logits
    vals, ids, hits = [[] for _ in subs], [[] for _ in subs], [[] for _ in subs]
    onehot = [jnp.zeros((N_EXPERTS, sub), F32) for _ in subs]
    for _ in range(TOP_K):
        for s in subs:
            mx = jnp.max(work[s], axis=0, keepdims=True)
            idx = jnp.min(jnp.where(work[s] == mx, expert, float(N_EXPERTS)), axis=0, keepdims=True)
            hit = expert == idx
            vals[s].append(mx)
            ids[s].append(idx)
            hits[s].append(hit)
            onehot[s] = jnp.where(hit, 1.0, onehot[s])
        work = [jnp.where(hits[s][-1], -jnp.inf, work[s]) for s in subs]
    earlier_counts = [_dot(onehot[s].astype(BF16), earlier_ref[...]) for s in subs]
    carry = carry_ref[:, 0:1]
    for s in subs:
        exps = [jnp.exp(vk - vals[s][0]) for vk in vals[s]]
        denom = exps[0] + exps[1] + exps[2] + exps[3]
        ranks = carry + earlier_counts[s]
        rank_rows = [jnp.sum(jnp.where(hit, ranks, 0.0), axis=0, keepdims=True) for hit in hits[s]]
        route_t = jnp.concatenate(ids[s] + [e / denom for e in exps] + rank_rows
                                  + [jnp.zeros((ROUTE_ROWS - 3 * TOP_K, sub), F32)], axis=0)
        route_t_ref[:, rows[s]] = route_t
        route_ref[rows[s], :] = jnp.concatenate([route_t, jnp.zeros((LANES - ROUTE_ROWS, sub), F32)], axis=0).T
        carry = carry + jnp.sum(onehot[s], axis=1, keepdims=True)
    carry_ref[...] = jnp.broadcast_to(carry, carry_ref.shape)
    counts_ref[...] = jnp.broadcast_to(carry, counts_ref.shape)


def _mix_out(x2d, gm, rw, p, tm):
    n = x2d.shape[0]
    full = lambda shape: pl.BlockSpec(shape, lambda i: (0,) * len(shape))
    tile = lambda w: pl.BlockSpec((tm, w), lambda i: (i, 0))
    sub = min(MIX_OUT_SUB, tm)
    earlier = (jnp.arange(sub)[:, None] < jnp.arange(sub)[None, :]).astype(BF16)
    return pl.pallas_call(
        _mix_out_body,
        grid=(n // tm,),
        in_specs=[tile(D_MODEL), tile(GM_WIDTH), tile(RW_WIDTH), full((D_MODEL, D_MODEL)), full((1, D_MODEL)),
                  full((N_EXPERTS, D_MODEL)), full((N_EXPERTS, D_MODEL)), full((N_EXPERTS, 1)), full((sub, sub))],
        out_specs=[tile(D_MODEL), tile(D_MODEL // 2), tile(LANES), pl.BlockSpec((ROUTE_ROWS, tm), lambda i: (0, i)),
                   full((N_EXPERTS, LANES))],
        out_shape=[
            jax.ShapeDtypeStruct((n, D_MODEL), F32),
            jax.ShapeDtypeStruct((n, D_MODEL // 2), jnp.int32),
            jax.ShapeDtypeStruct((n, LANES), F32),
            jax.ShapeDtypeStruct((ROUTE_ROWS, n), F32),
            jax.ShapeDtypeStruct((N_EXPERTS, LANES), F32),
        ],
        scratch_shapes=[pltpu.VMEM((N_EXPERTS, LANES), F32)],
        compiler_params=pltpu.CompilerParams(vmem_limit_bytes=VMEM_LIMIT),
        name="mix_out",
    )(x2d, gm, rw, p["w_out"], p["norm2_g"], p["router_hi"], p["router_lo"], p["router_b"], earlier)


def _sc_gather_rows(table, idx):
    n_idx = idx.shape[0]
    width = table.shape[1]
    n_workers = SC_CORES * SC_SUBCORES
    per_worker = n_idx // n_workers
    n_windows = per_worker // SC_WINDOW
    assert per_worker * n_workers == n_idx and n_windows * SC_WINDOW == per_worker
    mesh = plsc.VectorSubcoreMesh(core_axis_name="c", subcore_axis_name="s")

    @functools.partial(
        pl.kernel, mesh=mesh,
        out_type=jax.ShapeDtypeStruct((n_idx, width), table.dtype),
        scratch_types=[
            pltpu.VMEM((SC_WINDOW,), jnp.int32),
            pltpu.VMEM((SC_WINDOW, width), table.dtype),
            pltpu.SemaphoreType.DMA,
        ],
        name="sc_gather_rows",
    )
    def gather(table_hbm, idx_hbm, out_hbm, idx_v, rows_v, sem):
        worker = lax.axis_index("s") * SC_CORES + lax.axis_index("c")
        base = worker * per_worker

        @pl.loop(0, n_windows)
        def _(j):
            off = base + j * SC_WINDOW
            pltpu.sync_copy(idx_hbm.at[pl.ds(off, SC_WINDOW)], idx_v)
            pltpu.async_copy(table_hbm.at[idx_v], rows_v, sem).wait()
            pltpu.sync_copy(rows_v, out_hbm.at[pl.ds(off, SC_WINDOW)])

    return gather(table, idx)


def _sc_scatter_rows(src, dest_w, n_out):
    n, width = src.shape
    n_windows, top_k, window = dest_w.shape
    n_workers = SC_CORES * SC_SUBCORES
    per_worker = n_windows // n_workers
    assert n_windows * window == n and per_worker * n_workers == n_windows
    mesh = plsc.VectorSubcoreMesh(core_axis_name="c", subcore_axis_name="s")

    @functools.partial(
        pl.kernel, mesh=mesh,
        out_type=jax.ShapeDtypeStruct((n_out, width), src.dtype),
        scratch_types=[
            pltpu.VMEM((top_k, window), jnp.int32),
            pltpu.VMEM((window, width), src.dtype),
        ],
        name="sc_scatter_rows",
    )
    def scatter(src_hbm, dest_hbm, out_hbm, idx_v, rows_v):
        worker = lax.axis_index("s") * SC_CORES + lax.axis_index("c")

        @pl.loop(0, per_worker)
        def _(j):
            g = worker * per_worker + j
            pltpu.sync_copy(dest_hbm.at[g], idx_v)
            pltpu.sync_copy(src_hbm.at[pl.ds(g * window, window)], rows_v)
            for k in range(top_k):
                pltpu.sync_copy(rows_v, out_hbm.at[idx_v.at[k]])

    return scatter(src, dest_w)


def _experts_body(blk_e_ref, n_used_ref, valid_ref, x_ref, wgu_ref, bgu_ref, wd_ref, bd_ref, out_ref,
                  wgu_bf, wd_bf):
    i = pl.program_id(0)
    n_used = n_used_ref[0]

    @pl.when((i == 0) | (blk_e_ref[i] != blk_e_ref[jnp.maximum(i - 1, 0)]))
    def _():
        wgu_bf[...] = wgu_ref[0].astype(BF16)
        wd_bf[...] = wd_ref[0].astype(BF16)

    valid = jnp.where(i < n_used, valid_ref[i], 0)

    def expert_mlp(n_rows):
        row = lax.broadcasted_iota(jnp.int32, (n_rows, 1), 0)
        x_left, x_right = _unpack_bf16_halves(jnp.where(row < valid, x_ref[:n_rows, :], 0))
        half = D_MODEL // 2
        gu = (_dot(x_left.astype(BF16), wgu_bf[:half, :]) + _dot(x_right.astype(BF16), wgu_bf[half:, :])
              + bgu_ref[0])
        gate = jnp.minimum(gu[:, :D_EXPERT], SWIGLU_LIMIT)
        up = jnp.clip(gu[:, D_EXPERT:], -SWIGLU_LIMIT, SWIGLU_LIMIT)
        act = gate * _sigmoid(gate * SWIGLU_ALPHA) * (up + 1.0)
        out_ref[:n_rows, :] = _pack_bf16_halves(_dot(act.astype(BF16), wd_bf[...]) + bd_ref[0])

    half_rows = MOE_ROWS // 2

    @pl.when(valid > half_rows)
    def _():
        expert_mlp(MOE_ROWS)

    @pl.when((valid > 0) & (valid <= half_rows))
    def _():
        expert_mlp(half_rows)
        out_ref[half_rows:, :] = jnp.zeros((MOE_ROWS - half_rows, D_MODEL // 2), jnp.int32)

    @pl.when(valid == 0)
    def _():
        out_ref[...] = jnp.zeros_like(out_ref)


def _experts(x_rows, blk_e, n_used, valid, p):
    n_blocks = blk_e.shape[0]
    grid_spec = pltpu.PrefetchScalarGridSpec(
        num_scalar_prefetch=3,
        grid=(n_blocks,),
        in_specs=[
            pl.BlockSpec((MOE_ROWS, D_MODEL // 2), lambda i, be, nu, va: (jnp.minimum(i, nu[0] - 1), 0)),
            pl.BlockSpec((1, D_MODEL, 2 * D_EXPERT), lambda i, be, nu, va: (be[i], 0, 0)),
            pl.BlockSpec((1, 1, 2 * D_EXPERT), lambda i, be, nu, va: (be[i], 0, 0)),
            pl.BlockSpec((1, D_EXPERT, D_MODEL), lambda i, be, nu, va: (be[i], 0, 0)),
            pl.BlockSpec((1, 1, D_MODEL), lambda i, be, nu, va: (be[i], 0, 0)),
        ],
        out_specs=pl.BlockSpec((MOE_ROWS, D_MODEL // 2), lambda i, be, nu, va: (i, 0)),
        scratch_shapes=[pltpu.VMEM((D_MODEL, 2 * D_EXPERT), BF16), pltpu.VMEM((D_EXPERT, D_MODEL), BF16)],
    )
    return pl.pallas_call(
        _experts_body,
        grid_spec=grid_spec,
        out_shape=jax.ShapeDtypeStruct((n_blocks * MOE_ROWS, D_MODEL // 2), jnp.int32),
        compiler_params=pltpu.CompilerParams(vmem_limit_bytes=VMEM_LIMIT),
        name="experts",
    )(blk_e, n_used, valid, x_rows, p["w_gu"], p["b_gu"], p["w_down"], p["b_down"])


def _combine_body(yg_ref, x1_ref, gates_ref, fg_ref, out_ref):
    gates = gates_ref[...]
    half = D_MODEL // 2
    acc_left = x1_ref[:, :half]
    acc_right = x1_ref[:, half:]
    for kk in range(TOP_K):
        y_left, y_right = _unpack_bf16_halves(yg_ref[kk])
        gate = gates[:, TOP_K + kk:TOP_K + kk + 1]
        acc_left = acc_left + gate * y_left
        acc_right = acc_right + gate * y_right
    acc = jnp.concatenate([acc_left, acc_right], axis=1)
    out_ref[...] = acc * lax.rsqrt(jnp.mean(acc * acc, axis=-1, keepdims=True) + RMS_EPS) * fg_ref[...]


def _combine(yg, x1, gates, final_g, tm):
    n = x1.shape[0]
    return pl.pallas_call(
        _combine_body,
        grid=(n // tm,),
        in_specs=[
            pl.BlockSpec((TOP_K, tm, D_MODEL // 2), lambda i: (0, i, 0)),
            pl.BlockSpec((tm, D_MODEL), lambda i: (i, 0)),
            pl.BlockSpec((tm, LANES), lambda i: (i, 0)),
            pl.BlockSpec((1, D_MODEL), lambda i: (0, 0)),
        ],
        out_specs=pl.BlockSpec((tm, D_MODEL), lambda i: (i, 0)),
        out_shape=jax.ShapeDtypeStruct((n, D_MODEL), F32),
        compiler_params=pltpu.CompilerParams(vmem_limit_bytes=VMEM_LIMIT),
        name="combine",
    )(yg, x1, gates, final_g)


def _prepare(norm1_g, w_in, rw_mu, gm_ln_g, gm_ln_b, gm_ws, gm_bs, rw_w0, rw_w2, rw_a0, rw_a2, rw_g2,
             rw_k_k, rw_k_a, rw_r_k, rw_gn_g, rw_gn_b, w_out, norm2_g, router_w, router_b,
             w_gu, b_gu, w_down, b_down, final_g):
    l = 0
    row = lambda t: t.reshape(1, -1).astype(F32)
    p = {}
    p["norm1_g"] = row(norm1_g[l])
    w = w_in[l]
    p["w_gm"] = w[:, :2 * GM_WIDTH].astype(BF16)
    pad_cols = LORA_PAD - LORA_COLS
    p["w_z"] = jnp.pad(w[:, 2 * GM_WIDTH:], ((0, 0), (0, pad_cols))).astype(BF16)
    mu = jnp.pad(row(rw_mu[l]), ((0, 0), (0, pad_cols)))
    p["rw_mu"] = jnp.concatenate([1.0 - mu, 0.5 * mu], axis=0)
    p["gm_ln_g"] = row(gm_ln_g[l])
    p["gm_ln_b"] = row(gm_ln_b[l])
    p["gm_ws"] = gm_ws[l].astype(BF16)
    p["gm_bs"] = jnp.broadcast_to(gm_bs[l][:, :, None], (GM_HEADS, GM_CHUNK, GM_HEAD_DIM)).astype(F32)
    zeros = lambda r, c: jnp.zeros((r, c), F32)
    o_a = DECAY_LORA
    o_g = DECAY_LORA + ICLR_LORA

    def lora_matrix(d, with_epilogue):
        blocks = [jnp.concatenate([rw_w2[l, d], zeros(LORA_PAD - DECAY_LORA, RW_WIDTH)], axis=0),
                  jnp.concatenate([zeros(o_a, RW_WIDTH), rw_a2[l, d], zeros(LORA_PAD - o_g, RW_WIDTH)], axis=0)]
        if with_epilogue:
            blocks.append(jnp.concatenate([zeros(o_a, RW_WIDTH), rw_a2[l, 1 - d], zeros(LORA_PAD - o_g, RW_WIDTH)],
                                          axis=0))
            blocks.append(jnp.concatenate([zeros(o_g, RW_WIDTH), rw_g2[l], zeros(LORA_PAD - LORA_COLS, RW_WIDTH)],
                                          axis=0))
        return jnp.concatenate(blocks, axis=1)

    for d, name in ((0, "fwd"), (1, "rev")):
        mat = lora_matrix(d, with_epilogue=(d == 1))
        hi = mat.astype(BF16)
        p["lora_hi_" + name] = hi
        p["lora_lo_" + name] = (mat - hi.astype(F32)).astype(BF16)
        p["w0_" + name] = row(rw_w0[l, d])
        p["a0_" + name] = row(rw_a0[l, d])
    p["rw_k_k"] = row(rw_k_k[l])
    p["rw_k_a"] = row(rw_k_a[l])
    p["rw_r_k"] = row(rw_r_k[l])
    p["rw_gn_g"] = row(rw_gn_g[l])
    p["rw_gn_b"] = row(rw_gn_b[l])
    ch = jnp.arange(PAIR) // RW_HEAD_DIM
    p["head_ones"] = (ch[:, None] == ch[None, :]).astype(BF16)
    p["w_out"] = w_out[l].astype(BF16)
    p["norm2_g"] = row(norm2_g[l])
    rw_t = router_w[l].astype(F32).T
    hi = rw_t.astype(BF16)
    p["router_hi"] = hi
    p["router_lo"] = (rw_t - hi.astype(F32)).astype(BF16)
    p["router_b"] = router_b[l].astype(F32).reshape(N_EXPERTS, 1)
    p["w_gu"] = w_gu[l]
    p["b_gu"] = b_gu[l].reshape(N_EXPERTS, 1, 2 * D_EXPERT).astype(F32)
    p["w_down"] = w_down[l]
    p["b_down"] = b_down[l].reshape(N_EXPERTS, 1, D_MODEL).astype(F32)
    p["final_g"] = row(final_g)
    return p


def _pick_tile(n, want):
    t = want
    while n % t:
        t //= 2
    return t


def _encoder(x, p):
    b_sz, t_len, d = x.shape
    n = b_sz * t_len
    x2d = x.reshape(n, d)
    gm, z = _mix_in(x2d, p, _pick_tile(t_len, 1024), t_len)
    z3 = z.reshape(b_sz, t_len, Z_COLS)
    yf = _rwkv_pass(z3, None, p, reverse=False, bt=RW_BATCH_TILE)
    rw = _rwkv_pass(z3, yf, p, reverse=True, bt=RW_BATCH_TILE)
    tmo = _pick_tile(n, 1024)
    x1, h2, route, route_t, counts = _mix_out(x2d, gm, rw.reshape(n, RW_WIDTH), p, tmo)
    ids = route_t[:TOP_K].astype(jnp.int32)
    ranks = route_t[2 * TOP_K:3 * TOP_K].astype(jnp.int32)
    cnt = counts[:, 0].astype(jnp.int32)
    padded = (cnt + MOE_ROWS - 1) // MOE_ROWS * MOE_ROWS
    pad_end = jnp.cumsum(padded)
    pad_start = pad_end - padded
    expert_ids = jnp.arange(N_EXPERTS, dtype=jnp.int32)
    start_of = jnp.sum(jnp.where(ids[..., None] == expert_ids, pad_start, 0), axis=-1)
    dest = start_of + ranks
    n_blocks = n * TOP_K // MOE_ROWS + N_EXPERTS
    blk_start = jnp.arange(n_blocks, dtype=jnp.int32) * MOE_ROWS
    blk_e = jnp.minimum(jnp.sum((pad_end[None, :] <= blk_start[:, None]).astype(jnp.int32), axis=1), N_EXPERTS - 1)
    n_used = (pad_end[-1] // MOE_ROWS).astype(jnp.int32).reshape(1)
    blk_onehot = blk_e[:, None] == expert_ids
    blk_cnt = jnp.sum(jnp.where(blk_onehot, cnt, 0), axis=-1)
    blk_first = jnp.sum(jnp.where(blk_onehot, pad_start, 0), axis=-1)
    valid = jnp.clip(blk_cnt - (blk_start - blk_first), 0, MOE_ROWS).astype(jnp.int32)
    window = min(SC_SCATTER_WINDOW, n // (SC_CORES * SC_SUBCORES))
    dest_w = dest.reshape(TOP_K, n // window, window).transpose(1, 0, 2)
    x_rows = _sc_scatter_rows(h2, dest_w, n_blocks * MOE_ROWS)
    y_rows = _experts(x_rows, blk_e, n_used, valid, p)
    yg = _sc_gather_rows(y_rows, dest.reshape(-1)).reshape(TOP_K, n, D_MODEL // 2)
    out = _combine(yg, x1, route, p["final_g"], _pick_tile(n, 512))
    return out.reshape(b_sz, t_len, d)


def kernel(x_prompt, x_sample, norm1_g, w_in, rw_mu, gm_ln_g, gm_ln_b, gm_ws, gm_bs, rw_w0, rw_w2, rw_a0, rw_a2,
           rw_g2, rw_k_k, rw_k_a, rw_r_k, rw_gn_g, rw_gn_b, w_out, norm2_g, router_w, router_b, w_gu, b_gu,
           w_down, b_down, final_g):
    p = _prepare(norm1_g, w_in, rw_mu, gm_ln_g, gm_ln_b, gm_ws, gm_bs, rw_w0, rw_w2, rw_a0, rw_a2, rw_g2,
                 rw_k_k, rw_k_a, rw_r_k, rw_gn_g, rw_gn_b, w_out, norm2_g, router_w, router_b,
                 w_gu, b_gu, w_down, b_down, final_g)
    return (_encoder(x_prompt, p), _encoder(x_sample, p))
```

```python
import functools

import jax
import jax.numpy as jnp
from jax import lax
from jax.experimental import pallas as pl
from jax.experimental.pallas import tpu as pltpu
from jax.experimental.pallas import tpu_sc as plsc

F32 = jnp.float32
BF16 = jnp.bfloat16

D_MODEL = 1024
GM_WIDTH = 512
RW_WIDTH = 512
GM_HEADS = 4
GM_HEAD_DIM = 128
GM_CHUNK = 128
RW_HEAD_DIM = 64
DECAY_LORA = 32
ICLR_LORA = 32
GATE_LORA = 96
LORA_COLS = DECAY_LORA + ICLR_LORA + GATE_LORA
LORA_PAD = 256
Z_COLS = 3 * RW_WIDTH + LORA_PAD
Z_CHUNK = 256
N_EXPERTS = 32
TOP_K = 4
D_EXPERT = 1024
SWIGLU_LIMIT = 7.0
SWIGLU_ALPHA = 1.702
RMS_EPS = 1e-5
LN_EPS = 1e-5
GN_EPS = 64e-5

LANES = 128
SUBLANES = 8
VMEM_LIMIT = 56 * 1024 * 1024

RW_CHUNK = 64
PAIR = 2 * RW_HEAD_DIM
N_PAIRS = RW_WIDTH // PAIR
RW_BATCH_TILE = 4
MIX_OUT_SUB = 128
MOE_ROWS = 512
SC_CORES = 2
SC_SUBCORES = 16
SC_WINDOW = 64
SC_SCATTER_WINDOW = 128
ROUTE_ROWS = 16
DECAY_SCALE = 0.6065306597126334
KK_NORM_FLOOR_SQ = 1e-24


def _dot(a, b):
    return jnp.dot(a, b, preferred_element_type=F32)


def _dot_nt(a, b):
    return lax.dot_general(a, b, (((1,), (1,)), ((), ())), preferred_element_type=F32)


def _split(x):
    hi = x.astype(BF16)
    lo = (x - hi.astype(F32)).astype(BF16)
    return hi, lo


def _dot3(a, b_hi, b_lo):
    a_hi, a_lo = _split(a)
    return _dot(a_hi, b_hi) + _dot(a_lo, b_hi) + _dot(a_hi, b_lo)


def _dot2(a, b01):
    a_hi, a_lo = _split(a)
    return _dot(a_hi, b01) + _dot(a_lo, b01)


def _head_sums(x, pair_ones):
    rows = x.shape[0]
    n_tiles = x.shape[1] // LANES
    stacked = jnp.concatenate([x[:, t * LANES:(t + 1) * LANES] for t in range(n_tiles)], axis=0)
    sums = _dot2(stacked, pair_ones)
    return jnp.concatenate([sums[t * rows:(t + 1) * rows] for t in range(n_tiles)], axis=1)


def _dot2_lhs01(a01, b):
    b_hi, b_lo = _split(b)
    return _dot(a01, b_hi) + _dot(a01, b_lo)


def _pack_bf16_halves(x):
    w = x.shape[1] // 2
    hi = lax.bitcast_convert_type(x[:, :w].astype(BF16).astype(F32), jnp.int32)
    lo = lax.bitcast_convert_type(x[:, w:].astype(BF16).astype(F32), jnp.int32)
    return hi | lax.shift_right_logical(lo, 16)


def _unpack_bf16_halves(words):
    left = lax.bitcast_convert_type(words & jnp.int32(-65536), F32)
    right = lax.bitcast_convert_type(lax.shift_left(words, 16), F32)
    return left, right


def _gelu_tanh(x):
    return 0.5 * x * (1.0 + jnp.tanh(0.7978845608028654 * (x + 0.044715 * (x * x * x))))


def _sigmoid(x):
    return 1.0 / (1.0 + jnp.exp(-x))


def _mix_in_body(tiles_per_seq, x_ref, xp_ref, xn_ref, g1_ref, wgm_ref, wz_ref, mu_ref, lng_ref, lnb_ref, ws_ref,
                 bs_ref, gm_ref, z_ref):
    i = pl.program_id(0)
    tm = x_ref.shape[0]

    def normed(x):
        return (x * lax.rsqrt(jnp.mean(x * x, axis=-1, keepdims=True) + RMS_EPS) * g1_ref[...]).astype(BF16)

    hb = normed(x_ref[...])
    uv = _dot(hb, wgm_ref[...])
    hb_halo = normed(jnp.concatenate([xp_ref[...], xn_ref[...]], axis=0))
    has_prev = i % tiles_per_seq > 0
    has_next = i % tiles_per_seq < tiles_per_seq - 1
    row8 = lax.broadcasted_iota(jnp.int32, (SUBLANES, Z_CHUNK), 0)
    for j in range(Z_COLS // Z_CHUNK):
        cols = slice(j * Z_CHUNK, (j + 1) * Z_CHUNK)
        z = _dot(hb, wz_ref[:, cols])
        halo = _dot(hb_halo, wz_ref[:, cols])
        prev_row = jnp.where(has_prev, halo[SUBLANES - 1:SUBLANES, :], 0.0)
        next_row = jnp.where(has_next, halo[SUBLANES:SUBLANES + 1, :], 0.0)
        down = pltpu.roll(z, 1, axis=0)
        up = pltpu.roll(z, tm - 1, axis=0)
        prev = jnp.concatenate([jnp.where(row8 == 0, prev_row, down[:SUBLANES]), down[SUBLANES:]], axis=0)
        nxt = jnp.concatenate([up[:tm - SUBLANES], jnp.where(row8 == SUBLANES - 1, next_row, up[tm - SUBLANES:])],
                              axis=0)
        z_ref[:, cols] = z * mu_ref[0:1, cols] + (prev + nxt) * mu_ref[1:2, cols]
    u = _gelu_tanh(uv[:, :GM_WIDTH])
    v = _gelu_tanh(uv[:, GM_WIDTH:])
    mean = jnp.mean(v, axis=-1, keepdims=True)
    vc = v - mean
    var = jnp.mean(vc * vc, axis=-1, keepdims=True)
    vn = (vc * lax.rsqrt(var + LN_EPS) * lng_ref[...] + lnb_ref[...]).astype(BF16)
    for c in range(tm // GM_CHUNK):
        rows = slice(c * GM_CHUNK, (c + 1) * GM_CHUNK)
        for hd in range(GM_HEADS):
            cols = slice(hd * GM_HEAD_DIM, (hd + 1) * GM_HEAD_DIM)
            mixed = _dot(ws_ref[hd], vn[rows, cols]) + bs_ref[hd]
            gm_ref[rows, cols] = (u[rows, cols] * mixed).astype(BF16)


def _mix_in(x2d, p, tm, t_len):
    n = x2d.shape[0]
    assert t_len % tm == 0
    halo_per_tile = tm // SUBLANES
    n_halo = n // SUBLANES
    full = lambda shape: pl.BlockSpec(shape, lambda i: (0,) * len(shape))
    return pl.pallas_call(
        functools.partial(_mix_in_body, t_len // tm),
        grid=(n // tm,),
        in_specs=[
            pl.BlockSpec((tm, D_MODEL), lambda i: (i, 0)),
            pl.BlockSpec((SUBLANES, D_MODEL), lambda i: (jnp.maximum(i * halo_per_tile - 1, 0), 0)),
            pl.BlockSpec((SUBLANES, D_MODEL), lambda i: (jnp.minimum((i + 1) * halo_per_tile, n_halo - 1), 0)),
            full((1, D_MODEL)),
            full((D_MODEL, 2 * GM_WIDTH)),
            full((D_MODEL, Z_COLS)),
            full((2, Z_COLS)),
            full((1, GM_WIDTH)),
            full((1, GM_WIDTH)),
            full((GM_HEADS, GM_CHUNK, GM_CHUNK)),
            full((GM_HEADS, GM_CHUNK, GM_HEAD_DIM)),
        ],
        out_specs=[
            pl.BlockSpec((tm, GM_WIDTH), lambda i: (i, 0)),
            pl.BlockSpec((tm, Z_COLS), lambda i: (i, 0)),
        ],
        out_shape=[
            jax.ShapeDtypeStruct((n, GM_WIDTH), BF16),
            jax.ShapeDtypeStruct((n, Z_COLS), F32),
        ],
        compiler_params=pltpu.CompilerParams(vmem_limit_bytes=VMEM_LIMIT),
        name="mix_in",
    )(x2d, x2d, x2d, p["norm1_g"], p["w_gm"], p["w_z"], p["rw_mu"], p["gm_ln_g"], p["gm_ln_b"], p["gm_ws"],
      p["gm_bs"])


def _rwkv_masks(reverse):
    c = RW_CHUNK
    ti = lax.broadcasted_iota(jnp.int32, (c, c), 0)
    tj = lax.broadcasted_iota(jnp.int32, (c, c), 1)
    incl = (tj >= ti) if reverse else (tj <= ti)
    lane =lax.broadcasted_iota(jnp.int32, (c, PAIR), 1)
    head0 = lane < RW_HEAD_DIM
    lane2 = lax.broadcasted_iota(jnp.int32, (c, 2 * c), 1)
    left = lane2 < c
    ti2 = lax.broadcasted_iota(jnp.int32, (c, 2 * c), 0)
    tj2 = jnp.where(left, lane2, lane2 - c)
    incl2 = (tj2 >= ti2) if reverse else (tj2 <= ti2)
    strict2 = (tj2 > ti2) if reverse else (tj2 < ti2)
    eye2 = jnp.where(tj2 == ti2, 1.0, 0.0).astype(F32)
    si = lax.broadcasted_iota(jnp.int32, (PAIR, PAIR), 0)
    sj = lax.broadcasted_iota(jnp.int32, (PAIR, PAIR), 1)
    same_head = (si // RW_HEAD_DIM) == (sj // RW_HEAD_DIM)
    return dict(
        incl2=incl2, strict2=strict2, head0=head0, left=left, eye2=eye2, same_head=same_head,
        tri=jnp.where(incl, 1.0, 0.0).astype(BF16),
    )


def _stack_heads(x, head0):
    return jnp.concatenate([jnp.where(head0, x, 0.0), jnp.where(head0, 0.0, x)], axis=0)


def _block_diag(x, left):
    return jnp.concatenate([jnp.where(left, x, 0.0), jnp.where(left, 0.0, x)], axis=0)


def _units_chunk(states, ops, m):
    c = RW_CHUNK
    units = range(len(ops))
    bf = lambda t: t.astype(BF16)
    head0, left = m["head0"], m["left"]
    ar = [bf(jnp.concatenate([o["a_t"], o["r_t"]], axis=0)) for o in ops]
    rstack = [bf(jnp.concatenate([_stack_heads(o["b_t"], head0), _stack_heads(o["k_t"], head0)], axis=0))
              for o in ops]
    acat = [_dot_nt(ar[i], rstack[i]) for i in units]
    sa = [_dot_nt(ar[i], bf(states[i])) for i in units]
    a_ab = [jnp.where(m["strict2"], acat[i][:c, :2 * c], 0.0) for i in units]
    a_ak = [bf(jnp.where(m["strict2"], acat[i][:c, 2 * c:], 0.0)) for i in units]
    a_r = [bf(jnp.concatenate([jnp.where(m["incl2"], acat[i][c:, :2 * c], 0.0),
                               jnp.where(m["incl2"], acat[i][c:, 2 * c:], 0.0)], axis=1)) for i in units]
    p = [m["eye2"] + a_ab[i] for i in units]
    lk = [_dot(bf(a_ab[i]), bf(_block_diag(a_ab[i], left))) for i in units]
    vstack = [bf(_stack_heads(o["v"], head0)) for o in ops]
    rhs = [sa[i][:c] + _dot(a_ak[i], vstack[i]) for i in units]
    power = 2
    while 2 * power < c:
        res = [_dot(bf(jnp.concatenate([p[i], lk[i]], axis=0)), bf(_block_diag(lk[i], left))) for i in units]
        p = [p[i] + res[i][:c] for i in units]
        lk = [res[i][c:] for i in units]
        power *= 2
    p = [p[i] + _dot(bf(p[i]), bf(_block_diag(lk[i], left))) for i in units]
    u = [_dot(bf(p[i]), bf(_stack_heads(rhs[i], head0))) for i in units]
    ys = [sa[i][c:] + _dot(a_r[i], jnp.concatenate([bf(_stack_heads(u[i], head0)), vstack[i]], axis=0))
          for i in units]
    upd = [_dot(bf(jnp.concatenate([u[i], ops[i]["v"]], axis=0).T),
                bf(jnp.concatenate([ops[i]["b_h"], ops[i]["k_h"]], axis=0))) for i in units]
    new_states = [states[i] * ops[i]["w_tot"] + jnp.where(m["same_head"], upd[i], 0.0) for i in units]
    return new_states, ys


def _rwkv_body(reverse, bt, *refs):
    if reverse:
        (zc_ref, yf_ref, lw_hi_ref, lw_lo_ref, w0_ref, a0_ref, a0o_ref, kk_ref, ka_ref,
         rk_ref, bd_ref, gng_ref, gnb_ref, out_ref, state_ref) = refs
    else:
        (zc_ref, lw_hi_ref, lw_lo_ref, w0_ref, a0_ref, kk_ref, ka_ref,
         bd_ref, out_ref, state_ref) = refs
    c = RW_CHUNK

    @pl.when(pl.program_id(1) == 0)
    def _():
        state_ref[...] = jnp.zeros_like(state_ref)

    rows = bt * c
    m = _rwkv_masks(reverse)
    lora_lane = lax.broadcasted_iota(jnp.int32, (rows, LORA_PAD), 1)
    bd = bd_ref[...]
    zf = zc_ref[...].reshape(rows, Z_COLS)
    r = zf[:, :RW_WIDTH]
    k = zf[:, RW_WIDTH:2 * RW_WIDTH]
    v = zf[:, 2 * RW_WIDTH:3 * RW_WIDTH]
    lo = zf[:, 3 * RW_WIDTH:]
    act = jnp.where(lora_lane < DECAY_LORA, jnp.tanh(lo),
                    jnp.where(lora_lane < DECAY_LORA + ICLR_LORA, lo, _sigmoid(lo)))
    proj = _dot3(act, lw_hi_ref[...], lw_lo_ref[...])
    w_in = w0_ref[...] + proj[:, :RW_WIDTH]
    logw = -DECAY_SCALE * _sigmoid(w_in)
    a_sig = _sigmoid(a0_ref[...] + proj[:, RW_WIDTH:2 * RW_WIDTH])
    kk = k * kk_ref[...]
    kk = kk * lax.rsqrt(jnp.maximum(_head_sums(kk * kk, bd), KK_NORM_FLOOR_SQ))
    kd = k * (1.0 + (a_sig - 1.0) * ka_ref[...])
    a_vec = -kk
    b_vec = kk * a_sig
    ri = lax.broadcasted_iota(jnp.int32, (rows, rows), 0)
    rj = lax.broadcasted_iota(jnp.int32, (rows, rows), 1)
    ordered = (rj >= ri) if reverse else (rj <= ri)
    tri = jnp.where((ri // c) == (rj // c), jnp.where(ordered, 1.0, 0.0), 0.0).astype(BF16)
    cum = _dot2_lhs01(tri, logw)
    edge = 0 if reverse else c - 1
    tot = jnp.concatenate([jnp.broadcast_to(cum[b * c + edge:b * c + edge + 1, :], (c, RW_WIDTH))
                           for b in range(bt)], axis=0)
    w_inv = jnp.exp(-cum)
    w_end = jnp.exp(tot - cum)
    w_tot = jnp.exp(tot)
    full_ops = dict(r_t=r * jnp.exp(cum), a_t=a_vec * jnp.exp(cum - logw), b_t=b_vec * w_inv, k_t=kd * w_inv,
                    b_h=b_vec * w_end, k_h=kd * w_end, v=v)
    ops, states = [], []
    for b in range(bt):
        for pi in range(N_PAIRS):
            ln = slice(pi * PAIR, (pi + 1) * PAIR)
            unit = {name: t[b * c:(b + 1) * c, ln] for name, t in full_ops.items()}
            unit["w_tot"] = w_tot[b * c:b * c + 1, ln]
            ops.append(unit)
            states.append(state_ref[b, pi])
    new_states, ys = _units_chunk(states, ops, m)
    for b in range(bt):
        for pi in range(N_PAIRS):
            state_ref[b, pi] = new_states[b * N_PAIRS + pi]
    y = jnp.concatenate([jnp.concatenate(ys[b * N_PAIRS:(b + 1) * N_PAIRS], axis=1) for b in range(bt)], axis=0)
    if not reverse:
        out_ref[...] = y.reshape(bt, c, RW_WIDTH)
    else:
        y = y + yf_ref[...].reshape(rows, RW_WIDTH)
        a_other = _sigmoid(a0o_ref[...] + proj[:, 2 * RW_WIDTH:3 * RW_WIDTH])
        kd_sum = kd + k * (1.0 + (a_other - 1.0) * ka_ref[...])
        bonus = _head_sums(r * kd_sum * rk_ref[...], bd) * v
        inv_n = 1.0 / RW_HEAD_DIM
        mean = _head_sums(y, bd) * inv_n
        yc = y - mean
        var = _head_sums(yc * yc, bd) * inv_n
        yn = yc * lax.rsqrt(var + GN_EPS) * gng_ref[...] + gnb_ref[...]
        g = proj[:, 3 * RW_WIDTH:]
        out_ref[...] = ((yn + bonus) * g).astype(BF16).reshape(bt, c, RW_WIDTH)


def _rwkv_pass(z3, yf, p, reverse, bt):
    b_sz, t_len, _ = z3.shape
    c = RW_CHUNK
    n_chunks = t_len // c

    def cidx(ci):
        return (n_chunks - 1 - ci) if reverse else ci

    full = lambda shape: pl.BlockSpec(shape, lambda bi, ci: (0,) * len(shape))
    row512 = full((1, RW_WIDTH))
    d = "rev" if reverse else "fwd"
    n_proj = p["lora_hi_" + d].shape[1]
    in_specs = [pl.BlockSpec((bt, c, Z_COLS), lambda bi, ci: (bi, cidx(ci), 0))]
    args = [z3]
    if reverse:
        in_specs.append(pl.BlockSpec((bt, c, RW_WIDTH), lambda bi, ci: (bi, cidx(ci), 0)))
        args.append(yf)
    in_specs += [full((LORA_PAD, n_proj)), full((LORA_PAD, n_proj)), row512, row512]
    args += [p["lora_hi_" + d], p["lora_lo_" + d], p["w0_" + d], p["a0_" + d]]
    if reverse:
        in_specs.append(row512)
        args.append(p["a0_fwd"])
    in_specs += [row512, row512]
    args += [p["rw_k_k"], p["rw_k_a"]]
    if reverse:
        in_specs.append(row512)
        args.append(p["rw_r_k"])
    in_specs.append(full((PAIR, PAIR)))
    args.append(p["head_ones"])
    if reverse:
        in_specs += [row512, row512]
        args += [p["rw_gn_g"], p["rw_gn_b"]]
    return pl.pallas_call(
        functools.partial(_rwkv_body, reverse, bt),
        grid=(b_sz // bt, n_chunks),
        in_specs=in_specs,
        out_specs=pl.BlockSpec((bt, c, RW_WIDTH), lambda bi, ci: (bi, cidx(ci), 0)),
        out_shape=jax.ShapeDtypeStruct((b_sz, t_len, RW_WIDTH), BF16 if reverse else F32),
        scratch_shapes=[pltpu.VMEM((bt, N_PAIRS, PAIR, PAIR), F32)],
        compiler_params=pltpu.CompilerParams(vmem_limit_bytes=VMEM_LIMIT),
        name="rwkv_" + d,
    )(*args)


def _mix_out_body(x_ref, gm_ref, rw_ref, wo_ref, g2_ref, rw_hi_ref, rw_lo_ref, rb_ref, earlier_ref,
                  x1_ref, h2_ref, route_ref, route_t_ref, counts_ref, carry_ref):
    i = pl.program_id(0)

    @pl.when(i == 0)
    def _():
        carry_ref[...] = jnp.zeros_like(carry_ref)

    sub = earlier_ref.shape[0]
    subs = range(x_ref.shape[0] // sub)
    rows = [slice(s * sub, (s + 1) * sub) for s in subs]
    x1 = [x_ref[r, :] + _dot(gm_ref[r, :], wo_ref[:GM_WIDTH, :]) + _dot(rw_ref[r, :], wo_ref[GM_WIDTH:, :])
          for r in rows]
    h2 = [x1[s] * lax.rsqrt(jnp.mean(x1[s] * x1[s], axis=-1, keepdims=True) + RMS_EPS) * g2_ref[...] for s in subs]
    for s in subs:
        x1_ref[rows[s], :] = x1[s]
        h2_ref[rows[s], :] = _pack_bf16_halves(h2[s])
    split = [_split(h) for h in h2]
    logits = [_dot_nt(rw_hi_ref[...], hi) + _dot_nt(rw_hi_ref[...], lo) + _dot_nt(rw_lo_ref[...], hi) + rb_ref[...]
              for hi, lo in split]
    expert = lax.broadcasted_iota(jnp.int32, (N_EXPERTS, sub), 0).astype(F32)
    work = logits
    vals, ids, hits = [[] for _ in subs], [[] for _ in subs], [[] for _ in subs]
    onehot = [jnp.zeros((N_EXPERTS, sub), F32) for _ in subs]
    for _ in range(TOP_K):
        for s in subs:
            mx = jnp.max(work[s], axis=0, keepdims=True)
            idx = jnp.min(jnp.where(work[s] == mx, expert, float(N_EXPERTS)), axis=0, keepdims=True)
            hit = expert == idx
            vals[s].append(mx)
            ids[s].append(idx)
            hits[s].append(hit)
            onehot[s] = jnp.where(hit, 1.0, onehot[s])
        work = [jnp.where(hits[s][-1], -jnp.inf, work[s]) for s in subs]
    earlier_counts = [_dot(onehot[s].astype(BF16), earlier_ref[...]) for s in subs]
    carry = carry_ref[:, 0:1]
    for s in subs:
        exps = [jnp.exp(vk - vals[s][0]) for vk in vals[s]]
        denom = exps[0] + exps[1] + exps[2] + exps[3]
        ranks = carry + earlier_counts[s]
        rank_rows = [jnp.sum(jnp.where(hit, ranks, 0.0), axis=0, keepdims=True) for hit in hits[s]]
        route_t = jnp.concatenate(ids[s] + [e / denom for e in exps] + rank_rows
                                  + [jnp.zeros((ROUTE_ROWS - 3 * TOP_K, sub), F32)], axis=0)
        route_t_ref[:, rows[s]] = route_t
        route_ref[rows[s], :] = jnp.concatenate([route_t, jnp.zeros((LANES - ROUTE_ROWS, sub), F32)], axis=0).T
        carry = carry + jnp.sum(onehot[s], axis=1, keepdims=True)
    carry_ref[...] = jnp.broadcast_to(carry, carry_ref.shape)
    counts_ref[...] = jnp.broadcast_to(carry, counts_ref.shape)


def _mix_out(x2d, gm, rw, p, tm):
    n = x2d.shape[0]
    full = lambda shape: pl.BlockSpec(shape, lambda i: (0,) * len(shape))
    tile = lambda w: pl.BlockSpec((tm, w), lambda i: (i, 0))
    sub = min(MIX_OUT_SUB, tm)
    earlier = (jnp.arange(sub)[:, None] < jnp.arange(sub)[None, :]).astype(BF16)
    return pl.pallas_call(
        _mix_out_body,
        grid=(n // tm,),
        in_specs=[tile(D_MODEL), tile(GM_WIDTH), tile(RW_WIDTH), full((D_MODEL, D_MODEL)), full((1, D_MODEL)),
                  full((N_EXPERTS, D_MODEL)), full((N_EXPERTS, D_MODEL)), full((N_EXPERTS, 1)), full((sub, sub))],
        out_specs=[tile(D_MODEL), tile(D_MODEL // 2), tile(LANES), pl.BlockSpec((ROUTE_ROWS, tm), lambda i: (0, i)),
                   full((N_EXPERTS, LANES))],
        out_shape=[
            jax.ShapeDtypeStruct((n, D_MODEL), F32),
            jax.ShapeDtypeStruct((n, D_MODEL // 2), jnp.int32),
            jax.ShapeDtypeStruct((n, LANES), F32),
            jax.ShapeDtypeStruct((ROUTE_ROWS, n), F32),
            jax.ShapeDtypeStruct((N_EXPERTS, LANES), F32),
        ],
        scratch_shapes=[pltpu.VMEM((N_EXPERTS, LANES), F32)],
        compiler_params=pltpu.CompilerParams(vmem_limit_bytes=VMEM_LIMIT),
        name="mix_out",
    )(x2d, gm, rw, p["w_out"], p["norm2_g"], p["router_hi"], p["router_lo"], p["router_b"], earlier)


def _sc_gather_rows(table, idx):
    n_idx = idx.shape[0]
    width = table.shape[1]
    n_workers = SC_CORES * SC_SUBCORES
    per_worker = n_idx // n_workers
    n_windows = per_worker // SC_WINDOW
    assert per_worker * n_workers == n_idx and n_windows * SC_WINDOW == per_worker
    mesh = plsc.VectorSubcoreMesh(core_axis_name="c", subcore_axis_name="s")

    @functools.partial(
        pl.kernel, mesh=mesh,
        out_type=jax.ShapeDtypeStruct((n_idx, width), table.dtype),
        scratch_types=[
            pltpu.VMEM((SC_WINDOW,), jnp.int32),
            pltpu.VMEM((SC_WINDOW, width), table.dtype),
            pltpu.SemaphoreType.DMA,
        ],
        name="sc_gather_rows",
    )
    def gather(table_hbm, idx_hbm, out_hbm, idx_v, rows_v, sem):
        worker = lax.axis_index("s") * SC_CORES + lax.axis_index("c")
        base = worker * per_worker

        @pl.loop(0, n_windows)
        def _(j):
            off = base + j * SC_WINDOW
            pltpu.sync_copy(idx_hbm.at[pl.ds(off, SC_WINDOW)], idx_v)
            pltpu.async_copy(table_hbm.at[idx_v], rows_v, sem).wait()
            pltpu.sync_copy(rows_v, out_hbm.at[pl.ds(off, SC_WINDOW)])

    return gather(table, idx)


def _sc_scatter_rows(src, dest_w, n_out):
    n, width = src.shape
    n_windows, top_k, window = dest_w.shape
    n_workers = SC_CORES * SC_SUBCORES
    per_worker = n_windows // n_workers
    assert n_windows * window == n and per_worker * n_workers == n_windows
    mesh = plsc.VectorSubcoreMesh(core_axis_name="c", subcore_axis_name="s")

    @functools.partial(
        pl.kernel, mesh=mesh,
        out_type=jax.ShapeDtypeStruct((n_out, width), src.dtype),
        scratch_types=[
            pltpu.VMEM((top_k, window), jnp.int32),
            pltpu.VMEM((window, width), src.dtype),
        ],
        name="sc_scatter_rows",
    )
    def scatter(src_hbm, dest_hbm, out_hbm, idx_v, rows_v):
        worker = lax.axis_index("s") * SC_CORES + lax.axis_index("c")

        @pl.loop(0, per_worker)
        def _(j):
            g = worker * per_worker + j
            pltpu.sync_copy(dest_hbm.at[g], idx_v)
            pltpu.sync_copy(src_hbm.at[pl.ds(g * window, window)], rows_v)
            for k in range(top_k):
                pltpu.sync_copy(rows_v, out_hbm.at[idx_v.at[k]])

    return scatter(src, dest_w)


def _weight_copies(wgu_hbm, wd_hbm, wgu_buf, wd_buf, sem, expert, slot):
    return (pltpu.make_async_copy(wgu_hbm.at[expert], wgu_buf.at[slot], sem.at[0, slot]),
            pltpu.make_async_copy(wd_hbm.at[expert], wd_buf.at[slot], sem.at[1, slot]))


def _experts_body(blk_e_ref, n_used_ref, valid_ref, slot_ref, next_e_ref, x_ref, wgu_hbm, bgu_ref, wd_hbm, bd_ref,
                  out_ref, wgu_bf, wd_bf, wgu_buf, wd_buf, sem):
    i = pl.program_id(0)
    n_used = n_used_ref[0]
    expert = blk_e_ref[i]
    slot = slot_ref[i]
    copies = functools.partial(_weight_copies, wgu_hbm, wd_hbm, wgu_buf, wd_buf, sem)

    @pl.when(i == 0)
    def _():
        for cp in copies(expert, slot):
            cp.start()

    @pl.when((i < n_used) & ((i == 0) | (expert != blk_e_ref[jnp.maximum(i - 1, 0)])))
    def _():
        for cp in copies(expert, slot):
            cp.wait()
        wgu_bf[...] = wgu_buf[slot].astype(BF16)
        wd_bf[...] = wd_buf[slot].astype(BF16)
        following = next_e_ref[i]

        @pl.when(following >= 0)
        def _():
            for cp in copies(following, 1 - slot):
                cp.start()

    valid = jnp.where(i < n_used, valid_ref[i], 0)

    def expert_mlp(n_rows):
        row = lax.broadcasted_iota(jnp.int32, (n_rows, 1), 0)
        x_left, x_right = _unpack_bf16_halves(jnp.where(row < valid, x_ref[:n_rows, :], 0))
        half = D_MODEL // 2
        gu = (_dot(x_left.astype(BF16), wgu_bf[:half, :]) + _dot(x_right.astype(BF16), wgu_bf[half:, :])
              + bgu_ref[0])
        gate = jnp.minimum(gu[:, :D_EXPERT], SWIGLU_LIMIT)
        up = jnp.clip(gu[:, D_EXPERT:], -SWIGLU_LIMIT, SWIGLU_LIMIT)
        act = gate * _sigmoid(gate * SWIGLU_ALPHA) * (up + 1.0)
        out_ref[:n_rows, :] = _pack_bf16_halves(_dot(act.astype(BF16), wd_bf[...]) + bd_ref[0])

    half_rows = MOE_ROWS // 2

    @pl.when(valid > half_rows)
    def _():
        expert_mlp(MOE_ROWS)

    @pl.when((valid > 0) & (valid <= half_rows))
    def _():
        expert_mlp(half_rows)
        out_ref[half_rows:, :] = jnp.zeros((MOE_ROWS - half_rows, D_MODEL // 2), jnp.int32)

    @pl.when(valid == 0)
    def _():
        out_ref[...] = jnp.zeros_like(out_ref)


def _experts(x_rows, blk_e, n_used, valid, slot, next_e, p):
    n_blocks = blk_e.shape[0]
    grid_spec = pltpu.PrefetchScalarGridSpec(
        num_scalar_prefetch=5,
        grid=(n_blocks,),
        in_specs=[
            pl.BlockSpec((MOE_ROWS, D_MODEL // 2), lambda i, be, nu, *_: (jnp.minimum(i, nu[0] - 1), 0)),
            pl.BlockSpec(memory_space=pl.ANY),
            pl.BlockSpec((1, 1, 2 * D_EXPERT), lambda i, be, *_: (be[i], 0, 0)),
            pl.BlockSpec(memory_space=pl.ANY),
            pl.BlockSpec((1, 1, D_MODEL), lambda i, be, *_: (be[i], 0, 0)),
        ],
        out_specs=pl.BlockSpec((MOE_ROWS, D_MODEL // 2), lambda i, *_: (i, 0)),
        scratch_shapes=[
            pltpu.VMEM((D_MODEL, 2 * D_EXPERT), BF16),
            pltpu.VMEM((D_EXPERT, D_MODEL), BF16),
            pltpu.VMEM((2, D_MODEL, 2 * D_EXPERT), F32),
            pltpu.VMEM((2, D_EXPERT, D_MODEL), F32),
            pltpu.SemaphoreType.DMA((2, 2)),
        ],
    )
    return pl.pallas_call(
        _experts_body,
        grid_spec=grid_spec,
        out_shape=jax.ShapeDtypeStruct((n_blocks * MOE_ROWS, D_MODEL // 2), jnp.int32),
        compiler_params=pltpu.CompilerParams(vmem_limit_bytes=VMEM_LIMIT),
        name="experts",
    )(blk_e, n_used, valid, slot, next_e, x_rows, p["w_gu"], p["b_gu"], p["w_down"], p["b_down"])


def _combine_body(yg_ref, x1_ref, gates_ref, fg_ref, out_ref):
    gates = gates_ref[...]
    half = D_MODEL // 2
    acc_left = x1_ref[:, :half]
    acc_right = x1_ref[:, half:]
    for kk in range(TOP_K):
        y_left, y_right = _unpack_bf16_halves(yg_ref[kk])
        gate = gates[:, TOP_K + kk:TOP_K + kk + 1]
        acc_left = acc_left + gate * y_left
        acc_right = acc_right + gate * y_right
    acc = jnp.concatenate([acc_left, acc_right], axis=1)
    out_ref[...] = acc * lax.rsqrt(jnp.mean(acc * acc, axis=-1, keepdims=True) + RMS_EPS) * fg_ref[...]


def _combine(yg, x1, gates, final_g, tm):
    n = x1.shape[0]
    return pl.pallas_call(
        _combine_body,
        grid=(n // tm,),
        in_specs=[
            pl.BlockSpec((TOP_K, tm, D_MODEL // 2), lambda i: (0, i, 0)),
            pl.BlockSpec((tm, D_MODEL), lambda i: (i, 0)),
            pl.BlockSpec((tm, LANES), lambda i: (i, 0)),
            pl.BlockSpec((1, D_MODEL), lambda i: (0, 0)),
        ],
        out_specs=pl.BlockSpec((tm, D_MODEL), lambda i: (i, 0)),
        out_shape=jax.ShapeDtypeStruct((n, D_MODEL), F32),
        compiler_params=pltpu.CompilerParams(vmem_limit_bytes=VMEM_LIMIT),
        name="combine",
    )(yg, x1, gates, final_g)


def _prepare(norm1_g, w_in, rw_mu, gm_ln_g, gm_ln_b, gm_ws, gm_bs, rw_w0, rw_w2, rw_a0, rw_a2, rw_g2,
             rw_k_k, rw_k_a, rw_r_k, rw_gn_g, rw_gn_b, w_out, norm2_g, router_w, router_b,
             w_gu, b_gu, w_down, b_down, final_g):
    l = 0
    row = lambda t: t.reshape(1, -1).astype(F32)
    p = {}
    p["norm1_g"] = row(norm1_g[l])
    w = w_in[l]
    p["w_gm"] = w[:, :2 * GM_WIDTH].astype(BF16)
    pad_cols = LORA_PAD - LORA_COLS
    p["w_z"] = jnp.pad(w[:, 2 * GM_WIDTH:], ((0, 0), (0, pad_cols))).astype(BF16)
    mu = jnp.pad(row(rw_mu[l]), ((0, 0), (0, pad_cols)))
    p["rw_mu"] = jnp.concatenate([1.0 - mu, 0.5 * mu], axis=0)
    p["gm_ln_g"] = row(gm_ln_g[l])
    p["gm_ln_b"] = row(gm_ln_b[l])
    p["gm_ws"] = gm_ws[l].astype(BF16)
    p["gm_bs"] = jnp.broadcast_to(gm_bs[l][:, :, None], (GM_HEADS, GM_CHUNK, GM_HEAD_DIM)).astype(F32)
    zeros = lambda r, c: jnp.zeros((r, c), F32)
    o_a = DECAY_LORA
    o_g = DECAY_LORA + ICLR_LORA

    def lora_matrix(d, with_epilogue):
        blocks = [jnp.concatenate([rw_w2[l, d], zeros(LORA_PAD - DECAY_LORA, RW_WIDTH)], axis=0),
                  jnp.concatenate([zeros(o_a, RW_WIDTH), rw_a2[l, d], zeros(LORA_PAD - o_g, RW_WIDTH)], axis=0)]
        if with_epilogue:
            blocks.append(jnp.concatenate([zeros(o_a, RW_WIDTH), rw_a2[l, 1 - d], zeros(LORA_PAD - o_g, RW_WIDTH)],
                                          axis=0))
            blocks.append(jnp.concatenate([zeros(o_g, RW_WIDTH), rw_g2[l], zeros(LORA_PAD - LORA_COLS, RW_WIDTH)],
                                          axis=0))
        return jnp.concatenate(blocks, axis=1)

    for d, name in ((0, "fwd"), (1, "rev")):
        mat = lora_matrix(d, with_epilogue=(d == 1))
        hi = mat.astype(BF16)
        p["lora_hi_" + name] = hi
        p["lora_lo_" + name] = (mat - hi.astype(F32)).astype(BF16)
        p["w0_" + name] = row(rw_w0[l, d])
        p["a0_" + name] = row(rw_a0[l, d])
    p["rw_k_k"] = row(rw_k_k[l])
    p["rw_k_a"] = row(rw_k_a[l])
    p["rw_r_k"] = row(rw_r_k[l])
    p["rw_gn_g"] = row(rw_gn_g[l])
    p["rw_gn_b"] = row(rw_gn_b[l])
    ch = jnp.arange(PAIR) // RW_HEAD_DIM
    p["head_ones"] = (ch[:, None] == ch[None, :]).astype(BF16)
    p["w_out"] = w_out[l].astype(BF16)
    p["norm2_g"] = row(norm2_g[l])
    rw_t = router_w[l].astype(F32).T
    hi = rw_t.astype(BF16)
    p["router_hi"] = hi
    p["router_lo"] = (rw_t - hi.astype(F32)).astype(BF16)
    p["router_b"] = router_b[l].astype(F32).reshape(N_EXPERTS, 1)
    p["w_gu"] = w_gu[l]
    p["b_gu"] = b_gu[l].reshape(N_EXPERTS, 1, 2 * D_EXPERT).astype(F32)
    p["w_down"] = w_down[l]
    p["b_down"] = b_down[l].reshape(N_EXPERTS, 1, D_MODEL).astype(F32)
    p["final_g"] = row(final_g)
    return p


def _pick_tile(n, want):
    t = want
    while n % t:
        t //= 2
    return t


def _encoder(x, p):
    b_sz, t_len, d = x.shape
    n = b_sz * t_len
    x2d = x.reshape(n, d)
    gm, z = _mix_in(x2d, p, _pick_tile(t_len, 1024), t_len)
    z3 = z.reshape(b_sz, t_len, Z_COLS)
    yf = _rwkv_pass(z3, None, p, reverse=False, bt=RW_BATCH_TILE)
    rw = _rwkv_pass(z3, yf, p, reverse=True, bt=RW_BATCH_TILE)
    tmo = _pick_tile(n, 1024)
    x1, h2, route, route_t, counts = _mix_out(x2d, gm, rw.reshape(n, RW_WIDTH), p, tmo)
    ids = route_t[:TOP_K].astype(jnp.int32)
    ranks = route_t[2 * TOP_K:3 * TOP_K].astype(jnp.int32)
    cnt = counts[:, 0].astype(jnp.int32)
    padded = (cnt + MOE_ROWS - 1) // MOE_ROWS * MOE_ROWS
    pad_end = jnp.cumsum(padded)
    pad_start = pad_end - padded
    expert_ids = jnp.arange(N_EXPERTS, dtype=jnp.int32)
    start_of = jnp.sum(jnp.where(ids[..., None] == expert_ids, pad_start, 0), axis=-1)
    dest = start_of + ranks
    n_blocks = n * TOP_K // MOE_ROWS + N_EXPERTS
    blk_start = jnp.arange(n_blocks, dtype=jnp.int32) * MOE_ROWS
    blk_e = jnp.minimum(jnp.sum((pad_end[None, :] <= blk_start[:, None]).astype(jnp.int32), axis=1), N_EXPERTS - 1)
    n_used = (pad_end[-1] // MOE_ROWS).astype(jnp.int32).reshape(1)
    blk_onehot = blk_e[:, None] == expert_ids
    blk_cnt = jnp.sum(jnp.where(blk_onehot, cnt, 0), axis=-1)
    blk_first = jnp.sum(jnp.where(blk_onehot, pad_start, 0), axis=-1)
    valid = jnp.clip(blk_cnt - (blk_start - blk_first), 0, MOE_ROWS).astype(jnp.int32)
    window = min(SC_SCATTER_WINDOW, n // (SC_CORES * SC_SUBCORES))
    dest_w = dest.reshape(TOP_K, n // window, window).transpose(1, 0, 2)
    x_rows = _sc_scatter_rows(h2, dest_w, n_blocks * MOE_ROWS)
    present = cnt > 0
    slot_of_expert = (jnp.cumsum(present.astype(jnp.int32)) - 1) % 2
    later = present[None, :] & (expert_ids[None, :] > expert_ids[:, None])
    next_present = jnp.min(jnp.where(later, expert_ids[None, :], N_EXPERTS), axis=1)
    next_present = jnp.where(next_present < N_EXPERTS, next_present, -1)
    slot = jnp.sum(jnp.where(blk_onehot, slot_of_expert, 0), axis=-1).astype(jnp.int32)
    next_e = jnp.sum(jnp.where(blk_onehot, next_present, 0), axis=-1).astype(jnp.int32)
    y_rows = _experts(x_rows, blk_e, n_used, valid, slot, next_e, p)
    yg = _sc_gather_rows(y_rows, dest.reshape(-1)).reshape(TOP_K, n, D_MODEL // 2)
    out = _combine(yg, x1, route, p["final_g"], _pick_tile(n, 512))
    return out.reshape(b_sz, t_len, d)


def kernel(x_prompt, x_sample, norm1_g, w_in, rw_mu, gm_ln_g, gm_ln_b, gm_ws, gm_bs, rw_w0, rw_w2, rw_a0, rw_a2,
           rw_g2, rw_k_k, rw_k_a, rw_r_k, rw_gn_g, rw_gn_b, w_out, norm2_g, router_w, router_b, w_gu, b_gu,
           w_down, b_down, final_g):
    p = _prepare(norm1_g, w_in, rw_mu, gm_ln_g, gm_ln_b, gm_ws, gm_bs, rw_w0, rw_w2, rw_a0, rw_a2, rw_g2,
                 rw_k_k, rw_k_a, rw_r_k, rw_gn_g, rw_gn_b, w_out, norm2_g, router_w, router_b,
                 w_gu, b_gu, w_down, b_down, final_g)
    return (_encoder(x_prompt, p), _encoder(x_sample, p))
```

```python
import functools

import jax
import jax.numpy as jnp
from jax import lax
from jax.experimental import pallas as pl
from jax.experimental.pallas import tpu as pltpu
from jax.experimental.pallas import tpu_sc as plsc

F32 = jnp.float32
BF16 = jnp.bfloat16

D_MODEL = 1024
GM_WIDTH = 512
RW_WIDTH = 512
GM_HEADS = 4
GM_HEAD_DIM = 128
GM_CHUNK = 128
RW_HEAD_DIM = 64
DECAY_LORA = 32
ICLR_LORA = 32
GATE_LORA = 96
LORA_COLS = DECAY_LORA + ICLR_LORA + GATE_LORA
LORA_PAD = 256
Z_COLS = 3 * RW_WIDTH + LORA_PAD
Z_CHUNK = 256
N_EXPERTS = 32
TOP_K = 4
D_EXPERT = 1024
SWIGLU_LIMIT = 7.0
SWIGLU_ALPHA = 1.702
RMS_EPS = 1e-5
LN_EPS = 1e-5
GN_EPS = 64e-5

LANES = 128
SUBLANES = 8
VMEM_LIMIT = 56 * 1024 * 1024

RW_CHUNK = 64
PAIR = 2 * RW_HEAD_DIM
N_PAIRS = RW_WIDTH // PAIR
RW_BATCH_TILE = 4
MIX_OUT_SUB = 128
MOE_ROWS = 512
SC_CORES = 2
SC_SUBCORES = 16
SC_WINDOW = 64
SC_SCATTER_WINDOW = 128
ROUTE_ROWS = 16
DECAY_SCALE = 0.6065306597126334
KK_NORM_FLOOR_SQ = 1e-24


def _dot(a, b):
    return jnp.dot(a, b, preferred_element_type=F32)


def _dot_nt(a, b):
    return lax.dot_general(a, b, (((1,), (1,)), ((), ())), preferred_element_type=F32)


def _split(x):
    hi = x.astype(BF16)
    lo = (x - hi.astype(F32)).astype(BF16)
    return hi, lo


def _dot3(a, b_hi, b_lo):
    a_hi, a_lo = _split(a)
    return _dot(a_hi, b_hi) + _dot(a_lo, b_hi) + _dot(a_hi, b_lo)


def _dot2(a, b01):
    a_hi, a_lo = _split(a)
    return _dot(a_hi, b01) + _dot(a_lo, b01)


def _head_sums(x, pair_ones):
    rows = x.shape[0]
    n_tiles = x.shape[1] // LANES
    stacked = jnp.concatenate([x[:, t * LANES:(t + 1) * LANES] for t in range(n_tiles)], axis=0)
    sums = _dot2(stacked, pair_ones)
    return jnp.concatenate([sums[t * rows:(t + 1) * rows] for t in range(n_tiles)], axis=1)


def _dot2_lhs01(a01, b):
    b_hi, b_lo = _split(b)
    return _dot(a01, b_hi) + _dot(a01, b_lo)


def _pack_bf16_halves(x):
    w = x.shape[1] // 2
    hi = lax.bitcast_convert_type(x[:, :w].astype(BF16).astype(F32), jnp.int32)
    lo = lax.bitcast_convert_type(x[:, w:].astype(BF16).astype(F32), jnp.int32)
    return hi | lax.shift_right_logical(lo, 16)


def _unpack_bf16_halves(words):
    left = lax.bitcast_convert_type(words & jnp.int32(-65536), F32)
    right = lax.bitcast_convert_type(lax.shift_left(words, 16), F32)
    return left, right


def _gelu_tanh(x):
    return 0.5 * x * (1.0 + jnp.tanh(0.7978845608028654 * (x + 0.044715 * (x * x * x))))


def _sigmoid(x):
    return 0.5 * jnp.tanh(0.5 * x) + 0.5


def _mix_in_body(tiles_per_seq, x_ref, xp_ref, xn_ref, g1_ref, wgm_ref, wz_ref, mu_ref, lng_ref, lnb_ref, ws_ref,
                 bs_ref, gm_ref, z_ref):
    i = pl.program_id(0)
    tm = x_ref.shape[0]

    def normed(x):
        return (x * lax.rsqrt(jnp.mean(x * x, axis=-1, keepdims=True) + RMS_EPS) * g1_ref[...]).astype(BF16)

    hb = normed(x_ref[...])
    uv = _dot(hb, wgm_ref[...])
    hb_halo = normed(jnp.concatenate([xp_ref[...], xn_ref[...]], axis=0))
    has_prev = i % tiles_per_seq > 0
    has_next = i % tiles_per_seq < tiles_per_seq - 1
    row8 = lax.broadcasted_iota(jnp.int32, (SUBLANES, Z_CHUNK), 0)
    for j in range(Z_COLS // Z_CHUNK):
        cols = slice(j * Z_CHUNK, (j + 1) * Z_CHUNK)
        z = _dot(hb, wz_ref[:, cols])
        halo = _dot(hb_halo, wz_ref[:, cols])
        prev_row = jnp.where(has_prev, halo[SUBLANES - 1:SUBLANES, :], 0.0)
        next_row = jnp.where(has_next, halo[SUBLANES:SUBLANES + 1, :], 0.0)
        down = pltpu.roll(z, 1, axis=0)
        up = pltpu.roll(z, tm - 1, axis=0)
        prev = jnp.concatenate([jnp.where(row8 == 0, prev_row, down[:SUBLANES]), down[SUBLANES:]], axis=0)
        nxt = jnp.concatenate([up[:tm - SUBLANES], jnp.where(row8 == SUBLANES - 1, next_row, up[tm - SUBLANES:])],
                              axis=0)
        z_ref[:, cols] = z * mu_ref[0:1, cols] + (prev + nxt) * mu_ref[1:2, cols]
    u = _gelu_tanh(uv[:, :GM_WIDTH])
    v = _gelu_tanh(uv[:, GM_WIDTH:])
    mean = jnp.mean(v, axis=-1, keepdims=True)
    vc = v - mean
    var = jnp.mean(vc * vc, axis=-1, keepdims=True)
    vn = (vc * lax.rsqrt(var + LN_EPS) * lng_ref[...] + lnb_ref[...]).astype(BF16)
    for c in range(tm // GM_CHUNK):
        rows = slice(c * GM_CHUNK, (c + 1) * GM_CHUNK)
        for hd in range(GM_HEADS):
            cols = slice(hd * GM_HEAD_DIM, (hd + 1) * GM_HEAD_DIM)
            mixed = _dot(ws_ref[hd], vn[rows, cols]) + bs_ref[hd]
            gm_ref[rows, cols] = (u[rows, cols] * mixed).astype(BF16)


def _mix_in(x2d, p, tm, t_len):
    n = x2d.shape[0]
    assert t_len % tm == 0
    halo_per_tile = tm // SUBLANES
    n_halo = n // SUBLANES
    full = lambda shape: pl.BlockSpec(shape, lambda i: (0,) * len(shape))
    return pl.pallas_call(
        functools.partial(_mix_in_body, t_len // tm),
        grid=(n // tm,),
        in_specs=[
            pl.BlockSpec((tm, D_MODEL), lambda i: (i, 0)),
            pl.BlockSpec((SUBLANES, D_MODEL), lambda i: (jnp.maximum(i * halo_per_tile - 1, 0), 0)),
            pl.BlockSpec((SUBLANES, D_MODEL), lambda i: (jnp.minimum((i + 1) * halo_per_tile, n_halo - 1), 0)),
            full((1, D_MODEL)),
            full((D_MODEL, 2 * GM_WIDTH)),
            full((D_MODEL, Z_COLS)),
            full((2, Z_COLS)),
            full((1, GM_WIDTH)),
            full((1, GM_WIDTH)),
            full((GM_HEADS, GM_CHUNK, GM_CHUNK)),
            full((GM_HEADS, GM_CHUNK, GM_HEAD_DIM)),
        ],
        out_specs=[
            pl.BlockSpec((tm, GM_WIDTH), lambda i: (i, 0)),
            pl.BlockSpec((tm, Z_COLS), lambda i: (i, 0)),
        ],
        out_shape=[
            jax.ShapeDtypeStruct((n, GM_WIDTH), BF16),
            jax.ShapeDtypeStruct((n, Z_COLS), F32),
        ],
        compiler_params=pltpu.CompilerParams(vmem_limit_bytes=VMEM_LIMIT),
        name="mix_in",
    )(x2d, x2d, x2d, p["norm1_g"], p["w_gm"], p["w_z"], p["rw_mu"], p["gm_ln_g"], p["gm_ln_b"], p["gm_ws"],
      p["gm_bs"])


def _rwkv_masks(reverse):
    c = RW_CHUNK
    ti = lax.broadcasted_iota(jnp.int32, (c, c), 0)
    tj = lax.broadcasted_iota(jnp.int32, (c, c), 1)
    incl = (tj >= ti) if reverse else (tj <= ti)
    lane =lax.broadcasted_iota(jnp.int32, (c, PAIR), 1)
    head0 = lane < RW_HEAD_DIM
    lane2 = lax.broadcasted_iota(jnp.int32, (c, 2 * c), 1)
    left = lane2 < c
    ti2 = lax.broadcasted_iota(jnp.int32, (c, 2 * c), 0)
    tj2 = jnp.where(left, lane2, lane2 - c)
    incl2 = (tj2 >= ti2) if reverse else (tj2 <= ti2)
    strict2 = (tj2 > ti2) if reverse else (tj2 < ti2)
    eye2 = jnp.where(tj2 == ti2, 1.0, 0.0).astype(F32)
    si = lax.broadcasted_iota(jnp.int32, (PAIR, PAIR), 0)
    sj = lax.broadcasted_iota(jnp.int32, (PAIR, PAIR), 1)
    same_head = (si // RW_HEAD_DIM) == (sj // RW_HEAD_DIM)
    return dict(
        incl2=incl2, strict2=strict2, head0=head0, left=left, eye2=eye2, same_head=same_head,
        tri=jnp.where(incl, 1.0, 0.0).astype(BF16),
    )


def _stack_heads(x, head0):
    return jnp.concatenate([jnp.where(head0, x, 0.0), jnp.where(head0, 0.0, x)], axis=0)


def _block_diag(x, left):
    return jnp.concatenate([jnp.where(left, x, 0.0), jnp.where(left, 0.0, x)], axis=0)


def _units_chunk(states, ops, m):
    c = RW_CHUNK
    units = range(len(ops))
    bf = lambda t: t.astype(BF16)
    head0, left = m["head0"], m["left"]
    ar = [bf(jnp.concatenate([o["a_t"], o["r_t"]], axis=0)) for o in ops]
    rstack = [bf(jnp.concatenate([_stack_heads(o["b_t"], head0), _stack_heads(o["k_t"], head0)], axis=0))
              for o in ops]
    acat = [_dot_nt(ar[i], rstack[i]) for i in units]
    sa = [_dot_nt(ar[i], bf(states[i])) for i in units]
    a_ab = [jnp.where(m["strict2"], acat[i][:c, :2 * c], 0.0) for i in units]
    a_ak = [bf(jnp.where(m["strict2"], acat[i][:c, 2 * c:], 0.0)) for i in units]
    a_r = [bf(jnp.concatenate([jnp.where(m["incl2"], acat[i][c:, :2 * c], 0.0),
                               jnp.where(m["incl2"], acat[i][c:, 2 * c:], 0.0)], axis=1)) for i in units]
    p = [m["eye2"] + a_ab[i] for i in units]
    lk = [_dot(bf(a_ab[i]), bf(_block_diag(a_ab[i], left))) for i in units]
    vstack = [bf(_stack_heads(o["v"], head0)) for o in ops]
    rhs = [sa[i][:c] + _dot(a_ak[i], vstack[i]) for i in units]
    power = 2
    while 2 * power < c:
        res = [_dot(bf(jnp.concatenate([p[i], lk[i]], axis=0)), bf(_block_diag(lk[i], left))) for i in units]
        p = [p[i] + res[i][:c] for i in units]
        lk = [res[i][c:] for i in units]
        power *= 2
    p = [p[i] + _dot(bf(p[i]), bf(_block_diag(lk[i], left))) for i in units]
    u = [_dot(bf(p[i]), bf(_stack_heads(rhs[i], head0))) for i in units]
    ys = [sa[i][c:] + _dot(a_r[i], jnp.concatenate([bf(_stack_heads(u[i], head0)), vstack[i]], axis=0))
          for i in units]
    upd = [_dot(bf(jnp.concatenate([u[i], ops[i]["v"]], axis=0).T),
                bf(jnp.concatenate([ops[i]["b_h"], ops[i]["k_h"]], axis=0))) for i in units]
    new_states = [states[i] * ops[i]["w_tot"] + jnp.where(m["same_head"], upd[i], 0.0) for i in units]
    return new_states, ys


def _rwkv_body(reverse, bt, *refs):
    if reverse:
        (zc_ref, yf_ref, lw_hi_ref, lw_lo_ref, w0_ref, a0_ref, a0o_ref, kk_ref, ka_ref,
         rk_ref, bd_ref, gng_ref, gnb_ref, out_ref, state_ref) = refs
    else:
        (zc_ref, lw_hi_ref, lw_lo_ref, w0_ref, a0_ref, kk_ref, ka_ref,
         bd_ref, out_ref, state_ref) = refs
    c = RW_CHUNK

    @pl.when(pl.program_id(1) == 0)
    def _():
        state_ref[...] = jnp.zeros_like(state_ref)

    rows = bt * c
    m = _rwkv_masks(reverse)
    lora_lane = lax.broadcasted_iota(jnp.int32, (rows, LORA_PAD), 1)
    bd = bd_ref[...]
    zf = zc_ref[...].reshape(rows, Z_COLS)
    r = zf[:, :RW_WIDTH]
    k = zf[:, RW_WIDTH:2 * RW_WIDTH]
    v = zf[:, 2 * RW_WIDTH:3 * RW_WIDTH]
    lo = zf[:, 3 * RW_WIDTH:]
    act = jnp.where(lora_lane < DECAY_LORA, jnp.tanh(lo),
                    jnp.where(lora_lane < DECAY_LORA + ICLR_LORA, lo, _sigmoid(lo)))
    proj = _dot3(act, lw_hi_ref[...], lw_lo_ref[...])
    w_in = w0_ref[...] + proj[:, :RW_WIDTH]
    logw = -DECAY_SCALE * _sigmoid(w_in)
    a_sig = _sigmoid(a0_ref[...] + proj[:, RW_WIDTH:2 * RW_WIDTH])
    kk = k * kk_ref[...]
    kk = kk * lax.rsqrt(jnp.maximum(_head_sums(kk * kk, bd), KK_NORM_FLOOR_SQ))
    kd = k * (1.0 + (a_sig - 1.0) * ka_ref[...])
    a_vec = -kk
    b_vec = kk * a_sig
    ri = lax.broadcasted_iota(jnp.int32, (rows, rows), 0)
    rj = lax.broadcasted_iota(jnp.int32, (rows, rows), 1)
    ordered = (rj >= ri) if reverse else (rj <= ri)
    tri = jnp.where((ri // c) == (rj // c), jnp.where(ordered, 1.0, 0.0), 0.0).astype(BF16)
    cum = _dot2_lhs01(tri, logw)
    edge = 0 if reverse else c - 1
    tot = jnp.concatenate([jnp.broadcast_to(cum[b * c + edge:b * c + edge + 1, :], (c, RW_WIDTH))
                           for b in range(bt)], axis=0)
    w_inv = jnp.exp(-cum)
    w_end = jnp.exp(tot - cum)
    w_tot = jnp.exp(tot)
    full_ops = dict(r_t=r * jnp.exp(cum), a_t=a_vec * jnp.exp(cum - logw), b_t=b_vec * w_inv, k_t=kd * w_inv,
                    b_h=b_vec * w_end, k_h=kd * w_end, v=v)
    ops, states = [], []
    for b in range(bt):
        for pi in range(N_PAIRS):
            ln = slice(pi * PAIR, (pi + 1) * PAIR)
            unit = {name: t[b * c:(b + 1) * c, ln] for name, t in full_ops.items()}
            unit["w_tot"] = w_tot[b * c:b * c + 1, ln]
            ops.append(unit)
            states.append(state_ref[b, pi])
    new_states, ys = _units_chunk(states, ops, m)
    for b in range(bt):
        for pi in range(N_PAIRS):
            state_ref[b, pi] = new_states[b * N_PAIRS + pi]
    y = jnp.concatenate([jnp.concatenate(ys[b * N_PAIRS:(b + 1) * N_PAIRS], axis=1) for b in range(bt)], axis=0)
    if not reverse:
        out_ref[...] = y.reshape(bt, c, RW_WIDTH)
    else:
        y = y + yf_ref[...].reshape(rows, RW_WIDTH)
        a_other = _sigmoid(a0o_ref[...] + proj[:, 2 * RW_WIDTH:3 * RW_WIDTH])
        kd_sum = kd + k * (1.0 + (a_other - 1.0) * ka_ref[...])
        bonus = _head_sums(r * kd_sum * rk_ref[...], bd) * v
        inv_n = 1.0 / RW_HEAD_DIM
        mean = _head_sums(y, bd) * inv_n
        yc = y - mean
        var = _head_sums(yc * yc, bd) * inv_n
        yn = yc * lax.rsqrt(var + GN_EPS) * gng_ref[...] + gnb_ref[...]
        g = proj[:, 3 * RW_WIDTH:]
        out_ref[...] = ((yn + bonus) * g).astype(BF16).reshape(bt, c, RW_WIDTH)


def _rwkv_pass(z3, yf, p, reverse, bt):
    b_sz, t_len, _ = z3.shape
    c = RW_CHUNK
    n_chunks = t_len // c

    def cidx(ci):
        return (n_chunks - 1 - ci) if reverse else ci

    full = lambda shape: pl.BlockSpec(shape, lambda bi, ci: (0,) * len(shape))
    row512 = full((1, RW_WIDTH))
    d = "rev" if reverse else "fwd"
    n_proj = p["lora_hi_" + d].shape[1]
    in_specs = [pl.BlockSpec((bt, c, Z_COLS), lambda bi, ci: (bi, cidx(ci), 0))]
    args = [z3]
    if reverse:
        in_specs.append(pl.BlockSpec((bt, c, RW_WIDTH), lambda bi, ci: (bi, cidx(ci), 0)))
        args.append(yf)
    in_specs += [full((LORA_PAD, n_proj)), full((LORA_PAD, n_proj)), row512, row512]
    args += [p["lora_hi_" + d], p["lora_lo_" + d], p["w0_" + d], p["a0_" + d]]
    if reverse:
        in_specs.append(row512)
        args.append(p["a0_fwd"])
    in_specs += [row512, row512]
    args += [p["rw_k_k"], p["rw_k_a"]]
    if reverse:
        in_specs.append(row512)
        args.append(p["rw_r_k"])
    in_specs.append(full((PAIR, PAIR)))
    args.append(p["head_ones"])
    if reverse:
        in_specs += [row512, row512]
        args += [p["rw_gn_g"], p["rw_gn_b"]]
    return pl.pallas_call(
        functools.partial(_rwkv_body, reverse, bt),
        grid=(b_sz // bt, n_chunks),
        in_specs=in_specs,
        out_specs=pl.BlockSpec((bt, c, RW_WIDTH), lambda bi, ci: (bi, cidx(ci), 0)),
        out_shape=jax.ShapeDtypeStruct((b_sz, t_len, RW_WIDTH), BF16 if reverse else F32),
        scratch_shapes=[pltpu.VMEM((bt, N_PAIRS, PAIR, PAIR), F32)],
        compiler_params=pltpu.CompilerParams(vmem_limit_bytes=VMEM_LIMIT),
        name="rwkv_" + d,
    )(*args)


def _mix_out_body(x_ref, gm_ref, rw_ref, wo_ref, g2_ref, rw_hi_ref, rw_lo_ref, rb_ref, earlier_ref,
                  x1_ref, h2_ref, route_ref, route_t_ref, counts_ref, carry_ref):
    i = pl.program_id(0)

    @pl.when(i == 0)
    def _():
        carry_ref[...] = jnp.zeros_like(carry_ref)

    sub = earlier_ref.shape[0]
    subs = range(x_ref.shape[0] // sub)
    rows = [slice(s * sub, (s + 1) * sub) for s in subs]
    x1 = [x_ref[r, :] + _dot(gm_ref[r, :], wo_ref[:GM_WIDTH, :]) + _dot(rw_ref[r, :], wo_ref[GM_WIDTH:, :])
          for r in rows]
    h2 = [x1[s] * lax.rsqrt(jnp.mean(x1[s] * x1[s], axis=-1, keepdims=True) + RMS_EPS) * g2_ref[...] for s in subs]
    for s in subs:
        x1_ref[rows[s], :] = x1[s]
        h2_ref[rows[s], :] = _pack_bf16_halves(h2[s])
    split = [_split(h) for h in h2]
    logits = [_dot_nt(rw_hi_ref[...], hi) + _dot_nt(rw_hi_ref[...], lo) + _dot_nt(rw_lo_ref[...], hi) + rb_ref[...]
              for hi, lo in split]
    expert = lax.broadcasted_iota(jnp.int32, (N_EXPERTS, sub), 0).astype(F32)
    work = logits
    vals, ids, hits = [[] for _ in subs], [[] for _ in subs], [[] for _ in subs]
    onehot = [jnp.zeros((N_EXPERTS, sub), F32) for _ in subs]
    for _ in range(TOP_K):
        for s in subs:
            mx = jnp.max(work[s], axis=0, keepdims=True)
            idx = jnp.min(jnp.where(work[s] == mx, expert, float(N_EXPERTS)), axis=0, keepdims=True)
            hit = expert == idx
            vals[s].append(mx)
            ids[s].append(idx)
            hits[s].append(hit)
            onehot[s] = jnp.where(hit, 1.0, onehot[s])
        work = [jnp.where(hits[s][-1], -jnp.inf, work[s]) for s in subs]
    earlier_counts = [_dot(onehot[s].astype(BF16), earlier_ref[...]) for s in subs]
    carry = carry_ref[:, 0:1]
    for s in subs:
        exps = [jnp.exp(vk - vals[s][0]) for vk in vals[s]]
        denom = exps[0] + exps[1] + exps[2] + exps[3]
        ranks = carry + earlier_counts[s]
        rank_rows = [jnp.sum(jnp.where(hit, ranks, 0.0), axis=0, keepdims=True) for hit in hits[s]]
        route_t = jnp.concatenate(ids[s] + [e / denom for e in exps] + rank_rows
                                  + [jnp.zeros((ROUTE_ROWS - 3 * TOP_K, sub), F32)], axis=0)
        route_t_ref[:, rows[s]] = route_t
        route_ref[rows[s], :] = jnp.concatenate([route_t, jnp.zeros((LANES - ROUTE_ROWS, sub), F32)], axis=0).T
        carry = carry + jnp.sum(onehot[s], axis=1, keepdims=True)
    carry_ref[...] = jnp.broadcast_to(carry, carry_ref.shape)
    counts_ref[...] = jnp.broadcast_to(carry, counts_ref.shape)


def _mix_out(x2d, gm, rw, p, tm):
    n = x2d.shape[0]
    full = lambda shape: pl.BlockSpec(shape, lambda i: (0,) * len(shape))
    tile = lambda w: pl.BlockSpec((tm, w), lambda i: (i, 0))
    sub = min(MIX_OUT_SUB, tm)
    earlier = (jnp.arange(sub)[:, None] < jnp.arange(sub)[None, :]).astype(BF16)
    return pl.pallas_call(
        _mix_out_body,
        grid=(n // tm,),
        in_specs=[tile(D_MODEL), tile(GM_WIDTH), tile(RW_WIDTH), full((D_MODEL, D_MODEL)), full((1, D_MODEL)),
                  full((N_EXPERTS, D_MODEL)), full((N_EXPERTS, D_MODEL)), full((N_EXPERTS, 1)), full((sub, sub))],
        out_specs=[tile(D_MODEL), tile(D_MODEL // 2), tile(LANES), pl.BlockSpec((ROUTE_ROWS, tm), lambda i: (0, i)),
                   full((N_EXPERTS, LANES))],
        out_shape=[
            jax.ShapeDtypeStruct((n, D_MODEL), F32),
            jax.ShapeDtypeStruct((n, D_MODEL // 2), jnp.int32),
            jax.ShapeDtypeStruct((n, LANES), F32),
            jax.ShapeDtypeStruct((ROUTE_ROWS, n), F32),
            jax.ShapeDtypeStruct((N_EXPERTS, LANES), F32),
        ],
        scratch_shapes=[pltpu.VMEM((N_EXPERTS, LANES), F32)],
        compiler_params=pltpu.CompilerParams(vmem_limit_bytes=VMEM_LIMIT),
        name="mix_out",
    )(x2d, gm, rw, p["w_out"], p["norm2_g"], p["router_hi"], p["router_lo"], p["router_b"], earlier)


def _sc_gather_rows(table, idx):
    n_idx = idx.shape[0]
    width = table.shape[1]
    n_workers = SC_CORES * SC_SUBCORES
    per_worker = n_idx // n_workers
    n_windows = per_worker // SC_WINDOW
    assert per_worker * n_workers == n_idx and n_windows * SC_WINDOW == per_worker
    mesh = plsc.VectorSubcoreMesh(core_axis_name="c", subcore_axis_name="s")

    @functools.partial(
        pl.kernel, mesh=mesh,
        out_type=jax.ShapeDtypeStruct((n_idx, width), table.dtype),
        scratch_types=[
            pltpu.VMEM((SC_WINDOW,), jnp.int32),
            pltpu.VMEM((SC_WINDOW, width), table.dtype),
            pltpu.SemaphoreType.DMA,
        ],
        name="sc_gather_rows",
    )
    def gather(table_hbm, idx_hbm, out_hbm, idx_v, rows_v, sem):
        worker = lax.axis_index("s") * SC_CORES + lax.axis_index("c")
        base = worker * per_worker

        @pl.loop(0, n_windows)
        def _(j):
            off = base + j * SC_WINDOW
            pltpu.sync_copy(idx_hbm.at[pl.ds(off, SC_WINDOW)], idx_v)
            pltpu.async_copy(table_hbm.at[idx_v], rows_v, sem).wait()
            pltpu.sync_copy(rows_v, out_hbm.at[pl.ds(off, SC_WINDOW)])

    return gather(table, idx)


def _sc_scatter_rows(src, dest_w, n_out):
    n, width = src.shape
    n_windows, top_k, window = dest_w.shape
    n_workers = SC_CORES * SC_SUBCORES
    per_worker = n_windows // n_workers
    assert n_windows * window == n and per_worker * n_workers == n_windows
    mesh = plsc.VectorSubcoreMesh(core_axis_name="c", subcore_axis_name="s")

    @functools.partial(
        pl.kernel, mesh=mesh,
        out_type=jax.ShapeDtypeStruct((n_out, width), src.dtype),
        scratch_types=[
            pltpu.VMEM((top_k, window), jnp.int32),
            pltpu.VMEM((window, width), src.dtype),
        ],
        name="sc_scatter_rows",
    )
    def scatter(src_hbm, dest_hbm, out_hbm, idx_v, rows_v):
        worker = lax.axis_index("s") * SC_CORES + lax.axis_index("c")

        @pl.loop(0, per_worker)
        def _(j):
            g = worker * per_worker + j
            pltpu.sync_copy(dest_hbm.at[g], idx_v)
            pltpu.sync_copy(src_hbm.at[pl.ds(g * window, window)], rows_v)
            for k in range(top_k):
                pltpu.sync_copy(rows_v, out_hbm.at[idx_v.at[k]])

    return scatter(src, dest_w)


def _weight_copies(wgu_hbm, wd_hbm, wgu_buf, wd_buf, sem, expert, slot):
    return (pltpu.make_async_copy(wgu_hbm.at[expert], wgu_buf.at[slot], sem.at[0, slot]),
            pltpu.make_async_copy(wd_hbm.at[expert], wd_buf.at[slot], sem.at[1, slot]))


def _experts_body(blk_e_ref, n_used_ref, valid_ref, slot_ref, next_e_ref, x_ref, wgu_hbm, bgu_ref, wd_hbm, bd_ref,
                  out_ref, wgu_bf, wd_bf, wgu_buf, wd_buf, sem):
    i = pl.program_id(0)
    n_used = n_used_ref[0]
    expert = blk_e_ref[i]
    slot = slot_ref[i]
    copies = functools.partial(_weight_copies, wgu_hbm, wd_hbm, wgu_buf, wd_buf, sem)

    @pl.when(i == 0)
    def _():
        for cp in copies(expert, slot):
            cp.start()

    @pl.when((i < n_used) & ((i == 0) | (expert != blk_e_ref[jnp.maximum(i - 1, 0)])))
    def _():
        for cp in copies(expert, slot):
            cp.wait()
        wgu_bf[...] = wgu_buf[slot].astype(BF16)
        wd_bf[...] = wd_buf[slot].astype(BF16)
        following = next_e_ref[i]

        @pl.when(following >= 0)
        def _():
            for cp in copies(following, 1 - slot):
                cp.start()

    valid = jnp.where(i < n_used, valid_ref[i], 0)

    def expert_mlp(n_rows):
        row = lax.broadcasted_iota(jnp.int32, (n_rows, 1), 0)
        x_left, x_right = _unpack_bf16_halves(jnp.where(row < valid, x_ref[:n_rows, :], 0))
        half = D_MODEL // 2
        gu = (_dot(x_left.astype(BF16), wgu_bf[:half, :]) + _dot(x_right.astype(BF16), wgu_bf[half:, :])
              + bgu_ref[0])
        gate = jnp.minimum(gu[:, :D_EXPERT], SWIGLU_LIMIT)
        up = jnp.clip(gu[:, D_EXPERT:], -SWIGLU_LIMIT, SWIGLU_LIMIT)
        act = gate * _sigmoid(gate * SWIGLU_ALPHA) * (up + 1.0)
        out_ref[:n_rows, :] = _pack_bf16_halves(_dot(act.astype(BF16), wd_bf[...]) + bd_ref[0])

    half_rows = MOE_ROWS // 2

    @pl.when(valid > half_rows)
    def _():
        expert_mlp(MOE_ROWS)

    @pl.when((valid > 0) & (valid <= half_rows))
    def _():
        expert_mlp(half_rows)
        out_ref[half_rows:, :] = jnp.zeros((MOE_ROWS - half_rows, D_MODEL // 2), jnp.int32)

    @pl.when(valid == 0)
    def _():
        out_ref[...] = jnp.zeros_like(out_ref)


def _experts(x_rows, blk_e, n_used, valid, slot, next_e, p):
    n_blocks = blk_e.shape[0]
    grid_spec = pltpu.PrefetchScalarGridSpec(
        num_scalar_prefetch=5,
        grid=(n_blocks,),
        in_specs=[
            pl.BlockSpec((MOE_ROWS, D_MODEL // 2), lambda i, be, nu, *_: (jnp.minimum(i, nu[0] - 1), 0)),
            pl.BlockSpec(memory_space=pl.ANY),
            pl.BlockSpec((1, 1, 2 * D_EXPERT), lambda i, be, *_: (be[i], 0, 0)),
            pl.BlockSpec(memory_space=pl.ANY),
            pl.BlockSpec((1, 1, D_MODEL), lambda i, be, *_: (be[i], 0, 0)),
        ],
        out_specs=pl.BlockSpec((MOE_ROWS, D_MODEL // 2), lambda i, *_: (i, 0)),
        scratch_shapes=[
            pltpu.VMEM((D_MODEL, 2 * D_EXPERT), BF16),
            pltpu.VMEM((D_EXPERT, D_MODEL), BF16),
            pltpu.VMEM((2, D_MODEL, 2 * D_EXPERT), F32),
            pltpu.VMEM((2, D_EXPERT, D_MODEL), F32),
            pltpu.SemaphoreType.DMA((2, 2)),
        ],
    )
    return pl.pallas_call(
        _experts_body,
        grid_spec=grid_spec,
        out_shape=jax.ShapeDtypeStruct((n_blocks * MOE_ROWS, D_MODEL // 2), jnp.int32),
        compiler_params=pltpu.CompilerParams(vmem_limit_bytes=VMEM_LIMIT),
        name="experts",
    )(blk_e, n_used, valid, slot, next_e, x_rows, p["w_gu"], p["b_gu"], p["w_down"], p["b_down"])


def _combine_body(yg_ref, x1_ref, gates_ref, fg_ref, out_ref):
    gates = gates_ref[...]
    half = D_MODEL // 2
    acc_left = x1_ref[:, :half]
    acc_right = x1_ref[:, half:]
    for kk in range(TOP_K):
        y_left, y_right = _unpack_bf16_halves(yg_ref[kk])
        gate = gates[:, TOP_K + kk:TOP_K + kk + 1]
        acc_left = acc_left + gate * y_left
        acc_right = acc_right + gate * y_right
    acc = jnp.concatenate([acc_left, acc_right], axis=1)
    out_ref[...] = acc * lax.rsqrt(jnp.mean(acc * acc, axis=-1, keepdims=True) + RMS_EPS) * fg_ref[...]


def _combine(yg, x1, gates, final_g, tm):
    n = x1.shape[0]
    return pl.pallas_call(
        _combine_body,
        grid=(n // tm,),
        in_specs=[
            pl.BlockSpec((TOP_K, tm, D_MODEL // 2), lambda i: (0, i, 0)),
            pl.BlockSpec((tm, D_MODEL), lambda i: (i, 0)),
            pl.BlockSpec((tm, LANES), lambda i: (i, 0)),
            pl.BlockSpec((1, D_MODEL), lambda i: (0, 0)),
        ],
        out_specs=pl.BlockSpec((tm, D_MODEL), lambda i: (i, 0)),
        out_shape=jax.ShapeDtypeStruct((n, D_MODEL), F32),
        compiler_params=pltpu.CompilerParams(vmem_limit_bytes=VMEM_LIMIT),
        name="combine",
    )(yg, x1, gates, final_g)


def _prepare(norm1_g, w_in, rw_mu, gm_ln_g, gm_ln_b, gm_ws, gm_bs, rw_w0, rw_w2, rw_a0, rw_a2, rw_g2,
             rw_k_k, rw_k_a, rw_r_k, rw_gn_g, rw_gn_b, w_out, norm2_g, router_w, router_b,
             w_gu, b_gu, w_down, b_down, final_g):
    l = 0
    row = lambda t: t.reshape(1, -1).astype(F32)
    p = {}
    p["norm1_g"] = row(norm1_g[l])
    w = w_in[l]
    p["w_gm"] = w[:, :2 * GM_WIDTH].astype(BF16)
    pad_cols = LORA_PAD - LORA_COLS
    p["w_z"] = jnp.pad(w[:, 2 * GM_WIDTH:], ((0, 0), (0, pad_cols))).astype(BF16)
    mu = jnp.pad(row(rw_mu[l]), ((0, 0), (0, pad_cols)))
    p["rw_mu"] = jnp.concatenate([1.0 - mu, 0.5 * mu], axis=0)
    p["gm_ln_g"] = row(gm_ln_g[l])
    p["gm_ln_b"] = row(gm_ln_b[l])
    p["gm_ws"] = gm_ws[l].astype(BF16)
    p["gm_bs"] = jnp.broadcast_to(gm_bs[l][:, :, None], (GM_HEADS, GM_CHUNK, GM_HEAD_DIM)).astype(F32)
    zeros = lambda r, c: jnp.zeros((r, c), F32)
    o_a = DECAY_LORA
    o_g = DECAY_LORA + ICLR_LORA

    def lora_matrix(d, with_epilogue):
        blocks = [jnp.concatenate([rw_w2[l, d], zeros(LORA_PAD - DECAY_LORA, RW_WIDTH)], axis=0),
                  jnp.concatenate([zeros(o_a, RW_WIDTH), rw_a2[l, d], zeros(LORA_PAD - o_g, RW_WIDTH)], axis=0)]
        if with_epilogue:
            blocks.append(jnp.concatenate([zeros(o_a, RW_WIDTH), rw_a2[l, 1 - d], zeros(LORA_PAD - o_g, RW_WIDTH)],
                                          axis=0))
            blocks.append(jnp.concatenate([zeros(o_g, RW_WIDTH), rw_g2[l], zeros(LORA_PAD - LORA_COLS, RW_WIDTH)],
                                          axis=0))
        return jnp.concatenate(blocks, axis=1)

    for d, name in ((0, "fwd"), (1, "rev")):
        mat = lora_matrix(d, with_epilogue=(d == 1))
        hi = mat.astype(BF16)
        p["lora_hi_" + name] = hi
        p["lora_lo_" + name] = (mat - hi.astype(F32)).astype(BF16)
        p["w0_" + name] = row(rw_w0[l, d])
        p["a0_" + name] = row(rw_a0[l, d])
    p["rw_k_k"] = row(rw_k_k[l])
    p["rw_k_a"] = row(rw_k_a[l])
    p["rw_r_k"] = row(rw_r_k[l])
    p["rw_gn_g"] = row(rw_gn_g[l])
    p["rw_gn_b"] = row(rw_gn_b[l])
    ch = jnp.arange(PAIR) // RW_HEAD_DIM
    p["head_ones"] = (ch[:, None] == ch[None, :]).astype(BF16)
    p["w_out"] = w_out[l].astype(BF16)
    p["norm2_g"] = row(norm2_g[l])
    rw_t = router_w[l].astype(F32).T
    hi = rw_t.astype(BF16)
    p["router_hi"] = hi
    p["router_lo"] = (rw_t - hi.astype(F32)).astype(BF16)
    p["router_b"] = router_b[l].astype(F32).reshape(N_EXPERTS, 1)
    p["w_gu"] = w_gu[l]
    p["b_gu"] = b_gu[l].reshape(N_EXPERTS, 1, 2 * D_EXPERT).astype(F32)
    p["w_down"] = w_down[l]
    p["b_down"] = b_down[l].reshape(N_EXPERTS, 1, D_MODEL).astype(F32)
    p["final_g"] = row(final_g)
    return p


def _pick_tile(n, want):
    t = want
    while n % t:
        t //= 2
    return t


def _encoder(x, p):
    b_sz, t_len, d = x.shape
    n = b_sz * t_len
    x2d = x.reshape(n, d)
    gm, z = _mix_in(x2d, p, _pick_tile(t_len, 1024), t_len)
    z3 = z.reshape(b_sz, t_len, Z_COLS)
    yf = _rwkv_pass(z3, None, p, reverse=False, bt=RW_BATCH_TILE)
    rw = _rwkv_pass(z3, yf, p, reverse=True, bt=RW_BATCH_TILE)
    tmo = _pick_tile(n, 1024)
    x1, h2, route, route_t, counts = _mix_out(x2d, gm, rw.reshape(n, RW_WIDTH), p, tmo)
    ids = route_t[:TOP_K].astype(jnp.int32)
    ranks = route_t[2 * TOP_K:3 * TOP_K].astype(jnp.int32)
    cnt = counts[:, 0].astype(jnp.int32)
    padded = (cnt + MOE_ROWS - 1) // MOE_ROWS * MOE_ROWS
    pad_end = jnp.cumsum(padded)
    pad_start = pad_end - padded
    expert_ids = jnp.arange(N_EXPERTS, dtype=jnp.int32)
    start_of = jnp.sum(jnp.where(ids[..., None] == expert_ids, pad_start, 0), axis=-1)
    dest = start_of + ranks
    n_blocks = n * TOP_K // MOE_ROWS + N_EXPERTS
    blk_start = jnp.arange(n_blocks, dtype=jnp.int32) * MOE_ROWS
    blk_e = jnp.minimum(jnp.sum((pad_end[None, :] <= blk_start[:, None]).astype(jnp.int32), axis=1), N_EXPERTS - 1)
    n_used = (pad_end[-1] // MOE_ROWS).astype(jnp.int32).reshape(1)
    blk_onehot = blk_e[:, None] == expert_ids
    blk_cnt = jnp.sum(jnp.where(blk_onehot, cnt, 0), axis=-1)
    blk_first = jnp.sum(jnp.where(blk_onehot, pad_start, 0), axis=-1)
    valid = jnp.clip(blk_cnt - (blk_start - blk_first), 0, MOE_ROWS).astype(jnp.int32)
    window = min(SC_SCATTER_WINDOW, n // (SC_CORES * SC_SUBCORES))
    dest_w = dest.reshape(TOP_K, n // window, window).transpose(1, 0, 2)
    x_rows = _sc_scatter_rows(h2, dest_w, n_blocks * MOE_ROWS)
    present = cnt > 0
    slot_of_expert = (jnp.cumsum(present.astype(jnp.int32)) - 1) % 2
    later = present[None, :] & (expert_ids[None, :] > expert_ids[:, None])
    next_present = jnp.min(jnp.where(later, expert_ids[None, :], N_EXPERTS), axis=1)
    next_present = jnp.where(next_present < N_EXPERTS, next_present, -1)
    slot = jnp.sum(jnp.where(blk_onehot, slot_of_expert, 0), axis=-1).astype(jnp.int32)
    next_e = jnp.sum(jnp.where(blk_onehot, next_present, 0), axis=-1).astype(jnp.int32)
    y_rows = _experts(x_rows, blk_e, n_used, valid, slot, next_e, p)
    yg = _sc_gather_rows(y_rows, dest.reshape(-1)).reshape(TOP_K, n, D_MODEL // 2)
    out = _combine(yg, x1, route, p["final_g"], _pick_tile(n, 1024))
    return out.reshape(b_sz, t_len, d)


def kernel(x_prompt, x_sample, norm1_g, w_in, rw_mu, gm_ln_g, gm_ln_b, gm_ws, gm_bs, rw_w0, rw_w2, rw_a0, rw_a2,
           rw_g2, rw_k_k, rw_k_a, rw_r_k, rw_gn_g, rw_gn_b, w_out, norm2_g, router_w, router_b, w_gu, b_gu,
           w_down, b_down, final_g):
    p = _prepare(norm1_g, w_in, rw_mu, gm_ln_g, gm_ln_b, gm_ws, gm_bs, rw_w0, rw_w2, rw_a0, rw_a2, rw_g2,
                 rw_k_k, rw_k_a, rw_r_k, rw_gn_g, rw_gn_b, w_out, norm2_g, router_w, router_b,
                 w_gu, b_gu, w_down, b_down, final_g)
    return (_encoder(x_prompt, p), _encoder(x_sample, p))
```

```python
import functools

import jax
import jax.numpy as jnp
from jax import lax
from jax.experimental import pallas as pl
from jax.experimental.pallas import tpu as pltpu
from jax.experimental.pallas import tpu_sc as plsc

F32 = jnp.float32
BF16 = jnp.bfloat16

D_MODEL = 1024
GM_WIDTH = 512
RW_WIDTH = 512
GM_HEADS = 4
GM_HEAD_DIM = 128
GM_CHUNK = 128
RW_HEAD_DIM = 64
DECAY_LORA = 32
ICLR_LORA = 32
GATE_LORA = 96
LORA_COLS = DECAY_LORA + ICLR_LORA + GATE_LORA
LORA_PAD = 256
Z_COLS = 3 * RW_WIDTH + LORA_PAD
Z_CHUNK = 256
N_EXPERTS = 32
TOP_K = 4
D_EXPERT = 1024
SWIGLU_LIMIT = 7.0
SWIGLU_ALPHA = 1.702
RMS_EPS = 1e-5
LN_EPS = 1e-5
GN_EPS = 64e-5

LANES = 128
SUBLANES = 8
VMEM_LIMIT = 56 * 1024 * 1024

RW_CHUNK = 64
PAIR = 2 * RW_HEAD_DIM
N_PAIRS = RW_WIDTH // PAIR
RW_BATCH_TILE = 4
MIX_OUT_SUB = 128
MOE_ROWS = 512
SC_CORES = 2
SC_SUBCORES = 16
SC_WINDOW = 64
SC_SCATTER_WINDOW = 128
ROUTE_ROWS = 16
DECAY_SCALE = 0.6065306597126334
KK_NORM_FLOOR_SQ = 1e-24


def _dot(a, b):
    return jnp.dot(a, b, preferred_element_type=F32)


def _dot_nt(a, b):
    return lax.dot_general(a, b, (((1,), (1,)), ((), ())), preferred_element_type=F32)


def _split(x):
    hi = x.astype(BF16)
    lo = (x - hi.astype(F32)).astype(BF16)
    return hi, lo


def _dot2(a, b01):
    a_hi, a_lo = _split(a)
    return _dot(a_hi, b01) + _dot(a_lo, b01)


def _head_sums(x, pair_ones):
    rows = x.shape[0]
    n_tiles = x.shape[1] // LANES
    stacked = jnp.concatenate([x[:, t * LANES:(t + 1) * LANES] for t in range(n_tiles)], axis=0)
    sums = _dot2(stacked, pair_ones)
    return jnp.concatenate([sums[t * rows:(t + 1) * rows] for t in range(n_tiles)], axis=1)


def _dot2_lhs01(a01, b):
    b_hi, b_lo = _split(b)
    return _dot(a01, b_hi) + _dot(a01, b_lo)


def _pack_bf16_halves(x):
    w = x.shape[1] // 2
    hi = lax.bitcast_convert_type(x[:, :w].astype(BF16).astype(F32), jnp.int32)
    lo = lax.bitcast_convert_type(x[:, w:].astype(BF16).astype(F32), jnp.int32)
    return hi | lax.shift_right_logical(lo, 16)


def _unpack_bf16_halves(words):
    left = lax.bitcast_convert_type(words & jnp.int32(-65536), F32)
    right = lax.bitcast_convert_type(lax.shift_left(words, 16), F32)
    return left, right


def _gelu_tanh(x):
    return 0.5 * x * (1.0 + jnp.tanh(0.7978845608028654 * (x + 0.044715 * (x * x * x))))


def _sigmoid(x):
    return 0.5 * jnp.tanh(0.5 * x) + 0.5


def _mix_in_body(tiles_per_seq, x_ref, xp_ref, xn_ref, g1_ref, wgm_ref, wz_ref, mu_ref, lng_ref, lnb_ref, ws_ref,
                 bs_ref, gm_ref, z_ref):
    i = pl.program_id(0)
    tm = x_ref.shape[0]

    def normed(x):
        return (x * lax.rsqrt(jnp.mean(x * x, axis=-1, keepdims=True) + RMS_EPS) * g1_ref[...]).astype(BF16)

    hb = normed(x_ref[...])
    uv = _dot(hb, wgm_ref[...])
    hb_halo = normed(jnp.concatenate([xp_ref[...], xn_ref[...]], axis=0))
    has_prev = i % tiles_per_seq > 0
    has_next = i % tiles_per_seq < tiles_per_seq - 1
    row8 = lax.broadcasted_iota(jnp.int32, (SUBLANES, Z_CHUNK), 0)
    for j in range(Z_COLS // Z_CHUNK):
        cols = slice(j * Z_CHUNK, (j + 1) * Z_CHUNK)
        z = _dot(hb, wz_ref[:, cols])
        halo = _dot(hb_halo, wz_ref[:, cols])
        prev_row = jnp.where(has_prev, halo[SUBLANES - 1:SUBLANES, :], 0.0)
        next_row = jnp.where(has_next, halo[SUBLANES:SUBLANES + 1, :], 0.0)
        down = pltpu.roll(z, 1, axis=0)
        up = pltpu.roll(z, tm - 1, axis=0)
        prev = jnp.concatenate([jnp.where(row8 == 0, prev_row, down[:SUBLANES]), down[SUBLANES:]], axis=0)
        nxt = jnp.concatenate([up[:tm - SUBLANES], jnp.where(row8 == SUBLANES - 1, next_row, up[tm - SUBLANES:])],
                              axis=0)
        z_ref[:, cols] = z * mu_ref[0:1, cols] + (prev + nxt) * mu_ref[1:2, cols]
    u = _gelu_tanh(uv[:, :GM_WIDTH])
    v = _gelu_tanh(uv[:, GM_WIDTH:])
    mean = jnp.mean(v, axis=-1, keepdims=True)
    vc = v - mean
    var = jnp.mean(vc * vc, axis=-1, keepdims=True)
    vn = (vc * lax.rsqrt(var + LN_EPS) * lng_ref[...] + lnb_ref[...]).astype(BF16)
    for c in range(tm // GM_CHUNK):
        rows = slice(c * GM_CHUNK, (c + 1) * GM_CHUNK)
        for hd in range(GM_HEADS):
            cols = slice(hd * GM_HEAD_DIM, (hd + 1) * GM_HEAD_DIM)
            mixed = _dot(ws_ref[hd], vn[rows, cols]) + bs_ref[hd]
            gm_ref[rows, cols] = (u[rows, cols] * mixed).astype(BF16)


def _mix_in(x2d, p, tm, t_len):
    n = x2d.shape[0]
    assert t_len % tm == 0
    halo_per_tile = tm // SUBLANES
    n_halo = n // SUBLANES
    full = lambda shape: pl.BlockSpec(shape, lambda i: (0,) * len(shape))
    return pl.pallas_call(
        functools.partial(_mix_in_body, t_len // tm),
        grid=(n // tm,),
        in_specs=[
            pl.BlockSpec((tm, D_MODEL), lambda i: (i, 0)),
            pl.BlockSpec((SUBLANES, D_MODEL), lambda i: (jnp.maximum(i * halo_per_tile - 1, 0), 0)),
            pl.BlockSpec((SUBLANES, D_MODEL), lambda i: (jnp.minimum((i + 1) * halo_per_tile, n_halo - 1), 0)),
            full((1, D_MODEL)),
            full((D_MODEL, 2 * GM_WIDTH)),
            full((D_MODEL, Z_COLS)),
            full((2, Z_COLS)),
            full((1, GM_WIDTH)),
            full((1, GM_WIDTH)),
            full((GM_HEADS, GM_CHUNK, GM_CHUNK)),
            full((GM_HEADS, GM_CHUNK, GM_HEAD_DIM)),
        ],
        out_specs=[
            pl.BlockSpec((tm, GM_WIDTH), lambda i: (i, 0)),
            pl.BlockSpec((tm, Z_COLS), lambda i: (i, 0)),
        ],
        out_shape=[
            jax.ShapeDtypeStruct((n, GM_WIDTH), BF16),
            jax.ShapeDtypeStruct((n, Z_COLS), F32),
        ],
        compiler_params=pltpu.CompilerParams(vmem_limit_bytes=VMEM_LIMIT),
        name="mix_in",
    )(x2d, x2d, x2d, p["norm1_g"], p["w_gm"], p["w_z"], p["rw_mu"], p["gm_ln_g"], p["gm_ln_b"], p["gm_ws"],
      p["gm_bs"])


def _rwkv_masks(reverse):
    c = RW_CHUNK
    ti = lax.broadcasted_iota(jnp.int32, (c, c), 0)
    tj = lax.broadcasted_iota(jnp.int32, (c, c), 1)
    incl = (tj >= ti) if reverse else (tj <= ti)
    lane =lax.broadcasted_iota(jnp.int32, (c, PAIR), 1)
    head0 = lane < RW_HEAD_DIM
    lane2 = lax.broadcasted_iota(jnp.int32, (c, 2 * c), 1)
    left = lane2 < c
    ti2 = lax.broadcasted_iota(jnp.int32, (c, 2 * c), 0)
    tj2 = jnp.where(left, lane2, lane2 - c)
    incl2 = (tj2 >= ti2) if reverse else (tj2 <= ti2)
    strict2 = (tj2 > ti2) if reverse else (tj2 < ti2)
    eye2 = jnp.where(tj2 == ti2, 1.0, 0.0).astype(F32)
    si = lax.broadcasted_iota(jnp.int32, (PAIR, PAIR), 0)
    sj = lax.broadcasted_iota(jnp.int32, (PAIR, PAIR), 1)
    same_head = (si // RW_HEAD_DIM) == (sj // RW_HEAD_DIM)
    return dict(
        incl2=incl2, strict2=strict2, head0=head0, left=left, eye2=eye2, same_head=same_head,
        tri=jnp.where(incl, 1.0, 0.0).astype(BF16),
    )


def _stack_heads(x, head0):
    return jnp.concatenate([jnp.where(head0, x, 0.0), jnp.where(head0, 0.0, x)], axis=0)


def _block_diag(x, left):
    return jnp.concatenate([jnp.where(left, x, 0.0), jnp.where(left, 0.0, x)], axis=0)


def _units_chunk(states, ops, m):
    c = RW_CHUNK
    units = range(len(ops))
    bf = lambda t: t.astype(BF16)
    head0, left = m["head0"], m["left"]
    ar = [bf(jnp.concatenate([o["a_t"], o["r_t"]], axis=0)) for o in ops]
    rstack = [bf(jnp.concatenate([_stack_heads(o["b_t"], head0), _stack_heads(o["k_t"], head0)], axis=0))
              for o in ops]
    acat = [_dot_nt(ar[i], rstack[i]) for i in units]
    sa = [_dot_nt(ar[i], bf(states[i])) for i in units]
    a_ab = [jnp.where(m["strict2"], acat[i][:c, :2 * c], 0.0) for i in units]
    a_ak = [bf(jnp.where(m["strict2"], acat[i][:c, 2 * c:], 0.0)) for i in units]
    a_r = [bf(jnp.concatenate([jnp.where(m["incl2"], acat[i][c:, :2 * c], 0.0),
                               jnp.where(m["incl2"], acat[i][c:, 2 * c:], 0.0)], axis=1)) for i in units]
    p = [m["eye2"] + a_ab[i] for i in units]
    lk = [_dot(bf(a_ab[i]), bf(_block_diag(a_ab[i], left))) for i in units]
    vstack = [bf(_stack_heads(o["v"], head0)) for o in ops]
    rhs = [sa[i][:c] + _dot(a_ak[i], vstack[i]) for i in units]
    power = 2
    while 2 * power < c:
        res = [_dot(bf(jnp.concatenate([p[i], lk[i]], axis=0)), bf(_block_diag(lk[i], left))) for i in units]
        p = [p[i] + res[i][:c] for i in units]
        lk = [res[i][c:] for i in units]
        power *= 2
    p = [p[i] + _dot(bf(p[i]), bf(_block_diag(lk[i], left))) for i in units]
    u = [_dot(bf(p[i]), bf(_stack_heads(rhs[i], head0))) for i in units]
    ys = [sa[i][c:] + _dot(a_r[i], jnp.concatenate([bf(_stack_heads(u[i], head0)), vstack[i]], axis=0))
          for i in units]
    upd = [_dot(bf(jnp.concatenate([u[i], ops[i]["v"]], axis=0).T),
                bf(jnp.concatenate([ops[i]["b_h"], ops[i]["k_h"]], axis=0))) for i in units]
    new_states = [states[i] * ops[i]["w_tot"] + jnp.where(m["same_head"], upd[i], 0.0) for i in units]
    return new_states, ys


def _rwkv_body(reverse, bt, *refs):
    if reverse:
        (zc_ref, yf_ref, lw_ref, w0_ref, a0_ref, a0o_ref, kk_ref, ka_ref,
         rk_ref, bd_ref, gng_ref, gnb_ref, out_ref, state_ref) = refs
    else:
        (zc_ref, lw_ref, w0_ref, a0_ref, kk_ref, ka_ref,
         bd_ref, out_ref, state_ref) = refs
    c = RW_CHUNK

    @pl.when(pl.program_id(1) == 0)
    def _():
        state_ref[...] = jnp.zeros_like(state_ref)

    rows = bt * c
    m = _rwkv_masks(reverse)
    lora_lane = lax.broadcasted_iota(jnp.int32, (rows, LORA_PAD), 1)
    bd = bd_ref[...]
    zf = zc_ref[...].reshape(rows, Z_COLS)
    r = zf[:, :RW_WIDTH]
    k = zf[:, RW_WIDTH:2 * RW_WIDTH]
    v = zf[:, 2 * RW_WIDTH:3 * RW_WIDTH]
    lo = zf[:, 3 * RW_WIDTH:]
    act = jnp.where(lora_lane < DECAY_LORA, jnp.tanh(lo),
                    jnp.where(lora_lane < DECAY_LORA + ICLR_LORA, lo, _sigmoid(lo)))
    act_hi, act_lo = _split(act)
    w_in = w0_ref[...] + _dot(act_hi, lw_ref[:, :RW_WIDTH]) + _dot(act_lo, lw_ref[:, :RW_WIDTH])
    proj = _dot(act_hi, lw_ref[:, RW_WIDTH:])
    logw = -DECAY_SCALE * _sigmoid(w_in)
    a_sig = _sigmoid(a0_ref[...] + proj[:, :RW_WIDTH])
    kk = k * kk_ref[...]
    kk = kk * lax.rsqrt(jnp.maximum(_head_sums(kk * kk, bd), KK_NORM_FLOOR_SQ))
    kd = k * (1.0 + (a_sig - 1.0) * ka_ref[...])
    a_vec = -kk
    b_vec = kk * a_sig
    ri = lax.broadcasted_iota(jnp.int32, (rows, rows), 0)
    rj = lax.broadcasted_iota(jnp.int32, (rows, rows), 1)
    ordered = (rj >= ri) if reverse else (rj <= ri)
    tri = jnp.where((ri // c) == (rj // c), jnp.where(ordered, 1.0, 0.0), 0.0).astype(BF16)
    cum = _dot2_lhs01(tri, logw)
    edge = 0 if reverse else c - 1
    tot = jnp.concatenate([jnp.broadcast_to(cum[b * c + edge:b * c + edge + 1, :], (c, RW_WIDTH))
                           for b in range(bt)], axis=0)
    w_inv = jnp.exp(-cum)
    w_end = jnp.exp(tot - cum)
    w_tot = jnp.exp(tot)
    full_ops = dict(r_t=r * jnp.exp(cum), a_t=a_vec * jnp.exp(cum - logw), b_t=b_vec * w_inv, k_t=kd * w_inv,
                    b_h=b_vec * w_end, k_h=kd * w_end, v=v)
    ops, states = [], []
    for b in range(bt):
        for pi in range(N_PAIRS):
            ln = slice(pi * PAIR, (pi + 1) * PAIR)
            unit = {name: t[b * c:(b + 1) * c, ln] for name, t in full_ops.items()}
            unit["w_tot"] = w_tot[b * c:b * c + 1, ln]
            ops.append(unit)
            states.append(state_ref[b, pi])
    new_states, ys = _units_chunk(states, ops, m)
    for b in range(bt):
        for pi in range(N_PAIRS):
            state_ref[b, pi] = new_states[b * N_PAIRS + pi]
    y = jnp.concatenate([jnp.concatenate(ys[b * N_PAIRS:(b + 1) * N_PAIRS], axis=1) for b in range(bt)], axis=0)
    if not reverse:
        out_ref[...] = y.reshape(bt, c, RW_WIDTH)
    else:
        y = y + yf_ref[...].reshape(rows, RW_WIDTH)
        a_other = _sigmoid(a0o_ref[...] + proj[:, RW_WIDTH:2 * RW_WIDTH])
        kd_sum = kd + k * (1.0 + (a_other - 1.0) * ka_ref[...])
        bonus = _head_sums(r * kd_sum * rk_ref[...], bd) * v
        inv_n = 1.0 / RW_HEAD_DIM
        mean = _head_sums(y, bd) * inv_n
        yc = y - mean
        var = _head_sums(yc * yc, bd) * inv_n
        yn = yc * lax.rsqrt(var + GN_EPS) * gng_ref[...] + gnb_ref[...]
        g = proj[:, 2 * RW_WIDTH:]
        out_ref[...] = ((yn + bonus) * g).astype(BF16).reshape(bt, c, RW_WIDTH)


def _rwkv_pass(z3, yf, p, reverse, bt):
    b_sz, t_len, _ = z3.shape
    c = RW_CHUNK
    n_chunks = t_len // c

    def cidx(ci):
        return (n_chunks - 1 - ci) if reverse else ci

    full = lambda shape: pl.BlockSpec(shape, lambda bi, ci: (0,) * len(shape))
    row512 = full((1, RW_WIDTH))
    d = "rev" if reverse else "fwd"
    n_proj = p["lora_" + d].shape[1]
    in_specs = [pl.BlockSpec((bt, c, Z_COLS), lambda bi, ci: (bi, cidx(ci), 0))]
    args = [z3]
    if reverse:
        in_specs.append(pl.BlockSpec((bt, c, RW_WIDTH), lambda bi, ci: (bi, cidx(ci), 0)))
        args.append(yf)
    in_specs += [full((LORA_PAD, n_proj)), row512, row512]
    args += [p["lora_" + d], p["w0_" + d], p["a0_" + d]]
    if reverse:
        in_specs.append(row512)
        args.append(p["a0_fwd"])
    in_specs += [row512, row512]
    args += [p["rw_k_k"], p["rw_k_a"]]
    if reverse:
        in_specs.append(row512)
        args.append(p["rw_r_k"])
    in_specs.append(full((PAIR, PAIR)))
    args.append(p["head_ones"])
    if reverse:
        in_specs += [row512, row512]
        args += [p["rw_gn_g"], p["rw_gn_b"]]
    return pl.pallas_call(
        functools.partial(_rwkv_body, reverse, bt),
        grid=(b_sz // bt, n_chunks),
        in_specs=in_specs,
        out_specs=pl.BlockSpec((bt, c, RW_WIDTH), lambda bi, ci: (bi, cidx(ci), 0)),
        out_shape=jax.ShapeDtypeStruct((b_sz, t_len, RW_WIDTH), BF16 if reverse else F32),
        scratch_shapes=[pltpu.VMEM((bt, N_PAIRS, PAIR, PAIR), F32)],
        compiler_params=pltpu.CompilerParams(vmem_limit_bytes=VMEM_LIMIT),
        name="rwkv_" + d,
    )(*args)


def _mix_out_body(x_ref, gm_ref, rw_ref, wo_ref, g2_ref, rw_hi_ref, rw_lo_ref, rb_ref, earlier_ref,
                  x1_ref, h2_ref, route_ref, route_t_ref, counts_ref, carry_ref):
    i = pl.program_id(0)

    @pl.when(i == 0)
    def _():
        carry_ref[...] = jnp.zeros_like(carry_ref)

    sub = earlier_ref.shape[0]
    subs = range(x_ref.shape[0] // sub)
    rows = [slice(s * sub, (s + 1) * sub) for s in subs]
    x1 = [x_ref[r, :] + _dot(gm_ref[r, :], wo_ref[:GM_WIDTH, :]) + _dot(rw_ref[r, :], wo_ref[GM_WIDTH:, :])
          for r in rows]
    h2 = [x1[s] * lax.rsqrt(jnp.mean(x1[s] * x1[s], axis=-1, keepdims=True) + RMS_EPS) * g2_ref[...] for s in subs]
    for s in subs:
        x1_ref[rows[s], :] = x1[s]
        h2_ref[rows[s], :] = _pack_bf16_halves(h2[s])
    split = [_split(h) for h in h2]
    logits = [_dot_nt(rw_hi_ref[...], hi) + _dot_nt(rw_hi_ref[...], lo) + _dot_nt(rw_lo_ref[...], hi) + rb_ref[...]
              for hi, lo in split]
    expert = lax.broadcasted_iota(jnp.int32, (N_EXPERTS, sub), 0).astype(F32)
    work = logits
    vals, ids, hits = [[] for _ in subs], [[] for _ in subs], [[] for _ in subs]
    onehot = [jnp.zeros((N_EXPERTS, sub), F32) for _ in subs]
    for _ in range(TOP_K):
        for s in subs:
            mx = jnp.max(work[s], axis=0, keepdims=True)
            idx = jnp.min(jnp.where(work[s] == mx, expert, float(N_EXPERTS)), axis=0, keepdims=True)
            hit = expert == idx
            vals[s].append(mx)
            ids[s].append(idx)
            hits[s].append(hit)
            onehot[s] = jnp.where(hit, 1.0, onehot[s])
        work = [jnp.where(hits[s][-1], -jnp.inf, work[s]) for s in subs]
    earlier_counts = [_dot(onehot[s].astype(BF16), earlier_ref[...]) for s in subs]
    carry = carry_ref[:, 0:1]
    for s in subs:
        exps = [jnp.exp(vk - vals[s][0]) for vk in vals[s]]
        denom = exps[0] + exps[1] + exps[2] + exps[3]
        ranks = carry + earlier_counts[s]
        rank_rows = [jnp.sum(jnp.where(hit, ranks, 0.0), axis=0, keepdims=True) for hit in hits[s]]
        route_t = jnp.concatenate(ids[s] + [e / denom for e in exps] + rank_rows
                                  + [jnp.zeros((ROUTE_ROWS - 3 * TOP_K, sub), F32)], axis=0)
        route_t_ref[:, rows[s]] = route_t
        route_ref[rows[s], :] = jnp.concatenate([route_t, jnp.zeros((LANES - ROUTE_ROWS, sub), F32)], axis=0).T
        carry = carry + jnp.sum(onehot[s], axis=1, keepdims=True)
    carry_ref[...] = jnp.broadcast_to(carry, carry_ref.shape)
    counts_ref[...] = jnp.broadcast_to(carry, counts_ref.shape)


def _mix_out(x2d, gm, rw, p, tm):
    n = x2d.shape[0]
    full = lambda shape: pl.BlockSpec(shape, lambda i: (0,) * len(shape))
    tile = lambda w: pl.BlockSpec((tm, w), lambda i: (i, 0))
    sub = min(MIX_OUT_SUB, tm)
    earlier = (jnp.arange(sub)[:, None] < jnp.arange(sub)[None, :]).astype(BF16)
    return pl.pallas_call(
        _mix_out_body,
        grid=(n // tm,),
        in_specs=[tile(D_MODEL), tile(GM_WIDTH), tile(RW_WIDTH), full((D_MODEL, D_MODEL)), full((1, D_MODEL)),
                  full((N_EXPERTS, D_MODEL)), full((N_EXPERTS, D_MODEL)), full((N_EXPERTS, 1)), full((sub, sub))],
        out_specs=[tile(D_MODEL), tile(D_MODEL // 2), tile(LANES), pl.BlockSpec((ROUTE_ROWS, tm), lambda i: (0, i)),
                   full((N_EXPERTS, LANES))],
        out_shape=[
            jax.ShapeDtypeStruct((n, D_MODEL), F32),
            jax.ShapeDtypeStruct((n, D_MODEL // 2), jnp.int32),
            jax.ShapeDtypeStruct((n, LANES), F32),
            jax.ShapeDtypeStruct((ROUTE_ROWS, n), F32),
            jax.ShapeDtypeStruct((N_EXPERTS, LANES), F32),
        ],
        scratch_shapes=[pltpu.VMEM((N_EXPERTS, LANES), F32)],
        compiler_params=pltpu.CompilerParams(vmem_limit_bytes=VMEM_LIMIT),
        name="mix_out",
    )(x2d, gm, rw, p["w_out"], p["norm2_g"], p["router_hi"], p["router_lo"], p["router_b"], earlier)


def _sc_gather_rows(table, idx):
    n_idx = idx.shape[0]
    width = table.shape[1]
    n_workers = SC_CORES * SC_SUBCORES
    per_worker = n_idx // n_workers
    n_windows = per_worker // SC_WINDOW
    assert per_worker * n_workers == n_idx and n_windows * SC_WINDOW == per_worker
    mesh = plsc.VectorSubcoreMesh(core_axis_name="c", subcore_axis_name="s")

    @functools.partial(
        pl.kernel, mesh=mesh,
        out_type=jax.ShapeDtypeStruct((n_idx, width), table.dtype),
        scratch_types=[
            pltpu.VMEM((SC_WINDOW,), jnp.int32),
            pltpu.VMEM((SC_WINDOW, width), table.dtype),
            pltpu.SemaphoreType.DMA,
        ],
        name="sc_gather_rows",
    )
    def gather(table_hbm, idx_hbm, out_hbm, idx_v, rows_v, sem):
        worker = lax.axis_index("s") * SC_CORES + lax.axis_index("c")
        base = worker * per_worker

        @pl.loop(0, n_windows)
        def _(j):
            off = base + j * SC_WINDOW
            pltpu.sync_copy(idx_hbm.at[pl.ds(off, SC_WINDOW)], idx_v)
            pltpu.async_copy(table_hbm.at[idx_v], rows_v, sem).wait()
            pltpu.sync_copy(rows_v, out_hbm.at[pl.ds(off, SC_WINDOW)])

    return gather(table, idx)


def _sc_scatter_rows(src, dest_w, n_out):
    n, width = src.shape
    n_windows, top_k, window = dest_w.shape
    n_workers = SC_CORES * SC_SUBCORES
    per_worker = n_windows // n_workers
    assert n_windows * window == n and per_worker * n_workers == n_windows
    mesh = plsc.VectorSubcoreMesh(core_axis_name="c", subcore_axis_name="s")

    @functools.partial(
        pl.kernel, mesh=mesh,
        out_type=jax.ShapeDtypeStruct((n_out, width), src.dtype),
        scratch_types=[
            pltpu.VMEM((top_k, window), jnp.int32),
            pltpu.VMEM((window, width), src.dtype),
        ],
        name="sc_scatter_rows",
    )
    def scatter(src_hbm, dest_hbm, out_hbm, idx_v, rows_v):
        worker = lax.axis_index("s") * SC_CORES + lax.axis_index("c")

        @pl.loop(0, per_worker)
        def _(j):
            g = worker * per_worker + j
            pltpu.sync_copy(dest_hbm.at[g], idx_v)
            pltpu.sync_copy(src_hbm.at[pl.ds(g * window, window)], rows_v)
            for k in range(top_k):
                pltpu.sync_copy(rows_v, out_hbm.at[idx_v.at[k]])

    return scatter(src, dest_w)


def _weight_copies(wgu_hbm, wd_hbm, wgu_buf, wd_buf, sem, expert, slot):
    return (pltpu.make_async_copy(wgu_hbm.at[expert], wgu_buf.at[slot], sem.at[0, slot]),
            pltpu.make_async_copy(wd_hbm.at[expert], wd_buf.at[slot], sem.at[1, slot]))


def _experts_body(blk_e_ref, n_used_ref, valid_ref, slot_ref, next_e_ref, x_ref, wgu_hbm, bgu_ref, wd_hbm, bd_ref,
                  out_ref, wgu_bf, wd_bf, wgu_buf, wd_buf, sem):
    i = pl.program_id(0)
    n_used = n_used_ref[0]
    expert = blk_e_ref[i]
    slot = slot_ref[i]
    copies = functools.partial(_weight_copies, wgu_hbm, wd_hbm, wgu_buf, wd_buf, sem)

    @pl.when(i == 0)
    def _():
        for cp in copies(expert, slot):
            cp.start()

    @pl.when((i < n_used) & ((i == 0) | (expert != blk_e_ref[jnp.maximum(i - 1, 0)])))
    def _():
        for cp in copies(expert, slot):
            cp.wait()
        wgu_bf[...] = wgu_buf[slot].astype(BF16)
        wd_bf[...] = wd_buf[slot].astype(BF16)
        following = next_e_ref[i]

        @pl.when(following >= 0)
        def _():
            for cp in copies(following, 1 - slot):
                cp.start()

    valid = jnp.where(i < n_used, valid_ref[i], 0)

    def expert_mlp(n_rows):
        row = lax.broadcasted_iota(jnp.int32, (n_rows, 1), 0)
        x_left, x_right = _unpack_bf16_halves(jnp.where(row < valid, x_ref[:n_rows, :], 0))
        half = D_MODEL // 2
        gu = (_dot(x_left.astype(BF16), wgu_bf[:half, :]) + _dot(x_right.astype(BF16), wgu_bf[half:, :])
              + bgu_ref[0])
        gate = jnp.minimum(gu[:, :D_EXPERT], SWIGLU_LIMIT)
        up = jnp.clip(gu[:, D_EXPERT:], -SWIGLU_LIMIT, SWIGLU_LIMIT)
        act = gate * _sigmoid(gate * SWIGLU_ALPHA) * (up + 1.0)
        out_ref[:n_rows, :] = _pack_bf16_halves(_dot(act.astype(BF16), wd_bf[...]) + bd_ref[0])

    half_rows = MOE_ROWS // 2

    @pl.when(valid > half_rows)
    def _():
        expert_mlp(MOE_ROWS)

    @pl.when((valid > 0) & (valid <= half_rows))
    def _():
        expert_mlp(half_rows)
        out_ref[half_rows:, :] = jnp.zeros((MOE_ROWS - half_rows, D_MODEL // 2), jnp.int32)

    @pl.when(valid == 0)
    def _():
        out_ref[...] = jnp.zeros_like(out_ref)


def _experts(x_rows, blk_e, n_used, valid, slot, next_e, p):
    n_blocks = blk_e.shape[0]
    grid_spec = pltpu.PrefetchScalarGridSpec(
        num_scalar_prefetch=5,
        grid=(n_blocks,),
        in_specs=[
            pl.BlockSpec((MOE_ROWS, D_MODEL // 2), lambda i, be, nu, *_: (jnp.minimum(i, nu[0] - 1), 0)),
            pl.BlockSpec(memory_space=pl.ANY),
            pl.BlockSpec((1, 1, 2 * D_EXPERT), lambda i, be, *_: (be[i], 0, 0)),
            pl.BlockSpec(memory_space=pl.ANY),
            pl.BlockSpec((1, 1, D_MODEL), lambda i, be, *_: (be[i], 0, 0)),
        ],
        out_specs=pl.BlockSpec((MOE_ROWS, D_MODEL // 2), lambda i, *_: (i, 0)),
        scratch_shapes=[
            pltpu.VMEM((D_MODEL, 2 * D_EXPERT), BF16),
            pltpu.VMEM((D_EXPERT, D_MODEL), BF16),
            pltpu.VMEM((2, D_MODEL, 2 * D_EXPERT), F32),
            pltpu.VMEM((2, D_EXPERT, D_MODEL), F32),
            pltpu.SemaphoreType.DMA((2, 2)),
        ],
    )
    return pl.pallas_call(
        _experts_body,
        grid_spec=grid_spec,
        out_shape=jax.ShapeDtypeStruct((n_blocks * MOE_ROWS, D_MODEL // 2), jnp.int32),
        compiler_params=pltpu.CompilerParams(vmem_limit_bytes=VMEM_LIMIT),
        name="experts",
    )(blk_e, n_used, valid, slot, next_e, x_rows, p["w_gu"], p["b_gu"], p["w_down"], p["b_down"])


def _combine_body(yg_ref, x1_ref, gates_ref, fg_ref, out_ref):
    gates = gates_ref[...]
    half = D_MODEL // 2
    acc_left = x1_ref[:, :half]
    acc_right = x1_ref[:, half:]
    for kk in range(TOP_K):
        y_left, y_right = _unpack_bf16_halves(yg_ref[kk])
        gate = gates[:, TOP_K + kk:TOP_K + kk + 1]
        acc_left = acc_left + gate * y_left
        acc_right = acc_right + gate * y_right
    acc = jnp.concatenate([acc_left, acc_right], axis=1)
    out_ref[...] = acc * lax.rsqrt(jnp.mean(acc * acc, axis=-1, keepdims=True) + RMS_EPS) * fg_ref[...]


def _combine(yg, x1, gates, final_g, tm):
    n = x1.shape[0]
    return pl.pallas_call(
        _combine_body,
        grid=(n // tm,),
        in_specs=[
            pl.BlockSpec((TOP_K, tm, D_MODEL // 2), lambda i: (0, i, 0)),
            pl.BlockSpec((tm, D_MODEL), lambda i: (i, 0)),
            pl.BlockSpec((tm, LANES), lambda i: (i, 0)),
            pl.BlockSpec((1, D_MODEL), lambda i: (0, 0)),
        ],
        out_specs=pl.BlockSpec((tm, D_MODEL), lambda i: (i, 0)),
        out_shape=jax.ShapeDtypeStruct((n, D_MODEL), F32),
        compiler_params=pltpu.CompilerParams(vmem_limit_bytes=VMEM_LIMIT),
        name="combine",
    )(yg, x1, gates, final_g)


def _prepare(norm1_g, w_in, rw_mu, gm_ln_g, gm_ln_b, gm_ws, gm_bs, rw_w0, rw_w2, rw_a0, rw_a2, rw_g2,
             rw_k_k, rw_k_a, rw_r_k, rw_gn_g, rw_gn_b, w_out, norm2_g, router_w, router_b,
             w_gu, b_gu, w_down, b_down, final_g):
    l = 0
    row = lambda t: t.reshape(1, -1).astype(F32)
    p = {}
    p["norm1_g"] = row(norm1_g[l])
    w = w_in[l]
    p["w_gm"] = w[:, :2 * GM_WIDTH].astype(BF16)
    pad_cols = LORA_PAD - LORA_COLS
    p["w_z"] = jnp.pad(w[:, 2 * GM_WIDTH:], ((0, 0), (0, pad_cols))).astype(BF16)
    mu = jnp.pad(row(rw_mu[l]), ((0, 0), (0, pad_cols)))
    p["rw_mu"] = jnp.concatenate([1.0 - mu, 0.5 * mu], axis=0)
    p["gm_ln_g"] = row(gm_ln_g[l])
    p["gm_ln_b"] = row(gm_ln_b[l])
    p["gm_ws"] = gm_ws[l].astype(BF16)
    p["gm_bs"] = jnp.broadcast_to(gm_bs[l][:, :, None], (GM_HEADS, GM_CHUNK, GM_HEAD_DIM)).astype(F32)
    zeros = lambda r, c: jnp.zeros((r, c), F32)
    o_a = DECAY_LORA
    o_g = DECAY_LORA + ICLR_LORA

    def lora_matrix(d, with_epilogue):
        blocks = [jnp.concatenate([rw_w2[l, d], zeros(LORA_PAD - DECAY_LORA, RW_WIDTH)], axis=0),
                  jnp.concatenate([zeros(o_a, RW_WIDTH), rw_a2[l, d], zeros(LORA_PAD - o_g, RW_WIDTH)], axis=0)]
        if with_epilogue:
            blocks.append(jnp.concatenate([zeros(o_a, RW_WIDTH), rw_a2[l, 1 - d], zeros(LORA_PAD - o_g, RW_WIDTH)],
                                          axis=0))
            blocks.append(jnp.concatenate([zeros(o_g, RW_WIDTH), rw_g2[l], zeros(LORA_PAD - LORA_COLS, RW_WIDTH)],
                                          axis=0))
        return jnp.concatenate(blocks, axis=1)

    for d, name in ((0, "fwd"), (1, "rev")):
        p["lora_" + name] = lora_matrix(d, with_epilogue=(d == 1)).astype(BF16)
        p["w0_" + name] = row(rw_w0[l, d])
        p["a0_" + name] = row(rw_a0[l, d])
    p["rw_k_k"] = row(rw_k_k[l])
    p["rw_k_a"] = row(rw_k_a[l])
    p["rw_r_k"] = row(rw_r_k[l])
    p["rw_gn_g"] = row(rw_gn_g[l])
    p["rw_gn_b"] = row(rw_gn_b[l])
    ch = jnp.arange(PAIR) // RW_HEAD_DIM
    p["head_ones"] = (ch[:, None] == ch[None, :]).astype(BF16)
    p["w_out"] = w_out[l].astype(BF16)
    p["norm2_g"] = row(norm2_g[l])
    rw_t = router_w[l].astype(F32).T
    hi = rw_t.astype(BF16)
    p["router_hi"] = hi
    p["router_lo"] = (rw_t - hi.astype(F32)).astype(BF16)
    p["router_b"] = router_b[l].astype(F32).reshape(N_EXPERTS, 1)
    p["w_gu"] = w_gu[l]
    p["b_gu"] = b_gu[l].reshape(N_EXPERTS, 1, 2 * D_EXPERT).astype(F32)
    p["w_down"] = w_down[l]
    p["b_down"] = b_down[l].reshape(N_EXPERTS, 1, D_MODEL).astype(F32)
    p["final_g"] = row(final_g)
    return p


def _pick_tile(n, want):
    t = want
    while n % t:
        t //= 2
    return t


def _encoder(x, p):
    b_sz, t_len, d = x.shape
    n = b_sz * t_len
    x2d = x.reshape(n, d)
    gm, z = _mix_in(x2d, p, _pick_tile(t_len, 1024), t_len)
    z3 = z.reshape(b_sz, t_len, Z_COLS)
    yf = _rwkv_pass(z3, None, p, reverse=False, bt=RW_BATCH_TILE)
    rw = _rwkv_pass(z3, yf, p, reverse=True, bt=RW_BATCH_TILE)
    tmo = _pick_tile(n, 1024)
    x1, h2, route, route_t, counts = _mix_out(x2d, gm, rw.reshape(n, RW_WIDTH), p, tmo)
    ids = route_t[:TOP_K].astype(jnp.int32)
    ranks = route_t[2 * TOP_K:3 * TOP_K].astype(jnp.int32)
    cnt = counts[:, 0].astype(jnp.int32)
    padded = (cnt + MOE_ROWS - 1) // MOE_ROWS * MOE_ROWS
    pad_end = jnp.cumsum(padded)
    pad_start = pad_end - padded
    expert_ids = jnp.arange(N_EXPERTS, dtype=jnp.int32)
    start_of = jnp.sum(jnp.where(ids[..., None] == expert_ids, pad_start, 0), axis=-1)
    dest = start_of + ranks
    n_blocks = n * TOP_K // MOE_ROWS + N_EXPERTS
    blk_start = jnp.arange(n_blocks, dtype=jnp.int32) * MOE_ROWS
    blk_e = jnp.minimum(jnp.sum((pad_end[None, :] <= blk_start[:, None]).astype(jnp.int32), axis=1), N_EXPERTS - 1)
    n_used = (pad_end[-1] // MOE_ROWS).astype(jnp.int32).reshape(1)
    blk_onehot = blk_e[:, None] == expert_ids
    blk_cnt = jnp.sum(jnp.where(blk_onehot, cnt, 0), axis=-1)
    blk_first = jnp.sum(jnp.where(blk_onehot, pad_start, 0), axis=-1)
    valid = jnp.clip(blk_cnt - (blk_start - blk_first), 0, MOE_ROWS).astype(jnp.int32)
    window = min(SC_SCATTER_WINDOW, n // (SC_CORES * SC_SUBCORES))
    dest_w = dest.reshape(TOP_K, n // window, window).transpose(1, 0, 2)
    x_rows = _sc_scatter_rows(h2, dest_w, n_blocks * MOE_ROWS)
    present = cnt > 0
    slot_of_expert = (jnp.cumsum(present.astype(jnp.int32)) - 1) % 2
    later = present[None, :] & (expert_ids[None, :] > expert_ids[:, None])
    next_present = jnp.min(jnp.where(later, expert_ids[None, :], N_EXPERTS), axis=1)
    next_present = jnp.where(next_present < N_EXPERTS, next_present, -1)
    slot = jnp.sum(jnp.where(blk_onehot, slot_of_expert, 0), axis=-1).astype(jnp.int32)
    next_e = jnp.sum(jnp.where(blk_onehot, next_present, 0), axis=-1).astype(jnp.int32)
    y_rows = _experts(x_rows, blk_e, n_used, valid, slot, next_e, p)
    yg = _sc_gather_rows(y_rows, dest.reshape(-1)).reshape(TOP_K, n, D_MODEL // 2)
    out = _combine(yg, x1, route, p["final_g"], _pick_tile(n, 1024))
    return out.reshape(b_sz, t_len, d)


def kernel(x_prompt, x_sample, norm1_g, w_in, rw_mu, gm_ln_g, gm_ln_b, gm_ws, gm_bs, rw_w0, rw_w2, rw_a0, rw_a2,
           rw_g2, rw_k_k, rw_k_a, rw_r_k, rw_gn_g, rw_gn_b, w_out, norm2_g, router_w, router_b, w_gu, b_gu,
           w_down, b_down, final_g):
    p = _prepare(norm1_g, w_in, rw_mu, gm_ln_g, gm_ln_b, gm_ws, gm_bs, rw_w0, rw_w2, rw_a0, rw_a2, rw_g2,
                 rw_k_k, rw_k_a, rw_r_k, rw_gn_g, rw_gn_b, w_out, norm2_g, router_w, router_b,
                 w_gu, b_gu, w_down, b_down, final_g)
    return (_encoder(x_prompt, p), _encoder(x_sample, p))
```

```python
import functools

import jax
import jax.numpy as jnp
from jax import lax
from jax.experimental import pallas as pl
from jax.experimental.pallas import tpu as pltpu
from jax.experimental.pallas import tpu_sc as plsc

F32 = jnp.float32
BF16 = jnp.bfloat16

D_MODEL = 1024
GM_WIDTH = 512
RW_WIDTH = 512
GM_HEADS = 4
GM_HEAD_DIM = 128
GM_CHUNK = 128
RW_HEAD_DIM = 64
DECAY_LORA = 32
ICLR_LORA = 32
GATE_LORA = 96
LORA_COLS = DECAY_LORA + ICLR_LORA + GATE_LORA
LORA_PAD = 256
Z_COLS = 3 * RW_WIDTH + LORA_PAD
Z_CHUNK = 256
N_EXPERTS = 32
TOP_K = 4
D_EXPERT = 1024
SWIGLU_LIMIT = 7.0
SWIGLU_ALPHA = 1.702
RMS_EPS = 1e-5
LN_EPS = 1e-5
GN_EPS = 64e-5

LANES = 128
SUBLANES = 8
VMEM_LIMIT = 56 * 1024 * 1024

RW_CHUNK = 64
PAIR = 2 * RW_HEAD_DIM
N_PAIRS = RW_WIDTH // PAIR
RW_BATCH_TILE = 4
MIX_OUT_SUB = 128
MOE_ROWS = 512
SC_CORES = 2
SC_SUBCORES = 16
SC_WINDOW = 64
SC_SCATTER_WINDOW = 128
ROUTE_ROWS = 16
DECAY_SCALE = 0.6065306597126334
KK_NORM_FLOOR_SQ = 1e-24


def _dot(a, b):
    return jnp.dot(a, b, preferred_element_type=F32)


def _dot_nt(a, b):
    return lax.dot_general(a, b, (((1,), (1,)), ((), ())), preferred_element_type=F32)


def _split(x):
    hi = x.astype(BF16)
    lo = (x - hi.astype(F32)).astype(BF16)
    return hi, lo


def _head_sums(x, pair_ones):
    rows = x.shape[0]
    n_tiles = x.shape[1] // LANES
    stacked = jnp.concatenate([x[:, t * LANES:(t + 1) * LANES] for t in range(n_tiles)], axis=0)
    sums = _dot(stacked.astype(BF16), pair_ones)
    return jnp.concatenate([sums[t * rows:(t + 1) * rows] for t in range(n_tiles)], axis=1)


def _dot2_lhs01(a01, b):
    b_hi, b_lo = _split(b)
    return _dot(a01, b_hi) + _dot(a01, b_lo)


def _pack_bf16_halves(x):
    w = x.shape[1] // 2
    hi = lax.bitcast_convert_type(x[:, :w].astype(BF16).astype(F32), jnp.int32)
    lo = lax.bitcast_convert_type(x[:, w:].astype(BF16).astype(F32), jnp.int32)
    return hi | lax.shift_right_logical(lo, 16)


def _unpack_bf16_halves(words):
    left = lax.bitcast_convert_type(words & jnp.int32(-65536), F32)
    right = lax.bitcast_convert_type(lax.shift_left(words, 16), F32)
    return left, right


def _gelu_tanh(x):
    return 0.5 * x * (1.0 + jnp.tanh(0.7978845608028654 * (x + 0.044715 * (x * x * x))))


def _sigmoid(x):
    return 0.5 * jnp.tanh(0.5 * x) + 0.5


def _mix_in_body(tiles_per_seq, x_ref, xp_ref, xn_ref, g1_ref, wgm_ref, wz_ref, mu_ref, lng_ref, lnb_ref, ws_ref,
                 bs_ref, gm_ref, z_ref):
    i = pl.program_id(0)
    tm = x_ref.shape[0]

    def normed(x):
        return (x * lax.rsqrt(jnp.mean(x * x, axis=-1, keepdims=True) + RMS_EPS) * g1_ref[...]).astype(BF16)

    hb = normed(x_ref[...])
    uv = _dot(hb, wgm_ref[...])
    hb_halo = normed(jnp.concatenate([xp_ref[...], xn_ref[...]], axis=0))
    has_prev = i % tiles_per_seq > 0
    has_next = i % tiles_per_seq < tiles_per_seq - 1
    row8 = lax.broadcasted_iota(jnp.int32, (SUBLANES, Z_CHUNK), 0)
    for j in range(Z_COLS // Z_CHUNK):
        cols = slice(j * Z_CHUNK, (j + 1) * Z_CHUNK)
        z = _dot(hb, wz_ref[:, cols])
        halo = _dot(hb_halo, wz_ref[:, cols])
        prev_row = jnp.where(has_prev, halo[SUBLANES - 1:SUBLANES, :], 0.0)
        next_row = jnp.where(has_next, halo[SUBLANES:SUBLANES + 1, :], 0.0)
        down = pltpu.roll(z, 1, axis=0)
        up = pltpu.roll(z, tm - 1, axis=0)
        prev = jnp.concatenate([jnp.where(row8 == 0, prev_row, down[:SUBLANES]), down[SUBLANES:]], axis=0)
        nxt = jnp.concatenate([up[:tm - SUBLANES], jnp.where(row8 == SUBLANES - 1, next_row, up[tm - SUBLANES:])],
                              axis=0)
        z_ref[:, cols] = z * mu_ref[0:1, cols] + (prev + nxt) * mu_ref[1:2, cols]
    u = _gelu_tanh(uv[:, :GM_WIDTH])
    v = _gelu_tanh(uv[:, GM_WIDTH:])
    mean = jnp.mean(v, axis=-1, keepdims=True)
    vc = v - mean
    var = jnp.mean(vc * vc, axis=-1, keepdims=True)
    vn = (vc * lax.rsqrt(var + LN_EPS) * lng_ref[...] + lnb_ref[...]).astype(BF16)
    for c in range(tm // GM_CHUNK):
        rows = slice(c * GM_CHUNK, (c + 1) * GM_CHUNK)
        for hd in range(GM_HEADS):
            cols = slice(hd * GM_HEAD_DIM, (hd + 1) * GM_HEAD_DIM)
            mixed = _dot(ws_ref[hd], vn[rows, cols]) + bs_ref[hd]
            gm_ref[rows, cols] = (u[rows, cols] * mixed).astype(BF16)


def _mix_in(x2d, p, tm, t_len):
    n = x2d.shape[0]
    assert t_len % tm == 0
    halo_per_tile = tm // SUBLANES
    n_halo = n // SUBLANES
    full = lambda shape: pl.BlockSpec(shape, lambda i: (0,) * len(shape))
    return pl.pallas_call(
        functools.partial(_mix_in_body, t_len // tm),
        grid=(n // tm,),
        in_specs=[
            pl.BlockSpec((tm, D_MODEL), lambda i: (i, 0)),
            pl.BlockSpec((SUBLANES, D_MODEL), lambda i: (jnp.maximum(i * halo_per_tile - 1, 0), 0)),
            pl.BlockSpec((SUBLANES, D_MODEL), lambda i: (jnp.minimum((i + 1) * halo_per_tile, n_halo - 1), 0)),
            full((1, D_MODEL)),
            full((D_MODEL, 2 * GM_WIDTH)),
            full((D_MODEL, Z_COLS)),
            full((2, Z_COLS)),
            full((1, GM_WIDTH)),
            full((1, GM_WIDTH)),
            full((GM_HEADS, GM_CHUNK, GM_CHUNK)),
            full((GM_HEADS, GM_CHUNK, GM_HEAD_DIM)),
        ],
        out_specs=[
            pl.BlockSpec((tm, GM_WIDTH), lambda i: (i, 0)),
            pl.BlockSpec((tm, Z_COLS), lambda i: (i, 0)),
        ],
        out_shape=[
            jax.ShapeDtypeStruct((n, GM_WIDTH), BF16),
            jax.ShapeDtypeStruct((n, Z_COLS), F32),
        ],
        compiler_params=pltpu.CompilerParams(vmem_limit_bytes=VMEM_LIMIT),
        name="mix_in",
    )(x2d, x2d, x2d, p["norm1_g"], p["w_gm"], p["w_z"], p["rw_mu"], p["gm_ln_g"], p["gm_ln_b"], p["gm_ws"],
      p["gm_bs"])


def _rwkv_masks(reverse):
    c = RW_CHUNK
    ti = lax.broadcasted_iota(jnp.int32, (c, c), 0)
    tj = lax.broadcasted_iota(jnp.int32, (c, c), 1)
    incl = (tj >= ti) if reverse else (tj <= ti)
    lane =lax.broadcasted_iota(jnp.int32, (c, PAIR), 1)
    head0 = lane < RW_HEAD_DIM
    lane2 = lax.broadcasted_iota(jnp.int32, (c, 2 * c), 1)
    left = lane2 < c
    ti2 = lax.broadcasted_iota(jnp.int32, (c, 2 * c), 0)
    tj2 = jnp.where(left, lane2, lane2 - c)
    incl2 = (tj2 >= ti2) if reverse else (tj2 <= ti2)
    strict2 = (tj2 > ti2) if reverse else (tj2 < ti2)
    eye2 = jnp.where(tj2 == ti2, 1.0, 0.0).astype(F32)
    si = lax.broadcasted_iota(jnp.int32, (PAIR, PAIR), 0)
    sj = lax.broadcasted_iota(jnp.int32, (PAIR, PAIR), 1)
    same_head = (si // RW_HEAD_DIM) == (sj // RW_HEAD_DIM)
    return dict(
        incl2=incl2, strict2=strict2, head0=head0, left=left, eye2=eye2, same_head=same_head,
        tri=jnp.where(incl, 1.0, 0.0).astype(BF16),
    )


def _stack_heads(x, head0):
    return jnp.concatenate([jnp.where(head0, x, 0.0), jnp.where(head0, 0.0, x)], axis=0)


def _block_diag(x, left):
    return jnp.concatenate([jnp.where(left, x, 0.0), jnp.where(left, 0.0, x)], axis=0)


def _units_chunk(states, ops, m):
    c = RW_CHUNK
    units = range(len(ops))
    bf = lambda t: t.astype(BF16)
    head0, left = m["head0"], m["left"]
    ar = [bf(jnp.concatenate([o["a_t"], o["r_t"]], axis=0)) for o in ops]
    rstack = [bf(jnp.concatenate([_stack_heads(o["b_t"], head0), _stack_heads(o["k_t"], head0)], axis=0))
              for o in ops]
    acat = [_dot_nt(ar[i], rstack[i]) for i in units]
    sa = [_dot_nt(ar[i], bf(states[i])) for i in units]
    a_ab = [jnp.where(m["strict2"], acat[i][:c, :2 * c], 0.0) for i in units]
    a_ak = [bf(jnp.where(m["strict2"], acat[i][:c, 2 * c:], 0.0)) for i in units]
    a_r = [bf(jnp.concatenate([jnp.where(m["incl2"], acat[i][c:, :2 * c], 0.0),
                               jnp.where(m["incl2"], acat[i][c:, 2 * c:], 0.0)], axis=1)) for i in units]
    p = [m["eye2"] + a_ab[i] for i in units]
    lk = [_dot(bf(a_ab[i]), bf(_block_diag(a_ab[i], left))) for i in units]
    vstack = [bf(_stack_heads(o["v"], head0)) for o in ops]
    rhs = [sa[i][:c] + _dot(a_ak[i], vstack[i]) for i in units]
    power = 2
    while 2 * power < c:
        res = [_dot(bf(jnp.concatenate([p[i], lk[i]], axis=0)), bf(_block_diag(lk[i], left))) for i in units]
        p = [p[i] + res[i][:c] for i in units]
        lk = [res[i][c:] for i in units]
        power *= 2
    p = [p[i] + _dot(bf(p[i]), bf(_block_diag(lk[i], left))) for i in units]
    u = [_dot(bf(p[i]), bf(_stack_heads(rhs[i], head0))) for i in units]
    ys = [sa[i][c:] + _dot(a_r[i], jnp.concatenate([bf(_stack_heads(u[i], head0)), vstack[i]], axis=0))
          for i in units]
    upd = [_dot(bf(jnp.concatenate([u[i], ops[i]["v"]], axis=0).T),
                bf(jnp.concatenate([ops[i]["b_h"], ops[i]["k_h"]], axis=0))) for i in units]
    new_states = [states[i] * ops[i]["w_tot"] + jnp.where(m["same_head"], upd[i], 0.0) for i in units]
    return new_states, ys


def _rwkv_body(reverse, bt, *refs):
    if reverse:
        (zc_ref, yf_ref, lw_ref, w0_ref, a0_ref, a0o_ref, kk_ref, ka_ref,
         rk_ref, bd_ref, gng_ref, gnb_ref, out_ref, state_ref) = refs
    else:
        (zc_ref, lw_ref, w0_ref, a0_ref, kk_ref, ka_ref,
         bd_ref, out_ref, state_ref) = refs
    c = RW_CHUNK

    @pl.when(pl.program_id(1) == 0)
    def _():
        state_ref[...] = jnp.zeros_like(state_ref)

    rows = bt * c
    m = _rwkv_masks(reverse)
    lora_lane = lax.broadcasted_iota(jnp.int32, (rows, LORA_PAD), 1)
    bd = bd_ref[...]
    zf = zc_ref[...].reshape(rows, Z_COLS)
    r = zf[:, :RW_WIDTH]
    k = zf[:, RW_WIDTH:2 * RW_WIDTH]
    v = zf[:, 2 * RW_WIDTH:3 * RW_WIDTH]
    lo = zf[:, 3 * RW_WIDTH:]
    act = jnp.where(lora_lane < DECAY_LORA, jnp.tanh(lo),
                    jnp.where(lora_lane < DECAY_LORA + ICLR_LORA, lo, _sigmoid(lo)))
    act_hi, act_lo = _split(act)
    w_in = w0_ref[...] + _dot(act_hi, lw_ref[:, :RW_WIDTH]) + _dot(act_lo, lw_ref[:, :RW_WIDTH])
    proj = _dot(act_hi, lw_ref[:, RW_WIDTH:])
    logw = -DECAY_SCALE * _sigmoid(w_in)
    a_sig = _sigmoid(a0_ref[...] + proj[:, :RW_WIDTH])
    kk = k * kk_ref[...]
    kk = kk * lax.rsqrt(jnp.maximum(_head_sums(kk * kk, bd), KK_NORM_FLOOR_SQ))
    kd = k * (1.0 + (a_sig - 1.0) * ka_ref[...])
    a_vec = -kk
    b_vec = kk * a_sig
    ri = lax.broadcasted_iota(jnp.int32, (rows, rows), 0)
    rj = lax.broadcasted_iota(jnp.int32, (rows, rows), 1)
    ordered = (rj >= ri) if reverse else (rj <= ri)
    tri = jnp.where((ri // c) == (rj // c), jnp.where(ordered, 1.0, 0.0), 0.0).astype(BF16)
    cum = _dot2_lhs01(tri, logw)
    edge = 0 if reverse else c - 1
    tot = jnp.concatenate([jnp.broadcast_to(cum[b * c + edge:b * c + edge + 1, :], (c, RW_WIDTH))
                           for b in range(bt)], axis=0)
    w_inv = jnp.exp(-cum)
    w_end = jnp.exp(tot - cum)
    w_tot = jnp.exp(tot)
    full_ops = dict(r_t=r * jnp.exp(cum), a_t=a_vec * jnp.exp(cum - logw), b_t=b_vec * w_inv, k_t=kd * w_inv,
                    b_h=b_vec * w_end, k_h=kd * w_end, v=v)
    ops, states = [], []
    for b in range(bt):
        for pi in range(N_PAIRS):
            ln = slice(pi * PAIR, (pi + 1) * PAIR)
            unit = {name: t[b * c:(b + 1) * c, ln] for name, t in full_ops.items()}
            unit["w_tot"] = w_tot[b * c:b * c + 1, ln]
            ops.append(unit)
            states.append(state_ref[b, pi])
    new_states, ys = _units_chunk(states, ops, m)
    for b in range(bt):
        for pi in range(N_PAIRS):
            state_ref[b, pi] = new_states[b * N_PAIRS + pi]
    y = jnp.concatenate([jnp.concatenate(ys[b * N_PAIRS:(b + 1) * N_PAIRS], axis=1) for b in range(bt)], axis=0)
    if not reverse:
        out_ref[...] = y.reshape(bt, c, RW_WIDTH)
    else:
        y = y + yf_ref[...].reshape(rows, RW_WIDTH)
        a_other = _sigmoid(a0o_ref[...] + proj[:, RW_WIDTH:2 * RW_WIDTH])
        kd_sum = kd + k * (1.0 + (a_other - 1.0) * ka_ref[...])
        bonus = _head_sums(r * kd_sum * rk_ref[...], bd) * v
        inv_n = 1.0 / RW_HEAD_DIM
        mean = _head_sums(y, bd) * inv_n
        yc = y - mean
        var = _head_sums(yc * yc, bd) * inv_n
        yn = yc * lax.rsqrt(var + GN_EPS) * gng_ref[...] + gnb_ref[...]
        g = proj[:, 2 * RW_WIDTH:]
        out_ref[...] = ((yn + bonus) * g).astype(BF16).reshape(bt, c, RW_WIDTH)


def _rwkv_pass(z3, yf, p, reverse, bt):
    b_sz, t_len, _ = z3.shape
    c = RW_CHUNK
    n_chunks = t_len // c

    def cidx(ci):
        return (n_chunks - 1 - ci) if reverse else ci

    full = lambda shape: pl.BlockSpec(shape, lambda bi, ci: (0,) * len(shape))
    row512 = full((1, RW_WIDTH))
    d = "rev" if reverse else "fwd"
    n_proj = p["lora_" + d].shape[1]
    in_specs = [pl.BlockSpec((bt, c, Z_COLS), lambda bi, ci: (bi, cidx(ci), 0))]
    args = [z3]
    if reverse:
        in_specs.append(pl.BlockSpec((bt, c, RW_WIDTH), lambda bi, ci: (bi, cidx(ci), 0)))
        args.append(yf)
    in_specs += [full((LORA_PAD, n_proj)), row512, row512]
    args += [p["lora_" + d], p["w0_" + d], p["a0_" + d]]
    if reverse:
        in_specs.append(row512)
        args.append(p["a0_fwd"])
    in_specs += [row512, row512]
    args += [p["rw_k_k"], p["rw_k_a"]]
    if reverse:
        in_specs.append(row512)
        args.append(p["rw_r_k"])
    in_specs.append(full((PAIR, PAIR)))
    args.append(p["head_ones"])
    if reverse:
        in_specs += [row512, row512]
        args += [p["rw_gn_g"], p["rw_gn_b"]]
    return pl.pallas_call(
        functools.partial(_rwkv_body, reverse, bt),
        grid=(b_sz // bt, n_chunks),
        in_specs=in_specs,
        out_specs=pl.BlockSpec((bt, c, RW_WIDTH), lambda bi, ci: (bi, cidx(ci), 0)),
        out_shape=jax.ShapeDtypeStruct((b_sz, t_len, RW_WIDTH), BF16 if reverse else F32),
        scratch_shapes=[pltpu.VMEM((bt, N_PAIRS, PAIR, PAIR), F32)],
        compiler_params=pltpu.CompilerParams(vmem_limit_bytes=VMEM_LIMIT),
        name="rwkv_" + d,
    )(*args)


def _mix_out_body(x_ref, gm_ref, rw_ref, wo_ref, g2_ref, rw_hi_ref, rw_lo_ref, rb_ref, earlier_ref,
                  x1_ref, h2_ref, route_ref, route_t_ref, counts_ref, carry_ref):
    i = pl.program_id(0)

    @pl.when(i == 0)
    def _():
        carry_ref[...] = jnp.zeros_like(carry_ref)

    sub = earlier_ref.shape[0]
    subs = range(x_ref.shape[0] // sub)
    rows = [slice(s * sub, (s + 1) * sub) for s in subs]
    x1 = [x_ref[r, :] + _dot(gm_ref[r, :], wo_ref[:GM_WIDTH, :]) + _dot(rw_ref[r, :], wo_ref[GM_WIDTH:, :])
          for r in rows]
    h2 = [x1[s] * lax.rsqrt(jnp.mean(x1[s] * x1[s], axis=-1, keepdims=True) + RMS_EPS) * g2_ref[...] for s in subs]
    for s in subs:
        x1_ref[rows[s], :] = x1[s]
        h2_ref[rows[s], :] = _pack_bf16_halves(h2[s])
    split = [_split(h) for h in h2]
    logits = [_dot_nt(rw_hi_ref[...], hi) + _dot_nt(rw_hi_ref[...], lo) + _dot_nt(rw_lo_ref[...], hi) + rb_ref[...]
              for hi, lo in split]
    expert = lax.broadcasted_iota(jnp.int32, (N_EXPERTS, sub), 0).astype(F32)
    work = logits
    vals, ids, hits = [[] for _ in subs], [[] for _ in subs], [[] for _ in subs]
    onehot = [jnp.zeros((N_EXPERTS, sub), F32) for _ in subs]
    for _ in range(TOP_K):
        for s in subs:
            mx = jnp.max(work[s], axis=0, keepdims=True)
            idx = jnp.min(jnp.where(work[s] == mx, expert, float(N_EXPERTS)), axis=0, keepdims=True)
            hit = expert == idx
            vals[s].append(mx)
            ids[s].append(idx)
            hits[s].append(hit)
            onehot[s] = jnp.where(hit, 1.0, onehot[s])
        work = [jnp.where(hits[s][-1], -jnp.inf, work[s]) for s in subs]
    earlier_counts = [_dot(onehot[s].astype(BF16), earlier_ref[...]) for s in subs]
    carry = carry_ref[:, 0:1]
    for s in subs:
        exps = [jnp.exp(vk - vals[s][0]) for vk in vals[s]]
        denom = exps[0] + exps[1] + exps[2] + exps[3]
        ranks = carry + earlier_counts[s]
        rank_rows = [jnp.sum(jnp.where(hit, ranks, 0.0), axis=0, keepdims=True) for hit in hits[s]]
        route_t = jnp.concatenate(ids[s] + [e / denom for e in exps] + rank_rows
                                  + [jnp.zeros((ROUTE_ROWS - 3 * TOP_K, sub), F32)], axis=0)
        route_t_ref[:, rows[s]] = route_t
        route_ref[rows[s], :] = jnp.concatenate([route_t, jnp.zeros((LANES - ROUTE_ROWS, sub), F32)], axis=0).T
        carry = carry + jnp.sum(onehot[s], axis=1, keepdims=True)
    carry_ref[...] = jnp.broadcast_to(carry, carry_ref.shape)
    counts_ref[...] = jnp.broadcast_to(carry, counts_ref.shape)


def _mix_out(x2d, gm, rw, p, tm):
    n = x2d.shape[0]
    full = lambda shape: pl.BlockSpec(shape, lambda i: (0,) * len(shape))
    tile = lambda w: pl.BlockSpec((tm, w), lambda i: (i, 0))
    sub = min(MIX_OUT_SUB, tm)
    earlier = (jnp.arange(sub)[:, None] < jnp.arange(sub)[None, :]).astype(BF16)
    return pl.pallas_call(
        _mix_out_body,
        grid=(n // tm,),
        in_specs=[tile(D_MODEL), tile(GM_WIDTH), tile(RW_WIDTH), full((D_MODEL, D_MODEL)), full((1, D_MODEL)),
                  full((N_EXPERTS, D_MODEL)), full((N_EXPERTS, D_MODEL)), full((N_EXPERTS, 1)), full((sub, sub))],
        out_specs=[tile(D_MODEL), tile(D_MODEL // 2), tile(LANES), pl.BlockSpec((ROUTE_ROWS, tm), lambda i: (0, i)),
                   full((N_EXPERTS, LANES))],
        out_shape=[
            jax.ShapeDtypeStruct((n, D_MODEL), F32),
            jax.ShapeDtypeStruct((n, D_MODEL // 2), jnp.int32),
            jax.ShapeDtypeStruct((n, LANES), F32),
            jax.ShapeDtypeStruct((ROUTE_ROWS, n), F32),
            jax.ShapeDtypeStruct((N_EXPERTS, LANES), F32),
        ],
        scratch_shapes=[pltpu.VMEM((N_EXPERTS, LANES), F32)],
        compiler_params=pltpu.CompilerParams(vmem_limit_bytes=VMEM_LIMIT),
        name="mix_out",
    )(x2d, gm, rw, p["w_out"], p["norm2_g"], p["router_hi"], p["router_lo"], p["router_b"], earlier)


def _sc_gather_rows(table, idx):
    n_idx = idx.shape[0]
    width = table.shape[1]
    n_workers = SC_CORES * SC_SUBCORES
    per_worker = n_idx // n_workers
    n_windows = per_worker // SC_WINDOW
    assert per_worker * n_workers == n_idx and n_windows * SC_WINDOW == per_worker
    mesh = plsc.VectorSubcoreMesh(core_axis_name="c", subcore_axis_name="s")

    @functools.partial(
        pl.kernel, mesh=mesh,
        out_type=jax.ShapeDtypeStruct((n_idx, width), table.dtype),
        scratch_types=[
            pltpu.VMEM((SC_WINDOW,), jnp.int32),
            pltpu.VMEM((SC_WINDOW, width), table.dtype),
            pltpu.SemaphoreType.DMA,
        ],
        name="sc_gather_rows",
    )
    def gather(table_hbm, idx_hbm, out_hbm, idx_v, rows_v, sem):
        worker = lax.axis_index("s") * SC_CORES + lax.axis_index("c")
        base = worker * per_worker

        @pl.loop(0, n_windows)
        def _(j):
            off = base + j * SC_WINDOW
            pltpu.sync_copy(idx_hbm.at[pl.ds(off, SC_WINDOW)], idx_v)
            pltpu.async_copy(table_hbm.at[idx_v], rows_v, sem).wait()
            pltpu.sync_copy(rows_v, out_hbm.at[pl.ds(off, SC_WINDOW)])

    return gather(table, idx)


def _sc_scatter_rows(src, dest_w, n_out):
    n, width = src.shape
    n_windows, top_k, window = dest_w.shape
    n_workers = SC_CORES * SC_SUBCORES
    per_worker = n_windows // n_workers
    assert n_windows * window == n and per_worker * n_workers == n_windows
    mesh = plsc.VectorSubcoreMesh(core_axis_name="c", subcore_axis_name="s")

    @functools.partial(
        pl.kernel, mesh=mesh,
        out_type=jax.ShapeDtypeStruct((n_out, width), src.dtype),
        scratch_types=[
            pltpu.VMEM((top_k, window), jnp.int32),
            pltpu.VMEM((window, width), src.dtype),
        ],
        name="sc_scatter_rows",
    )
    def scatter(src_hbm, dest_hbm, out_hbm, idx_v, rows_v):
        worker = lax.axis_index("s") * SC_CORES + lax.axis_index("c")

        @pl.loop(0, per_worker)
        def _(j):
            g = worker * per_worker + j
            pltpu.sync_copy(dest_hbm.at[g], idx_v)
            pltpu.sync_copy(src_hbm.at[pl.ds(g * window, window)], rows_v)
            for k in range(top_k):
                pltpu.sync_copy(rows_v, out_hbm.at[idx_v.at[k]])

    return scatter(src, dest_w)


def _weight_copies(wgu_hbm, wd_hbm, wgu_buf, wd_buf, sem, expert, slot):
    return (pltpu.make_async_copy(wgu_hbm.at[expert], wgu_buf.at[slot], sem.at[0, slot]),
            pltpu.make_async_copy(wd_hbm.at[expert], wd_buf.at[slot], sem.at[1, slot]))


def _experts_body(blk_e_ref, n_used_ref, valid_ref, slot_ref, next_e_ref, x_ref, wgu_hbm, bgu_ref, wd_hbm, bd_ref,
                  out_ref, wgu_bf, wd_bf, wgu_buf, wd_buf, sem):
    i = pl.program_id(0)
    n_used = n_used_ref[0]
    expert = blk_e_ref[i]
    slot = slot_ref[i]
    copies = functools.partial(_weight_copies, wgu_hbm, wd_hbm, wgu_buf, wd_buf, sem)

    @pl.when(i == 0)
    def _():
        for cp in copies(expert, slot):
            cp.start()

    @pl.when((i < n_used) & ((i == 0) | (expert != blk_e_ref[jnp.maximum(i - 1, 0)])))
    def _():
        for cp in copies(expert, slot):
            cp.wait()
        wgu_bf[...] = wgu_buf[slot].astype(BF16)
        wd_bf[...] = wd_buf[slot].astype(BF16)
        following = next_e_ref[i]

        @pl.when(following >= 0)
        def _():
            for cp in copies(following, 1 - slot):
                cp.start()

    valid = jnp.where(i < n_used, valid_ref[i], 0)

    def expert_mlp(n_rows):
        row = lax.broadcasted_iota(jnp.int32, (n_rows, 1), 0)
        x_left, x_right = _unpack_bf16_halves(jnp.where(row < valid, x_ref[:n_rows, :], 0))
        half = D_MODEL // 2
        gu = (_dot(x_left.astype(BF16), wgu_bf[:half, :]) + _dot(x_right.astype(BF16), wgu_bf[half:, :])
              + bgu_ref[0])
        gate = jnp.minimum(gu[:, :D_EXPERT], SWIGLU_LIMIT)
        up = jnp.clip(gu[:, D_EXPERT:], -SWIGLU_LIMIT, SWIGLU_LIMIT)
        act = gate * _sigmoid(gate * SWIGLU_ALPHA) * (up + 1.0)
        out_ref[:n_rows, :] = _pack_bf16_halves(_dot(act.astype(BF16), wd_bf[...]) + bd_ref[0])

    half_rows = MOE_ROWS // 2

    @pl.when(valid > half_rows)
    def _():
        expert_mlp(MOE_ROWS)

    @pl.when((valid > 0) & (valid <= half_rows))
    def _():
        expert_mlp(half_rows)
        out_ref[half_rows:, :] = jnp.zeros((MOE_ROWS - half_rows, D_MODEL // 2), jnp.int32)

    @pl.when(valid == 0)
    def _():
        out_ref[...] = jnp.zeros_like(out_ref)


def _experts(x_rows, blk_e, n_used, valid, slot, next_e, p):
    n_blocks = blk_e.shape[0]
    grid_spec = pltpu.PrefetchScalarGridSpec(
        num_scalar_prefetch=5,
        grid=(n_blocks,),
        in_specs=[
            pl.BlockSpec((MOE_ROWS, D_MODEL // 2), lambda i, be, nu, *_: (jnp.minimum(i, nu[0] - 1), 0)),
            pl.BlockSpec(memory_space=pl.ANY),
            pl.BlockSpec((1, 1, 2 * D_EXPERT), lambda i, be, *_: (be[i], 0, 0)),
            pl.BlockSpec(memory_space=pl.ANY),
            pl.BlockSpec((1, 1, D_MODEL), lambda i, be, *_: (be[i], 0, 0)),
        ],
        out_specs=pl.BlockSpec((MOE_ROWS, D_MODEL // 2), lambda i, *_: (i, 0)),
        scratch_shapes=[
            pltpu.VMEM((D_MODEL, 2 * D_EXPERT), BF16),
            pltpu.VMEM((D_EXPERT, D_MODEL), BF16),
            pltpu.VMEM((2, D_MODEL, 2 * D_EXPERT), F32),
            pltpu.VMEM((2, D_EXPERT, D_MODEL), F32),
            pltpu.SemaphoreType.DMA((2, 2)),
        ],
    )
    return pl.pallas_call(
        _experts_body,
        grid_spec=grid_spec,
        out_shape=jax.ShapeDtypeStruct((n_blocks * MOE_ROWS, D_MODEL // 2), jnp.int32),
        compiler_params=pltpu.CompilerParams(vmem_limit_bytes=VMEM_LIMIT),
        name="experts",
    )(blk_e, n_used, valid, slot, next_e, x_rows, p["w_gu"], p["b_gu"], p["w_down"], p["b_down"])


def _combine_body(yg_ref, x1_ref, gates_ref, fg_ref, out_ref):
    gates = gates_ref[...]
    half = D_MODEL // 2
    acc_left = x1_ref[:, :half]
    acc_right = x1_ref[:, half:]
    for kk in range(TOP_K):
        y_left, y_right = _unpack_bf16_halves(yg_ref[kk])
        gate = gates[:, TOP_K + kk:TOP_K + kk + 1]
        acc_left = acc_left + gate * y_left
        acc_right = acc_right + gate * y_right
    acc = jnp.concatenate([acc_left, acc_right], axis=1)
    out_ref[...] = acc * lax.rsqrt(jnp.mean(acc * acc, axis=-1, keepdims=True) + RMS_EPS) * fg_ref[...]


def _combine(yg, x1, gates, final_g, tm):
    n = x1.shape[0]
    return pl.pallas_call(
        _combine_body,
        grid=(n // tm,),
        in_specs=[
            pl.BlockSpec((TOP_K, tm, D_MODEL // 2), lambda i: (0, i, 0)),
            pl.BlockSpec((tm, D_MODEL), lambda i: (i, 0)),
            pl.BlockSpec((tm, LANES), lambda i: (i, 0)),
            pl.BlockSpec((1, D_MODEL), lambda i: (0, 0)),
        ],
        out_specs=pl.BlockSpec((tm, D_MODEL), lambda i: (i, 0)),
        out_shape=jax.ShapeDtypeStruct((n, D_MODEL), F32),
        compiler_params=pltpu.CompilerParams(vmem_limit_bytes=VMEM_LIMIT),
        name="combine",
    )(yg, x1, gates, final_g)


def _prepare(norm1_g, w_in, rw_mu, gm_ln_g, gm_ln_b, gm_ws, gm_bs, rw_w0, rw_w2, rw_a0, rw_a2, rw_g2,
             rw_k_k, rw_k_a, rw_r_k, rw_gn_g, rw_gn_b, w_out, norm2_g, router_w, router_b,
             w_gu, b_gu, w_down, b_down, final_g):
    l = 0
    row = lambda t: t.reshape(1, -1).astype(F32)
    p = {}
    p["norm1_g"] = row(norm1_g[l])
    w = w_in[l]
    p["w_gm"] = w[:, :2 * GM_WIDTH].astype(BF16)
    pad_cols = LORA_PAD - LORA_COLS
    p["w_z"] = jnp.pad(w[:, 2 * GM_WIDTH:], ((0, 0), (0, pad_cols))).astype(BF16)
    mu = jnp.pad(row(rw_mu[l]), ((0, 0), (0, pad_cols)))
    p["rw_mu"] = jnp.concatenate([1.0 - mu, 0.5 * mu], axis=0)
    p["gm_ln_g"] = row(gm_ln_g[l])
    p["gm_ln_b"] = row(gm_ln_b[l])
    p["gm_ws"] = gm_ws[l].astype(BF16)
    p["gm_bs"] = jnp.broadcast_to(gm_bs[l][:, :, None], (GM_HEADS, GM_CHUNK, GM_HEAD_DIM)).astype(F32)
    zeros = lambda r, c: jnp.zeros((r, c), F32)
    o_a = DECAY_LORA
    o_g = DECAY_LORA + ICLR_LORA

    def lora_matrix(d, with_epilogue):
        blocks = [jnp.concatenate([rw_w2[l, d], zeros(LORA_PAD - DECAY_LORA, RW_WIDTH)], axis=0),
                  jnp.concatenate([zeros(o_a, RW_WIDTH), rw_a2[l, d], zeros(LORA_PAD - o_g, RW_WIDTH)], axis=0)]
        if with_epilogue:
            blocks.append(jnp.concatenate([zeros(o_a, RW_WIDTH), rw_a2[l, 1 - d], zeros(LORA_PAD - o_g, RW_WIDTH)],
                                          axis=0))
            blocks.append(jnp.concatenate([zeros(o_g, RW_WIDTH), rw_g2[l], zeros(LORA_PAD - LORA_COLS, RW_WIDTH)],
                                          axis=0))
        return jnp.concatenate(blocks, axis=1)

    for d, name in ((0, "fwd"), (1, "rev")):
        p["lora_" + name] = lora_matrix(d, with_epilogue=(d == 1)).astype(BF16)
        p["w0_" + name] = row(rw_w0[l, d])
        p["a0_" + name] = row(rw_a0[l, d])
    p["rw_k_k"] = row(rw_k_k[l])
    p["rw_k_a"] = row(rw_k_a[l])
    p["rw_r_k"] = row(rw_r_k[l])
    p["rw_gn_g"] = row(rw_gn_g[l])
    p["rw_gn_b"] = row(rw_gn_b[l])
    ch = jnp.arange(PAIR) // RW_HEAD_DIM
    p["head_ones"] = (ch[:, None] == ch[None, :]).astype(BF16)
    p["w_out"] = w_out[l].astype(BF16)
    p["norm2_g"] = row(norm2_g[l])
    rw_t = router_w[l].astype(F32).T
    hi = rw_t.astype(BF16)
    p["router_hi"] = hi
    p["router_lo"] = (rw_t - hi.astype(F32)).astype(BF16)
    p["router_b"] = router_b[l].astype(F32).reshape(N_EXPERTS, 1)
    p["w_gu"] = w_gu[l]
    p["b_gu"] = b_gu[l].reshape(N_EXPERTS, 1, 2 * D_EXPERT).astype(F32)
    p["w_down"] = w_down[l]
    p["b_down"] = b_down[l].reshape(N_EXPERTS, 1, D_MODEL).astype(F32)
    p["final_g"] = row(final_g)
    return p


def _pick_tile(n, want):
    t = want
    while n % t:
        t //= 2
    return t


def _encoder(x, p):
    b_sz, t_len, d = x.shape
    n = b_sz * t_len
    x2d = x.reshape(n, d)
    gm, z = _mix_in(x2d, p, _pick_tile(t_len, 1024), t_len)
    z3 = z.reshape(b_sz, t_len, Z_COLS)
    yf = _rwkv_pass(z3, None, p, reverse=False, bt=RW_BATCH_TILE)
    rw = _rwkv_pass(z3, yf, p, reverse=True, bt=RW_BATCH_TILE)
    tmo = _pick_tile(n, 1024)
    x1, h2, route, route_t, counts = _mix_out(x2d, gm, rw.reshape(n, RW_WIDTH), p, tmo)
    ids = route_t[:TOP_K].astype(jnp.int32)
    ranks = route_t[2 * TOP_K:3 * TOP_K].astype(jnp.int32)
    cnt = counts[:, 0].astype(jnp.int32)
    padded = (cnt + MOE_ROWS - 1) // MOE_ROWS * MOE_ROWS
    pad_end = jnp.cumsum(padded)
    pad_start = pad_end - padded
    expert_ids = jnp.arange(N_EXPERTS, dtype=jnp.int32)
    start_of = jnp.sum(jnp.where(ids[..., None] == expert_ids, pad_start, 0), axis=-1)
    dest = start_of + ranks
    n_blocks = n * TOP_K // MOE_ROWS + N_EXPERTS
    blk_start = jnp.arange(n_blocks, dtype=jnp.int32) * MOE_ROWS
    blk_e = jnp.minimum(jnp.sum((pad_end[None, :] <= blk_start[:, None]).astype(jnp.int32), axis=1), N_EXPERTS - 1)
    n_used = (pad_end[-1] // MOE_ROWS).astype(jnp.int32).reshape(1)
    blk_onehot = blk_e[:, None] == expert_ids
    blk_cnt = jnp.sum(jnp.where(blk_onehot, cnt, 0), axis=-1)
    blk_first = jnp.sum(jnp.where(blk_onehot, pad_start, 0), axis=-1)
    valid = jnp.clip(blk_cnt - (blk_start - blk_first), 0, MOE_ROWS).astype(jnp.int32)
    window = min(SC_SCATTER_WINDOW, n // (SC_CORES * SC_SUBCORES))
    dest_w = dest.reshape(TOP_K, n // window, window).transpose(1, 0, 2)
    x_rows = _sc_scatter_rows(h2, dest_w, n_blocks * MOE_ROWS)
    present = cnt > 0
    slot_of_expert = (jnp.cumsum(present.astype(jnp.int32)) - 1) % 2
    later = present[None, :] & (expert_ids[None, :] > expert_ids[:, None])
    next_present = jnp.min(jnp.where(later, expert_ids[None, :], N_EXPERTS), axis=1)
    next_present = jnp.where(next_present < N_EXPERTS, next_present, -1)
    slot = jnp.sum(jnp.where(blk_onehot, slot_of_expert, 0), axis=-1).astype(jnp.int32)
    next_e = jnp.sum(jnp.where(blk_onehot, next_present, 0), axis=-1).astype(jnp.int32)
    y_rows = _experts(x_rows, blk_e, n_used, valid, slot, next_e, p)
    yg = _sc_gather_rows(y_rows, dest.reshape(-1)).reshape(TOP_K, n, D_MODEL // 2)
    out = _combine(yg, x1, route, p["final_g"], _pick_tile(n, 1024))
    return out.reshape(b_sz, t_len, d)


def kernel(x_prompt, x_sample, norm1_g, w_in, rw_mu, gm_ln_g, gm_ln_b, gm_ws, gm_bs, rw_w0, rw_w2, rw_a0, rw_a2,
           rw_g2, rw_k_k, rw_k_a, rw_r_k, rw_gn_g, rw_gn_b, w_out, norm2_g, router_w, router_b, w_gu, b_gu,
           w_down, b_down, final_g):
    p = _prepare(norm1_g, w_in, rw_mu, gm_ln_g, gm_ln_b, gm_ws, gm_bs, rw_w0, rw_w2, rw_a0, rw_a2, rw_g2,
                 rw_k_k, rw_k_a, rw_r_k, rw_gn_g, rw_gn_b, w_out, norm2_g, router_w, router_b,
                 w_gu, b_gu, w_down, b_down, final_g)
    return (_encoder(x_prompt, p), _encoder(x_sample, p))
```

```python
import functools

import jax
import jax.numpy as jnp
from jax import lax
from jax.experimental import pallas as pl
from jax.experimental.pallas import tpu as pltpu
from jax.experimental.pallas import tpu_sc as plsc

F32 = jnp.float32
BF16 = jnp.bfloat16

D_MODEL = 1024
GM_WIDTH = 512
RW_WIDTH = 512
GM_HEADS = 4
GM_HEAD_DIM = 128
GM_CHUNK = 128
RW_HEAD_DIM = 64
DECAY_LORA = 32
ICLR_LORA = 32
GATE_LORA = 96
LORA_COLS = DECAY_LORA + ICLR_LORA + GATE_LORA
LORA_PAD = 256
Z_COLS = 3 * RW_WIDTH + LORA_PAD
Z_CHUNK = 256
N_EXPERTS = 32
TOP_K = 4
D_EXPERT = 1024
SWIGLU_LIMIT = 7.0
SWIGLU_ALPHA = 1.702
RMS_EPS = 1e-5
LN_EPS = 1e-5
GN_EPS = 64e-5

LANES = 128
SUBLANES = 8
VMEM_LIMIT = 56 * 1024 * 1024

RW_CHUNK = 64
PAIR = 2 * RW_HEAD_DIM
N_PAIRS = RW_WIDTH // PAIR
RW_BATCH_TILE = 4
MIX_OUT_SUB = 128
MOE_ROWS = 512
SC_CORES = 2
SC_SUBCORES = 16
SC_WINDOW = 64
SC_SCATTER_WINDOW = 128
ROUTE_ROWS = 16
DECAY_SCALE = 0.6065306597126334
KK_NORM_FLOOR_SQ = 1e-24


def _dot(a, b):
    return jnp.dot(a, b, preferred_element_type=F32)


def _dot_nt(a, b):
    return lax.dot_general(a, b, (((1,), (1,)), ((), ())), preferred_element_type=F32)


def _split(x):
    hi = x.astype(BF16)
    lo = (x - hi.astype(F32)).astype(BF16)
    return hi, lo


def _head_sums(x, pair_ones):
    rows = x.shape[0]
    n_tiles = x.shape[1] // LANES
    stacked = jnp.concatenate([x[:, t * LANES:(t + 1) * LANES] for t in range(n_tiles)], axis=0)
    sums = _dot(stacked.astype(BF16), pair_ones)
    return jnp.concatenate([sums[t * rows:(t + 1) * rows] for t in range(n_tiles)], axis=1)


def _dot2_lhs01(a01, b):
    b_hi, b_lo = _split(b)
    return _dot(a01, b_hi) + _dot(a01, b_lo)


def _pack_bf16_halves(x):
    w = x.shape[1] // 2
    hi = lax.bitcast_convert_type(x[:, :w].astype(BF16).astype(F32), jnp.int32)
    lo = lax.bitcast_convert_type(x[:, w:].astype(BF16).astype(F32), jnp.int32)
    return hi | lax.shift_right_logical(lo, 16)


def _unpack_bf16_halves(words):
    left = lax.bitcast_convert_type(words & jnp.int32(-65536), F32)
    right = lax.bitcast_convert_type(lax.shift_left(words, 16), F32)
    return left, right


def _gelu_tanh(x):
    return 0.5 * x * (1.0 + jnp.tanh(0.7978845608028654 * (x + 0.044715 * (x * x * x))))


def _sigmoid(x):
    return 0.5 * jnp.tanh(0.5 * x) + 0.5


def _mix_in_body(tiles_per_seq, x_ref, xp_ref, xn_ref, g1_ref, wgm_ref, wz_ref, mu_ref, lng_ref, lnb_ref, ws_ref,
                 bs_ref, gm_ref, z_ref):
    i = pl.program_id(0)
    tm = x_ref.shape[0]

    def normed(x):
        return (x * lax.rsqrt(jnp.mean(x * x, axis=-1, keepdims=True) + RMS_EPS) * g1_ref[...]).astype(BF16)

    hb = normed(x_ref[...])
    uv = _dot(hb, wgm_ref[...])
    hb_halo = normed(jnp.concatenate([xp_ref[...], xn_ref[...]], axis=0))
    has_prev = i % tiles_per_seq > 0
    has_next = i % tiles_per_seq < tiles_per_seq - 1
    row8 = lax.broadcasted_iota(jnp.int32, (SUBLANES, Z_CHUNK), 0)
    for j in range(Z_COLS // Z_CHUNK):
        cols = slice(j * Z_CHUNK, (j + 1) * Z_CHUNK)
        z = _dot(hb, wz_ref[:, cols])
        halo = _dot(hb_halo, wz_ref[:, cols])
        prev_row = jnp.where(has_prev, halo[SUBLANES - 1:SUBLANES, :], 0.0)
        next_row = jnp.where(has_next, halo[SUBLANES:SUBLANES + 1, :], 0.0)
        down = pltpu.roll(z, 1, axis=0)
        up = pltpu.roll(z, tm - 1, axis=0)
        prev = jnp.concatenate([jnp.where(row8 == 0, prev_row, down[:SUBLANES]), down[SUBLANES:]], axis=0)
        nxt = jnp.concatenate([up[:tm - SUBLANES], jnp.where(row8 == SUBLANES - 1, next_row, up[tm - SUBLANES:])],
                              axis=0)
        z_ref[:, cols] = z * mu_ref[0:1, cols] + (prev + nxt) * mu_ref[1:2, cols]
    u = _gelu_tanh(uv[:, :GM_WIDTH])
    v = _gelu_tanh(uv[:, GM_WIDTH:])
    mean = jnp.mean(v, axis=-1, keepdims=True)
    vc = v - mean
    var = jnp.mean(vc * vc, axis=-1, keepdims=True)
    vn = (vc * lax.rsqrt(var + LN_EPS) * lng_ref[...] + lnb_ref[...]).astype(BF16)
    for c in range(tm // GM_CHUNK):
        rows = slice(c * GM_CHUNK, (c + 1) * GM_CHUNK)
        for hd in range(GM_HEADS):
            cols = slice(hd * GM_HEAD_DIM, (hd + 1) * GM_HEAD_DIM)
            mixed = _dot(ws_ref[hd], vn[rows, cols]) + bs_ref[hd]
            gm_ref[rows, cols] = (u[rows, cols] * mixed).astype(BF16)


def _mix_in(x2d, p, tm, t_len):
    n = x2d.shape[0]
    assert t_len % tm == 0
    halo_per_tile = tm // SUBLANES
    n_halo = n // SUBLANES
    full = lambda shape: pl.BlockSpec(shape, lambda i: (0,) * len(shape))
    return pl.pallas_call(
        functools.partial(_mix_in_body, t_len // tm),
        grid=(n // tm,),
        in_specs=[
            pl.BlockSpec((tm, D_MODEL), lambda i: (i, 0)),
            pl.BlockSpec((SUBLANES, D_MODEL), lambda i: (jnp.maximum(i * halo_per_tile - 1, 0), 0)),
            pl.BlockSpec((SUBLANES, D_MODEL), lambda i: (jnp.minimum((i + 1) * halo_per_tile, n_halo - 1), 0)),
            full((1, D_MODEL)),
            full((D_MODEL, 2 * GM_WIDTH)),
            full((D_MODEL, Z_COLS)),
            full((2, Z_COLS)),
            full((1, GM_WIDTH)),
            full((1, GM_WIDTH)),
            full((GM_HEADS, GM_CHUNK, GM_CHUNK)),
            full((GM_HEADS, GM_CHUNK, GM_HEAD_DIM)),
        ],
        out_specs=[
            pl.BlockSpec((tm, GM_WIDTH), lambda i: (i, 0)),
            pl.BlockSpec((tm, Z_COLS), lambda i: (i, 0)),
        ],
        out_shape=[
            jax.ShapeDtypeStruct((n, GM_WIDTH), BF16),
            jax.ShapeDtypeStruct((n, Z_COLS), F32),
        ],
        compiler_params=pltpu.CompilerParams(vmem_limit_bytes=VMEM_LIMIT),
        name="mix_in",
    )(x2d, x2d, x2d, p["norm1_g"], p["w_gm"], p["w_z"], p["rw_mu"], p["gm_ln_g"], p["gm_ln_b"], p["gm_ws"],
      p["gm_bs"])


def _rwkv_masks(reverse):
    c = RW_CHUNK
    lane = lax.broadcasted_iota(jnp.int32, (c, PAIR), 1)
    head0 = lane < RW_HEAD_DIM
    lane2 = lax.broadcasted_iota(jnp.int32, (c, 2 * c), 1)
    left = lane2 < c
    ti2 = lax.broadcasted_iota(jnp.int32, (c, 2 * c), 0)
    tj2 = jnp.where(left, lane2, lane2 - c)
    incl2 = (tj2 >= ti2) if reverse else (tj2 <= ti2)
    strict2 = (tj2 > ti2) if reverse else (tj2 < ti2)
    eye2 = jnp.where(tj2 == ti2, 1.0, 0.0).astype(F32)
    si = lax.broadcasted_iota(jnp.int32, (PAIR, PAIR), 0)
    sj = lax.broadcasted_iota(jnp.int32, (PAIR, PAIR), 1)
    same_head = (si // RW_HEAD_DIM) == (sj // RW_HEAD_DIM)
    return dict(incl2=incl2, strict2=strict2, head0=head0, left=left, eye2=eye2, same_head=same_head)


def _stack_heads(x, head0):
    return jnp.concatenate([jnp.where(head0, x, 0.0), jnp.where(head0, 0.0, x)], axis=0)


def _block_diag(x, left):
    return jnp.concatenate([jnp.where(left, x, 0.0), jnp.where(left, 0.0, x)], axis=0)


def _units_chunk(states, ops, m):
    c = RW_CHUNK
    units = range(len(ops))
    bf = lambda t: t.astype(BF16)
    head0, left = m["head0"], m["left"]
    ar = [bf(jnp.concatenate([o["a_t"], o["r_t"]], axis=0)) for o in ops]
    rstack = [bf(jnp.concatenate([_stack_heads(o["b_t"], head0), _stack_heads(o["k_t"], head0)], axis=0))
              for o in ops]
    acat = [_dot_nt(ar[i], rstack[i]) for i in units]
    sa = [_dot_nt(ar[i], bf(states[i])) for i in units]
    a_ab = [jnp.where(m["strict2"], acat[i][:c, :2 * c], 0.0) for i in units]
    a_ak = [bf(jnp.where(m["strict2"], acat[i][:c, 2 * c:], 0.0)) for i in units]
    a_r = [bf(jnp.concatenate([jnp.where(m["incl2"], acat[i][c:, :2 * c], 0.0),
                               jnp.where(m["incl2"], acat[i][c:, 2 * c:], 0.0)], axis=1)) for i in units]
    p = [m["eye2"] + a_ab[i] for i in units]
    lk = [_dot(bf(a_ab[i]), bf(_block_diag(a_ab[i], left))) for i in units]
    vstack = [bf(_stack_heads(o["v"], head0)) for o in ops]
    rhs = [sa[i][:c] + _dot(a_ak[i], vstack[i]) for i in units]
    power = 2
    while 2 * power < c:
        res = [_dot(bf(jnp.concatenate([p[i], lk[i]], axis=0)), bf(_block_diag(lk[i], left))) for i in units]
        p = [p[i] + res[i][:c] for i in units]
        lk = [res[i][c:] for i in units]
        power *= 2
    p = [p[i] + _dot(bf(p[i]), bf(_block_diag(lk[i], left))) for i in units]
    u = [_dot(bf(p[i]), bf(_stack_heads(rhs[i], head0))) for i in units]
    ys = [sa[i][c:] + _dot(a_r[i], jnp.concatenate([bf(_stack_heads(u[i], head0)), vstack[i]], axis=0))
          for i in units]
    upd = [_dot(bf(jnp.concatenate([u[i], ops[i]["v"]], axis=0).T),
                bf(jnp.concatenate([ops[i]["b_h"], ops[i]["k_h"]], axis=0))) for i in units]
    new_states = [states[i] * ops[i]["w_tot"] + jnp.where(m["same_head"], upd[i], 0.0) for i in units]
    return new_states, ys


def _rwkv_body(reverse, bt, *refs):
    if reverse:
        (zc_ref, yf_ref, lw_ref, w0_ref, a0_ref, a0o_ref, kk_ref, ka_ref,
         rk_ref, bd_ref, gng_ref, gnb_ref, out_ref, state_ref) = refs
    else:
        (zc_ref, lw_ref, w0_ref, a0_ref, kk_ref, ka_ref,
         bd_ref, out_ref, state_ref) = refs
    c = RW_CHUNK

    @pl.when(pl.program_id(1) == 0)
    def _():
        state_ref[...] = jnp.zeros_like(state_ref)

    rows = bt * c
    m = _rwkv_masks(reverse)
    lora_lane = lax.broadcasted_iota(jnp.int32, (rows, LORA_PAD), 1)
    bd = bd_ref[...]
    zf = zc_ref[...].reshape(rows, Z_COLS)
    r = zf[:, :RW_WIDTH]
    k = zf[:, RW_WIDTH:2 * RW_WIDTH]
    v = zf[:, 2 * RW_WIDTH:3 * RW_WIDTH]
    lo = zf[:, 3 * RW_WIDTH:]
    act = jnp.where(lora_lane < DECAY_LORA, jnp.tanh(lo),
                    jnp.where(lora_lane < DECAY_LORA + ICLR_LORA, lo, _sigmoid(lo)))
    act_hi, act_lo = _split(act)
    w_in = w0_ref[...] + _dot(act_hi, lw_ref[:, :RW_WIDTH]) + _dot(act_lo, lw_ref[:, :RW_WIDTH])
    proj = _dot(act_hi, lw_ref[:, RW_WIDTH:])
    logw = -DECAY_SCALE * _sigmoid(w_in)
    a_sig = _sigmoid(a0_ref[...] + proj[:, :RW_WIDTH])
    kk = k * kk_ref[...]
    kk = kk * lax.rsqrt(jnp.maximum(_head_sums(kk * kk, bd), KK_NORM_FLOOR_SQ))
    kd = k * (1.0 + (a_sig - 1.0) * ka_ref[...])
    a_vec = -kk
    b_vec = kk * a_sig
    ri = lax.broadcasted_iota(jnp.int32, (rows, rows), 0)
    rj = lax.broadcasted_iota(jnp.int32, (rows, rows), 1)
    ordered = (rj >= ri) if reverse else (rj <= ri)
    tri = jnp.where((ri // c) == (rj // c), jnp.where(ordered, 1.0, 0.0), 0.0).astype(BF16)
    cum = _dot2_lhs01(tri, logw)
    edge = 0 if reverse else c - 1
    tot = jnp.concatenate([jnp.broadcast_to(cum[b * c + edge:b * c + edge + 1, :], (c, RW_WIDTH))
                           for b in range(bt)], axis=0)
    w_inv = jnp.exp(-cum)
    w_end = jnp.exp(tot - cum)
    w_tot = jnp.exp(tot)
    full_ops = dict(r_t=r * jnp.exp(cum), a_t=a_vec * jnp.exp(cum - logw), b_t=b_vec * w_inv, k_t=kd * w_inv,
                    b_h=b_vec * w_end, k_h=kd * w_end, v=v)
    ops, states = [], []
    for b in range(bt):
        for pi in range(N_PAIRS):
            ln = slice(pi * PAIR, (pi + 1) * PAIR)
            unit = {name: t[b * c:(b + 1) * c, ln] for name, t in full_ops.items()}
            unit["w_tot"] = w_tot[b * c:b * c + 1, ln]
            ops.append(unit)
            states.append(state_ref[b, pi])
    new_states, ys = _units_chunk(states, ops, m)
    for b in range(bt):
        for pi in range(N_PAIRS):
            state_ref[b, pi] = new_states[b * N_PAIRS + pi]
    y = jnp.concatenate([jnp.concatenate(ys[b * N_PAIRS:(b + 1) * N_PAIRS], axis=1) for b in range(bt)], axis=0)
    if not reverse:
        out_ref[...] = y.reshape(bt, c, RW_WIDTH)
    else:
        y = y + yf_ref[...].reshape(rows, RW_WIDTH)
        a_other = _sigmoid(a0o_ref[...] + proj[:, RW_WIDTH:2 * RW_WIDTH])
        kd_sum = kd + k * (1.0 + (a_other - 1.0) * ka_ref[...])
        bonus = _head_sums(r * kd_sum * rk_ref[...], bd) * v
        inv_n = 1.0 / RW_HEAD_DIM
        mean = _head_sums(y, bd) * inv_n
        yc = y - mean
        var = _head_sums(yc * yc, bd) * inv_n
        yn = yc * lax.rsqrt(var + GN_EPS) * gng_ref[...] + gnb_ref[...]
        g = proj[:, 2 * RW_WIDTH:]
        out_ref[...] = ((yn + bonus) * g).astype(BF16).reshape(bt, c, RW_WIDTH)


def _rwkv_pass(z3, yf, p, reverse, bt):
    b_sz, t_len, _ = z3.shape
    c = RW_CHUNK
    n_chunks = t_len // c

    def cidx(ci):
        return (n_chunks - 1 - ci) if reverse else ci

    full = lambda shape: pl.BlockSpec(shape, lambda bi, ci: (0,) * len(shape))
    row512 = full((1, RW_WIDTH))
    d = "rev" if reverse else "fwd"
    n_proj = p["lora_" + d].shape[1]
    in_specs = [pl.BlockSpec((bt, c, Z_COLS), lambda bi, ci: (bi, cidx(ci), 0))]
    args = [z3]
    if reverse:
        in_specs.append(pl.BlockSpec((bt, c, RW_WIDTH), lambda bi, ci: (bi, cidx(ci), 0)))
        args.append(yf)
    in_specs += [full((LORA_PAD, n_proj)), row512, row512]
    args += [p["lora_" + d], p["w0_" + d], p["a0_" + d]]
    if reverse:
        in_specs.append(row512)
        args.append(p["a0_fwd"])
    in_specs += [row512, row512]
    args += [p["rw_k_k"], p["rw_k_a"]]
    if reverse:
        in_specs.append(row512)
        args.append(p["rw_r_k"])
    in_specs.append(full((PAIR, PAIR)))
    args.append(p["head_ones"])
    if reverse:
        in_specs += [row512, row512]
        args += [p["rw_gn_g"], p["rw_gn_b"]]
    return pl.pallas_call(
        functools.partial(_rwkv_body, reverse, bt),
        grid=(b_sz // bt, n_chunks),
        in_specs=in_specs,
        out_specs=pl.BlockSpec((bt, c, RW_WIDTH), lambda bi, ci: (bi, cidx(ci), 0)),
        out_shape=jax.ShapeDtypeStruct((b_sz, t_len, RW_WIDTH), BF16 if reverse else F32),
        scratch_shapes=[pltpu.VMEM((bt, N_PAIRS, PAIR, PAIR), F32)],
        compiler_params=pltpu.CompilerParams(vmem_limit_bytes=VMEM_LIMIT),
        name="rwkv_" + d,
    )(*args)


def _mix_out_body(x_ref, gm_ref, rw_ref, wo_ref, g2_ref, rw_hi_ref, rw_lo_ref, rb_ref, earlier_ref,
                  x1_ref, h2_ref, route_ref, route_t_ref, counts_ref, carry_ref):
    i = pl.program_id(0)

    @pl.when(i == 0)
    def _():
        carry_ref[...] = jnp.zeros_like(carry_ref)

    sub = earlier_ref.shape[0]
    subs = range(x_ref.shape[0] // sub)
    rows = [slice(s * sub, (s + 1) * sub) for s in subs]
    x1 = [x_ref[r, :] + _dot(gm_ref[r, :], wo_ref[:GM_WIDTH, :]) + _dot(rw_ref[r, :], wo_ref[GM_WIDTH:, :])
          for r in rows]
    h2 = [x1[s] * lax.rsqrt(jnp.mean(x1[s] * x1[s], axis=-1, keepdims=True) + RMS_EPS) * g2_ref[...] for s in subs]
    for s in subs:
        x1_ref[rows[s], :] = x1[s]
        h2_ref[rows[s], :] = _pack_bf16_halves(h2[s])
    split = [_split(h) for h in h2]
    logits = [_dot_nt(rw_hi_ref[...], hi) + _dot_nt(rw_hi_ref[...], lo) + _dot_nt(rw_lo_ref[...], hi) + rb_ref[...]
              for hi, lo in split]
    expert = lax.broadcasted_iota(jnp.int32, (N_EXPERTS, sub), 0).astype(F32)
    work = logits
    vals, ids, hits = [[] for _ in subs], [[] for _ in subs], [[] for _ in subs]
    onehot = [jnp.zeros((N_EXPERTS, sub), F32) for _ in subs]
    for _ in range(TOP_K):
        for s in subs:
            mx = jnp.max(work[s], axis=0, keepdims=True)
            idx = jnp.min(jnp.where(work[s] == mx, expert, float(N_EXPERTS)), axis=0, keepdims=True)
            hit = expert == idx
            vals[s].append(mx)
            ids[s].append(idx)
            hits[s].append(hit)
            onehot[s] = jnp.where(hit, 1.0, onehot[s])
        work = [jnp.where(hits[s][-1], -jnp.inf, work[s]) for s in subs]
    earlier_counts = [_dot(onehot[s].astype(BF16), earlier_ref[...]) for s in subs]
    carry = carry_ref[:, 0:1]
    for s in subs:
        exps = [jnp.exp(vk - vals[s][0]) for vk in vals[s]]
        denom = exps[0] + exps[1] + exps[2] + exps[3]
        ranks = carry + earlier_counts[s]
        rank_rows = [jnp.sum(jnp.where(hit, ranks, 0.0), axis=0, keepdims=True) for hit in hits[s]]
        route_t = jnp.concatenate(ids[s] + [e / denom for e in exps] + rank_rows
                                  + [jnp.zeros((ROUTE_ROWS - 3 * TOP_K, sub), F32)], axis=0)
        route_t_ref[:, rows[s]] = route_t
        route_ref[rows[s], :] = jnp.concatenate([route_t, jnp.zeros((LANES - ROUTE_ROWS, sub), F32)], axis=0).T
        carry = carry + jnp.sum(onehot[s], axis=1, keepdims=True)
    carry_ref[...] = jnp.broadcast_to(carry, carry_ref.shape)
    counts_ref[...] = jnp.broadcast_to(carry, counts_ref.shape)


def _mix_out(x2d, gm, rw, p, tm):
    n = x2d.shape[0]
    full = lambda shape: pl.BlockSpec(shape, lambda i: (0,) * len(shape))
    tile = lambda w: pl.BlockSpec((tm, w), lambda i: (i, 0))
    sub = min(MIX_OUT_SUB, tm)
    earlier = (jnp.arange(sub)[:, None] < jnp.arange(sub)[None, :]).astype(BF16)
    return pl.pallas_call(
        _mix_out_body,
        grid=(n // tm,),
        in_specs=[tile(D_MODEL), tile(GM_WIDTH), tile(RW_WIDTH), full((D_MODEL, D_MODEL)), full((1, D_MODEL)),
                  full((N_EXPERTS, D_MODEL)), full((N_EXPERTS, D_MODEL)), full((N_EXPERTS, 1)), full((sub, sub))],
        out_specs=[tile(D_MODEL), tile(D_MODEL // 2), tile(LANES), pl.BlockSpec((ROUTE_ROWS, tm), lambda i: (0, i)),
                   full((N_EXPERTS, LANES))],
        out_shape=[
            jax.ShapeDtypeStruct((n, D_MODEL), F32),
            jax.ShapeDtypeStruct((n, D_MODEL // 2), jnp.int32),
            jax.ShapeDtypeStruct((n, LANES), F32),
            jax.ShapeDtypeStruct((ROUTE_ROWS, n), F32),
            jax.ShapeDtypeStruct((N_EXPERTS, LANES), F32),
        ],
        scratch_shapes=[pltpu.VMEM((N_EXPERTS, LANES), F32)],
        compiler_params=pltpu.CompilerParams(vmem_limit_bytes=VMEM_LIMIT),
        name="mix_out",
    )(x2d, gm, rw, p["w_out"], p["norm2_g"], p["router_hi"], p["router_lo"], p["router_b"], earlier)


def _sc_gather_rows(table, idx):
    n_idx = idx.shape[0]
    width = table.shape[1]
    n_workers = SC_CORES * SC_SUBCORES
    per_worker = n_idx // n_workers
    n_windows = per_worker // SC_WINDOW
    assert per_worker * n_workers == n_idx and n_windows * SC_WINDOW == per_worker
    mesh = plsc.VectorSubcoreMesh(core_axis_name="c", subcore_axis_name="s")

    @functools.partial(
        pl.kernel, mesh=mesh,
        out_type=jax.ShapeDtypeStruct((n_idx, width), table.dtype),
        scratch_types=[
            pltpu.VMEM((SC_WINDOW,), jnp.int32),
            pltpu.VMEM((SC_WINDOW, width), table.dtype),
            pltpu.SemaphoreType.DMA,
        ],
        name="sc_gather_rows",
    )
    def gather(table_hbm, idx_hbm, out_hbm, idx_v, rows_v, sem):
        worker = lax.axis_index("s") * SC_CORES + lax.axis_index("c")
        base = worker * per_worker

        @pl.loop(0, n_windows)
        def _(j):
            off = base + j * SC_WINDOW
            pltpu.sync_copy(idx_hbm.at[pl.ds(off, SC_WINDOW)], idx_v)
            pltpu.async_copy(table_hbm.at[idx_v], rows_v, sem).wait()
            pltpu.sync_copy(rows_v, out_hbm.at[pl.ds(off, SC_WINDOW)])

    return gather(table, idx)


def _sc_scatter_rows(src, dest_w, n_out):
    n, width = src.shape
    n_windows, top_k, window = dest_w.shape
    n_workers = SC_CORES * SC_SUBCORES
    per_worker = n_windows // n_workers
    assert n_windows * window == n and per_worker * n_workers == n_windows
    mesh = plsc.VectorSubcoreMesh(core_axis_name="c", subcore_axis_name="s")

    @functools.partial(
        pl.kernel, mesh=mesh,
        out_type=jax.ShapeDtypeStruct((n_out, width), src.dtype),
        scratch_types=[
            pltpu.VMEM((top_k, window), jnp.int32),
            pltpu.VMEM((window, width), src.dtype),
        ],
        name="sc_scatter_rows",
    )
    def scatter(src_hbm, dest_hbm, out_hbm, idx_v, rows_v):
        worker = lax.axis_index("s") * SC_CORES + lax.axis_index("c")

        @pl.loop(0, per_worker)
        def _(j):
            g = worker * per_worker + j
            pltpu.sync_copy(dest_hbm.at[g], idx_v)
            pltpu.sync_copy(src_hbm.at[pl.ds(g * window, window)], rows_v)
            for k in range(top_k):
                pltpu.sync_copy(rows_v, out_hbm.at[idx_v.at[k]])

    return scatter(src, dest_w)


def _weight_copies(wgu_hbm, wd_hbm, wgu_buf, wd_buf, sem, expert, slot):
    return (pltpu.make_async_copy(wgu_hbm.at[expert], wgu_buf.at[slot], sem.at[0, slot]),
            pltpu.make_async_copy(wd_hbm.at[expert], wd_buf.at[slot], sem.at[1, slot]))


def _experts_body(blk_e_ref, n_used_ref, valid_ref, slot_ref, next_e_ref, x_ref, wgu_hbm, bgu_ref, wd_hbm, bd_ref,
                  out_ref, wgu_bf, wd_bf, wgu_buf, wd_buf, sem):
    i = pl.program_id(0)
    n_used = n_used_ref[0]
    expert = blk_e_ref[i]
    slot = slot_ref[i]
    copies = functools.partial(_weight_copies, wgu_hbm, wd_hbm, wgu_buf, wd_buf, sem)

    @pl.when(i == 0)
    def _():
        for cp in copies(expert, slot):
            cp.start()

    @pl.when((i < n_used) & ((i == 0) | (expert != blk_e_ref[jnp.maximum(i - 1, 0)])))
    def _():
        for cp in copies(expert, slot):
            cp.wait()
        wgu_bf[...] = wgu_buf[slot].astype(BF16)
        wd_bf[...] = wd_buf[slot].astype(BF16)
        following = next_e_ref[i]

        @pl.when(following >= 0)
        def _():
            for cp in copies(following, 1 - slot):
                cp.start()

    valid = jnp.where(i < n_used, valid_ref[i], 0)

    def expert_mlp(n_rows):
        row = lax.broadcasted_iota(jnp.int32, (n_rows, 1), 0)
        x_left, x_right = _unpack_bf16_halves(jnp.where(row < valid, x_ref[:n_rows, :], 0))
        half = D_MODEL // 2
        gu = (_dot(x_left.astype(BF16), wgu_bf[:half, :]) + _dot(x_right.astype(BF16), wgu_bf[half:, :])
              + bgu_ref[0])
        gate = jnp.minimum(gu[:, :D_EXPERT], SWIGLU_LIMIT)
        up = jnp.clip(gu[:, D_EXPERT:], -SWIGLU_LIMIT, SWIGLU_LIMIT)
        act = gate * _sigmoid(gate * SWIGLU_ALPHA) * (up + 1.0)
        out_ref[:n_rows, :] = _pack_bf16_halves(_dot(act.astype(BF16), wd_bf[...]) + bd_ref[0])

    half_rows = MOE_ROWS // 2

    @pl.when(valid > half_rows)
    def _():
        expert_mlp(MOE_ROWS)

    @pl.when((valid > 0) & (valid <= half_rows))
    def _():
        expert_mlp(half_rows)
        out_ref[half_rows:, :] = jnp.zeros((MOE_ROWS - half_rows, D_MODEL // 2), jnp.int32)

    @pl.when(valid == 0)
    def _():
        out_ref[...] = jnp.zeros_like(out_ref)


def _experts(x_rows, blk_e, n_used, valid, slot, next_e, p):
    n_blocks = blk_e.shape[0]
    grid_spec = pltpu.PrefetchScalarGridSpec(
        num_scalar_prefetch=5,
        grid=(n_blocks,),
        in_specs=[
            pl.BlockSpec((MOE_ROWS, D_MODEL // 2), lambda i, be, nu, *_: (jnp.minimum(i, nu[0] - 1), 0)),
            pl.BlockSpec(memory_space=pl.ANY),
            pl.BlockSpec((1, 1, 2 * D_EXPERT), lambda i, be, *_: (be[i], 0, 0)),
            pl.BlockSpec(memory_space=pl.ANY),
            pl.BlockSpec((1, 1, D_MODEL), lambda i, be, *_: (be[i], 0, 0)),
        ],
        out_specs=pl.BlockSpec((MOE_ROWS, D_MODEL // 2), lambda i, *_: (i, 0)),
        scratch_shapes=[
            pltpu.VMEM((D_MODEL, 2 * D_EXPERT), BF16),
            pltpu.VMEM((D_EXPERT, D_MODEL), BF16),
            pltpu.VMEM((2, D_MODEL, 2 * D_EXPERT), F32),
            pltpu.VMEM((2, D_EXPERT, D_MODEL), F32),
            pltpu.SemaphoreType.DMA((2, 2)),
        ],
    )
    return pl.pallas_call(
        _experts_body,
        grid_spec=grid_spec,
        out_shape=jax.ShapeDtypeStruct((n_blocks * MOE_ROWS, D_MODEL // 2), jnp.int32),
        compiler_params=pltpu.CompilerParams(vmem_limit_bytes=VMEM_LIMIT),
        name="experts",
    )(blk_e, n_used, valid, slot, next_e, x_rows, p["w_gu"], p["b_gu"], p["w_down"], p["b_down"])


def _combine_body(yg_ref, x1_ref, gates_ref, fg_ref, out_ref):
    gates = gates_ref[...]
    half = D_MODEL // 2
    acc_left = x1_ref[:, :half]
    acc_right = x1_ref[:, half:]
    for kk in range(TOP_K):
        y_left, y_right = _unpack_bf16_halves(yg_ref[kk])
        gate = gates[:, TOP_K + kk:TOP_K + kk + 1]
        acc_left = acc_left + gate * y_left
        acc_right = acc_right + gate * y_right
    acc = jnp.concatenate([acc_left, acc_right], axis=1)
    out_ref[...] = acc * lax.rsqrt(jnp.mean(acc * acc, axis=-1, keepdims=True) + RMS_EPS) * fg_ref[...]


def _combine(yg, x1, gates, final_g, tm):
    n = x1.shape[0]
    return pl.pallas_call(
        _combine_body,
        grid=(n // tm,),
        in_specs=[
            pl.BlockSpec((TOP_K, tm, D_MODEL // 2), lambda i: (0, i, 0)),
            pl.BlockSpec((tm, D_MODEL), lambda i: (i, 0)),
            pl.BlockSpec((tm, LANES), lambda i: (i, 0)),
            pl.BlockSpec((1, D_MODEL), lambda i: (0, 0)),
        ],
        out_specs=pl.BlockSpec((tm, D_MODEL), lambda i: (i, 0)),
        out_shape=jax.ShapeDtypeStruct((n, D_MODEL), F32),
        compiler_params=pltpu.CompilerParams(vmem_limit_bytes=VMEM_LIMIT),
        name="combine",
    )(yg, x1, gates, final_g)


def _prepare(norm1_g, w_in, rw_mu, gm_ln_g, gm_ln_b, gm_ws, gm_bs, rw_w0, rw_w2, rw_a0, rw_a2, rw_g2,
             rw_k_k, rw_k_a, rw_r_k, rw_gn_g, rw_gn_b, w_out, norm2_g, router_w, router_b,
             w_gu, b_gu, w_down, b_down, final_g):
    l = 0
    row = lambda t: t.reshape(1, -1).astype(F32)
    p = {}
    p["norm1_g"] = row(norm1_g[l])
    w = w_in[l]
    p["w_gm"] = w[:, :2 * GM_WIDTH].astype(BF16)
    pad_cols = LORA_PAD - LORA_COLS
    p["w_z"] = jnp.pad(w[:, 2 * GM_WIDTH:], ((0, 0), (0, pad_cols))).astype(BF16)
    mu = jnp.pad(row(rw_mu[l]), ((0, 0), (0, pad_cols)))
    p["rw_mu"] = jnp.concatenate([1.0 - mu, 0.5 * mu], axis=0)
    p["gm_ln_g"] = row(gm_ln_g[l])
    p["gm_ln_b"] = row(gm_ln_b[l])
    p["gm_ws"] = gm_ws[l].astype(BF16)
    p["gm_bs"] = jnp.broadcast_to(gm_bs[l][:, :, None], (GM_HEADS, GM_CHUNK, GM_HEAD_DIM)).astype(F32)
    zeros = lambda r, c: jnp.zeros((r, c), F32)
    o_a = DECAY_LORA
    o_g = DECAY_LORA + ICLR_LORA

    def lora_matrix(d, with_epilogue):
        blocks = [jnp.concatenate([rw_w2[l, d], zeros(LORA_PAD - DECAY_LORA, RW_WIDTH)], axis=0),
                  jnp.concatenate([zeros(o_a, RW_WIDTH), rw_a2[l, d], zeros(LORA_PAD - o_g, RW_WIDTH)], axis=0)]
        if with_epilogue:
            blocks.append(jnp.concatenate([zeros(o_a, RW_WIDTH), rw_a2[l, 1 - d], zeros(LORA_PAD - o_g, RW_WIDTH)],
                                          axis=0))
            blocks.append(jnp.concatenate([zeros(o_g, RW_WIDTH), rw_g2[l], zeros(LORA_PAD - LORA_COLS, RW_WIDTH)],
                                          axis=0))
        return jnp.concatenate(blocks, axis=1)

    for d, name in ((0, "fwd"), (1, "rev")):
        p["lora_" + name] = lora_matrix(d, with_epilogue=(d == 1)).astype(BF16)
        p["w0_" + name] = row(rw_w0[l, d])
        p["a0_" + name] = row(rw_a0[l, d])
    p["rw_k_k"] = row(rw_k_k[l])
    p["rw_k_a"] = row(rw_k_a[l])
    p["rw_r_k"] = row(rw_r_k[l])
    p["rw_gn_g"] = row(rw_gn_g[l])
    p["rw_gn_b"] = row(rw_gn_b[l])
    ch = jnp.arange(PAIR) // RW_HEAD_DIM
    p["head_ones"] = (ch[:, None] == ch[None, :]).astype(BF16)
    p["w_out"] = w_out[l].astype(BF16)
    p["norm2_g"] = row(norm2_g[l])
    rw_t = router_w[l].astype(F32).T
    hi = rw_t.astype(BF16)
    p["router_hi"] = hi
    p["router_lo"] = (rw_t - hi.astype(F32)).astype(BF16)
    p["router_b"] = router_b[l].astype(F32).reshape(N_EXPERTS, 1)
    p["w_gu"] = w_gu[l]
    p["b_gu"] = b_gu[l].reshape(N_EXPERTS, 1, 2 * D_EXPERT).astype(F32)
    p["w_down"] = w_down[l]
    p["b_down"] = b_down[l].reshape(N_EXPERTS, 1, D_MODEL).astype(F32)
    p["final_g"] = row(final_g)
    return p


def _pick_tile(n, want):
    t = want
    while n % t:
        t //= 2
    return t


def _encoder(x, p):
    b_sz, t_len, d = x.shape
    n = b_sz * t_len
    x2d = x.reshape(n, d)
    gm, z = _mix_in(x2d, p, _pick_tile(t_len, 1024), t_len)
    z3 = z.reshape(b_sz, t_len, Z_COLS)
    yf = _rwkv_pass(z3, None, p, reverse=False, bt=RW_BATCH_TILE)
    rw = _rwkv_pass(z3, yf, p, reverse=True, bt=RW_BATCH_TILE)
    tmo = _pick_tile(n, 1024)
    x1, h2, route, route_t, counts = _mix_out(x2d, gm, rw.reshape(n, RW_WIDTH), p, tmo)
    ids = route_t[:TOP_K].astype(jnp.int32)
    ranks = route_t[2 * TOP_K:3 * TOP_K].astype(jnp.int32)
    cnt = counts[:, 0].astype(jnp.int32)
    padded = (cnt + MOE_ROWS - 1) // MOE_ROWS * MOE_ROWS
    pad_end = jnp.cumsum(padded)
    pad_start = pad_end - padded
    expert_ids = jnp.arange(N_EXPERTS, dtype=jnp.int32)
    start_of = jnp.sum(jnp.where(ids[..., None] == expert_ids, pad_start, 0), axis=-1)
    dest = start_of + ranks
    n_blocks = n * TOP_K // MOE_ROWS + N_EXPERTS
    blk_start = jnp.arange(n_blocks, dtype=jnp.int32) * MOE_ROWS
    blk_e = jnp.minimum(jnp.sum((pad_end[None, :] <= blk_start[:, None]).astype(jnp.int32), axis=1), N_EXPERTS - 1)
    n_used = (pad_end[-1] // MOE_ROWS).astype(jnp.int32).reshape(1)
    blk_onehot = blk_e[:, None] == expert_ids
    blk_cnt = jnp.sum(jnp.where(blk_onehot, cnt, 0), axis=-1)
    blk_first = jnp.sum(jnp.where(blk_onehot, pad_start, 0), axis=-1)
    valid = jnp.clip(blk_cnt - (blk_start - blk_first), 0, MOE_ROWS).astype(jnp.int32)
    window = min(SC_SCATTER_WINDOW, n // (SC_CORES * SC_SUBCORES))
    dest_w = dest.reshape(TOP_K, n // window, window).transpose(1, 0, 2)
    x_rows = _sc_scatter_rows(h2, dest_w, n_blocks * MOE_ROWS)
    present = cnt > 0
    slot_of_expert = (jnp.cumsum(present.astype(jnp.int32)) - 1) % 2
    later = present[None, :] & (expert_ids[None, :] > expert_ids[:, None])
    next_present = jnp.min(jnp.where(later, expert_ids[None, :], N_EXPERTS), axis=1)
    next_present = jnp.where(next_present < N_EXPERTS, next_present, -1)
    slot = jnp.sum(jnp.where(blk_onehot, slot_of_expert, 0), axis=-1).astype(jnp.int32)
    next_e = jnp.sum(jnp.where(blk_onehot, next_present, 0), axis=-1).astype(jnp.int32)
    y_rows = _experts(x_rows, blk_e, n_used, valid, slot, next_e, p)
    yg = _sc_gather_rows(y_rows, dest.reshape(-1)).reshape(TOP_K, n, D_MODEL // 2)
    out = _combine(yg, x1, route, p["final_g"], _pick_tile(n, 1024))
    return out.reshape(b_sz, t_len, d)


def kernel(x_prompt, x_sample, norm1_g, w_in, rw_mu, gm_ln_g, gm_ln_b, gm_ws, gm_bs, rw_w0, rw_w2, rw_a0, rw_a2,
           rw_g2, rw_k_k, rw_k_a, rw_r_k, rw_gn_g, rw_gn_b, w_out, norm2_g, router_w, router_b, w_gu, b_gu,
           w_down, b_down, final_g):
    p = _prepare(norm1_g, w_in, rw_mu, gm_ln_g, gm_ln_b, gm_ws, gm_bs, rw_w0, rw_w2, rw_a0, rw_a2, rw_g2,
                 rw_k_k, rw_k_a, rw_r_k, rw_gn_g, rw_gn_b, w_out, norm2_g, router_w, router_b,
                 w_gu, b_gu, w_down, b_down, final_g)
    return (_encoder(x_prompt, p), _encoder(x_sample, p))
```

```python
import functools

import jax
import jax.numpy as jnp
from jax import lax
from jax.experimental import pallas as pl
from jax.experimental.pallas import tpu as pltpu
from jax.experimental.pallas import tpu_sc as plsc

F32 = jnp.float32
BF16 = jnp.bfloat16

D_MODEL = 1024
GM_WIDTH = 512
RW_WIDTH = 512
GM_HEADS = 4
GM_HEAD_DIM = 128
GM_CHUNK = 128
RW_HEAD_DIM = 64
DECAY_LORA = 32
ICLR_LORA = 32
GATE_LORA = 96
LORA_COLS = DECAY_LORA + ICLR_LORA + GATE_LORA
LORA_PAD = 256
Z_COLS = 3 * RW_WIDTH + LORA_PAD
Z_CHUNK = 256
N_EXPERTS = 32
TOP_K = 4
D_EXPERT = 1024
SWIGLU_LIMIT = 7.0
SWIGLU_ALPHA = 1.702
RMS_EPS = 1e-5
LN_EPS = 1e-5
GN_EPS = 64e-5

LANES = 128
SUBLANES = 8
VMEM_LIMIT = 56 * 1024 * 1024

RW_CHUNK = 64
PAIR = 2 * RW_HEAD_DIM
N_PAIRS = RW_WIDTH // PAIR
RW_BATCH_TILE = 4
MIX_OUT_SUB = 128
MOE_ROWS = 512
MOE_ROW_STEPS = 4
SC_CORES = 2
SC_SUBCORES = 16
SC_WINDOW = 64
SC_SCATTER_WINDOW = 128
ROUTE_ROWS = 16
DECAY_SCALE = 0.6065306597126334
KK_NORM_FLOOR_SQ = 1e-24


def _dot(a, b):
    return jnp.dot(a, b, preferred_element_type=F32)


def _dot_nt(a, b):
    return lax.dot_general(a, b, (((1,), (1,)), ((), ())), preferred_element_type=F32)


def _split(x):
    hi = x.astype(BF16)
    lo = (x - hi.astype(F32)).astype(BF16)
    return hi, lo


def _head_sums(x, pair_ones):
    rows = x.shape[0]
    n_tiles = x.shape[1] // LANES
    stacked = jnp.concatenate([x[:, t * LANES:(t + 1) * LANES] for t in range(n_tiles)], axis=0)
    sums = _dot(stacked.astype(BF16), pair_ones)
    return jnp.concatenate([sums[t * rows:(t + 1) * rows] for t in range(n_tiles)], axis=1)


def _dot2_lhs01(a01, b):
    b_hi, b_lo = _split(b)
    return _dot(a01, b_hi) + _dot(a01, b_lo)


def _pack_bf16_halves(x):
    w = x.shape[1] // 2
    hi = lax.bitcast_convert_type(x[:, :w].astype(BF16).astype(F32), jnp.int32)
    lo = lax.bitcast_convert_type(x[:, w:].astype(BF16).astype(F32), jnp.int32)
    return hi | lax.shift_right_logical(lo, 16)


def _unpack_bf16_halves(words):
    left = lax.bitcast_convert_type(words & jnp.int32(-65536), F32)
    right = lax.bitcast_convert_type(lax.shift_left(words, 16), F32)
    return left, right


def _gelu_tanh(x):
    return 0.5 * x * (1.0 + jnp.tanh(0.7978845608028654 * (x + 0.044715 * (x * x * x))))


def _sigmoid(x):
    return 0.5 * jnp.tanh(0.5 * x) + 0.5


def _mix_in_body(tiles_per_seq, x_ref, xp_ref, xn_ref, g1_ref, wgm_ref, wz_ref, mu_ref, lng_ref, lnb_ref, ws_ref,
                 bs_ref, gm_ref, z_ref):
    i = pl.program_id(0)
    tm = x_ref.shape[0]

    def normed(x):
        return (x * lax.rsqrt(jnp.mean(x * x, axis=-1, keepdims=True) + RMS_EPS) * g1_ref[...]).astype(BF16)

    hb = normed(x_ref[...])
    uv = _dot(hb, wgm_ref[...])
    hb_halo = normed(jnp.concatenate([xp_ref[...], xn_ref[...]], axis=0))
    has_prev = i % tiles_per_seq > 0
    has_next = i % tiles_per_seq < tiles_per_seq - 1
    row8 = lax.broadcasted_iota(jnp.int32, (SUBLANES, Z_CHUNK), 0)
    for j in range(Z_COLS // Z_CHUNK):
        cols = slice(j * Z_CHUNK, (j + 1) * Z_CHUNK)
        z = _dot(hb, wz_ref[:, cols])
        halo = _dot(hb_halo, wz_ref[:, cols])
        prev_row = jnp.where(has_prev, halo[SUBLANES - 1:SUBLANES, :], 0.0)
        next_row = jnp.where(has_next, halo[SUBLANES:SUBLANES + 1, :], 0.0)
        down = pltpu.roll(z, 1, axis=0)
        up = pltpu.roll(z, tm - 1, axis=0)
        prev = jnp.concatenate([jnp.where(row8 == 0, prev_row, down[:SUBLANES]), down[SUBLANES:]], axis=0)
        nxt = jnp.concatenate([up[:tm - SUBLANES], jnp.where(row8 == SUBLANES - 1, next_row, up[tm - SUBLANES:])],
                              axis=0)
        z_ref[:, cols] = z * mu_ref[0:1, cols] + (prev + nxt) * mu_ref[1:2, cols]
    u = _gelu_tanh(uv[:, :GM_WIDTH])
    v = _gelu_tanh(uv[:, GM_WIDTH:])
    mean = jnp.mean(v, axis=-1, keepdims=True)
    vc = v - mean
    var = jnp.mean(vc * vc, axis=-1, keepdims=True)
    vn = (vc * lax.rsqrt(var + LN_EPS) * lng_ref[...] + lnb_ref[...]).astype(BF16)
    for c in range(tm // GM_CHUNK):
        rows = slice(c * GM_CHUNK, (c + 1) * GM_CHUNK)
        for hd in range(GM_HEADS):
            cols = slice(hd * GM_HEAD_DIM, (hd + 1) * GM_HEAD_DIM)
            mixed = _dot(ws_ref[hd], vn[rows, cols]) + bs_ref[hd]
            gm_ref[rows, cols] = (u[rows, cols] * mixed).astype(BF16)


def _mix_in(x2d, p, tm, t_len):
    n = x2d.shape[0]
    assert t_len % tm == 0
    halo_per_tile = tm // SUBLANES
    n_halo = n // SUBLANES
    full = lambda shape: pl.BlockSpec(shape, lambda i: (0,) * len(shape))
    return pl.pallas_call(
        functools.partial(_mix_in_body, t_len // tm),
        grid=(n // tm,),
        in_specs=[
            pl.BlockSpec((tm, D_MODEL), lambda i: (i, 0)),
            pl.BlockSpec((SUBLANES, D_MODEL), lambda i: (jnp.maximum(i * halo_per_tile - 1, 0), 0)),
            pl.BlockSpec((SUBLANES, D_MODEL), lambda i: (jnp.minimum((i + 1) * halo_per_tile, n_halo - 1), 0)),
            full((1, D_MODEL)),
            full((D_MODEL, 2 * GM_WIDTH)),
            full((D_MODEL, Z_COLS)),
            full((2, Z_COLS)),
            full((1, GM_WIDTH)),
            full((1, GM_WIDTH)),
            full((GM_HEADS, GM_CHUNK, GM_CHUNK)),
            full((GM_HEADS, GM_CHUNK, GM_HEAD_DIM)),
        ],
        out_specs=[
            pl.BlockSpec((tm, GM_WIDTH), lambda i: (i, 0)),
            pl.BlockSpec((tm, Z_COLS), lambda i: (i, 0)),
        ],
        out_shape=[
            jax.ShapeDtypeStruct((n, GM_WIDTH), BF16),
            jax.ShapeDtypeStruct((n, Z_COLS), F32),
        ],
        compiler_params=pltpu.CompilerParams(vmem_limit_bytes=VMEM_LIMIT),
        name="mix_in",
    )(x2d, x2d, x2d, p["norm1_g"], p["w_gm"], p["w_z"], p["rw_mu"], p["gm_ln_g"], p["gm_ln_b"], p["gm_ws"],
      p["gm_bs"])


def _rwkv_masks(reverse):
    c = RW_CHUNK
    lane = lax.broadcasted_iota(jnp.int32, (c, PAIR), 1)
    head0 = lane < RW_HEAD_DIM
    lane2 = lax.broadcasted_iota(jnp.int32, (c, 2 * c), 1)
    left = lane2 < c
    ti2 = lax.broadcasted_iota(jnp.int32, (c, 2 * c), 0)
    tj2 = jnp.where(left, lane2, lane2 - c)
    incl2 = (tj2 >= ti2) if reverse else (tj2 <= ti2)
    strict2 = (tj2 > ti2) if reverse else (tj2 < ti2)
    eye2 = jnp.where(tj2 == ti2, 1.0, 0.0).astype(F32)
    si = lax.broadcasted_iota(jnp.int32, (PAIR, PAIR), 0)
    sj = lax.broadcasted_iota(jnp.int32, (PAIR, PAIR), 1)
    same_head = (si // RW_HEAD_DIM) == (sj // RW_HEAD_DIM)
    return dict(incl2=incl2, strict2=strict2, head0=head0, left=left, eye2=eye2, same_head=same_head)


def _stack_heads(x, head0):
    return jnp.concatenate([jnp.where(head0, x, 0.0), jnp.where(head0, 0.0, x)], axis=0)


def _block_diag(x, left):
    return jnp.concatenate([jnp.where(left, x, 0.0), jnp.where(left, 0.0, x)], axis=0)


def _units_chunk(states, ops, m):
    c = RW_CHUNK
    units = range(len(ops))
    bf = lambda t: t.astype(BF16)
    head0, left = m["head0"], m["left"]
    ar = [bf(jnp.concatenate([o["a_t"], o["r_t"]], axis=0)) for o in ops]
    rstack = [bf(jnp.concatenate([_stack_heads(o["b_t"], head0), _stack_heads(o["k_t"], head0)], axis=0))
              for o in ops]
    acat = [_dot_nt(ar[i], rstack[i]) for i in units]
    sa = [_dot_nt(ar[i], bf(states[i])) for i in units]
    a_ab = [jnp.where(m["strict2"], acat[i][:c, :2 * c], 0.0) for i in units]
    a_ak = [bf(jnp.where(m["strict2"], acat[i][:c, 2 * c:], 0.0)) for i in units]
    a_r = [bf(jnp.concatenate([jnp.where(m["incl2"], acat[i][c:, :2 * c], 0.0),
                               jnp.where(m["incl2"], acat[i][c:, 2 * c:], 0.0)], axis=1)) for i in units]
    p = [m["eye2"] + a_ab[i] for i in units]
    lk = [_dot(bf(a_ab[i]), bf(_block_diag(a_ab[i], left))) for i in units]
    vstack = [bf(_stack_heads(o["v"], head0)) for o in ops]
    rhs = [sa[i][:c] + _dot(a_ak[i], vstack[i]) for i in units]
    power = 2
    while 2 * power < c:
        res = [_dot(bf(jnp.concatenate([p[i], lk[i]], axis=0)), bf(_block_diag(lk[i], left))) for i in units]
        p = [p[i] + res[i][:c] for i in units]
        lk = [res[i][c:] for i in units]
        power *= 2
    p = [p[i] + _dot(bf(p[i]), bf(_block_diag(lk[i], left))) for i in units]
    u = [_dot(bf(p[i]), bf(_stack_heads(rhs[i], head0))) for i in units]
    ys = [sa[i][c:] + _dot(a_r[i], jnp.concatenate([bf(_stack_heads(u[i], head0)), vstack[i]], axis=0))
          for i in units]
    upd = [_dot(bf(jnp.concatenate([u[i], ops[i]["v"]], axis=0).T),
                bf(jnp.concatenate([ops[i]["b_h"], ops[i]["k_h"]], axis=0))) for i in units]
    new_states = [states[i] * ops[i]["w_tot"] + jnp.where(m["same_head"], upd[i], 0.0) for i in units]
    return new_states, ys


def _rwkv_body(reverse, bt, *refs):
    if reverse:
        (zc_ref, yf_ref, lw_ref, w0_ref, a0_ref, a0o_ref, kk_ref, ka_ref,
         rk_ref, bd_ref, gng_ref, gnb_ref, out_ref, state_ref) = refs
    else:
        (zc_ref, lw_ref, w0_ref, a0_ref, kk_ref, ka_ref,
         bd_ref, out_ref, state_ref) = refs
    c = RW_CHUNK

    @pl.when(pl.program_id(1) == 0)
    def _():
        state_ref[...] = jnp.zeros_like(state_ref)

    rows = bt * c
    m = _rwkv_masks(reverse)
    lora_lane = lax.broadcasted_iota(jnp.int32, (rows, LORA_PAD), 1)
    bd = bd_ref[...]
    zf = zc_ref[...].reshape(rows, Z_COLS)
    r = zf[:, :RW_WIDTH]
    k = zf[:, RW_WIDTH:2 * RW_WIDTH]
    v = zf[:, 2 * RW_WIDTH:3 * RW_WIDTH]
    lo = zf[:, 3 * RW_WIDTH:]
    act = jnp.where(lora_lane < DECAY_LORA, jnp.tanh(lo),
                    jnp.where(lora_lane < DECAY_LORA + ICLR_LORA, lo, _sigmoid(lo)))
    act_hi, act_lo = _split(act)
    w_in = w0_ref[...] + _dot(act_hi, lw_ref[:, :RW_WIDTH]) + _dot(act_lo, lw_ref[:, :RW_WIDTH])
    proj = _dot(act_hi, lw_ref[:, RW_WIDTH:])
    logw = -DECAY_SCALE * _sigmoid(w_in)
    a_sig = _sigmoid(a0_ref[...] + proj[:, :RW_WIDTH])
    kk = k * kk_ref[...]
    kk = kk * lax.rsqrt(jnp.maximum(_head_sums(kk * kk, bd), KK_NORM_FLOOR_SQ))
    kd = k * (1.0 + (a_sig - 1.0) * ka_ref[...])
    a_vec = -kk
    b_vec = kk * a_sig
    ri = lax.broadcasted_iota(jnp.int32, (rows, rows), 0)
    rj = lax.broadcasted_iota(jnp.int32, (rows, rows), 1)
    ordered = (rj >= ri) if reverse else (rj <= ri)
    tri = jnp.where((ri // c) == (rj // c), jnp.where(ordered, 1.0, 0.0), 0.0).astype(BF16)
    cum = _dot2_lhs01(tri, logw)
    edge = 0 if reverse else c - 1
    tot = jnp.concatenate([jnp.broadcast_to(cum[b * c + edge:b * c + edge + 1, :], (c, RW_WIDTH))
                           for b in range(bt)], axis=0)
    w_inv = jnp.exp(-cum)
    w_end = jnp.exp(tot - cum)
    w_tot = jnp.exp(tot)
    full_ops = dict(r_t=r * jnp.exp(cum), a_t=a_vec * jnp.exp(cum - logw), b_t=b_vec * w_inv, k_t=kd * w_inv,
                    b_h=b_vec * w_end, k_h=kd * w_end, v=v)
    ops, states = [], []
    for b in range(bt):
        for pi in range(N_PAIRS):
            ln = slice(pi * PAIR, (pi + 1) * PAIR)
            unit = {name: t[b * c:(b + 1) * c, ln] for name, t in full_ops.items()}
            unit["w_tot"] = w_tot[b * c:b * c + 1, ln]
            ops.append(unit)
            states.append(state_ref[b, pi])
    new_states, ys = _units_chunk(states, ops, m)
    for b in range(bt):
        for pi in range(N_PAIRS):
            state_ref[b, pi] = new_states[b * N_PAIRS + pi]
    y = jnp.concatenate([jnp.concatenate(ys[b * N_PAIRS:(b + 1) * N_PAIRS], axis=1) for b in range(bt)], axis=0)
    if not reverse:
        out_ref[...] = y.reshape(bt, c, RW_WIDTH)
    else:
        y = y + yf_ref[...].reshape(rows, RW_WIDTH)
        a_other = _sigmoid(a0o_ref[...] + proj[:, RW_WIDTH:2 * RW_WIDTH])
        kd_sum = kd + k * (1.0 + (a_other - 1.0) * ka_ref[...])
        bonus = _head_sums(r * kd_sum * rk_ref[...], bd) * v
        inv_n = 1.0 / RW_HEAD_DIM
        mean = _head_sums(y, bd) * inv_n
        yc = y - mean
        var = _head_sums(yc * yc, bd) * inv_n
        yn = yc * lax.rsqrt(var + GN_EPS) * gng_ref[...] + gnb_ref[...]
        g = proj[:, 2 * RW_WIDTH:]
        out_ref[...] = ((yn + bonus) * g).astype(BF16).reshape(bt, c, RW_WIDTH)


def _rwkv_pass(z3, yf, p, reverse, bt):
    b_sz, t_len, _ = z3.shape
    c = RW_CHUNK
    n_chunks = t_len // c

    def cidx(ci):
        return (n_chunks - 1 - ci) if reverse else ci

    full = lambda shape: pl.BlockSpec(shape, lambda bi, ci: (0,) * len(shape))
    row512 = full((1, RW_WIDTH))
    d = "rev" if reverse else "fwd"
    n_proj = p["lora_" + d].shape[1]
    in_specs = [pl.BlockSpec((bt, c, Z_COLS), lambda bi, ci: (bi, cidx(ci), 0))]
    args = [z3]
    if reverse:
        in_specs.append(pl.BlockSpec((bt, c, RW_WIDTH), lambda bi, ci: (bi, cidx(ci), 0)))
        args.append(yf)
    in_specs += [full((LORA_PAD, n_proj)), row512, row512]
    args += [p["lora_" + d], p["w0_" + d], p["a0_" + d]]
    if reverse:
        in_specs.append(row512)
        args.append(p["a0_fwd"])
    in_specs += [row512, row512]
    args += [p["rw_k_k"], p["rw_k_a"]]
    if reverse:
        in_specs.append(row512)
        args.append(p["rw_r_k"])
    in_specs.append(full((PAIR, PAIR)))
    args.append(p["head_ones"])
    if reverse:
        in_specs += [row512, row512]
        args += [p["rw_gn_g"], p["rw_gn_b"]]
    return pl.pallas_call(
        functools.partial(_rwkv_body, reverse, bt),
        grid=(b_sz // bt, n_chunks),
        in_specs=in_specs,
        out_specs=pl.BlockSpec((bt, c, RW_WIDTH), lambda bi, ci: (bi, cidx(ci), 0)),
        out_shape=jax.ShapeDtypeStruct((b_sz, t_len, RW_WIDTH), BF16 if reverse else F32),
        scratch_shapes=[pltpu.VMEM((bt, N_PAIRS, PAIR, PAIR), F32)],
        compiler_params=pltpu.CompilerParams(vmem_limit_bytes=VMEM_LIMIT),
        name="rwkv_" + d,
    )(*args)


def _mix_out_body(x_ref, gm_ref, rw_ref, wo_ref, g2_ref, rw_hi_ref, rw_lo_ref, rb_ref, earlier_ref,
                  x1_ref, h2_ref, route_ref, route_t_ref, counts_ref, carry_ref):
    i = pl.program_id(0)

    @pl.when(i == 0)
    def _():
        carry_ref[...] = jnp.zeros_like(carry_ref)

    sub = earlier_ref.shape[0]
    subs = range(x_ref.shape[0] // sub)
    rows = [slice(s * sub, (s + 1) * sub) for s in subs]
    x1 = [x_ref[r, :] + _dot(gm_ref[r, :], wo_ref[:GM_WIDTH, :]) + _dot(rw_ref[r, :], wo_ref[GM_WIDTH:, :])
          for r in rows]
    h2 = [x1[s] * lax.rsqrt(jnp.mean(x1[s] * x1[s], axis=-1, keepdims=True) + RMS_EPS) * g2_ref[...] for s in subs]
    for s in subs:
        x1_ref[rows[s], :] = x1[s]
        h2_ref[rows[s], :] = _pack_bf16_halves(h2[s])
    split = [_split(h) for h in h2]
    logits = [_dot_nt(rw_hi_ref[...], hi) + _dot_nt(rw_hi_ref[...], lo) + _dot_nt(rw_lo_ref[...], hi) + rb_ref[...]
              for hi, lo in split]
    expert = lax.broadcasted_iota(jnp.int32, (N_EXPERTS, sub), 0).astype(F32)
    work = logits
    vals, ids, hits = [[] for _ in subs], [[] for _ in subs], [[] for _ in subs]
    onehot = [jnp.zeros((N_EXPERTS, sub), F32) for _ in subs]
    for _ in range(TOP_K):
        for s in subs:
            mx = jnp.max(work[s], axis=0, keepdims=True)
            idx = jnp.min(jnp.where(work[s] == mx, expert, float(N_EXPERTS)), axis=0, keepdims=True)
            hit = expert == idx
            vals[s].append(mx)
            ids[s].append(idx)
            hits[s].append(hit)
            onehot[s] = jnp.where(hit, 1.0, onehot[s])
        work = [jnp.where(hits[s][-1], -jnp.inf, work[s]) for s in subs]
    earlier_counts = [_dot(onehot[s].astype(BF16), earlier_ref[...]) for s in subs]
    carry = carry_ref[:, 0:1]
    for s in subs:
        exps = [jnp.exp(vk - vals[s][0]) for vk in vals[s]]
        denom = exps[0] + exps[1] + exps[2] + exps[3]
        ranks = carry + earlier_counts[s]
        rank_rows = [jnp.sum(jnp.where(hit, ranks, 0.0), axis=0, keepdims=True) for hit in hits[s]]
        route_t = jnp.concatenate(ids[s] + [e / denom for e in exps] + rank_rows
                                  + [jnp.zeros((ROUTE_ROWS - 3 * TOP_K, sub), F32)], axis=0)
        route_t_ref[:, rows[s]] = route_t
        route_ref[rows[s], :] = jnp.concatenate([route_t, jnp.zeros((LANES - ROUTE_ROWS, sub), F32)], axis=0).T
        carry = carry + jnp.sum(onehot[s], axis=1, keepdims=True)
    carry_ref[...] = jnp.broadcast_to(carry, carry_ref.shape)
    counts_ref[...] = jnp.broadcast_to(carry, counts_ref.shape)


def _mix_out(x2d, gm, rw, p, tm):
    n = x2d.shape[0]
    full = lambda shape: pl.BlockSpec(shape, lambda i: (0,) * len(shape))
    tile = lambda w: pl.BlockSpec((tm, w), lambda i: (i, 0))
    sub = min(MIX_OUT_SUB, tm)
    earlier = (jnp.arange(sub)[:, None] < jnp.arange(sub)[None, :]).astype(BF16)
    return pl.pallas_call(
        _mix_out_body,
        grid=(n // tm,),
        in_specs=[tile(D_MODEL), tile(GM_WIDTH), tile(RW_WIDTH), full((D_MODEL, D_MODEL)), full((1, D_MODEL)),
                  full((N_EXPERTS, D_MODEL)), full((N_EXPERTS, D_MODEL)), full((N_EXPERTS, 1)), full((sub, sub))],
        out_specs=[tile(D_MODEL), tile(D_MODEL // 2), tile(LANES), pl.BlockSpec((ROUTE_ROWS, tm), lambda i: (0, i)),
                   full((N_EXPERTS, LANES))],
        out_shape=[
            jax.ShapeDtypeStruct((n, D_MODEL), F32),
            jax.ShapeDtypeStruct((n, D_MODEL // 2), jnp.int32),
            jax.ShapeDtypeStruct((n, LANES), F32),
            jax.ShapeDtypeStruct((ROUTE_ROWS, n), F32),
            jax.ShapeDtypeStruct((N_EXPERTS, LANES), F32),
        ],
        scratch_shapes=[pltpu.VMEM((N_EXPERTS, LANES), F32)],
        compiler_params=pltpu.CompilerParams(vmem_limit_bytes=VMEM_LIMIT),
        name="mix_out",
    )(x2d, gm, rw, p["w_out"], p["norm2_g"], p["router_hi"], p["router_lo"], p["router_b"], earlier)


def _sc_gather_rows(table, idx):
    n_idx = idx.shape[0]
    width = table.shape[1]
    n_workers = SC_CORES * SC_SUBCORES
    per_worker = n_idx // n_workers
    n_windows = per_worker // SC_WINDOW
    assert per_worker * n_workers == n_idx and n_windows * SC_WINDOW == per_worker
    mesh = plsc.VectorSubcoreMesh(core_axis_name="c", subcore_axis_name="s")

    @functools.partial(
        pl.kernel, mesh=mesh,
        out_type=jax.ShapeDtypeStruct((n_idx, width), table.dtype),
        scratch_types=[
            pltpu.VMEM((SC_WINDOW,), jnp.int32),
            pltpu.VMEM((SC_WINDOW, width), table.dtype),
            pltpu.SemaphoreType.DMA,
        ],
        name="sc_gather_rows",
    )
    def gather(table_hbm, idx_hbm, out_hbm, idx_v, rows_v, sem):
        worker = lax.axis_index("s") * SC_CORES + lax.axis_index("c")
        base = worker * per_worker

        @pl.loop(0, n_windows)
        def _(j):
            off = base + j * SC_WINDOW
            pltpu.sync_copy(idx_hbm.at[pl.ds(off, SC_WINDOW)], idx_v)
            pltpu.async_copy(table_hbm.at[idx_v], rows_v, sem).wait()
            pltpu.sync_copy(rows_v, out_hbm.at[pl.ds(off, SC_WINDOW)])

    return gather(table, idx)


def _sc_scatter_rows(src, dest_w, n_out):
    n, width = src.shape
    n_windows, top_k, window = dest_w.shape
    n_workers = SC_CORES * SC_SUBCORES
    per_worker = n_windows // n_workers
    assert n_windows * window == n and per_worker * n_workers == n_windows
    mesh = plsc.VectorSubcoreMesh(core_axis_name="c", subcore_axis_name="s")

    @functools.partial(
        pl.kernel, mesh=mesh,
        out_type=jax.ShapeDtypeStruct((n_out, width), src.dtype),
        scratch_types=[
            pltpu.VMEM((top_k, window), jnp.int32),
            pltpu.VMEM((window, width), src.dtype),
        ],
        name="sc_scatter_rows",
    )
    def scatter(src_hbm, dest_hbm, out_hbm, idx_v, rows_v):
        worker = lax.axis_index("s") * SC_CORES + lax.axis_index("c")

        @pl.loop(0, per_worker)
        def _(j):
            g = worker * per_worker + j
            pltpu.sync_copy(dest_hbm.at[g], idx_v)
            pltpu.sync_copy(src_hbm.at[pl.ds(g * window, window)], rows_v)
            for k in range(top_k):
                pltpu.sync_copy(rows_v, out_hbm.at[idx_v.at[k]])

    return scatter(src, dest_w)


def _weight_copies(wgu_hbm, wd_hbm, wgu_buf, wd_buf, sem, expert, slot):
    return (pltpu.make_async_copy(wgu_hbm.at[expert], wgu_buf.at[slot], sem.at[0, slot]),
            pltpu.make_async_copy(wd_hbm.at[expert], wd_buf.at[slot], sem.at[1, slot]))


def _experts_body(blk_e_ref, n_used_ref, valid_ref, slot_ref, next_e_ref, x_ref, wgu_hbm, bgu_ref, wd_hbm, bd_ref,
                  out_ref, wgu_bf, wd_bf, wgu_buf, wd_buf, sem):
    i = pl.program_id(0)
    n_used = n_used_ref[0]
    expert = blk_e_ref[i]
    slot = slot_ref[i]
    copies = functools.partial(_weight_copies, wgu_hbm, wd_hbm, wgu_buf, wd_buf, sem)

    @pl.when(i == 0)
    def _():
        for cp in copies(expert, slot):
            cp.start()

    @pl.when((i < n_used) & ((i == 0) | (expert != blk_e_ref[jnp.maximum(i - 1, 0)])))
    def _():
        for cp in copies(expert, slot):
            cp.wait()
        wgu_bf[...] = wgu_buf[slot].astype(BF16)
        wd_bf[...] = wd_buf[slot].astype(BF16)
        following = next_e_ref[i]

        @pl.when(following >= 0)
        def _():
            for cp in copies(following, 1 - slot):
                cp.start()

    valid = jnp.where(i < n_used, valid_ref[i], 0)

    def expert_mlp(n_rows):
        row = lax.broadcasted_iota(jnp.int32, (n_rows, 1), 0)
        x_left, x_right = _unpack_bf16_halves(jnp.where(row < valid, x_ref[:n_rows, :], 0))
        half = D_MODEL // 2
        gu = (_dot(x_left.astype(BF16), wgu_bf[:half, :]) + _dot(x_right.astype(BF16), wgu_bf[half:, :])
              + bgu_ref[0])
        gate = jnp.minimum(gu[:, :D_EXPERT], SWIGLU_LIMIT)
        up = jnp.clip(gu[:, D_EXPERT:], -SWIGLU_LIMIT, SWIGLU_LIMIT)
        act = gate * _sigmoid(gate * SWIGLU_ALPHA) * (up + 1.0)
        out_ref[:n_rows, :] = _pack_bf16_halves(_dot(act.astype(BF16), wd_bf[...]) + bd_ref[0])

    for quarters in range(1, MOE_ROW_STEPS + 1):
        n_rows = quarters * MOE_ROWS // MOE_ROW_STEPS

        @pl.when((valid > n_rows - MOE_ROWS // MOE_ROW_STEPS) & (valid <= n_rows))
        def _(n_rows=n_rows):
            expert_mlp(n_rows)
            if n_rows < MOE_ROWS:
                out_ref[n_rows:, :] = jnp.zeros((MOE_ROWS - n_rows, D_MODEL // 2), jnp.int32)

    @pl.when(valid == 0)
    def _():
        out_ref[...] = jnp.zeros_like(out_ref)


def _experts(x_rows, blk_e, n_used, valid, slot, next_e, p):
    n_blocks = blk_e.shape[0]
    grid_spec = pltpu.PrefetchScalarGridSpec(
        num_scalar_prefetch=5,
        grid=(n_blocks,),
        in_specs=[
            pl.BlockSpec((MOE_ROWS, D_MODEL // 2), lambda i, be, nu, *_: (jnp.minimum(i, nu[0] - 1), 0)),
            pl.BlockSpec(memory_space=pl.ANY),
            pl.BlockSpec((1, 1, 2 * D_EXPERT), lambda i, be, *_: (be[i], 0, 0)),
            pl.BlockSpec(memory_space=pl.ANY),
            pl.BlockSpec((1, 1, D_MODEL), lambda i, be, *_: (be[i], 0, 0)),
        ],
        out_specs=pl.BlockSpec((MOE_ROWS, D_MODEL // 2), lambda i, *_: (i, 0)),
        scratch_shapes=[
            pltpu.VMEM((D_MODEL, 2 * D_EXPERT), BF16),
            pltpu.VMEM((D_EXPERT, D_MODEL), BF16),
            pltpu.VMEM((2, D_MODEL, 2 * D_EXPERT), F32),
            pltpu.VMEM((2, D_EXPERT, D_MODEL), F32),
            pltpu.SemaphoreType.DMA((2, 2)),
        ],
    )
    return pl.pallas_call(
        _experts_body,
        grid_spec=grid_spec,
        out_shape=jax.ShapeDtypeStruct((n_blocks * MOE_ROWS, D_MODEL // 2), jnp.int32),
        compiler_params=pltpu.CompilerParams(vmem_limit_bytes=VMEM_LIMIT),
        name="experts",
    )(blk_e, n_used, valid, slot, next_e, x_rows, p["w_gu"], p["b_gu"], p["w_down"], p["b_down"])


def _combine_body(yg_ref, x1_ref, gates_ref, fg_ref, out_ref):
    gates = gates_ref[...]
    half = D_MODEL // 2
    acc_left = x1_ref[:, :half]
    acc_right = x1_ref[:, half:]
    for kk in range(TOP_K):
        y_left, y_right = _unpack_bf16_halves(yg_ref[kk])
        gate = gates[:, TOP_K + kk:TOP_K + kk + 1]
        acc_left = acc_left + gate * y_left
        acc_right = acc_right + gate * y_right
    acc = jnp.concatenate([acc_left, acc_right], axis=1)
    out_ref[...] = acc * lax.rsqrt(jnp.mean(acc * acc, axis=-1, keepdims=True) + RMS_EPS) * fg_ref[...]


def _combine(yg, x1, gates, final_g, tm):
    n = x1.shape[0]
    return pl.pallas_call(
        _combine_body,
        grid=(n // tm,),
        in_specs=[
            pl.BlockSpec((TOP_K, tm, D_MODEL // 2), lambda i: (0, i, 0)),
            pl.BlockSpec((tm, D_MODEL), lambda i: (i, 0)),
            pl.BlockSpec((tm, LANES), lambda i: (i, 0)),
            pl.BlockSpec((1, D_MODEL), lambda i: (0, 0)),
        ],
        out_specs=pl.BlockSpec((tm, D_MODEL), lambda i: (i, 0)),
        out_shape=jax.ShapeDtypeStruct((n, D_MODEL), F32),
        compiler_params=pltpu.CompilerParams(vmem_limit_bytes=VMEM_LIMIT),
        name="combine",
    )(yg, x1, gates, final_g)


def _prepare(norm1_g, w_in, rw_mu, gm_ln_g, gm_ln_b, gm_ws, gm_bs, rw_w0, rw_w2, rw_a0, rw_a2, rw_g2,
             rw_k_k, rw_k_a, rw_r_k, rw_gn_g, rw_gn_b, w_out, norm2_g, router_w, router_b,
             w_gu, b_gu, w_down, b_down, final_g):
    l = 0
    row = lambda t: t.reshape(1, -1).astype(F32)
    p = {}
    p["norm1_g"] = row(norm1_g[l])
    w = w_in[l]
    p["w_gm"] = w[:, :2 * GM_WIDTH].astype(BF16)
    pad_cols = LORA_PAD - LORA_COLS
    p["w_z"] = jnp.pad(w[:, 2 * GM_WIDTH:], ((0, 0), (0, pad_cols))).astype(BF16)
    mu = jnp.pad(row(rw_mu[l]), ((0, 0), (0, pad_cols)))
    p["rw_mu"] = jnp.concatenate([1.0 - mu, 0.5 * mu], axis=0)
    p["gm_ln_g"] = row(gm_ln_g[l])
    p["gm_ln_b"] = row(gm_ln_b[l])
    p["gm_ws"] = gm_ws[l].astype(BF16)
    p["gm_bs"] = jnp.broadcast_to(gm_bs[l][:, :, None], (GM_HEADS, GM_CHUNK, GM_HEAD_DIM)).astype(F32)
    zeros = lambda r, c: jnp.zeros((r, c), F32)
    o_a = DECAY_LORA
    o_g = DECAY_LORA + ICLR_LORA

    def lora_matrix(d, with_epilogue):
        blocks = [jnp.concatenate([rw_w2[l, d], zeros(LORA_PAD - DECAY_LORA, RW_WIDTH)], axis=0),
                  jnp.concatenate([zeros(o_a, RW_WIDTH), rw_a2[l, d], zeros(LORA_PAD - o_g, RW_WIDTH)], axis=0)]
        if with_epilogue:
            blocks.append(jnp.concatenate([zeros(o_a, RW_WIDTH), rw_a2[l, 1 - d], zeros(LORA_PAD - o_g, RW_WIDTH)],
                                          axis=0))
            blocks.append(jnp.concatenate([zeros(o_g, RW_WIDTH), rw_g2[l], zeros(LORA_PAD - LORA_COLS, RW_WIDTH)],
                                          axis=0))
        return jnp.concatenate(blocks, axis=1)

    for d, name in ((0, "fwd"), (1, "rev")):
        p["lora_" + name] = lora_matrix(d, with_epilogue=(d == 1)).astype(BF16)
        p["w0_" + name] = row(rw_w0[l, d])
        p["a0_" + name] = row(rw_a0[l, d])
    p["rw_k_k"] = row(rw_k_k[l])
    p["rw_k_a"] = row(rw_k_a[l])
    p["rw_r_k"] = row(rw_r_k[l])
    p["rw_gn_g"] = row(rw_gn_g[l])
    p["rw_gn_b"] = row(rw_gn_b[l])
    ch = jnp.arange(PAIR) // RW_HEAD_DIM
    p["head_ones"] = (ch[:, None] == ch[None, :]).astype(BF16)
    p["w_out"] = w_out[l].astype(BF16)
    p["norm2_g"] = row(norm2_g[l])
    rw_t = router_w[l].astype(F32).T
    hi = rw_t.astype(BF16)
    p["router_hi"] = hi
    p["router_lo"] = (rw_t - hi.astype(F32)).astype(BF16)
    p["router_b"] = router_b[l].astype(F32).reshape(N_EXPERTS, 1)
    p["w_gu"] = w_gu[l]
    p["b_gu"] = b_gu[l].reshape(N_EXPERTS, 1, 2 * D_EXPERT).astype(F32)
    p["w_down"] = w_down[l]
    p["b_down"] = b_down[l].reshape(N_EXPERTS, 1, D_MODEL).astype(F32)
    p["final_g"] = row(final_g)
    return p


def _pick_tile(n, want):
    t = want
    while n % t:
        t //= 2
    return t


def _encoder(x, p):
    b_sz, t_len, d = x.shape
    n = b_sz * t_len
    x2d = x.reshape(n, d)
    gm, z = _mix_in(x2d, p, _pick_tile(t_len, 1024), t_len)
    z3 = z.reshape(b_sz, t_len, Z_COLS)
    yf = _rwkv_pass(z3, None, p, reverse=False, bt=RW_BATCH_TILE)
    rw = _rwkv_pass(z3, yf, p, reverse=True, bt=RW_BATCH_TILE)
    tmo = _pick_tile(n, 1024)
    x1, h2, route, route_t, counts = _mix_out(x2d, gm, rw.reshape(n, RW_WIDTH), p, tmo)
    ids = route_t[:TOP_K].astype(jnp.int32)
    ranks = route_t[2 * TOP_K:3 * TOP_K].astype(jnp.int32)
    cnt = counts[:, 0].astype(jnp.int32)
    padded = (cnt + MOE_ROWS - 1) // MOE_ROWS * MOE_ROWS
    pad_end = jnp.cumsum(padded)
    pad_start = pad_end - padded
    expert_ids = jnp.arange(N_EXPERTS, dtype=jnp.int32)
    start_of = jnp.sum(jnp.where(ids[..., None] == expert_ids, pad_start, 0), axis=-1)
    dest = start_of + ranks
    n_blocks = n * TOP_K // MOE_ROWS + N_EXPERTS
    blk_start = jnp.arange(n_blocks, dtype=jnp.int32) * MOE_ROWS
    blk_e = jnp.minimum(jnp.sum((pad_end[None, :] <= blk_start[:, None]).astype(jnp.int32), axis=1), N_EXPERTS - 1)
    n_used = (pad_end[-1] // MOE_ROWS).astype(jnp.int32).reshape(1)
    blk_onehot = blk_e[:, None] == expert_ids
    blk_cnt = jnp.sum(jnp.where(blk_onehot, cnt, 0), axis=-1)
    blk_first = jnp.sum(jnp.where(blk_onehot, pad_start, 0), axis=-1)
    valid = jnp.clip(blk_cnt - (blk_start - blk_first), 0, MOE_ROWS).astype(jnp.int32)
    window = min(SC_SCATTER_WINDOW, n // (SC_CORES * SC_SUBCORES))
    dest_w = dest.reshape(TOP_K, n // window, window).transpose(1, 0, 2)
    x_rows = _sc_scatter_rows(h2, dest_w, n_blocks * MOE_ROWS)
    present = cnt > 0
    slot_of_expert = (jnp.cumsum(present.astype(jnp.int32)) - 1) % 2
    later = present[None, :] & (expert_ids[None, :] > expert_ids[:, None])
    next_present = jnp.min(jnp.where(later, expert_ids[None, :], N_EXPERTS), axis=1)
    next_present = jnp.where(next_present < N_EXPERTS, next_present, -1)
    slot = jnp.sum(jnp.where(blk_onehot, slot_of_expert, 0), axis=-1).astype(jnp.int32)
    next_e = jnp.sum(jnp.where(blk_onehot, next_present, 0), axis=-1).astype(jnp.int32)
    y_rows = _experts(x_rows, blk_e, n_used, valid, slot, next_e, p)
    yg = _sc_gather_rows(y_rows, dest.reshape(-1)).reshape(TOP_K, n, D_MODEL // 2)
    out = _combine(yg, x1, route, p["final_g"], _pick_tile(n, 1024))
    return out.reshape(b_sz, t_len, d)


def kernel(x_prompt, x_sample, norm1_g, w_in, rw_mu, gm_ln_g, gm_ln_b, gm_ws, gm_bs, rw_w0, rw_w2, rw_a0, rw_a2,
           rw_g2, rw_k_k, rw_k_a, rw_r_k, rw_gn_g, rw_gn_b, w_out, norm2_g, router_w, router_b, w_gu, b_gu,
           w_down, b_down, final_g):
    p = _prepare(norm1_g, w_in, rw_mu, gm_ln_g, gm_ln_b, gm_ws, gm_bs, rw_w0, rw_w2, rw_a0, rw_a2, rw_g2,
                 rw_k_k, rw_k_a, rw_r_k, rw_gn_g, rw_gn_b, w_out, norm2_g, router_w, router_b,
                 w_gu, b_gu, w_down, b_down, final_g)
    return (_encoder(x_prompt, p), _encoder(x_sample, p))
```

```python
import functools

import jax
import jax.numpy as jnp
from jax import lax
from jax.experimental import pallas as pl
from jax.experimental.pallas import tpu as pltpu
from jax.experimental.pallas import tpu_sc as plsc

F32 = jnp.float32
BF16 = jnp.bfloat16

D_MODEL = 1024
GM_WIDTH = 512
RW_WIDTH = 512
GM_HEADS = 4
GM_HEAD_DIM = 128
GM_CHUNK = 128
RW_HEAD_DIM = 64
DECAY_LORA = 32
ICLR_LORA = 32
GATE_LORA = 96
LORA_COLS = DECAY_LORA + ICLR_LORA + GATE_LORA
LORA_PAD = 256
Z_COLS = 3 * RW_WIDTH + LORA_PAD
Z_CHUNK = 256
N_EXPERTS = 32
TOP_K = 4
D_EXPERT = 1024
SWIGLU_LIMIT = 7.0
SWIGLU_ALPHA = 1.702
RMS_EPS = 1e-5
LN_EPS = 1e-5
GN_EPS = 64e-5

LANES = 128
SUBLANES = 8
VMEM_LIMIT = 56 * 1024 * 1024

RW_CHUNK = 64
PAIR = 2 * RW_HEAD_DIM
N_PAIRS = RW_WIDTH // PAIR
RW_BATCH_TILE = 4
MIX_OUT_SUB = 128
MOE_ROWS = 512
MOE_ROW_STEPS = 4
SC_CORES = 2
SC_SUBCORES = 16
SC_WINDOW = 64
SC_SCATTER_WINDOW = 128
ROUTE_ROWS = 16
DECAY_SCALE = 0.6065306597126334
KK_NORM_FLOOR_SQ = 1e-24


def _dot(a, b):
    return jnp.dot(a, b, preferred_element_type=F32)


def _dot_nt(a, b):
    return lax.dot_general(a, b, (((1,), (1,)), ((), ())), preferred_element_type=F32)


def _split(x):
    hi = x.astype(BF16)
    lo = (x - hi.astype(F32)).astype(BF16)
    return hi, lo


def _head_sums(x, pair_ones):
    rows = x.shape[0]
    n_tiles = x.shape[1] // LANES
    stacked = jnp.concatenate([x[:, t * LANES:(t + 1) * LANES] for t in range(n_tiles)], axis=0)
    sums = _dot(stacked.astype(BF16), pair_ones)
    return jnp.concatenate([sums[t * rows:(t + 1) * rows] for t in range(n_tiles)], axis=1)


def _dot2_lhs01(a01, b):
    b_hi, b_lo = _split(b)
    return _dot(a01, b_hi) + _dot(a01, b_lo)


def _pack_bf16_halves(x):
    w = x.shape[1] // 2
    hi = lax.bitcast_convert_type(x[:, :w].astype(BF16).astype(F32), jnp.int32)
    lo = lax.bitcast_convert_type(x[:, w:].astype(BF16).astype(F32), jnp.int32)
    return hi | lax.shift_right_logical(lo, 16)


def _unpack_bf16_halves(words):
    left = lax.bitcast_convert_type(words & jnp.int32(-65536), F32)
    right = lax.bitcast_convert_type(lax.shift_left(words, 16), F32)
    return left, right


def _gelu_tanh(x):
    return 0.5 * x * (1.0 + jnp.tanh(0.7978845608028654 * (x + 0.044715 * (x * x * x))))


def _sigmoid(x):
    return 0.5 * jnp.tanh(0.5 * x) + 0.5


def _mix_in_body(tiles_per_seq, x_ref, xp_ref, xn_ref, g1_ref, wgm_ref, wz_ref, mu_ref, lng_ref, lnb_ref, ws_ref,
                 bs_ref, gm_ref, z_ref):
    i = pl.program_id(0)
    tm = x_ref.shape[0]

    def normed(x):
        return (x * lax.rsqrt(jnp.mean(x * x, axis=-1, keepdims=True) + RMS_EPS) * g1_ref[...]).astype(BF16)

    hb = normed(x_ref[...])
    uv = _dot(hb, wgm_ref[...])
    hb_halo = normed(jnp.concatenate([xp_ref[...], xn_ref[...]], axis=0))
    has_prev = i % tiles_per_seq > 0
    has_next = i % tiles_per_seq < tiles_per_seq - 1
    row8 = lax.broadcasted_iota(jnp.int32, (SUBLANES, Z_CHUNK), 0)
    for j in range(Z_COLS // Z_CHUNK):
        cols = slice(j * Z_CHUNK, (j + 1) * Z_CHUNK)
        z = _dot(hb, wz_ref[:, cols])
        halo = _dot(hb_halo, wz_ref[:, cols])
        prev_row = jnp.where(has_prev, halo[SUBLANES - 1:SUBLANES, :], 0.0)
        next_row = jnp.where(has_next, halo[SUBLANES:SUBLANES + 1, :], 0.0)
        down = pltpu.roll(z, 1, axis=0)
        up = pltpu.roll(z, tm - 1, axis=0)
        prev = jnp.concatenate([jnp.where(row8 == 0, prev_row, down[:SUBLANES]), down[SUBLANES:]], axis=0)
        nxt = jnp.concatenate([up[:tm - SUBLANES], jnp.where(row8 == SUBLANES - 1, next_row, up[tm - SUBLANES:])],
                              axis=0)
        z_ref[:, cols] = z * mu_ref[0:1, cols] + (prev + nxt) * mu_ref[1:2, cols]
    u = _gelu_tanh(uv[:, :GM_WIDTH])
    v = _gelu_tanh(uv[:, GM_WIDTH:])
    mean = jnp.mean(v, axis=-1, keepdims=True)
    vc = v - mean
    var = jnp.mean(vc * vc, axis=-1, keepdims=True)
    vn = (vc * lax.rsqrt(var + LN_EPS) * lng_ref[...] + lnb_ref[...]).astype(BF16)
    for c in range(tm // GM_CHUNK):
        rows = slice(c * GM_CHUNK, (c + 1) * GM_CHUNK)
        for hd in range(GM_HEADS):
            cols = slice(hd * GM_HEAD_DIM, (hd + 1) * GM_HEAD_DIM)
            mixed = _dot(ws_ref[hd], vn[rows, cols]) + bs_ref[hd]
            gm_ref[rows, cols] = (u[rows, cols] * mixed).astype(BF16)


def _mix_in(x2d, p, tm, t_len):
    n = x2d.shape[0]
    assert t_len % tm == 0
    halo_per_tile = tm // SUBLANES
    n_halo = n // SUBLANES
    full = lambda shape: pl.BlockSpec(shape, lambda i: (0,) * len(shape))
    return pl.pallas_call(
        functools.partial(_mix_in_body, t_len // tm),
        grid=(n // tm,),
        in_specs=[
            pl.BlockSpec((tm, D_MODEL), lambda i: (i, 0)),
            pl.BlockSpec((SUBLANES, D_MODEL), lambda i: (jnp.maximum(i * halo_per_tile - 1, 0), 0)),
            pl.BlockSpec((SUBLANES, D_MODEL), lambda i: (jnp.minimum((i + 1) * halo_per_tile, n_halo - 1), 0)),
            full((1, D_MODEL)),
            full((D_MODEL, 2 * GM_WIDTH)),
            full((D_MODEL, Z_COLS)),
            full((2, Z_COLS)),
            full((1, GM_WIDTH)),
            full((1, GM_WIDTH)),
            full((GM_HEADS, GM_CHUNK, GM_CHUNK)),
            full((GM_HEADS, GM_CHUNK, GM_HEAD_DIM)),
        ],
        out_specs=[
            pl.BlockSpec((tm, GM_WIDTH), lambda i: (i, 0)),
            pl.BlockSpec((tm, Z_COLS), lambda i: (i, 0)),
        ],
        out_shape=[
            jax.ShapeDtypeStruct((n, GM_WIDTH), BF16),
            jax.ShapeDtypeStruct((n, Z_COLS), F32),
        ],
        compiler_params=pltpu.CompilerParams(vmem_limit_bytes=VMEM_LIMIT),
        name="mix_in",
    )(x2d, x2d, x2d, p["norm1_g"], p["w_gm"], p["w_z"], p["rw_mu"], p["gm_ln_g"], p["gm_ln_b"], p["gm_ws"],
      p["gm_bs"])


def _rwkv_masks(reverse):
    c = RW_CHUNK
    lane = lax.broadcasted_iota(jnp.int32, (c, PAIR), 1)
    head0 = lane < RW_HEAD_DIM
    lane2 = lax.broadcasted_iota(jnp.int32, (c, 2 * c), 1)
    left = lane2 < c
    ti2 = lax.broadcasted_iota(jnp.int32, (c, 2 * c), 0)
    tj2 = jnp.where(left, lane2, lane2 - c)
    incl2 = (tj2 >= ti2) if reverse else (tj2 <= ti2)
    strict2 = (tj2 > ti2) if reverse else (tj2 < ti2)
    eye2 = jnp.where(tj2 == ti2, 1.0, 0.0).astype(F32)
    si = lax.broadcasted_iota(jnp.int32, (PAIR, PAIR), 0)
    sj = lax.broadcasted_iota(jnp.int32, (PAIR, PAIR), 1)
    same_head = (si // RW_HEAD_DIM) == (sj // RW_HEAD_DIM)
    return dict(incl2=incl2, strict2=strict2, head0=head0, left=left, eye2=eye2, same_head=same_head)


def _stack_heads(x, head0):
    return jnp.concatenate([jnp.where(head0, x, 0.0), jnp.where(head0, 0.0, x)], axis=0)


def _block_diag(x, left):
    return jnp.concatenate([jnp.where(left, x, 0.0), jnp.where(left, 0.0, x)], axis=0)


def _units_chunk(states, ops, m):
    c = RW_CHUNK
    units = range(len(ops))
    bf = lambda t: t.astype(BF16)
    head0, left = m["head0"], m["left"]
    ar = [bf(jnp.concatenate([o["a_t"], o["r_t"]], axis=0)) for o in ops]
    rstack = [bf(jnp.concatenate([_stack_heads(o["b_t"], head0), _stack_heads(o["k_t"], head0)], axis=0))
              for o in ops]
    acat = [_dot_nt(ar[i], rstack[i]) for i in units]
    sa = [_dot_nt(ar[i], bf(states[i])) for i in units]
    a_ab = [jnp.where(m["strict2"], acat[i][:c, :2 * c], 0.0) for i in units]
    a_ak = [bf(jnp.where(m["strict2"], acat[i][:c, 2 * c:], 0.0)) for i in units]
    a_r = [bf(jnp.concatenate([jnp.where(m["incl2"], acat[i][c:, :2 * c], 0.0),
                               jnp.where(m["incl2"], acat[i][c:, 2 * c:], 0.0)], axis=1)) for i in units]
    p = [m["eye2"] + a_ab[i] for i in units]
    lk = [_dot(bf(a_ab[i]), bf(_block_diag(a_ab[i], left))) for i in units]
    vstack = [bf(_stack_heads(o["v"], head0)) for o in ops]
    rhs = [sa[i][:c] + _dot(a_ak[i], vstack[i]) for i in units]
    power = 2
    while 2 * power < c:
        res = [_dot(bf(jnp.concatenate([p[i], lk[i]], axis=0)), bf(_block_diag(lk[i], left))) for i in units]
        p = [p[i] + res[i][:c] for i in units]
        lk = [res[i][c:] for i in units]
        power *= 2
    p = [p[i] + _dot(bf(p[i]), bf(_block_diag(lk[i], left))) for i in units]
    u = [_dot(bf(p[i]), bf(_stack_heads(rhs[i], head0))) for i in units]
    ys = [sa[i][c:] + _dot(a_r[i], jnp.concatenate([bf(_stack_heads(u[i], head0)), vstack[i]], axis=0))
          for i in units]
    upd = [_dot(bf(jnp.concatenate([u[i], ops[i]["v"]], axis=0).T),
                bf(jnp.concatenate([ops[i]["b_h"], ops[i]["k_h"]], axis=0))) for i in units]
    new_states = [states[i] * ops[i]["w_tot"] + jnp.where(m["same_head"], upd[i], 0.0) for i in units]
    return new_states, ys


def _rwkv_body(reverse, bt, *refs):
    if reverse:
        (zc_ref, yf_ref, lw_ref, w0_ref, a0_ref, a0o_ref, kk_ref, ka_ref,
         rk_ref, bd_ref, gng_ref, gnb_ref, out_ref, state_ref) = refs
    else:
        (zc_ref, lw_ref, w0_ref, a0_ref, kk_ref, ka_ref,
         bd_ref, out_ref, state_ref) = refs
    c = RW_CHUNK

    @pl.when(pl.program_id(1) == 0)
    def _():
        state_ref[...] = jnp.zeros_like(state_ref)

    rows = bt * c
    m = _rwkv_masks(reverse)
    lora_lane = lax.broadcasted_iota(jnp.int32, (rows, LORA_PAD), 1)
    bd = bd_ref[...]
    zf = zc_ref[...].reshape(rows, Z_COLS)
    r = zf[:, :RW_WIDTH]
    k = zf[:, RW_WIDTH:2 * RW_WIDTH]
    v = zf[:, 2 * RW_WIDTH:3 * RW_WIDTH]
    lo = zf[:, 3 * RW_WIDTH:]
    act = jnp.where(lora_lane < DECAY_LORA, jnp.tanh(lo),
                    jnp.where(lora_lane < DECAY_LORA + ICLR_LORA, lo, _sigmoid(lo)))
    act_bf = act.astype(BF16)
    w_in = w0_ref[...] + _dot(act_bf, lw_ref[:, :RW_WIDTH])
    proj = _dot(act_bf, lw_ref[:, RW_WIDTH:])
    logw = -DECAY_SCALE * _sigmoid(w_in)
    a_sig = _sigmoid(a0_ref[...] + proj[:, :RW_WIDTH])
    kk = k * kk_ref[...]
    kk = kk * lax.rsqrt(jnp.maximum(_head_sums(kk * kk, bd), KK_NORM_FLOOR_SQ))
    kd = k * (1.0 + (a_sig - 1.0) * ka_ref[...])
    a_vec = -kk
    b_vec = kk * a_sig
    ri = lax.broadcasted_iota(jnp.int32, (rows, rows), 0)
    rj = lax.broadcasted_iota(jnp.int32, (rows, rows), 1)
    ordered = (rj >= ri) if reverse else (rj <= ri)
    tri = jnp.where((ri // c) == (rj // c), jnp.where(ordered, 1.0, 0.0), 0.0).astype(BF16)
    cum = _dot2_lhs01(tri, logw)
    edge = 0 if reverse else c - 1
    tot = jnp.concatenate([jnp.broadcast_to(cum[b * c + edge:b * c + edge + 1, :], (c, RW_WIDTH))
                           for b in range(bt)], axis=0)
    w_inv = jnp.exp(-cum)
    w_end = jnp.exp(tot - cum)
    w_tot = jnp.exp(tot)
    full_ops = dict(r_t=r * jnp.exp(cum), a_t=a_vec * jnp.exp(cum - logw), b_t=b_vec * w_inv, k_t=kd * w_inv,
                    b_h=b_vec * w_end, k_h=kd * w_end, v=v)
    ops, states = [], []
    for b in range(bt):
        for pi in range(N_PAIRS):
            ln = slice(pi * PAIR, (pi + 1) * PAIR)
            unit = {name: t[b * c:(b + 1) * c, ln] for name, t in full_ops.items()}
            unit["w_tot"] = w_tot[b * c:b * c + 1, ln]
            ops.append(unit)
            states.append(state_ref[b, pi])
    new_states, ys = _units_chunk(states, ops, m)
    for b in range(bt):
        for pi in range(N_PAIRS):
            state_ref[b, pi] = new_states[b * N_PAIRS + pi]
    y = jnp.concatenate([jnp.concatenate(ys[b * N_PAIRS:(b + 1) * N_PAIRS], axis=1) for b in range(bt)], axis=0)
    if not reverse:
        out_ref[...] = y.reshape(bt, c, RW_WIDTH)
    else:
        y = y + yf_ref[...].reshape(rows, RW_WIDTH)
        a_other = _sigmoid(a0o_ref[...] + proj[:, RW_WIDTH:2 * RW_WIDTH])
        kd_sum = kd + k * (1.0 + (a_other - 1.0) * ka_ref[...])
        bonus = _head_sums(r * kd_sum * rk_ref[...], bd) * v
        inv_n = 1.0 / RW_HEAD_DIM
        mean = _head_sums(y, bd) * inv_n
        yc = y - mean
        var = _head_sums(yc * yc, bd) * inv_n
        yn = yc * lax.rsqrt(var + GN_EPS) * gng_ref[...] + gnb_ref[...]
        g = proj[:, 2 * RW_WIDTH:]
        out_ref[...] = ((yn + bonus) * g).astype(BF16).reshape(bt, c, RW_WIDTH)


def _rwkv_pass(z3, yf, p, reverse, bt):
    b_sz, t_len, _ = z3.shape
    c = RW_CHUNK
    n_chunks = t_len // c

    def cidx(ci):
        return (n_chunks - 1 - ci) if reverse else ci

    full = lambda shape: pl.BlockSpec(shape, lambda bi, ci: (0,) * len(shape))
    row512 = full((1, RW_WIDTH))
    d = "rev" if reverse else "fwd"
    n_proj = p["lora_" + d].shape[1]
    in_specs = [pl.BlockSpec((bt, c, Z_COLS), lambda bi, ci: (bi, cidx(ci), 0))]
    args = [z3]
    if reverse:
        in_specs.append(pl.BlockSpec((bt, c, RW_WIDTH), lambda bi, ci: (bi, cidx(ci), 0)))
        args.append(yf)
    in_specs += [full((LORA_PAD, n_proj)), row512, row512]
    args += [p["lora_" + d], p["w0_" + d], p["a0_" + d]]
    if reverse:
        in_specs.append(row512)
        args.append(p["a0_fwd"])
    in_specs += [row512, row512]
    args += [p["rw_k_k"], p["rw_k_a"]]
    if reverse:
        in_specs.append(row512)
        args.append(p["rw_r_k"])
    in_specs.append(full((PAIR, PAIR)))
    args.append(p["head_ones"])
    if reverse:
        in_specs += [row512, row512]
        args += [p["rw_gn_g"], p["rw_gn_b"]]
    return pl.pallas_call(
        functools.partial(_rwkv_body, reverse, bt),
        grid=(b_sz // bt, n_chunks),
        in_specs=in_specs,
        out_specs=pl.BlockSpec((bt, c, RW_WIDTH), lambda bi, ci: (bi, cidx(ci), 0)),
        out_shape=jax.ShapeDtypeStruct((b_sz, t_len, RW_WIDTH), BF16 if reverse else F32),
        scratch_shapes=[pltpu.VMEM((bt, N_PAIRS, PAIR, PAIR), F32)],
        compiler_params=pltpu.CompilerParams(vmem_limit_bytes=VMEM_LIMIT),
        name="rwkv_" + d,
    )(*args)


def _mix_out_body(x_ref, gm_ref, rw_ref, wo_ref, g2_ref, rw_hi_ref, rw_lo_ref, rb_ref, earlier_ref,
                  x1_ref, h2_ref, route_ref, route_t_ref, counts_ref, carry_ref):
    i = pl.program_id(0)

    @pl.when(i == 0)
    def _():
        carry_ref[...] = jnp.zeros_like(carry_ref)

    sub = earlier_ref.shape[0]
    subs = range(x_ref.shape[0] // sub)
    rows = [slice(s * sub, (s + 1) * sub) for s in subs]
    x1 = [x_ref[r, :] + _dot(gm_ref[r, :], wo_ref[:GM_WIDTH, :]) + _dot(rw_ref[r, :], wo_ref[GM_WIDTH:, :])
          for r in rows]
    h2 = [x1[s] * lax.rsqrt(jnp.mean(x1[s] * x1[s], axis=-1, keepdims=True) + RMS_EPS) * g2_ref[...] for s in subs]
    for s in subs:
        x1_ref[rows[s], :] = x1[s]
        h2_ref[rows[s], :] = _pack_bf16_halves(h2[s])
    split = [_split(h) for h in h2]
    logits = [_dot_nt(rw_hi_ref[...], hi) + _dot_nt(rw_hi_ref[...], lo) + _dot_nt(rw_lo_ref[...], hi) + rb_ref[...]
              for hi, lo in split]
    expert = lax.broadcasted_iota(jnp.int32, (N_EXPERTS, sub), 0).astype(F32)
    work = logits
    vals, ids, hits = [[] for _ in subs], [[] for _ in subs], [[] for _ in subs]
    onehot = [jnp.zeros((N_EXPERTS, sub), F32) for _ in subs]
    for _ in range(TOP_K):
        for s in subs:
            mx = jnp.max(work[s], axis=0, keepdims=True)
            idx = jnp.min(jnp.where(work[s] == mx, expert, float(N_EXPERTS)), axis=0, keepdims=True)
            hit = expert == idx
            vals[s].append(mx)
            ids[s].append(idx)
            hits[s].append(hit)
            onehot[s] = jnp.where(hit, 1.0, onehot[s])
        work = [jnp.where(hits[s][-1], -jnp.inf, work[s]) for s in subs]
    earlier_counts = [_dot(onehot[s].astype(BF16), earlier_ref[...]) for s in subs]
    carry = carry_ref[:, 0:1]
    for s in subs:
        exps = [jnp.exp(vk - vals[s][0]) for vk in vals[s]]
        denom = exps[0] + exps[1] + exps[2] + exps[3]
        ranks = carry + earlier_counts[s]
        rank_rows = [jnp.sum(jnp.where(hit, ranks, 0.0), axis=0, keepdims=True) for hit in hits[s]]
        route_t = jnp.concatenate(ids[s] + [e / denom for e in exps] + rank_rows
                                  + [jnp.zeros((ROUTE_ROWS - 3 * TOP_K, sub), F32)], axis=0)
        route_t_ref[:, rows[s]] = route_t
        route_ref[rows[s], :] = jnp.concatenate([route_t, jnp.zeros((LANES - ROUTE_ROWS, sub), F32)], axis=0).T
        carry = carry + jnp.sum(onehot[s], axis=1, keepdims=True)
    carry_ref[...] = jnp.broadcast_to(carry, carry_ref.shape)
    counts_ref[...] = jnp.broadcast_to(carry, counts_ref.shape)


def _mix_out(x2d, gm, rw, p, tm):
    n = x2d.shape[0]
    full = lambda shape: pl.BlockSpec(shape, lambda i: (0,) * len(shape))
    tile = lambda w: pl.BlockSpec((tm, w), lambda i: (i, 0))
    sub = min(MIX_OUT_SUB, tm)
    earlier = (jnp.arange(sub)[:, None] < jnp.arange(sub)[None, :]).astype(BF16)
    return pl.pallas_call(
        _mix_out_body,
        grid=(n // tm,),
        in_specs=[tile(D_MODEL), tile(GM_WIDTH), tile(RW_WIDTH), full((D_MODEL, D_MODEL)), full((1, D_MODEL)),
                  full((N_EXPERTS, D_MODEL)), full((N_EXPERTS, D_MODEL)), full((N_EXPERTS, 1)), full((sub, sub))],
        out_specs=[tile(D_MODEL), tile(D_MODEL // 2), tile(LANES), pl.BlockSpec((ROUTE_ROWS, tm), lambda i: (0, i)),
                   full((N_EXPERTS, LANES))],
        out_shape=[
            jax.ShapeDtypeStruct((n, D_MODEL), F32),
            jax.ShapeDtypeStruct((n, D_MODEL // 2), jnp.int32),
            jax.ShapeDtypeStruct((n, LANES), F32),
            jax.ShapeDtypeStruct((ROUTE_ROWS, n), F32),
            jax.ShapeDtypeStruct((N_EXPERTS, LANES), F32),
        ],
        scratch_shapes=[pltpu.VMEM((N_EXPERTS, LANES), F32)],
        compiler_params=pltpu.CompilerParams(vmem_limit_bytes=VMEM_LIMIT),
        name="mix_out",
    )(x2d, gm, rw, p["w_out"], p["norm2_g"], p["router_hi"], p["router_lo"], p["router_b"], earlier)


def _sc_gather_rows(table, idx):
    n_idx = idx.shape[0]
    width = table.shape[1]
    n_workers = SC_CORES * SC_SUBCORES
    per_worker = n_idx // n_workers
    n_windows = per_worker // SC_WINDOW
    assert per_worker * n_workers == n_idx and n_windows * SC_WINDOW == per_worker
    mesh = plsc.VectorSubcoreMesh(core_axis_name="c", subcore_axis_name="s")

    @functools.partial(
        pl.kernel, mesh=mesh,
        out_type=jax.ShapeDtypeStruct((n_idx, width), table.dtype),
        scratch_types=[
            pltpu.VMEM((SC_WINDOW,), jnp.int32),
            pltpu.VMEM((SC_WINDOW, width), table.dtype),
            pltpu.SemaphoreType.DMA,
        ],
        name="sc_gather_rows",
    )
    def gather(table_hbm, idx_hbm, out_hbm, idx_v, rows_v, sem):
        worker = lax.axis_index("s") * SC_CORES + lax.axis_index("c")
        base = worker * per_worker

        @pl.loop(0, n_windows)
        def _(j):
            off = base + j * SC_WINDOW
            pltpu.sync_copy(idx_hbm.at[pl.ds(off, SC_WINDOW)], idx_v)
            pltpu.async_copy(table_hbm.at[idx_v], rows_v, sem).wait()
            pltpu.sync_copy(rows_v, out_hbm.at[pl.ds(off, SC_WINDOW)])

    return gather(table, idx)


def _sc_scatter_rows(src, dest_w, n_out):
    n, width = src.shape
    n_windows, top_k, window = dest_w.shape
    n_workers = SC_CORES * SC_SUBCORES
    per_worker = n_windows // n_workers
    assert n_windows * window == n and per_worker * n_workers == n_windows
    mesh = plsc.VectorSubcoreMesh(core_axis_name="c", subcore_axis_name="s")

    @functools.partial(
        pl.kernel, mesh=mesh,
        out_type=jax.ShapeDtypeStruct((n_out, width), src.dtype),
        scratch_types=[
            pltpu.VMEM((top_k, window), jnp.int32),
            pltpu.VMEM((window, width), src.dtype),
        ],
        name="sc_scatter_rows",
    )
    def scatter(src_hbm, dest_hbm, out_hbm, idx_v, rows_v):
        worker = lax.axis_index("s") * SC_CORES + lax.axis_index("c")

        @pl.loop(0, per_worker)
        def _(j):
            g = worker * per_worker + j
            pltpu.sync_copy(dest_hbm.at[g], idx_v)
            pltpu.sync_copy(src_hbm.at[pl.ds(g * window, window)], rows_v)
            for k in range(top_k):
                pltpu.sync_copy(rows_v, out_hbm.at[idx_v.at[k]])

    return scatter(src, dest_w)


def _weight_copies(wgu_hbm, wd_hbm, wgu_buf, wd_buf, sem, expert, slot):
    return (pltpu.make_async_copy(wgu_hbm.at[expert], wgu_buf.at[slot], sem.at[0, slot]),
            pltpu.make_async_copy(wd_hbm.at[expert], wd_buf.at[slot], sem.at[1, slot]))


def _experts_body(blk_e_ref, n_used_ref, valid_ref, slot_ref, next_e_ref, x_ref, wgu_hbm, bgu_ref, wd_hbm, bd_ref,
                  out_ref, wgu_bf, wd_bf, wgu_buf, wd_buf, sem):
    i = pl.program_id(0)
    n_used = n_used_ref[0]
    expert = blk_e_ref[i]
    slot = slot_ref[i]
    copies = functools.partial(_weight_copies, wgu_hbm, wd_hbm, wgu_buf, wd_buf, sem)

    @pl.when(i == 0)
    def _():
        for cp in copies(expert, slot):
            cp.start()

    @pl.when((i < n_used) & ((i == 0) | (expert != blk_e_ref[jnp.maximum(i - 1, 0)])))
    def _():
        for cp in copies(expert, slot):
            cp.wait()
        wgu_bf[...] = wgu_buf[slot].astype(BF16)
        wd_bf[...] = wd_buf[slot].astype(BF16)
        following = next_e_ref[i]

        @pl.when(following >= 0)
        def _():
            for cp in copies(following, 1 - slot):
                cp.start()

    valid = jnp.where(i < n_used, valid_ref[i], 0)

    def expert_mlp(n_rows):
        row = lax.broadcasted_iota(jnp.int32, (n_rows, 1), 0)
        x_left, x_right = _unpack_bf16_halves(jnp.where(row < valid, x_ref[:n_rows, :], 0))
        half = D_MODEL // 2
        gu = (_dot(x_left.astype(BF16), wgu_bf[:half, :]) + _dot(x_right.astype(BF16), wgu_bf[half:, :])
              + bgu_ref[0])
        gate = jnp.minimum(gu[:, :D_EXPERT], SWIGLU_LIMIT)
        up = jnp.clip(gu[:, D_EXPERT:], -SWIGLU_LIMIT, SWIGLU_LIMIT)
        act = gate * _sigmoid(gate * SWIGLU_ALPHA) * (up + 1.0)
        out_ref[:n_rows, :] = _pack_bf16_halves(_dot(act.astype(BF16), wd_bf[...]) + bd_ref[0])

    for quarters in range(1, MOE_ROW_STEPS + 1):
        n_rows = quarters * MOE_ROWS // MOE_ROW_STEPS

        @pl.when((valid > n_rows - MOE_ROWS // MOE_ROW_STEPS) & (valid <= n_rows))
        def _(n_rows=n_rows):
            expert_mlp(n_rows)
            if n_rows < MOE_ROWS:
                out_ref[n_rows:, :] = jnp.zeros((MOE_ROWS - n_rows, D_MODEL // 2), jnp.int32)

    @pl.when(valid == 0)
    def _():
        out_ref[...] = jnp.zeros_like(out_ref)


def _experts(x_rows, blk_e, n_used, valid, slot, next_e, p):
    n_blocks = blk_e.shape[0]
    grid_spec = pltpu.PrefetchScalarGridSpec(
        num_scalar_prefetch=5,
        grid=(n_blocks,),
        in_specs=[
            pl.BlockSpec((MOE_ROWS, D_MODEL // 2), lambda i, be, nu, *_: (jnp.minimum(i, nu[0] - 1), 0)),
            pl.BlockSpec(memory_space=pl.ANY),
            pl.BlockSpec((1, 1, 2 * D_EXPERT), lambda i, be, *_: (be[i], 0, 0)),
            pl.BlockSpec(memory_space=pl.ANY),
            pl.BlockSpec((1, 1, D_MODEL), lambda i, be, *_: (be[i], 0, 0)),
        ],
        out_specs=pl.BlockSpec((MOE_ROWS, D_MODEL // 2), lambda i, *_: (i, 0)),
        scratch_shapes=[
            pltpu.VMEM((D_MODEL, 2 * D_EXPERT), BF16),
            pltpu.VMEM((D_EXPERT, D_MODEL), BF16),
            pltpu.VMEM((2, D_MODEL, 2 * D_EXPERT), F32),
            pltpu.VMEM((2, D_EXPERT, D_MODEL), F32),
            pltpu.SemaphoreType.DMA((2, 2)),
        ],
    )
    return pl.pallas_call(
        _experts_body,
        grid_spec=grid_spec,
        out_shape=jax.ShapeDtypeStruct((n_blocks * MOE_ROWS, D_MODEL // 2), jnp.int32),
        compiler_params=pltpu.CompilerParams(vmem_limit_bytes=VMEM_LIMIT),
        name="experts",
    )(blk_e, n_used, valid, slot, next_e, x_rows, p["w_gu"], p["b_gu"], p["w_down"], p["b_down"])


def _combine_body(yg_ref, x1_ref, gates_ref, fg_ref, out_ref):
    gates = gates_ref[...]
    half = D_MODEL // 2
    acc_left = x1_ref[:, :half]
    acc_right = x1_ref[:, half:]
    for kk in range(TOP_K):
        y_left, y_right = _unpack_bf16_halves(yg_ref[kk])
        gate = gates[:, TOP_K + kk:TOP_K + kk + 1]
        acc_left = acc_left + gate * y_left
        acc_right = acc_right + gate * y_right
    acc = jnp.concatenate([acc_left, acc_right], axis=1)
    out_ref[...] = acc * lax.rsqrt(jnp.mean(acc * acc, axis=-1, keepdims=True) + RMS_EPS) * fg_ref[...]


def _combine(yg, x1, gates, final_g, tm):
    n = x1.shape[0]
    return pl.pallas_call(
        _combine_body,
        grid=(n // tm,),
        in_specs=[
            pl.BlockSpec((TOP_K, tm, D_MODEL // 2), lambda i: (0, i, 0)),
            pl.BlockSpec((tm, D_MODEL), lambda i: (i, 0)),
            pl.BlockSpec((tm, LANES), lambda i: (i, 0)),
            pl.BlockSpec((1, D_MODEL), lambda i: (0, 0)),
        ],
        out_specs=pl.BlockSpec((tm, D_MODEL), lambda i: (i, 0)),
        out_shape=jax.ShapeDtypeStruct((n, D_MODEL), F32),
        compiler_params=pltpu.CompilerParams(vmem_limit_bytes=VMEM_LIMIT),
        name="combine",
    )(yg, x1, gates, final_g)


def _prepare(norm1_g, w_in, rw_mu, gm_ln_g, gm_ln_b, gm_ws, gm_bs, rw_w0, rw_w2, rw_a0, rw_a2, rw_g2,
             rw_k_k, rw_k_a, rw_r_k, rw_gn_g, rw_gn_b, w_out, norm2_g, router_w, router_b,
             w_gu, b_gu, w_down, b_down, final_g):
    l = 0
    row = lambda t: t.reshape(1, -1).astype(F32)
    p = {}
    p["norm1_g"] = row(norm1_g[l])
    w = w_in[l]
    p["w_gm"] = w[:, :2 * GM_WIDTH].astype(BF16)
    pad_cols = LORA_PAD - LORA_COLS
    p["w_z"] = jnp.pad(w[:, 2 * GM_WIDTH:], ((0, 0), (0, pad_cols))).astype(BF16)
    mu = jnp.pad(row(rw_mu[l]), ((0, 0), (0, pad_cols)))
    p["rw_mu"] = jnp.concatenate([1.0 - mu, 0.5 * mu], axis=0)
    p["gm_ln_g"] = row(gm_ln_g[l])
    p["gm_ln_b"] = row(gm_ln_b[l])
    p["gm_ws"] = gm_ws[l].astype(BF16)
    p["gm_bs"] = jnp.broadcast_to(gm_bs[l][:, :, None], (GM_HEADS, GM_CHUNK, GM_HEAD_DIM)).astype(F32)
    zeros = lambda r, c: jnp.zeros((r, c), F32)
    o_a = DECAY_LORA
    o_g = DECAY_LORA + ICLR_LORA

    def lora_matrix(d, with_epilogue):
        blocks = [jnp.concatenate([rw_w2[l, d], zeros(LORA_PAD - DECAY_LORA, RW_WIDTH)], axis=0),
                  jnp.concatenate([zeros(o_a, RW_WIDTH), rw_a2[l, d], zeros(LORA_PAD - o_g, RW_WIDTH)], axis=0)]
        if with_epilogue:
            blocks.append(jnp.concatenate([zeros(o_a, RW_WIDTH), rw_a2[l, 1 - d], zeros(LORA_PAD - o_g, RW_WIDTH)],
                                          axis=0))
            blocks.append(jnp.concatenate([zeros(o_g, RW_WIDTH), rw_g2[l], zeros(LORA_PAD - LORA_COLS, RW_WIDTH)],
                                          axis=0))
        return jnp.concatenate(blocks, axis=1)

    for d, name in ((0, "fwd"), (1, "rev")):
        p["lora_" + name] = lora_matrix(d, with_epilogue=(d == 1)).astype(BF16)
        p["w0_" + name] = row(rw_w0[l, d])
        p["a0_" + name] = row(rw_a0[l, d])
    p["rw_k_k"] = row(rw_k_k[l])
    p["rw_k_a"] = row(rw_k_a[l])
    p["rw_r_k"] = row(rw_r_k[l])
    p["rw_gn_g"] = row(rw_gn_g[l])
    p["rw_gn_b"] = row(rw_gn_b[l])
    ch = jnp.arange(PAIR) // RW_HEAD_DIM
    p["head_ones"] = (ch[:, None] == ch[None, :]).astype(BF16)
    p["w_out"] = w_out[l].astype(BF16)
    p["norm2_g"] = row(norm2_g[l])
    rw_t = router_w[l].astype(F32).T
    hi = rw_t.astype(BF16)
    p["router_hi"] = hi
    p["router_lo"] = (rw_t - hi.astype(F32)).astype(BF16)
    p["router_b"] = router_b[l].astype(F32).reshape(N_EXPERTS, 1)
    p["w_gu"] = w_gu[l]
    p["b_gu"] = b_gu[l].reshape(N_EXPERTS, 1, 2 * D_EXPERT).astype(F32)
    p["w_down"] = w_down[l]
    p["b_down"] = b_down[l].reshape(N_EXPERTS, 1, D_MODEL).astype(F32)
    p["final_g"] = row(final_g)
    return p


def _pick_tile(n, want):
    t = want
    while n % t:
        t //= 2
    return t


def _encoder(x, p):
    b_sz, t_len, d = x.shape
    n = b_sz * t_len
    x2d = x.reshape(n, d)
    gm, z = _mix_in(x2d, p, _pick_tile(t_len, 1024), t_len)
    z3 = z.reshape(b_sz, t_len, Z_COLS)
    yf = _rwkv_pass(z3, None, p, reverse=False, bt=RW_BATCH_TILE)
    rw = _rwkv_pass(z3, yf, p, reverse=True, bt=RW_BATCH_TILE)
    tmo = _pick_tile(n, 1024)
    x1, h2, route, route_t, counts = _mix_out(x2d, gm, rw.reshape(n, RW_WIDTH), p, tmo)
    ids = route_t[:TOP_K].astype(jnp.int32)
    ranks = route_t[2 * TOP_K:3 * TOP_K].astype(jnp.int32)
    cnt = counts[:, 0].astype(jnp.int32)
    padded = (cnt + MOE_ROWS - 1) // MOE_ROWS * MOE_ROWS
    pad_end = jnp.cumsum(padded)
    pad_start = pad_end - padded
    expert_ids = jnp.arange(N_EXPERTS, dtype=jnp.int32)
    start_of = jnp.sum(jnp.where(ids[..., None] == expert_ids, pad_start, 0), axis=-1)
    dest = start_of + ranks
    n_blocks = n * TOP_K // MOE_ROWS + N_EXPERTS
    blk_start = jnp.arange(n_blocks, dtype=jnp.int32) * MOE_ROWS
    blk_e = jnp.minimum(jnp.sum((pad_end[None, :] <= blk_start[:, None]).astype(jnp.int32), axis=1), N_EXPERTS - 1)
    n_used = (pad_end[-1] // MOE_ROWS).astype(jnp.int32).reshape(1)
    blk_onehot = blk_e[:, None] == expert_ids
    blk_cnt = jnp.sum(jnp.where(blk_onehot, cnt, 0), axis=-1)
    blk_first = jnp.sum(jnp.where(blk_onehot, pad_start, 0), axis=-1)
    valid = jnp.clip(blk_cnt - (blk_start - blk_first), 0, MOE_ROWS).astype(jnp.int32)
    window = min(SC_SCATTER_WINDOW, n // (SC_CORES * SC_SUBCORES))
    dest_w = dest.reshape(TOP_K, n // window, window).transpose(1, 0, 2)
    x_rows = _sc_scatter_rows(h2, dest_w, n_blocks * MOE_ROWS)
    present = cnt > 0
    slot_of_expert = (jnp.cumsum(present.astype(jnp.int32)) - 1) % 2
    later = present[None, :] & (expert_ids[None, :] > expert_ids[:, None])
    next_present = jnp.min(jnp.where(later, expert_ids[None, :], N_EXPERTS), axis=1)
    next_present = jnp.where(next_present < N_EXPERTS, next_present, -1)
    slot = jnp.sum(jnp.where(blk_onehot, slot_of_expert, 0), axis=-1).astype(jnp.int32)
    next_e = jnp.sum(jnp.where(blk_onehot, next_present, 0), axis=-1).astype(jnp.int32)
    y_rows = _experts(x_rows, blk_e, n_used, valid, slot, next_e, p)
    yg = _sc_gather_rows(y_rows, dest.reshape(-1)).reshape(TOP_K, n, D_MODEL // 2)
    out = _combine(yg, x1, route, p["final_g"], _pick_tile(n, 1024))
    return out.reshape(b_sz, t_len, d)


def kernel(x_prompt, x_sample, norm1_g, w_in, rw_mu, gm_ln_g, gm_ln_b, gm_ws, gm_bs, rw_w0, rw_w2, rw_a0, rw_a2,
           rw_g2, rw_k_k, rw_k_a, rw_r_k, rw_gn_g, rw_gn_b, w_out, norm2_g, router_w, router_b, w_gu, b_gu,
           w_down, b_down, final_g):
    p = _prepare(norm1_g, w_in, rw_mu, gm_ln_g, gm_ln_b, gm_ws, gm_bs, rw_w0, rw_w2, rw_a0, rw_a2, rw_g2,
                 rw_k_k, rw_k_a, rw_r_k, rw_gn_g, rw_gn_b, w_out, norm2_g, router_w, router_b,
                 w_gu, b_gu, w_down, b_down, final_g)
    return (_encoder(x_prompt, p), _encoder(x_sample, p))
```

```python
import functools

import jax
import jax.numpy as jnp
from jax import lax
from jax.experimental import pallas as pl
from jax.experimental.pallas import tpu as pltpu
from jax.experimental.pallas import tpu_sc as plsc

F32 = jnp.float32
BF16 = jnp.bfloat16

D_MODEL = 1024
GM_WIDTH = 512
RW_WIDTH = 512
GM_HEADS = 4
GM_HEAD_DIM = 128
GM_CHUNK = 128
RW_HEAD_DIM = 64
DECAY_LORA = 32
ICLR_LORA = 32
GATE_LORA = 96
LORA_COLS = DECAY_LORA + ICLR_LORA + GATE_LORA
LORA_PAD = 256
Z_COLS = 3 * RW_WIDTH + LORA_PAD
Z_CHUNK = 256
N_EXPERTS = 32
TOP_K = 4
D_EXPERT = 1024
SWIGLU_LIMIT = 7.0
SWIGLU_ALPHA = 1.702
RMS_EPS = 1e-5
LN_EPS = 1e-5
GN_EPS = 64e-5

LANES = 128
SUBLANES = 8
VMEM_LIMIT = 56 * 1024 * 1024

RW_CHUNK = 64
PAIR = 2 * RW_HEAD_DIM
N_PAIRS = RW_WIDTH // PAIR
RW_BATCH_TILE = 4
MIX_OUT_SUB = 128
MOE_ROWS = 512
MOE_ROW_STEPS = 4
SC_CORES = 2
SC_SUBCORES = 16
SC_WINDOW = 64
SC_SCATTER_WINDOW = 128
ROUTE_ROWS = 16
DECAY_SCALE = 0.6065306597126334
KK_NORM_FLOOR_SQ = 1e-24


def _dot(a, b):
    return jnp.dot(a, b, preferred_element_type=F32)


def _dot_nt(a, b):
    return lax.dot_general(a, b, (((1,), (1,)), ((), ())), preferred_element_type=F32)


def _split(x):
    hi = x.astype(BF16)
    lo = (x - hi.astype(F32)).astype(BF16)
    return hi, lo


def _head_sums(x, pair_ones):
    rows = x.shape[0]
    n_tiles = x.shape[1] // LANES
    stacked = jnp.concatenate([x[:, t * LANES:(t + 1) * LANES] for t in range(n_tiles)], axis=0)
    sums = _dot(stacked.astype(BF16), pair_ones)
    return jnp.concatenate([sums[t * rows:(t + 1) * rows] for t in range(n_tiles)], axis=1)


def _dot2_lhs01(a01, b):
    b_hi, b_lo = _split(b)
    return _dot(a01, b_hi) + _dot(a01, b_lo)


def _pack_bf16_halves(x):
    w = x.shape[1] // 2
    hi = lax.bitcast_convert_type(x[:, :w].astype(BF16).astype(F32), jnp.int32)
    lo = lax.bitcast_convert_type(x[:, w:].astype(BF16).astype(F32), jnp.int32)
    return hi | lax.shift_right_logical(lo, 16)


def _unpack_bf16_halves(words):
    left = lax.bitcast_convert_type(words & jnp.int32(-65536), F32)
    right = lax.bitcast_convert_type(lax.shift_left(words, 16), F32)
    return left, right


def _gelu_tanh(x):
    return 0.5 * x * (1.0 + jnp.tanh(0.7978845608028654 * (x + 0.044715 * (x * x * x))))


def _sigmoid(x):
    return 0.5 * jnp.tanh(0.5 * x) + 0.5


def _mix_in_body(tiles_per_seq, x_ref, xp_ref, xn_ref, g1_ref, wgm_ref, wz_ref, mu_ref, lng_ref, lnb_ref, ws_ref,
                 bs_ref, gm_ref, z_ref):
    i = pl.program_id(0)
    tm = x_ref.shape[0]

    def normed(x):
        return (x * lax.rsqrt(jnp.mean(x * x, axis=-1, keepdims=True) + RMS_EPS) * g1_ref[...]).astype(BF16)

    hb = normed(x_ref[...])
    uv = _dot(hb, wgm_ref[...])
    hb_halo = normed(jnp.concatenate([xp_ref[...], xn_ref[...]], axis=0))
    has_prev = i % tiles_per_seq > 0
    has_next = i % tiles_per_seq < tiles_per_seq - 1
    row8 = lax.broadcasted_iota(jnp.int32, (SUBLANES, Z_CHUNK), 0)
    for j in range(Z_COLS // Z_CHUNK):
        cols = slice(j * Z_CHUNK, (j + 1) * Z_CHUNK)
        z = _dot(hb, wz_ref[:, cols])
        halo = _dot(hb_halo, wz_ref[:, cols])
        prev_row = jnp.where(has_prev, halo[SUBLANES - 1:SUBLANES, :], 0.0)
        next_row = jnp.where(has_next, halo[SUBLANES:SUBLANES + 1, :], 0.0)
        down = pltpu.roll(z, 1, axis=0)
        up = pltpu.roll(z, tm - 1, axis=0)
        prev = jnp.concatenate([jnp.where(row8 == 0, prev_row, down[:SUBLANES]), down[SUBLANES:]], axis=0)
        nxt = jnp.concatenate([up[:tm - SUBLANES], jnp.where(row8 == SUBLANES - 1, next_row, up[tm - SUBLANES:])],
                              axis=0)
        z_ref[:, cols] = z * mu_ref[0:1, cols] + (prev + nxt) * mu_ref[1:2, cols]
    u = _gelu_tanh(uv[:, :GM_WIDTH])
    v = _gelu_tanh(uv[:, GM_WIDTH:])
    mean = jnp.mean(v, axis=-1, keepdims=True)
    vc = v - mean
    var = jnp.mean(vc * vc, axis=-1, keepdims=True)
    vn = (vc * lax.rsqrt(var + LN_EPS) * lng_ref[...] + lnb_ref[...]).astype(BF16)
    for c in range(tm // GM_CHUNK):
        rows = slice(c * GM_CHUNK, (c + 1) * GM_CHUNK)
        for hd in range(GM_HEADS):
            cols = slice(hd * GM_HEAD_DIM, (hd + 1) * GM_HEAD_DIM)
            mixed = _dot(ws_ref[hd], vn[rows, cols]) + bs_ref[hd]
            gm_ref[rows, cols] = (u[rows, cols] * mixed).astype(BF16)


def _mix_in(x2d, p, tm, t_len):
    n = x2d.shape[0]
    assert t_len % tm == 0
    halo_per_tile = tm // SUBLANES
    n_halo = n // SUBLANES
    full = lambda shape: pl.BlockSpec(shape, lambda i: (0,) * len(shape))
    return pl.pallas_call(
        functools.partial(_mix_in_body, t_len // tm),
        grid=(n // tm,),
        in_specs=[
            pl.BlockSpec((tm, D_MODEL), lambda i: (i, 0)),
            pl.BlockSpec((SUBLANES, D_MODEL), lambda i: (jnp.maximum(i * halo_per_tile - 1, 0), 0)),
            pl.BlockSpec((SUBLANES, D_MODEL), lambda i: (jnp.minimum((i + 1) * halo_per_tile, n_halo - 1), 0)),
            full((1, D_MODEL)),
            full((D_MODEL, 2 * GM_WIDTH)),
            full((D_MODEL, Z_COLS)),
            full((2, Z_COLS)),
            full((1, GM_WIDTH)),
            full((1, GM_WIDTH)),
            full((GM_HEADS, GM_CHUNK, GM_CHUNK)),
            full((GM_HEADS, GM_CHUNK, GM_HEAD_DIM)),
        ],
        out_specs=[
            pl.BlockSpec((tm, GM_WIDTH), lambda i: (i, 0)),
            pl.BlockSpec((tm, Z_COLS), lambda i: (i, 0)),
        ],
        out_shape=[
            jax.ShapeDtypeStruct((n, GM_WIDTH), BF16),
            jax.ShapeDtypeStruct((n, Z_COLS), F32),
        ],
        compiler_params=pltpu.CompilerParams(vmem_limit_bytes=VMEM_LIMIT),
        name="mix_in",
    )(x2d, x2d, x2d, p["norm1_g"], p["w_gm"], p["w_z"], p["rw_mu"], p["gm_ln_g"], p["gm_ln_b"], p["gm_ws"],
      p["gm_bs"])


def _rwkv_masks(reverse):
    c = RW_CHUNK
    lane = lax.broadcasted_iota(jnp.int32, (c, PAIR), 1)
    head0 = lane < RW_HEAD_DIM
    lane2 = lax.broadcasted_iota(jnp.int32, (c, 2 * c), 1)
    left = lane2 < c
    ti2 = lax.broadcasted_iota(jnp.int32, (c, 2 * c), 0)
    tj2 = jnp.where(left, lane2, lane2 - c)
    incl2 = (tj2 >= ti2) if reverse else (tj2 <= ti2)
    strict2 = (tj2 > ti2) if reverse else (tj2 < ti2)
    eye2 = jnp.where(tj2 == ti2, 1.0, 0.0).astype(F32)
    si = lax.broadcasted_iota(jnp.int32, (PAIR, PAIR), 0)
    sj = lax.broadcasted_iota(jnp.int32, (PAIR, PAIR), 1)
    same_head = (si // RW_HEAD_DIM) == (sj // RW_HEAD_DIM)
    return dict(incl2=incl2, strict2=strict2, head0=head0, left=left, eye2=eye2, same_head=same_head)


def _stack_heads(x, head0):
    return jnp.concatenate([jnp.where(head0, x, 0.0), jnp.where(head0, 0.0, x)], axis=0)


def _block_diag(x, left):
    return jnp.concatenate([jnp.where(left, x, 0.0), jnp.where(left, 0.0, x)], axis=0)


def _units_chunk(states, ops, m):
    c = RW_CHUNK
    units = range(len(ops))
    bf = lambda t: t.astype(BF16)
    head0, left = m["head0"], m["left"]
    ar = [bf(jnp.concatenate([o["a_t"], o["r_t"]], axis=0)) for o in ops]
    rstack = [bf(jnp.concatenate([_stack_heads(o["b_t"], head0), _stack_heads(o["k_t"], head0)], axis=0))
              for o in ops]
    acat = [_dot_nt(ar[i], rstack[i]) for i in units]
    sa = [_dot_nt(ar[i], bf(states[i])) for i in units]
    a_ab = [jnp.where(m["strict2"], acat[i][:c, :2 * c], 0.0) for i in units]
    a_ak = [bf(jnp.where(m["strict2"], acat[i][:c, 2 * c:], 0.0)) for i in units]
    a_r = [bf(jnp.concatenate([jnp.where(m["incl2"], acat[i][c:, :2 * c], 0.0),
                               jnp.where(m["incl2"], acat[i][c:, 2 * c:], 0.0)], axis=1)) for i in units]
    p = [m["eye2"] + a_ab[i] for i in units]
    lk = [_dot(bf(a_ab[i]), bf(_block_diag(a_ab[i], left))) for i in units]
    vstack = [bf(_stack_heads(o["v"], head0)) for o in ops]
    rhs = [sa[i][:c] + _dot(a_ak[i], vstack[i]) for i in units]
    power = 2
    while 2 * power < c:
        res = [_dot(bf(jnp.concatenate([p[i], lk[i]], axis=0)), bf(_block_diag(lk[i], left))) for i in units]
        p = [p[i] + res[i][:c] for i in units]
        lk = [res[i][c:] for i in units]
        power *= 2
    p = [p[i] + _dot(bf(p[i]), bf(_block_diag(lk[i], left))) for i in units]
    u = [_dot(bf(p[i]), bf(_stack_heads(rhs[i], head0))) for i in units]
    ys = [sa[i][c:] + _dot(a_r[i], jnp.concatenate([bf(_stack_heads(u[i], head0)), vstack[i]], axis=0))
          for i in units]
    upd = [_dot(bf(jnp.concatenate([u[i], ops[i]["v"]], axis=0).T),
                bf(jnp.concatenate([ops[i]["b_h"], ops[i]["k_h"]], axis=0))) for i in units]
    new_states = [states[i] * ops[i]["w_tot"] + jnp.where(m["same_head"], upd[i], 0.0) for i in units]
    return new_states, ys


def _rwkv_body(reverse, bt, *refs):
    if reverse:
        (zc_ref, yf_ref, lw_ref, w0_ref, a0_ref, a0o_ref, kk_ref, ka_ref,
         rk_ref, bd_ref, gng_ref, gnb_ref, out_ref, state_ref) = refs
    else:
        (zc_ref, lw_ref, w0_ref, a0_ref, kk_ref, ka_ref,
         bd_ref, out_ref, state_ref) = refs
    c = RW_CHUNK

    @pl.when(pl.program_id(1) == 0)
    def _():
        state_ref[...] = jnp.zeros_like(state_ref)

    rows = bt * c
    m = _rwkv_masks(reverse)
    lora_lane = lax.broadcasted_iota(jnp.int32, (rows, LORA_PAD), 1)
    bd = bd_ref[...]
    zf = zc_ref[...].reshape(rows, Z_COLS)
    r = zf[:, :RW_WIDTH]
    k = zf[:, RW_WIDTH:2 * RW_WIDTH]
    v = zf[:, 2 * RW_WIDTH:3 * RW_WIDTH]
    lo = zf[:, 3 * RW_WIDTH:]
    act = jnp.where(lora_lane < DECAY_LORA, jnp.tanh(lo),
                    jnp.where(lora_lane < DECAY_LORA + ICLR_LORA, lo, _sigmoid(lo)))
    act_bf = act.astype(BF16)
    w_in = w0_ref[...] + _dot(act_bf, lw_ref[:, :RW_WIDTH])
    proj = _dot(act_bf, lw_ref[:, RW_WIDTH:])
    logw = -DECAY_SCALE * _sigmoid(w_in)
    a_sig = _sigmoid(a0_ref[...] + proj[:, :RW_WIDTH])
    kk = k * kk_ref[...]
    kk = kk * lax.rsqrt(jnp.maximum(_head_sums(kk * kk, bd), KK_NORM_FLOOR_SQ))
    kd = k * (1.0 + (a_sig - 1.0) * ka_ref[...])
    a_vec = -kk
    b_vec = kk * a_sig
    ri = lax.broadcasted_iota(jnp.int32, (rows, rows), 0)
    rj = lax.broadcasted_iota(jnp.int32, (rows, rows), 1)
    ordered = (rj >= ri) if reverse else (rj <= ri)
    tri = jnp.where((ri // c) == (rj // c), jnp.where(ordered, 1.0, 0.0), 0.0).astype(BF16)
    cum = _dot2_lhs01(tri, logw)
    edge = 0 if reverse else c - 1
    tot = jnp.concatenate([jnp.broadcast_to(cum[b * c + edge:b * c + edge + 1, :], (c, RW_WIDTH))
                           for b in range(bt)], axis=0)
    w_inv = jnp.exp(-cum)
    w_end = jnp.exp(tot - cum)
    w_tot = jnp.exp(tot)
    full_ops = dict(r_t=r * jnp.exp(cum), a_t=a_vec * jnp.exp(cum - logw), b_t=b_vec * w_inv, k_t=kd * w_inv,
                    b_h=b_vec * w_end, k_h=kd * w_end, v=v)
    ops, states = [], []
    for b in range(bt):
        for pi in range(N_PAIRS):
            ln = slice(pi * PAIR, (pi + 1) * PAIR)
            unit = {name: t[b * c:(b + 1) * c, ln] for name, t in full_ops.items()}
            unit["w_tot"] = w_tot[b * c:b * c + 1, ln]
            ops.append(unit)
            states.append(state_ref[b, pi])
    new_states, ys = _units_chunk(states, ops, m)
    for b in range(bt):
        for pi in range(N_PAIRS):
            state_ref[b, pi] = new_states[b * N_PAIRS + pi]
    y = jnp.concatenate([jnp.concatenate(ys[b * N_PAIRS:(b + 1) * N_PAIRS], axis=1) for b in range(bt)], axis=0)
    if not reverse:
        out_ref[...] = y.reshape(bt, c, RW_WIDTH)
    else:
        y = y + yf_ref[...].reshape(rows, RW_WIDTH)
        a_other = _sigmoid(a0o_ref[...] + proj[:, RW_WIDTH:2 * RW_WIDTH])
        kd_sum = kd + k * (1.0 + (a_other - 1.0) * ka_ref[...])
        bonus = _head_sums(r * kd_sum * rk_ref[...], bd) * v
        inv_n = 1.0 / RW_HEAD_DIM
        mean = _head_sums(y, bd) * inv_n
        yc = y - mean
        var = _head_sums(yc * yc, bd) * inv_n
        yn = yc * lax.rsqrt(var + GN_EPS) * gng_ref[...] + gnb_ref[...]
        g = proj[:, 2 * RW_WIDTH:]
        out_ref[...] = ((yn + bonus) * g).astype(BF16).reshape(bt, c, RW_WIDTH)


def _rwkv_pass(z3, yf, p, reverse, bt):
    b_sz, t_len, _ = z3.shape
    c = RW_CHUNK
    n_chunks = t_len // c

    def cidx(ci):
        return (n_chunks - 1 - ci) if reverse else ci

    full = lambda shape: pl.BlockSpec(shape, lambda bi, ci: (0,) * len(shape))
    row512 = full((1, RW_WIDTH))
    d = "rev" if reverse else "fwd"
    n_proj = p["lora_" + d].shape[1]
    in_specs = [pl.BlockSpec((bt, c, Z_COLS), lambda bi, ci: (bi, cidx(ci), 0))]
    args = [z3]
    if reverse:
        in_specs.append(pl.BlockSpec((bt, c, RW_WIDTH), lambda bi, ci: (bi, cidx(ci), 0)))
        args.append(yf)
    in_specs += [full((LORA_PAD, n_proj)), row512, row512]
    args += [p["lora_" + d], p["w0_" + d], p["a0_" + d]]
    if reverse:
        in_specs.append(row512)
        args.append(p["a0_fwd"])
    in_specs += [row512, row512]
    args += [p["rw_k_k"], p["rw_k_a"]]
    if reverse:
        in_specs.append(row512)
        args.append(p["rw_r_k"])
    in_specs.append(full((PAIR, PAIR)))
    args.append(p["head_ones"])
    if reverse:
        in_specs += [row512, row512]
        args += [p["rw_gn_g"], p["rw_gn_b"]]
    return pl.pallas_call(
        functools.partial(_rwkv_body, reverse, bt),
        grid=(b_sz // bt, n_chunks),
        in_specs=in_specs,
        out_specs=pl.BlockSpec((bt, c, RW_WIDTH), lambda bi, ci: (bi, cidx(ci), 0)),
        out_shape=jax.ShapeDtypeStruct((b_sz, t_len, RW_WIDTH), BF16 if reverse else F32),
        scratch_shapes=[pltpu.VMEM((bt, N_PAIRS, PAIR, PAIR), F32)],
        compiler_params=pltpu.CompilerParams(vmem_limit_bytes=VMEM_LIMIT),
        name="rwkv_" + d,
    )(*args)


def _mix_out_body(x_ref, gm_ref, rw_ref, wo_ref, g2_ref, rw_hi_ref, rw_lo_ref, rb_ref, earlier_ref,
                  x1_ref, h2_ref, route_ref, route_t_ref, counts_ref, carry_ref):
    i = pl.program_id(0)

    @pl.when(i == 0)
    def _():
        carry_ref[...] = jnp.zeros_like(carry_ref)

    sub = earlier_ref.shape[0]
    subs = range(x_ref.shape[0] // sub)
    rows = [slice(s * sub, (s + 1) * sub) for s in subs]
    x1 = [x_ref[r, :] + _dot(gm_ref[r, :], wo_ref[:GM_WIDTH, :]) + _dot(rw_ref[r, :], wo_ref[GM_WIDTH:, :])
          for r in rows]
    h2 = [x1[s] * lax.rsqrt(jnp.mean(x1[s] * x1[s], axis=-1, keepdims=True) + RMS_EPS) * g2_ref[...] for s in subs]
    for s in subs:
        x1_ref[rows[s], :] = x1[s]
        h2_ref[rows[s], :] = _pack_bf16_halves(h2[s])
    split = [_split(h) for h in h2]
    logits = [_dot_nt(rw_hi_ref[...], hi) + _dot_nt(rw_hi_ref[...], lo) + _dot_nt(rw_lo_ref[...], hi) + rb_ref[...]
              for hi, lo in split]
    expert = lax.broadcasted_iota(jnp.int32, (N_EXPERTS, sub), 0).astype(F32)
    work = logits
    vals, ids, hits = [[] for _ in subs], [[] for _ in subs], [[] for _ in subs]
    onehot = [jnp.zeros((N_EXPERTS, sub), F32) for _ in subs]
    for _ in range(TOP_K):
        for s in subs:
            mx = jnp.max(work[s], axis=0, keepdims=True)
            idx = jnp.min(jnp.where(work[s] == mx, expert, float(N_EXPERTS)), axis=0, keepdims=True)
            hit = expert == idx
            vals[s].append(mx)
            ids[s].append(idx)
            hits[s].append(hit)
            onehot[s] = jnp.where(hit, 1.0, onehot[s])
        work = [jnp.where(hits[s][-1], -jnp.inf, work[s]) for s in subs]
    earlier_counts = [_dot(onehot[s].astype(BF16), earlier_ref[...]) for s in subs]
    carry = carry_ref[:, 0:1]
    for s in subs:
        exps = [jnp.exp(vk - vals[s][0]) for vk in vals[s]]
        denom = exps[0] + exps[1] + exps[2] + exps[3]
        ranks = carry + earlier_counts[s]
        rank_rows = [jnp.sum(jnp.where(hit, ranks, 0.0), axis=0, keepdims=True) for hit in hits[s]]
        route_t = jnp.concatenate(ids[s] + [e / denom for e in exps] + rank_rows
                                  + [jnp.zeros((ROUTE_ROWS - 3 * TOP_K, sub), F32)], axis=0)
        route_t_ref[:, rows[s]] = route_t
        route_ref[rows[s], :] = jnp.concatenate([route_t, jnp.zeros((LANES - ROUTE_ROWS, sub), F32)], axis=0).T
        carry = carry + jnp.sum(onehot[s], axis=1, keepdims=True)
    carry_ref[...] = jnp.broadcast_to(carry, carry_ref.shape)
    counts_ref[...] = jnp.broadcast_to(carry, counts_ref.shape)


def _mix_out(x2d, gm, rw, p, tm):
    n = x2d.shape[0]
    full = lambda shape: pl.BlockSpec(shape, lambda i: (0,) * len(shape))
    tile = lambda w: pl.BlockSpec((tm, w), lambda i: (i, 0))
    sub = min(MIX_OUT_SUB, tm)
    earlier = (jnp.arange(sub)[:, None] < jnp.arange(sub)[None, :]).astype(BF16)
    return pl.pallas_call(
        _mix_out_body,
        grid=(n // tm,),
        in_specs=[tile(D_MODEL), tile(GM_WIDTH), tile(RW_WIDTH), full((D_MODEL, D_MODEL)), full((1, D_MODEL)),
                  full((N_EXPERTS, D_MODEL)), full((N_EXPERTS, D_MODEL)), full((N_EXPERTS, 1)), full((sub, sub))],
        out_specs=[tile(D_MODEL), tile(D_MODEL // 2), tile(LANES), pl.BlockSpec((ROUTE_ROWS, tm), lambda i: (0, i)),
                   full((N_EXPERTS, LANES))],
        out_shape=[
            jax.ShapeDtypeStruct((n, D_MODEL), F32),
            jax.ShapeDtypeStruct((n, D_MODEL // 2), jnp.int32),
            jax.ShapeDtypeStruct((n, LANES), F32),
            jax.ShapeDtypeStruct((ROUTE_ROWS, n), F32),
            jax.ShapeDtypeStruct((N_EXPERTS, LANES), F32),
        ],
        scratch_shapes=[pltpu.VMEM((N_EXPERTS, LANES), F32)],
        compiler_params=pltpu.CompilerParams(vmem_limit_bytes=VMEM_LIMIT),
        name="mix_out",
    )(x2d, gm, rw, p["w_out"], p["norm2_g"], p["router_hi"], p["router_lo"], p["router_b"], earlier)


def _sc_gather_rows(table, idx):
    n_idx = idx.shape[0]
    width = table.shape[1]
    n_workers = SC_CORES * SC_SUBCORES
    per_worker = n_idx // n_workers
    n_windows = per_worker // SC_WINDOW
    assert per_worker * n_workers == n_idx and n_windows * SC_WINDOW == per_worker
    mesh = plsc.VectorSubcoreMesh(core_axis_name="c", subcore_axis_name="s")

    @functools.partial(
        pl.kernel, mesh=mesh,
        out_type=jax.ShapeDtypeStruct((n_idx, width), table.dtype),
        scratch_types=[
            pltpu.VMEM((SC_WINDOW,), jnp.int32),
            pltpu.VMEM((SC_WINDOW, width), table.dtype),
            pltpu.SemaphoreType.DMA,
        ],
        name="sc_gather_rows",
    )
    def gather(table_hbm, idx_hbm, out_hbm, idx_v, rows_v, sem):
        worker = lax.axis_index("s") * SC_CORES + lax.axis_index("c")
        base = worker * per_worker

        @pl.loop(0, n_windows)
        def _(j):
            off = base + j * SC_WINDOW
            pltpu.sync_copy(idx_hbm.at[pl.ds(off, SC_WINDOW)], idx_v)
            pltpu.async_copy(table_hbm.at[idx_v], rows_v, sem).wait()
            pltpu.sync_copy(rows_v, out_hbm.at[pl.ds(off, SC_WINDOW)])

    return gather(table, idx)


def _sc_scatter_rows(src, dest_w, n_out):
    n, width = src.shape
    n_windows, top_k, window = dest_w.shape
    n_workers = SC_CORES * SC_SUBCORES
    per_worker = n_windows // n_workers
    assert n_windows * window == n and per_worker * n_workers == n_windows
    mesh = plsc.VectorSubcoreMesh(core_axis_name="c", subcore_axis_name="s")

    @functools.partial(
        pl.kernel, mesh=mesh,
        out_type=jax.ShapeDtypeStruct((n_out, width), src.dtype),
        scratch_types=[
            pltpu.VMEM((top_k, window), jnp.int32),
            pltpu.VMEM((window, width), src.dtype),
        ],
        name="sc_scatter_rows",
    )
    def scatter(src_hbm, dest_hbm, out_hbm, idx_v, rows_v):
        worker = lax.axis_index("s") * SC_CORES + lax.axis_index("c")

        @pl.loop(0, per_worker)
        def _(j):
            g = worker * per_worker + j
            pltpu.sync_copy(dest_hbm.at[g], idx_v)
            pltpu.sync_copy(src_hbm.at[pl.ds(g * window, window)], rows_v)
            for k in range(top_k):
                pltpu.sync_copy(rows_v, out_hbm.at[idx_v.at[k]])

    return scatter(src, dest_w)


def _weight_copies(wgu_hbm, wd_hbm, wgu_buf, wd_buf, sem, expert, slot):
    return (pltpu.make_async_copy(wgu_hbm.at[expert], wgu_buf.at[slot], sem.at[0, slot]),
            pltpu.make_async_copy(wd_hbm.at[expert], wd_buf.at[slot], sem.at[1, slot]))


def _experts_body(blk_e_ref, n_used_ref, valid_ref, slot_ref, next_e_ref, x_ref, wgu_hbm, bgu_ref, wd_hbm, bd_ref,
                  out_ref, wgu_bf, wd_bf, wgu_buf, wd_buf, sem):
    i = pl.program_id(0)
    n_used = n_used_ref[0]
    expert = blk_e_ref[i]
    slot = slot_ref[i]
    copies = functools.partial(_weight_copies, wgu_hbm, wd_hbm, wgu_buf, wd_buf, sem)

    @pl.when(i == 0)
    def _():
        for cp in copies(expert, slot):
            cp.start()

    @pl.when((i < n_used) & ((i == 0) | (expert != blk_e_ref[jnp.maximum(i - 1, 0)])))
    def _():
        for cp in copies(expert, slot):
            cp.wait()
        wgu_bf[...] = wgu_buf[slot].astype(BF16)
        wd_bf[...] = wd_buf[slot].astype(BF16)
        following = next_e_ref[i]

        @pl.when(following >= 0)
        def _():
            for cp in copies(following, 1 - slot):
                cp.start()

    valid = jnp.where(i < n_used, valid_ref[i], 0)

    def expert_mlp(n_rows):
        row = lax.broadcasted_iota(jnp.int32, (n_rows, 1), 0)
        x_left, x_right = _unpack_bf16_halves(jnp.where(row < valid, x_ref[:n_rows, :], 0))
        x = jnp.concatenate([x_left.astype(BF16), x_right.astype(BF16)], axis=1)
        gu = _dot(x, wgu_bf[...]) + bgu_ref[0]
        gate = jnp.minimum(gu[:, :D_EXPERT], SWIGLU_LIMIT)
        up = jnp.clip(gu[:, D_EXPERT:], -SWIGLU_LIMIT, SWIGLU_LIMIT)
        act = gate * _sigmoid(gate * SWIGLU_ALPHA) * (up + 1.0)
        out_ref[:n_rows, :] = _pack_bf16_halves(_dot(act.astype(BF16), wd_bf[...]) + bd_ref[0])

    for quarters in range(1, MOE_ROW_STEPS + 1):
        n_rows = quarters * MOE_ROWS // MOE_ROW_STEPS

        @pl.when((valid > n_rows - MOE_ROWS // MOE_ROW_STEPS) & (valid <= n_rows))
        def _(n_rows=n_rows):
            expert_mlp(n_rows)
            if n_rows < MOE_ROWS:
                out_ref[n_rows:, :] = jnp.zeros((MOE_ROWS - n_rows, D_MODEL // 2), jnp.int32)

    @pl.when(valid == 0)
    def _():
        out_ref[...] = jnp.zeros_like(out_ref)


def _experts(x_rows, blk_e, n_used, valid, slot, next_e, p):
    n_blocks = blk_e.shape[0]
    grid_spec = pltpu.PrefetchScalarGridSpec(
        num_scalar_prefetch=5,
        grid=(n_blocks,),
        in_specs=[
            pl.BlockSpec((MOE_ROWS, D_MODEL // 2), lambda i, be, nu, *_: (jnp.minimum(i, nu[0] - 1), 0)),
            pl.BlockSpec(memory_space=pl.ANY),
            pl.BlockSpec((1, 1, 2 * D_EXPERT), lambda i, be, *_: (be[i], 0, 0)),
            pl.BlockSpec(memory_space=pl.ANY),
            pl.BlockSpec((1, 1, D_MODEL), lambda i, be, *_: (be[i], 0, 0)),
        ],
        out_specs=pl.BlockSpec((MOE_ROWS, D_MODEL // 2), lambda i, *_: (i, 0)),
        scratch_shapes=[
            pltpu.VMEM((D_MODEL, 2 * D_EXPERT), BF16),
            pltpu.VMEM((D_EXPERT, D_MODEL), BF16),
            pltpu.VMEM((2, D_MODEL, 2 * D_EXPERT), F32),
            pltpu.VMEM((2, D_EXPERT, D_MODEL), F32),
            pltpu.SemaphoreType.DMA((2, 2)),
        ],
    )
    return pl.pallas_call(
        _experts_body,
        grid_spec=grid_spec,
        out_shape=jax.ShapeDtypeStruct((n_blocks * MOE_ROWS, D_MODEL // 2), jnp.int32),
        compiler_params=pltpu.CompilerParams(vmem_limit_bytes=VMEM_LIMIT),
        name="experts",
    )(blk_e, n_used, valid, slot, next_e, x_rows, p["w_gu"], p["b_gu"], p["w_down"], p["b_down"])


def _combine_body(yg_ref, x1_ref, gates_ref, fg_ref, out_ref):
    gates = gates_ref[...]
    half = D_MODEL // 2
    acc_left = x1_ref[:, :half]
    acc_right = x1_ref[:, half:]
    for kk in range(TOP_K):
        y_left, y_right = _unpack_bf16_halves(yg_ref[kk])
        gate = gates[:, TOP_K + kk:TOP_K + kk + 1]
        acc_left = acc_left + gate * y_left
        acc_right = acc_right + gate * y_right
    acc = jnp.concatenate([acc_left, acc_right], axis=1)
    out_ref[...] = acc * lax.rsqrt(jnp.mean(acc * acc, axis=-1, keepdims=True) + RMS_EPS) * fg_ref[...]


def _combine(yg, x1, gates, final_g, tm):
    n = x1.shape[0]
    return pl.pallas_call(
        _combine_body,
        grid=(n // tm,),
        in_specs=[
            pl.BlockSpec((TOP_K, tm, D_MODEL // 2), lambda i: (0, i, 0)),
            pl.BlockSpec((tm, D_MODEL), lambda i: (i, 0)),
            pl.BlockSpec((tm, LANES), lambda i: (i, 0)),
            pl.BlockSpec((1, D_MODEL), lambda i: (0, 0)),
        ],
        out_specs=pl.BlockSpec((tm, D_MODEL), lambda i: (i, 0)),
        out_shape=jax.ShapeDtypeStruct((n, D_MODEL), F32),
        compiler_params=pltpu.CompilerParams(vmem_limit_bytes=VMEM_LIMIT),
        name="combine",
    )(yg, x1, gates, final_g)


def _prepare(norm1_g, w_in, rw_mu, gm_ln_g, gm_ln_b, gm_ws, gm_bs, rw_w0, rw_w2, rw_a0, rw_a2, rw_g2,
             rw_k_k, rw_k_a, rw_r_k, rw_gn_g, rw_gn_b, w_out, norm2_g, router_w, router_b,
             w_gu, b_gu, w_down, b_down, final_g):
    l = 0
    row = lambda t: t.reshape(1, -1).astype(F32)
    p = {}
    p["norm1_g"] = row(norm1_g[l])
    w = w_in[l]
    p["w_gm"] = w[:, :2 * GM_WIDTH].astype(BF16)
    pad_cols = LORA_PAD - LORA_COLS
    p["w_z"] = jnp.pad(w[:, 2 * GM_WIDTH:], ((0, 0), (0, pad_cols))).astype(BF16)
    mu = jnp.pad(row(rw_mu[l]), ((0, 0), (0, pad_cols)))
    p["rw_mu"] = jnp.concatenate([1.0 - mu, 0.5 * mu], axis=0)
    p["gm_ln_g"] = row(gm_ln_g[l])
    p["gm_ln_b"] = row(gm_ln_b[l])
    p["gm_ws"] = gm_ws[l].astype(BF16)
    p["gm_bs"] = jnp.broadcast_to(gm_bs[l][:, :, None], (GM_HEADS, GM_CHUNK, GM_HEAD_DIM)).astype(F32)
    zeros = lambda r, c: jnp.zeros((r, c), F32)
    o_a = DECAY_LORA
    o_g = DECAY_LORA + ICLR_LORA

    def lora_matrix(d, with_epilogue):
        blocks = [jnp.concatenate([rw_w2[l, d], zeros(LORA_PAD - DECAY_LORA, RW_WIDTH)], axis=0),
                  jnp.concatenate([zeros(o_a, RW_WIDTH), rw_a2[l, d], zeros(LORA_PAD - o_g, RW_WIDTH)], axis=0)]
        if with_epilogue:
            blocks.append(jnp.concatenate([zeros(o_a, RW_WIDTH), rw_a2[l, 1 - d], zeros(LORA_PAD - o_g, RW_WIDTH)],
                                          axis=0))
            blocks.append(jnp.concatenate([zeros(o_g, RW_WIDTH), rw_g2[l], zeros(LORA_PAD - LORA_COLS, RW_WIDTH)],
                                          axis=0))
        return jnp.concatenate(blocks, axis=1)

    for d, name in ((0, "fwd"), (1, "rev")):
        p["lora_" + name] = lora_matrix(d, with_epilogue=(d == 1)).astype(BF16)
        p["w0_" + name] = row(rw_w0[l, d])
        p["a0_" + name] = row(rw_a0[l, d])
    p["rw_k_k"] = row(rw_k_k[l])
    p["rw_k_a"] = row(rw_k_a[l])
    p["rw_r_k"] = row(rw_r_k[l])
    p["rw_gn_g"] = row(rw_gn_g[l])
    p["rw_gn_b"] = row(rw_gn_b[l])
    ch = jnp.arange(PAIR) // RW_HEAD_DIM
    p["head_ones"] = (ch[:, None] == ch[None, :]).astype(BF16)
    p["w_out"] = w_out[l].astype(BF16)
    p["norm2_g"] = row(norm2_g[l])
    rw_t = router_w[l].astype(F32).T
    hi = rw_t.astype(BF16)
    p["router_hi"] = hi
    p["router_lo"] = (rw_t - hi.astype(F32)).astype(BF16)
    p["router_b"] = router_b[l].astype(F32).reshape(N_EXPERTS, 1)
    p["w_gu"] = w_gu[l]
    p["b_gu"] = b_gu[l].reshape(N_EXPERTS, 1, 2 * D_EXPERT).astype(F32)
    p["w_down"] = w_down[l]
    p["b_down"] = b_down[l].reshape(N_EXPERTS, 1, D_MODEL).astype(F32)
    p["final_g"] = row(final_g)
    return p


def _pick_tile(n, want):
    t = want
    while n % t:
        t //= 2
    return t


def _encoder(x, p):
    b_sz, t_len, d = x.shape
    n = b_sz * t_len
    x2d = x.reshape(n, d)
    gm, z = _mix_in(x2d, p, _pick_tile(t_len, 1024), t_len)
    z3 = z.reshape(b_sz, t_len, Z_COLS)
    yf = _rwkv_pass(z3, None, p, reverse=False, bt=RW_BATCH_TILE)
    rw = _rwkv_pass(z3, yf, p, reverse=True, bt=RW_BATCH_TILE)
    tmo = _pick_tile(n, 1024)
    x1, h2, route, route_t, counts = _mix_out(x2d, gm, rw.reshape(n, RW_WIDTH), p, tmo)
    ids = route_t[:TOP_K].astype(jnp.int32)
    ranks = route_t[2 * TOP_K:3 * TOP_K].astype(jnp.int32)
    cnt = counts[:, 0].astype(jnp.int32)
    padded = (cnt + MOE_ROWS - 1) // MOE_ROWS * MOE_ROWS
    pad_end = jnp.cumsum(padded)
    pad_start = pad_end - padded
    expert_ids = jnp.arange(N_EXPERTS, dtype=jnp.int32)
    start_of = jnp.sum(jnp.where(ids[..., None] == expert_ids, pad_start, 0), axis=-1)
    dest = start_of + ranks
    n_blocks = n * TOP_K // MOE_ROWS + N_EXPERTS
    blk_start = jnp.arange(n_blocks, dtype=jnp.int32) * MOE_ROWS
    blk_e = jnp.minimum(jnp.sum((pad_end[None, :] <= blk_start[:, None]).astype(jnp.int32), axis=1), N_EXPERTS - 1)
    n_used = (pad_end[-1] // MOE_ROWS).astype(jnp.int32).reshape(1)
    blk_onehot = blk_e[:, None] == expert_ids
    blk_cnt = jnp.sum(jnp.where(blk_onehot, cnt, 0), axis=-1)
    blk_first = jnp.sum(jnp.where(blk_onehot, pad_start, 0), axis=-1)
    valid = jnp.clip(blk_cnt - (blk_start - blk_first), 0, MOE_ROWS).astype(jnp.int32)
    window = min(SC_SCATTER_WINDOW, n // (SC_CORES * SC_SUBCORES))
    dest_w = dest.reshape(TOP_K, n // window, window).transpose(1, 0, 2)
    x_rows = _sc_scatter_rows(h2, dest_w, n_blocks * MOE_ROWS)
    present = cnt > 0
    slot_of_expert = (jnp.cumsum(present.astype(jnp.int32)) - 1) % 2
    later = present[None, :] & (expert_ids[None, :] > expert_ids[:, None])
    next_present = jnp.min(jnp.where(later, expert_ids[None, :], N_EXPERTS), axis=1)
    next_present = jnp.where(next_present < N_EXPERTS, next_present, -1)
    slot = jnp.sum(jnp.where(blk_onehot, slot_of_expert, 0), axis=-1).astype(jnp.int32)
    next_e = jnp.sum(jnp.where(blk_onehot, next_present, 0), axis=-1).astype(jnp.int32)
    y_rows = _experts(x_rows, blk_e, n_used, valid, slot, next_e, p)
    yg = _sc_gather_rows(y_rows, dest.reshape(-1)).reshape(TOP_K, n, D_MODEL // 2)
    out = _combine(yg, x1, route, p["final_g"], _pick_tile(n, 1024))
    return out.reshape(b_sz, t_len, d)


def kernel(x_prompt, x_sample, norm1_g, w_in, rw_mu, gm_ln_g, gm_ln_b, gm_ws, gm_bs, rw_w0, rw_w2, rw_a0, rw_a2,
           rw_g2, rw_k_k, rw_k_a, rw_r_k, rw_gn_g, rw_gn_b, w_out, norm2_g, router_w, router_b, w_gu, b_gu,
           w_down, b_down, final_g):
    p = _prepare(norm1_g, w_in, rw_mu, gm_ln_g, gm_ln_b, gm_ws, gm_bs, rw_w0, rw_w2, rw_a0, rw_a2, rw_g2,
                 rw_k_k, rw_k_a, rw_r_k, rw_gn_g, rw_gn_b, w_out, norm2_g, router_w, router_b,
                 w_gu, b_gu, w_down, b_down, final_g)
    return (_encoder(x_prompt, p), _encoder(x_sample, p))
```

```python
import functools

import jax
import jax.numpy as jnp
from jax import lax
from jax.experimental import pallas as pl
from jax.experimental.pallas import tpu as pltpu
from jax.experimental.pallas import tpu_sc as plsc

F32 = jnp.float32
BF16 = jnp.bfloat16

D_MODEL = 1024
GM_WIDTH = 512
RW_WIDTH = 512
GM_HEADS = 4
GM_HEAD_DIM = 128
GM_CHUNK = 128
RW_HEAD_DIM = 64
DECAY_LORA = 32
ICLR_LORA = 32
GATE_LORA = 96
LORA_COLS = DECAY_LORA + ICLR_LORA + GATE_LORA
LORA_PAD = 256
Z_COLS = 3 * RW_WIDTH + LORA_PAD
Z_CHUNK = 256
N_EXPERTS = 32
TOP_K = 4
D_EXPERT = 1024
SWIGLU_LIMIT = 7.0
SWIGLU_ALPHA = 1.702
RMS_EPS = 1e-5
LN_EPS = 1e-5
GN_EPS = 64e-5

LANES = 128
SUBLANES = 8
VMEM_LIMIT = 56 * 1024 * 1024

RW_CHUNK = 64
PAIR = 2 * RW_HEAD_DIM
N_PAIRS = RW_WIDTH // PAIR
RW_BATCH_TILE = 4
MIX_OUT_SUB = 128
MOE_ROWS = 512
MOE_ROW_STEPS = 4
SC_CORES = 2
SC_SUBCORES = 16
SC_WINDOW = 64
SC_SCATTER_WINDOW = 128
ROUTE_ROWS = 16
DECAY_SCALE = 0.6065306597126334
KK_NORM_FLOOR_SQ = 1e-24


def _dot(a, b):
    return jnp.dot(a, b, preferred_element_type=F32)


def _dot_nt(a, b):
    return lax.dot_general(a, b, (((1,), (1,)), ((), ())), preferred_element_type=F32)


def _split(x):
    hi = x.astype(BF16)
    lo = (x - hi.astype(F32)).astype(BF16)
    return hi, lo


def _head_sums(x, pair_ones):
    rows = x.shape[0]
    n_tiles = x.shape[1] // LANES
    stacked = jnp.concatenate([x[:, t * LANES:(t + 1) * LANES] for t in range(n_tiles)], axis=0)
    sums = _dot(stacked.astype(BF16), pair_ones)
    return jnp.concatenate([sums[t * rows:(t + 1) * rows] for t in range(n_tiles)], axis=1)


def _dot2_lhs01(a01, b):
    b_hi, b_lo = _split(b)
    return _dot(a01, b_hi) + _dot(a01, b_lo)


def _pack_bf16_halves(x):
    w = x.shape[1] // 2
    hi = lax.bitcast_convert_type(x[:, :w].astype(BF16).astype(F32), jnp.int32)
    lo = lax.bitcast_convert_type(x[:, w:].astype(BF16).astype(F32), jnp.int32)
    return hi | lax.shift_right_logical(lo, 16)


def _unpack_bf16_halves(words):
    left = lax.bitcast_convert_type(words & jnp.int32(-65536), F32)
    right = lax.bitcast_convert_type(lax.shift_left(words, 16), F32)
    return left, right


def _gelu_tanh(x):
    return 0.5 * x * (1.0 + jnp.tanh(0.7978845608028654 * (x + 0.044715 * (x * x * x))))


def _sigmoid(x):
    return 0.5 * jnp.tanh(0.5 * x) + 0.5


def _mix_in_body(tiles_per_seq, x_ref, xp_ref, xn_ref, g1_ref, wgm_ref, wz_ref, mu_ref, lng_ref, lnb_ref, ws_ref,
                 bs_ref, gm_ref, z_ref):
    i = pl.program_id(0)
    tm = x_ref.shape[0]

    def normed(x):
        return (x * lax.rsqrt(jnp.mean(x * x, axis=-1, keepdims=True) + RMS_EPS) * g1_ref[...]).astype(BF16)

    hb = normed(x_ref[...])
    uv = _dot(hb, wgm_ref[...])
    hb_halo = normed(jnp.concatenate([xp_ref[...], xn_ref[...]], axis=0))
    has_prev = i % tiles_per_seq > 0
    has_next = i % tiles_per_seq < tiles_per_seq - 1
    row8 = lax.broadcasted_iota(jnp.int32, (SUBLANES, Z_CHUNK), 0)
    for j in range(Z_COLS // Z_CHUNK):
        cols = slice(j * Z_CHUNK, (j + 1) * Z_CHUNK)
        z = _dot(hb, wz_ref[:, cols])
        halo = _dot(hb_halo, wz_ref[:, cols])
        prev_row = jnp.where(has_prev, halo[SUBLANES - 1:SUBLANES, :], 0.0)
        next_row = jnp.where(has_next, halo[SUBLANES:SUBLANES + 1, :], 0.0)
        down = pltpu.roll(z, 1, axis=0)
        up = pltpu.roll(z, tm - 1, axis=0)
        prev = jnp.concatenate([jnp.where(row8 == 0, prev_row, down[:SUBLANES]), down[SUBLANES:]], axis=0)
        nxt = jnp.concatenate([up[:tm - SUBLANES], jnp.where(row8 == SUBLANES - 1, next_row, up[tm - SUBLANES:])],
                              axis=0)
        z_ref[:, cols] = z * mu_ref[0:1, cols] + (prev + nxt) * mu_ref[1:2, cols]
    u = _gelu_tanh(uv[:, :GM_WIDTH])
    v = _gelu_tanh(uv[:, GM_WIDTH:])
    mean = jnp.mean(v, axis=-1, keepdims=True)
    vc = v - mean
    var = jnp.mean(vc * vc, axis=-1, keepdims=True)
    vn = (vc * lax.rsqrt(var + LN_EPS) * lng_ref[...] + lnb_ref[...]).astype(BF16)
    for c in range(tm // GM_CHUNK):
        rows = slice(c * GM_CHUNK, (c + 1) * GM_CHUNK)
        for hd in range(GM_HEADS):
            cols = slice(hd * GM_HEAD_DIM, (hd + 1) * GM_HEAD_DIM)
            mixed = _dot(ws_ref[hd], vn[rows, cols]) + bs_ref[hd]
            gm_ref[rows, cols] = (u[rows, cols] * mixed).astype(BF16)


def _mix_in(x2d, p, tm, t_len):
    n = x2d.shape[0]
    assert t_len % tm == 0
    halo_per_tile = tm // SUBLANES
    n_halo = n // SUBLANES
    full = lambda shape: pl.BlockSpec(shape, lambda i: (0,) * len(shape))
    return pl.pallas_call(
        functools.partial(_mix_in_body, t_len // tm),
        grid=(n // tm,),
        in_specs=[
            pl.BlockSpec((tm, D_MODEL), lambda i: (i, 0)),
            pl.BlockSpec((SUBLANES, D_MODEL), lambda i: (jnp.maximum(i * halo_per_tile - 1, 0), 0)),
            pl.BlockSpec((SUBLANES, D_MODEL), lambda i: (jnp.minimum((i + 1) * halo_per_tile, n_halo - 1), 0)),
            full((1, D_MODEL)),
            full((D_MODEL, 2 * GM_WIDTH)),
            full((D_MODEL, Z_COLS)),
            full((2, Z_COLS)),
            full((1, GM_WIDTH)),
            full((1, GM_WIDTH)),
            full((GM_HEADS, GM_CHUNK, GM_CHUNK)),
            full((GM_HEADS, GM_CHUNK, GM_HEAD_DIM)),
        ],
        out_specs=[
            pl.BlockSpec((tm, GM_WIDTH), lambda i: (i, 0)),
            pl.BlockSpec((tm, Z_COLS), lambda i: (i, 0)),
        ],
        out_shape=[
            jax.ShapeDtypeStruct((n, GM_WIDTH), BF16),
            jax.ShapeDtypeStruct((n, Z_COLS), F32),
        ],
        compiler_params=pltpu.CompilerParams(vmem_limit_bytes=VMEM_LIMIT),
        name="mix_in",
    )(x2d, x2d, x2d, p["norm1_g"], p["w_gm"], p["w_z"], p["rw_mu"], p["gm_ln_g"], p["gm_ln_b"], p["gm_ws"],
      p["gm_bs"])


def _rwkv_masks(reverse):
    c = RW_CHUNK
    lane = lax.broadcasted_iota(jnp.int32, (c, PAIR), 1)
    head0 = lane < RW_HEAD_DIM
    lane2 = lax.broadcasted_iota(jnp.int32, (c, 2 * c), 1)
    left = lane2 < c
    ti2 = lax.broadcasted_iota(jnp.int32, (c, 2 * c), 0)
    tj2 = jnp.where(left, lane2, lane2 - c)
    incl2 = (tj2 >= ti2) if reverse else (tj2 <= ti2)
    strict2 = (tj2 > ti2) if reverse else (tj2 < ti2)
    eye2 = jnp.where(tj2 == ti2, 1.0, 0.0).astype(F32)
    si = lax.broadcasted_iota(jnp.int32, (PAIR, PAIR), 0)
    sj = lax.broadcasted_iota(jnp.int32, (PAIR, PAIR), 1)
    same_head = (si // RW_HEAD_DIM) == (sj // RW_HEAD_DIM)
    return dict(incl2=incl2, strict2=strict2, head0=head0, left=left, eye2=eye2, same_head=same_head)


def _stack_heads(x, head0):
    return jnp.concatenate([jnp.where(head0, x, 0.0), jnp.where(head0, 0.0, x)], axis=0)


def _block_diag(x, left):
    return jnp.concatenate([jnp.where(left, x, 0.0), jnp.where(left, 0.0, x)], axis=0)


def _units_chunk(states, ops, m):
    c = RW_CHUNK
    units = range(len(ops))
    bf = lambda t: t.astype(BF16)
    head0, left = m["head0"], m["left"]
    ar = [bf(jnp.concatenate([o["a_t"], o["r_t"]], axis=0)) for o in ops]
    rstack = [bf(jnp.concatenate([_stack_heads(o["b_t"], head0), _stack_heads(o["k_t"], head0)], axis=0))
              for o in ops]
    acat = [_dot_nt(ar[i], rstack[i]) for i in units]
    sa = [_dot_nt(ar[i], bf(states[i])) for i in units]
    a_ab = [jnp.where(m["strict2"], acat[i][:c, :2 * c], 0.0) for i in units]
    a_ak = [bf(jnp.where(m["strict2"], acat[i][:c, 2 * c:], 0.0)) for i in units]
    a_r = [bf(jnp.concatenate([jnp.where(m["incl2"], acat[i][c:, :2 * c], 0.0),
                               jnp.where(m["incl2"], acat[i][c:, 2 * c:], 0.0)], axis=1)) for i in units]
    p = [m["eye2"] + a_ab[i] for i in units]
    lk = [_dot(bf(a_ab[i]), bf(_block_diag(a_ab[i], left))) for i in units]
    vstack = [bf(_stack_heads(o["v"], head0)) for o in ops]
    rhs = [sa[i][:c] + _dot(a_ak[i], vstack[i]) for i in units]
    power = 2
    while 2 * power < c:
        res = [_dot(bf(jnp.concatenate([p[i], lk[i]], axis=0)), bf(_block_diag(lk[i], left))) for i in units]
        p = [p[i] + res[i][:c] for i in units]
        lk = [res[i][c:] for i in units]
        power *= 2
    p = [p[i] + _dot(bf(p[i]), bf(_block_diag(lk[i], left))) for i in units]
    u = [_dot(bf(p[i]), bf(_stack_heads(rhs[i], head0))) for i in units]
    ys = [sa[i][c:] + _dot(a_r[i], jnp.concatenate([bf(_stack_heads(u[i], head0)), vstack[i]], axis=0))
          for i in units]
    upd = [_dot(bf(jnp.concatenate([u[i], ops[i]["v"]], axis=0).T),
                bf(jnp.concatenate([ops[i]["b_h"], ops[i]["k_h"]], axis=0))) for i in units]
    new_states = [states[i] * ops[i]["w_tot"] + jnp.where(m["same_head"], upd[i], 0.0) for i in units]
    return new_states, ys


def _rwkv_body(reverse, bt, *refs):
    if reverse:
        (zc_ref, yf_ref, lw_ref, w0_ref, a0_ref, a0o_ref, kk_ref, ka_ref,
         rk_ref, bd_ref, gng_ref, gnb_ref, out_ref, state_ref) = refs
    else:
        (zc_ref, lw_ref, w0_ref, a0_ref, kk_ref, ka_ref,
         bd_ref, out_ref, state_ref) = refs
    c = RW_CHUNK

    @pl.when(pl.program_id(1) == 0)
    def _():
        state_ref[...] = jnp.zeros_like(state_ref)

    rows = bt * c
    m = _rwkv_masks(reverse)
    lora_lane = lax.broadcasted_iota(jnp.int32, (rows, LORA_PAD), 1)
    bd = bd_ref[...]
    zf = zc_ref[...].reshape(rows, Z_COLS)
    r = zf[:, :RW_WIDTH]
    k = zf[:, RW_WIDTH:2 * RW_WIDTH]
    v = zf[:, 2 * RW_WIDTH:3 * RW_WIDTH]
    lo = zf[:, 3 * RW_WIDTH:]
    act = jnp.where(lora_lane < DECAY_LORA, jnp.tanh(lo),
                    jnp.where(lora_lane < DECAY_LORA + ICLR_LORA, lo, _sigmoid(lo)))
    act_bf = act.astype(BF16)
    w_in = w0_ref[...] + _dot(act_bf, lw_ref[:, :RW_WIDTH])
    proj = _dot(act_bf, lw_ref[:, RW_WIDTH:])
    logw = -DECAY_SCALE * _sigmoid(w_in)
    a_sig = _sigmoid(a0_ref[...] + proj[:, :RW_WIDTH])
    kk = k * kk_ref[...]
    kk = kk * lax.rsqrt(jnp.maximum(_head_sums(kk * kk, bd), KK_NORM_FLOOR_SQ))
    kd = k * (1.0 + (a_sig - 1.0) * ka_ref[...])
    a_vec = -kk
    b_vec = kk * a_sig
    ri = lax.broadcasted_iota(jnp.int32, (rows, rows), 0)
    rj = lax.broadcasted_iota(jnp.int32, (rows, rows), 1)
    ordered = (rj >= ri) if reverse else (rj <= ri)
    tri = jnp.where((ri // c) == (rj // c), jnp.where(ordered, 1.0, 0.0), 0.0).astype(BF16)
    cum = _dot2_lhs01(tri, logw)
    edge = 0 if reverse else c - 1
    tot = jnp.concatenate([jnp.broadcast_to(cum[b * c + edge:b * c + edge + 1, :], (c, RW_WIDTH))
                           for b in range(bt)], axis=0)
    w_inv = jnp.exp(-cum)
    w_end = jnp.exp(tot - cum)
    w_tot = jnp.exp(tot)
    full_ops = dict(r_t=r * jnp.exp(cum), a_t=a_vec * jnp.exp(cum - logw), b_t=b_vec * w_inv, k_t=kd * w_inv,
                    b_h=b_vec * w_end, k_h=kd * w_end, v=v)
    ops, states = [], []
    for b in range(bt):
        for pi in range(N_PAIRS):
            ln = slice(pi * PAIR, (pi + 1) * PAIR)
            unit = {name: t[b * c:(b + 1) * c, ln] for name, t in full_ops.items()}
            unit["w_tot"] = w_tot[b * c:b * c + 1, ln]
            ops.append(unit)
            states.append(state_ref[b, pi])
    new_states, ys = _units_chunk(states, ops, m)
    for b in range(bt):
        for pi in range(N_PAIRS):
            state_ref[b, pi] = new_states[b * N_PAIRS + pi]
    y = jnp.concatenate([jnp.concatenate(ys[b * N_PAIRS:(b + 1) * N_PAIRS], axis=1) for b in range(bt)], axis=0)
    if not reverse:
        out_ref[...] = y.reshape(bt, c, RW_WIDTH)
    else:
        y = y + yf_ref[...].reshape(rows, RW_WIDTH)
        a_other = _sigmoid(a0o_ref[...] + proj[:, RW_WIDTH:2 * RW_WIDTH])
        kd_sum = kd + k * (1.0 + (a_other - 1.0) * ka_ref[...])
        bonus = _head_sums(r * kd_sum * rk_ref[...], bd) * v
        inv_n = 1.0 / RW_HEAD_DIM
        mean = _head_sums(y, bd) * inv_n
        yc = y - mean
        var = _head_sums(yc * yc, bd) * inv_n
        yn = yc * lax.rsqrt(var + GN_EPS) * gng_ref[...] + gnb_ref[...]
        g = proj[:, 2 * RW_WIDTH:]
        out_ref[...] = ((yn + bonus) * g).astype(BF16).reshape(bt, c, RW_WIDTH)


def _rwkv_pass(z3, yf, p, reverse, bt):
    b_sz, t_len, _ = z3.shape
    c = RW_CHUNK
    n_chunks = t_len // c

    def cidx(ci):
        return (n_chunks - 1 - ci) if reverse else ci

    full = lambda shape: pl.BlockSpec(shape, lambda bi, ci: (0,) * len(shape))
    row512 = full((1, RW_WIDTH))
    d = "rev" if reverse else "fwd"
    n_proj = p["lora_" + d].shape[1]
    in_specs = [pl.BlockSpec((bt, c, Z_COLS), lambda bi, ci: (bi, cidx(ci), 0))]
    args = [z3]
    if reverse:
        in_specs.append(pl.BlockSpec((bt, c, RW_WIDTH), lambda bi, ci: (bi, cidx(ci), 0)))
        args.append(yf)
    in_specs += [full((LORA_PAD, n_proj)), row512, row512]
    args += [p["lora_" + d], p["w0_" + d], p["a0_" + d]]
    if reverse:
        in_specs.append(row512)
        args.append(p["a0_fwd"])
    in_specs += [row512, row512]
    args += [p["rw_k_k"], p["rw_k_a"]]
    if reverse:
        in_specs.append(row512)
        args.append(p["rw_r_k"])
    in_specs.append(full((PAIR, PAIR)))
    args.append(p["head_ones"])
    if reverse:
        in_specs += [row512, row512]
        args += [p["rw_gn_g"], p["rw_gn_b"]]
    return pl.pallas_call(
        functools.partial(_rwkv_body, reverse, bt),
        grid=(b_sz // bt, n_chunks),
        in_specs=in_specs,
        out_specs=pl.BlockSpec((bt, c, RW_WIDTH), lambda bi, ci: (bi, cidx(ci), 0)),
        out_shape=jax.ShapeDtypeStruct((b_sz, t_len, RW_WIDTH), BF16 if reverse else F32),
        scratch_shapes=[pltpu.VMEM((bt, N_PAIRS, PAIR, PAIR), F32)],
        compiler_params=pltpu.CompilerParams(vmem_limit_bytes=VMEM_LIMIT),
        name="rwkv_" + d,
    )(*args)


def _mix_out_body(x_ref, gm_ref, rw_ref, wo_ref, g2_ref, rw_hi_ref, rw_lo_ref, rb_ref, earlier_ref,
                  x1_ref, h2_ref, route_t_ref, counts_ref, carry_ref):
    i = pl.program_id(0)

    @pl.when(i == 0)
    def _():
        carry_ref[...] = jnp.zeros_like(carry_ref)

    sub = earlier_ref.shape[0]
    subs = range(x_ref.shape[0] // sub)
    rows = [slice(s * sub, (s + 1) * sub) for s in subs]
    x1 = [x_ref[r, :] + _dot(gm_ref[r, :], wo_ref[:GM_WIDTH, :]) + _dot(rw_ref[r, :], wo_ref[GM_WIDTH:, :])
          for r in rows]
    h2 = [x1[s] * lax.rsqrt(jnp.mean(x1[s] * x1[s], axis=-1, keepdims=True) + RMS_EPS) * g2_ref[...] for s in subs]
    for s in subs:
        x1_ref[rows[s], :] = x1[s]
        h2_ref[rows[s], :] = _pack_bf16_halves(h2[s])
    split = [_split(h) for h in h2]
    logits = [_dot_nt(rw_hi_ref[...], hi) + _dot_nt(rw_hi_ref[...], lo) + _dot_nt(rw_lo_ref[...], hi) + rb_ref[...]
              for hi, lo in split]
    expert = lax.broadcasted_iota(jnp.int32, (N_EXPERTS, sub), 0).astype(F32)
    work = logits
    vals, ids, hits = [[] for _ in subs], [[] for _ in subs], [[] for _ in subs]
    onehot = [jnp.zeros((N_EXPERTS, sub), F32) for _ in subs]
    for _ in range(TOP_K):
        for s in subs:
            mx = jnp.max(work[s], axis=0, keepdims=True)
            idx = jnp.min(jnp.where(work[s] == mx, expert, float(N_EXPERTS)), axis=0, keepdims=True)
            hit = expert == idx
            vals[s].append(mx)
            ids[s].append(idx)
            hits[s].append(hit)
            onehot[s] = jnp.where(hit, 1.0, onehot[s])
        work = [jnp.where(hits[s][-1], -jnp.inf, work[s]) for s in subs]
    earlier_counts = [_dot(onehot[s].astype(BF16), earlier_ref[...]) for s in subs]
    carry = carry_ref[:, 0:1]
    for s in subs:
        exps = [jnp.exp(vk - vals[s][0]) for vk in vals[s]]
        denom = exps[0] + exps[1] + exps[2] + exps[3]
        ranks = carry + earlier_counts[s]
        rank_rows = [jnp.sum(jnp.where(hit, ranks, 0.0), axis=0, keepdims=True) for hit in hits[s]]
        route_t = jnp.concatenate(ids[s] + [e / denom for e in exps] + rank_rows
                                  + [jnp.zeros((ROUTE_ROWS - 3 * TOP_K, sub), F32)], axis=0)
        route_t_ref[:, rows[s]] = route_t
        carry = carry + jnp.sum(onehot[s], axis=1, keepdims=True)
    carry_ref[...] = jnp.broadcast_to(carry, carry_ref.shape)
    counts_ref[...] = jnp.broadcast_to(carry, counts_ref.shape)


def _mix_out(x2d, gm, rw, p, tm):
    n = x2d.shape[0]
    full = lambda shape: pl.BlockSpec(shape, lambda i: (0,) * len(shape))
    tile = lambda w: pl.BlockSpec((tm, w), lambda i: (i, 0))
    sub = min(MIX_OUT_SUB, tm)
    earlier = (jnp.arange(sub)[:, None] < jnp.arange(sub)[None, :]).astype(BF16)
    return pl.pallas_call(
        _mix_out_body,
        grid=(n // tm,),
        in_specs=[tile(D_MODEL), tile(GM_WIDTH), tile(RW_WIDTH), full((D_MODEL, D_MODEL)), full((1, D_MODEL)),
                  full((N_EXPERTS, D_MODEL)), full((N_EXPERTS, D_MODEL)), full((N_EXPERTS, 1)), full((sub, sub))],
        out_specs=[tile(D_MODEL), tile(D_MODEL // 2), pl.BlockSpec((ROUTE_ROWS, tm), lambda i: (0, i)),
                   full((N_EXPERTS, LANES))],
        out_shape=[
            jax.ShapeDtypeStruct((n, D_MODEL), F32),
            jax.ShapeDtypeStruct((n, D_MODEL // 2), jnp.int32),
            jax.ShapeDtypeStruct((ROUTE_ROWS, n), F32),
            jax.ShapeDtypeStruct((N_EXPERTS, LANES), F32),
        ],
        scratch_shapes=[pltpu.VMEM((N_EXPERTS, LANES), F32)],
        compiler_params=pltpu.CompilerParams(vmem_limit_bytes=VMEM_LIMIT),
        name="mix_out",
    )(x2d, gm, rw, p["w_out"], p["norm2_g"], p["router_hi"], p["router_lo"], p["router_b"], earlier)


def _sc_gather_rows(table, idx):
    n_idx = idx.shape[0]
    width = table.shape[1]
    n_workers = SC_CORES * SC_SUBCORES
    per_worker = n_idx // n_workers
    n_windows = per_worker // SC_WINDOW
    assert per_worker * n_workers == n_idx and n_windows * SC_WINDOW == per_worker
    mesh = plsc.VectorSubcoreMesh(core_axis_name="c", subcore_axis_name="s")

    @functools.partial(
        pl.kernel, mesh=mesh,
        out_type=jax.ShapeDtypeStruct((n_idx, width), table.dtype),
        scratch_types=[
            pltpu.VMEM((SC_WINDOW,), jnp.int32),
            pltpu.VMEM((SC_WINDOW, width), table.dtype),
            pltpu.SemaphoreType.DMA,
        ],
        name="sc_gather_rows",
    )
    def gather(table_hbm, idx_hbm, out_hbm, idx_v, rows_v, sem):
        worker = lax.axis_index("s") * SC_CORES + lax.axis_index("c")
        base = worker * per_worker

        @pl.loop(0, n_windows)
        def _(j):
            off = base + j * SC_WINDOW
            pltpu.sync_copy(idx_hbm.at[pl.ds(off, SC_WINDOW)], idx_v)
            pltpu.async_copy(table_hbm.at[idx_v], rows_v, sem).wait()
            pltpu.sync_copy(rows_v, out_hbm.at[pl.ds(off, SC_WINDOW)])

    return gather(table, idx)


def _sc_scatter_rows(src, dest_w, n_out):
    n, width = src.shape
    n_windows, top_k, window = dest_w.shape
    n_workers = SC_CORES * SC_SUBCORES
    per_worker = n_windows // n_workers
    assert n_windows * window == n and per_worker * n_workers == n_windows
    mesh = plsc.VectorSubcoreMesh(core_axis_name="c", subcore_axis_name="s")

    @functools.partial(
        pl.kernel, mesh=mesh,
        out_type=jax.ShapeDtypeStruct((n_out, width), src.dtype),
        scratch_types=[
            pltpu.VMEM((top_k, window), jnp.int32),
            pltpu.VMEM((window, width), src.dtype),
        ],
        name="sc_scatter_rows",
    )
    def scatter(src_hbm, dest_hbm, out_hbm, idx_v, rows_v):
        worker = lax.axis_index("s") * SC_CORES + lax.axis_index("c")

        @pl.loop(0, per_worker)
        def _(j):
            g = worker * per_worker + j
            pltpu.sync_copy(dest_hbm.at[g], idx_v)
            pltpu.sync_copy(src_hbm.at[pl.ds(g * window, window)], rows_v)
            for k in range(top_k):
                pltpu.sync_copy(rows_v, out_hbm.at[idx_v.at[k]])

    return scatter(src, dest_w)


def _weight_copies(wgu_hbm, wd_hbm, wgu_buf, wd_buf, sem, expert, slot):
    return (pltpu.make_async_copy(wgu_hbm.at[expert], wgu_buf.at[slot], sem.at[0, slot]),
            pltpu.make_async_copy(wd_hbm.at[expert], wd_buf.at[slot], sem.at[1, slot]))


def _experts_body(blk_e_ref, n_used_ref, valid_ref, slot_ref, next_e_ref, x_ref, wgu_hbm, bgu_ref, wd_hbm, bd_ref,
                  out_ref, wgu_bf, wd_bf, wgu_buf, wd_buf, sem):
    i = pl.program_id(0)
    n_used = n_used_ref[0]
    expert = blk_e_ref[i]
    slot = slot_ref[i]
    copies = functools.partial(_weight_copies, wgu_hbm, wd_hbm, wgu_buf, wd_buf, sem)

    @pl.when(i == 0)
    def _():
        for cp in copies(expert, slot):
            cp.start()

    @pl.when((i < n_used) & ((i == 0) | (expert != blk_e_ref[jnp.maximum(i - 1, 0)])))
    def _():
        for cp in copies(expert, slot):
            cp.wait()
        wgu_bf[...] = wgu_buf[slot].astype(BF16)
        wd_bf[...] = wd_buf[slot].astype(BF16)
        following = next_e_ref[i]

        @pl.when(following >= 0)
        def _():
            for cp in copies(following, 1 - slot):
                cp.start()

    valid = jnp.where(i < n_used, valid_ref[i], 0)

    def expert_mlp(n_rows):
        row = lax.broadcasted_iota(jnp.int32, (n_rows, 1), 0)
        x_left, x_right = _unpack_bf16_halves(jnp.where(row < valid, x_ref[:n_rows, :], 0))
        x = jnp.concatenate([x_left.astype(BF16), x_right.astype(BF16)], axis=1)
        gu = _dot(x, wgu_bf[...]) + bgu_ref[0]
        gate = jnp.minimum(gu[:, :D_EXPERT], SWIGLU_LIMIT)
        up = jnp.clip(gu[:, D_EXPERT:], -SWIGLU_LIMIT, SWIGLU_LIMIT)
        act = gate * _sigmoid(gate * SWIGLU_ALPHA) * (up + 1.0)
        out_ref[:n_rows, :] = _pack_bf16_halves(_dot(act.astype(BF16), wd_bf[...]) + bd_ref[0])

    for quarters in range(1, MOE_ROW_STEPS + 1):
        n_rows = quarters * MOE_ROWS // MOE_ROW_STEPS

        @pl.when((valid > n_rows - MOE_ROWS // MOE_ROW_STEPS) & (valid <= n_rows))
        def _(n_rows=n_rows):
            expert_mlp(n_rows)
            if n_rows < MOE_ROWS:
                out_ref[n_rows:, :] = jnp.zeros((MOE_ROWS - n_rows, D_MODEL // 2), jnp.int32)

    @pl.when(valid == 0)
    def _():
        out_ref[...] = jnp.zeros_like(out_ref)


def _experts(x_rows, blk_e, n_used, valid, slot, next_e, p):
    n_blocks = blk_e.shape[0]
    grid_spec = pltpu.PrefetchScalarGridSpec(
        num_scalar_prefetch=5,
        grid=(n_blocks,),
        in_specs=[
            pl.BlockSpec((MOE_ROWS, D_MODEL // 2), lambda i, be, nu, *_: (jnp.minimum(i, nu[0] - 1), 0)),
            pl.BlockSpec(memory_space=pl.ANY),
            pl.BlockSpec((1, 1, 2 * D_EXPERT), lambda i, be, *_: (be[i], 0, 0)),
            pl.BlockSpec(memory_space=pl.ANY),
            pl.BlockSpec((1, 1, D_MODEL), lambda i, be, *_: (be[i], 0, 0)),
        ],
        out_specs=pl.BlockSpec((MOE_ROWS, D_MODEL // 2), lambda i, *_: (i, 0)),
        scratch_shapes=[
            pltpu.VMEM((D_MODEL, 2 * D_EXPERT), BF16),
            pltpu.VMEM((D_EXPERT, D_MODEL), BF16),
            pltpu.VMEM((2, D_MODEL, 2 * D_EXPERT), F32),
            pltpu.VMEM((2, D_EXPERT, D_MODEL), F32),
            pltpu.SemaphoreType.DMA((2, 2)),
        ],
    )
    return pl.pallas_call(
        _experts_body,
        grid_spec=grid_spec,
        out_shape=jax.ShapeDtypeStruct((n_blocks * MOE_ROWS, D_MODEL // 2), jnp.int32),
        compiler_params=pltpu.CompilerParams(vmem_limit_bytes=VMEM_LIMIT),
        name="experts",
    )(blk_e, n_used, valid, slot, next_e, x_rows, p["w_gu"], p["b_gu"], p["w_down"], p["b_down"])


def _combine_body(yg_ref, x1_ref, route_t_ref, fg_ref, out_ref):
    tm = x1_ref.shape[0]
    gates = jnp.concatenate([route_t_ref[...], jnp.zeros((LANES - ROUTE_ROWS, tm), F32)], axis=0).T
    half = D_MODEL // 2
    acc_left = x1_ref[:, :half]
    acc_right = x1_ref[:, half:]
    for kk in range(TOP_K):
        y_left, y_right = _unpack_bf16_halves(yg_ref[kk])
        gate = gates[:, TOP_K + kk:TOP_K + kk + 1]
        acc_left = acc_left + gate * y_left
        acc_right = acc_right + gate * y_right
    acc = jnp.concatenate([acc_left, acc_right], axis=1)
    out_ref[...] = acc * lax.rsqrt(jnp.mean(acc * acc, axis=-1, keepdims=True) + RMS_EPS) * fg_ref[...]


def _combine(yg, x1, gates, final_g, tm):
    n = x1.shape[0]
    return pl.pallas_call(
        _combine_body,
        grid=(n // tm,),
        in_specs=[
            pl.BlockSpec((TOP_K, tm, D_MODEL // 2), lambda i: (0, i, 0)),
            pl.BlockSpec((tm, D_MODEL), lambda i: (i, 0)),
            pl.BlockSpec((ROUTE_ROWS, tm), lambda i: (0, i)),
            pl.BlockSpec((1, D_MODEL), lambda i: (0, 0)),
        ],
        out_specs=pl.BlockSpec((tm, D_MODEL), lambda i: (i, 0)),
        out_shape=jax.ShapeDtypeStruct((n, D_MODEL), F32),
        compiler_params=pltpu.CompilerParams(vmem_limit_bytes=VMEM_LIMIT),
        name="combine",
    )(yg, x1, gates, final_g)


def _prepare(norm1_g, w_in, rw_mu, gm_ln_g, gm_ln_b, gm_ws, gm_bs, rw_w0, rw_w2, rw_a0, rw_a2, rw_g2,
             rw_k_k, rw_k_a, rw_r_k, rw_gn_g, rw_gn_b, w_out, norm2_g, router_w, router_b,
             w_gu, b_gu, w_down, b_down, final_g):
    l = 0
    row = lambda t: t.reshape(1, -1).astype(F32)
    p = {}
    p["norm1_g"] = row(norm1_g[l])
    w = w_in[l]
    p["w_gm"] = w[:, :2 * GM_WIDTH].astype(BF16)
    pad_cols = LORA_PAD - LORA_COLS
    p["w_z"] = jnp.pad(w[:, 2 * GM_WIDTH:], ((0, 0), (0, pad_cols))).astype(BF16)
    mu = jnp.pad(row(rw_mu[l]), ((0, 0), (0, pad_cols)))
    p["rw_mu"] = jnp.concatenate([1.0 - mu, 0.5 * mu], axis=0)
    p["gm_ln_g"] = row(gm_ln_g[l])
    p["gm_ln_b"] = row(gm_ln_b[l])
    p["gm_ws"] = gm_ws[l].astype(BF16)
    p["gm_bs"] = jnp.broadcast_to(gm_bs[l][:, :, None], (GM_HEADS, GM_CHUNK, GM_HEAD_DIM)).astype(F32)
    zeros = lambda r, c: jnp.zeros((r, c), F32)
    o_a = DECAY_LORA
    o_g = DECAY_LORA + ICLR_LORA

    def lora_matrix(d, with_epilogue):
        blocks = [jnp.concatenate([rw_w2[l, d], zeros(LORA_PAD - DECAY_LORA, RW_WIDTH)], axis=0),
                  jnp.concatenate([zeros(o_a, RW_WIDTH), rw_a2[l, d], zeros(LORA_PAD - o_g, RW_WIDTH)], axis=0)]
        if with_epilogue:
            blocks.append(jnp.concatenate([zeros(o_a, RW_WIDTH), rw_a2[l, 1 - d], zeros(LORA_PAD - o_g, RW_WIDTH)],
                                          axis=0))
            blocks.append(jnp.concatenate([zeros(o_g, RW_WIDTH), rw_g2[l], zeros(LORA_PAD - LORA_COLS, RW_WIDTH)],
                                          axis=0))
        return jnp.concatenate(blocks, axis=1)

    for d, name in ((0, "fwd"), (1, "rev")):
        p["lora_" + name] = lora_matrix(d, with_epilogue=(d == 1)).astype(BF16)
        p["w0_" + name] = row(rw_w0[l, d])
        p["a0_" + name] = row(rw_a0[l, d])
    p["rw_k_k"] = row(rw_k_k[l])
    p["rw_k_a"] = row(rw_k_a[l])
    p["rw_r_k"] = row(rw_r_k[l])
    p["rw_gn_g"] = row(rw_gn_g[l])
    p["rw_gn_b"] = row(rw_gn_b[l])
    ch = jnp.arange(PAIR) // RW_HEAD_DIM
    p["head_ones"] = (ch[:, None] == ch[None, :]).astype(BF16)
    p["w_out"] = w_out[l].astype(BF16)
    p["norm2_g"] = row(norm2_g[l])
    rw_t = router_w[l].astype(F32).T
    hi = rw_t.astype(BF16)
    p["router_hi"] = hi
    p["router_lo"] = (rw_t - hi.astype(F32)).astype(BF16)
    p["router_b"] = router_b[l].astype(F32).reshape(N_EXPERTS, 1)
    p["w_gu"] = w_gu[l]
    p["b_gu"] = b_gu[l].reshape(N_EXPERTS, 1, 2 * D_EXPERT).astype(F32)
    p["w_down"] = w_down[l]
    p["b_down"] = b_down[l].reshape(N_EXPERTS, 1, D_MODEL).astype(F32)
    p["final_g"] = row(final_g)
    return p


def _pick_tile(n, want):
    t = want
    while n % t:
        t //= 2
    return t


def _encoder(x, p):
    b_sz, t_len, d = x.shape
    n = b_sz * t_len
    x2d = x.reshape(n, d)
    gm, z = _mix_in(x2d, p, _pick_tile(t_len, 1024), t_len)
    z3 = z.reshape(b_sz, t_len, Z_COLS)
    yf = _rwkv_pass(z3, None, p, reverse=False, bt=RW_BATCH_TILE)
    rw = _rwkv_pass(z3, yf, p, reverse=True, bt=RW_BATCH_TILE)
    tmo = _pick_tile(n, 1024)
    x1, h2, route_t, counts = _mix_out(x2d, gm, rw.reshape(n, RW_WIDTH), p, tmo)
    ids = route_t[:TOP_K].astype(jnp.int32)
    ranks = route_t[2 * TOP_K:3 * TOP_K].astype(jnp.int32)
    cnt = counts[:, 0].astype(jnp.int32)
    padded = (cnt + MOE_ROWS - 1) // MOE_ROWS * MOE_ROWS
    pad_end = jnp.cumsum(padded)
    pad_start = pad_end - padded
    expert_ids = jnp.arange(N_EXPERTS, dtype=jnp.int32)
    start_of = jnp.sum(jnp.where(ids[..., None] == expert_ids, pad_start, 0), axis=-1)
    dest = start_of + ranks
    n_blocks = n * TOP_K // MOE_ROWS + N_EXPERTS
    blk_start = jnp.arange(n_blocks, dtype=jnp.int32) * MOE_ROWS
    blk_e = jnp.minimum(jnp.sum((pad_end[None, :] <= blk_start[:, None]).astype(jnp.int32), axis=1), N_EXPERTS - 1)
    n_used = (pad_end[-1] // MOE_ROWS).astype(jnp.int32).reshape(1)
    blk_onehot = blk_e[:, None] == expert_ids
    blk_cnt = jnp.sum(jnp.where(blk_onehot, cnt, 0), axis=-1)
    blk_first = jnp.sum(jnp.where(blk_onehot, pad_start, 0), axis=-1)
    valid = jnp.clip(blk_cnt - (blk_start - blk_first), 0, MOE_ROWS).astype(jnp.int32)
    window = min(SC_SCATTER_WINDOW, n // (SC_CORES * SC_SUBCORES))
    dest_w = dest.reshape(TOP_K, n // window, window).transpose(1, 0, 2)
    x_rows = _sc_scatter_rows(h2, dest_w, n_blocks * MOE_ROWS)
    present = cnt > 0
    slot_of_expert = (jnp.cumsum(present.astype(jnp.int32)) - 1) % 2
    later = present[None, :] & (expert_ids[None, :] > expert_ids[:, None])
    next_present = jnp.min(jnp.where(later, expert_ids[None, :], N_EXPERTS), axis=1)
    next_present = jnp.where(next_present < N_EXPERTS, next_present, -1)
    slot = jnp.sum(jnp.where(blk_onehot, slot_of_expert, 0), axis=-1).astype(jnp.int32)
    next_e = jnp.sum(jnp.where(blk_onehot, next_present, 0), axis=-1).astype(jnp.int32)
    y_rows = _experts(x_rows, blk_e, n_used, valid, slot, next_e, p)
    yg = _sc_gather_rows(y_rows, dest.reshape(-1)).reshape(TOP_K, n, D_MODEL // 2)
    out = _combine(yg, x1, route_t, p["final_g"], _pick_tile(n, 1024))
    return out.reshape(b_sz, t_len, d)


def kernel(x_prompt, x_sample, norm1_g, w_in, rw_mu, gm_ln_g, gm_ln_b, gm_ws, gm_bs, rw_w0, rw_w2, rw_a0, rw_a2,
           rw_g2, rw_k_k, rw_k_a, rw_r_k, rw_gn_g, rw_gn_b, w_out, norm2_g, router_w, router_b, w_gu, b_gu,
           w_down, b_down, final_g):
    p = _prepare(norm1_g, w_in, rw_mu, gm_ln_g, gm_ln_b, gm_ws, gm_bs, rw_w0, rw_w2, rw_a0, rw_a2, rw_g2,
                 rw_k_k, rw_k_a, rw_r_k, rw_gn_g, rw_gn_b, w_out, norm2_g, router_w, router_b,
                 w_gu, b_gu, w_down, b_down, final_g)
    return (_encoder(x_prompt, p), _encoder(x_sample, p))
```
